```python
import jax, jax.numpy as jnp
from jax import lax
import numpy as np

D_MODEL = 2048
BATCH = 2
SEQ = 4096
DEPTH = 1
DEC_BATCH = 32
DEC_SEQ = 1
PAST_LEN = 16384
PAGE_SIZE = 128

HEAD_DIM = 128
N_ATTN_HEADS = D_MODEL // (2 * HEAD_DIM)
N_KV_HEADS = 2
GQA_GROUP = N_ATTN_HEADS // N_KV_HEADS
ATTN_WIDTH = N_ATTN_HEADS * HEAD_DIM
WINDOW = 128
ATTN_BLOCK = 128
N_RET_HEADS = D_MODEL // 256
RET_QK_DIM = (D_MODEL // 2) // N_RET_HEADS
RET_V_DIM = RET_QK_DIM
RET_WIDTH = N_RET_HEADS * RET_V_DIM
RET_CHUNK = 128
MIX_WIDTH = ATTN_WIDTH + RET_WIDTH
D_FF = -(-8 * D_MODEL // (3 * 256)) * 256
PLE_DIM = 256
ROPE_BASE = 10000.0
NORM_EPS = 1e-6
ATTN_SCALE = HEAD_DIM ** -0.5
RET_K_SCALE = RET_QK_DIM ** -0.5
_SPLIT_SIZES = (ATTN_WIDTH, N_KV_HEADS * HEAD_DIM, N_KV_HEADS * HEAD_DIM,
                N_RET_HEADS * RET_QK_DIM, N_RET_HEADS * RET_QK_DIM, RET_WIDTH, RET_WIDTH)
_SPLIT_POINTS = tuple(int(c) for c in np.cumsum(_SPLIT_SIZES)[:-1])
IN_WIDTH = sum(_SPLIT_SIZES)

kernel_name = 'hybrid_swa_sink_retention_decoder_step'


def _rms_norm(x, g):
    xf = x.astype(jnp.float32)
    y = xf * lax.rsqrt(jnp.mean(xf * xf, axis=-1, keepdims=True) + NORM_EPS)
    return (y * g.astype(jnp.float32)).astype(x.dtype)


def _rotary(x, pos):
    half = x.shape[-1] // 2
    inv = ROPE_BASE ** (-jnp.arange(half, dtype=jnp.float32) / half)
    ang = pos.astype(jnp.float32)[:, None] * inv[None, :]
    cos = jnp.cos(ang)[None, :, None, :]
    sin = jnp.sin(ang)[None, :, None, :]
    xf = x.astype(jnp.float32)
    x1, x2 = xf[..., :half], xf[..., half:]
    return jnp.concatenate([x1 * cos - x2 * sin, x1 * sin + x2 * cos], axis=-1)


def _sink_softmax(s, mask, sink):
    s = jnp.where(mask, s, -jnp.inf)
    m = jnp.maximum(jnp.max(s, axis=-1), sink)
    e = jnp.exp(s - m[..., None])
    denom = jnp.sum(e, axis=-1) + jnp.exp(sink - m)
    return e / denom[..., None]


def _swa_prompt(q, k, v, sinks):
    b, t = q.shape[0], q.shape[1]
    nb = t // ATTN_BLOCK
    qb = q.reshape(b, nb, ATTN_BLOCK, N_KV_HEADS, GQA_GROUP, HEAD_DIM)

    def band(a):
        ap = jnp.pad(a, ((0, 0), (ATTN_BLOCK, 0), (0, 0), (0, 0)))
        ap = ap.reshape(b, nb + 1, ATTN_BLOCK, N_KV_HEADS, HEAD_DIM)
        return jnp.concatenate([ap[:, :-1], ap[:, 1:]], axis=2)

    kb, vb = band(k), band(v)
    blk = jnp.arange(nb)[:, None]
    qpos = blk * ATTN_BLOCK + jnp.arange(ATTN_BLOCK)[None, :]
    kpos = (blk - 1) * ATTN_BLOCK + jnp.arange(2 * ATTN_BLOCK)[None, :]
    dist = qpos[:, :, None] - kpos[:, None, :]
    mask = (dist >= 0) & (dist <= WINDOW) & (kpos[:, None, :] >= 0)
    s = jnp.einsum('bnqhgd,bnkhd->bnhgqk', qb, kb, preferred_element_type=jnp.float32) * ATTN_SCALE
    sink = sinks.astype(jnp.float32).reshape(N_KV_HEADS, GQA_GROUP)[None, None, :, :, None]
    p = _sink_softmax(s, mask[None, :, None, None], sink)
    o = jnp.einsum('bnhgqk,bnkhd->bnqhgd', p.astype(vb.dtype), vb)
    return o.reshape(b, t, ATTN_WIDTH)


def _swa_sample(q, k, v, k_buf, v_buf, sinks):
    b, t = q.shape[0], q.shape[1]
    w = k_buf.shape[1]
    kk = jnp.concatenate([k_buf.astype(k.dtype), k], axis=1)
    vv = jnp.concatenate([v_buf.astype(v.dtype), v], axis=1)
    qpos = PAST_LEN + jnp.arange(t)
    kpos = PAST_LEN - w + jnp.arange(w + t)
    dist = qpos[:, None] - kpos[None, :]
    mask = (dist >= 0) & (dist <= WINDOW)
    qg = q.reshape(b, t, N_KV_HEADS, GQA_GROUP, HEAD_DIM)
    s = jnp.einsum('bqhgd,bkhd->bhgqk', qg, kk, preferred_element_type=jnp.float32) * ATTN_SCALE
    sink = sinks.astype(jnp.float32).reshape(N_KV_HEADS, GQA_GROUP)[None, :, :, None]
    p = _sink_softmax(s, mask, sink)
    o = jnp.einsum('bhgqk,bkhd->bqhgd', p.astype(vv.dtype), vv).reshape(b, t, ATTN_WIDTH)
    return o, kk[:, -w:], vv[:, -w:]


def _ret_log_decay():
    return jnp.log1p(-jnp.exp2(-5.0 - jnp.arange(N_RET_HEADS, dtype=jnp.float32)))


def _retention(q, k, v, s0):
    b, t, h = q.shape[0], q.shape[1], q.shape[2]
    c = RET_CHUNK if t % RET_CHUNK == 0 else t
    nc = t // c
    lg = _ret_log_decay()
    idx = jnp.arange(c, dtype=jnp.float32)
    diff = idx[:, None] - idx[None, :]
    intra = jnp.where(diff >= 0, jnp.exp(lg[:, None, None] * jnp.maximum(diff, 0.0)), 0.0)
    q_dec = jnp.exp(lg[:, None] * (idx[None, :] + 1.0))[None, :, :, None]
    k_dec = jnp.exp(lg[:, None] * (c - 1.0 - idx[None, :]))[None, :, :, None]
    chunk_dec = jnp.exp(lg * c)[None, :, None, None]

    def to_chunks(a):
        return a.reshape(b, nc, c, h, a.shape[-1]).transpose(1, 0, 3, 2, 4)

    def step(s, inp):
        qc, kc, vc = inp
        sc = jnp.einsum('bhcd,bhed->bhce', qc, kc) * intra
        o = jnp.einsum('bhce,bhev->bhcv', sc, vc) + jnp.einsum('bhcd,bhdv->bhcv', qc * q_dec, s)
        s = s * chunk_dec + jnp.einsum('bhcd,bhcv->bhdv', kc * k_dec, vc)
        return s, o

    s_final, o = lax.scan(step, s0, (to_chunks(q), to_chunks(k), to_chunks(v)))
    o = o.transpose(1, 0, 3, 2, 4).reshape(b, t, h, v.shape[-1])
    return o, s_final


def _layer(h, pe, k_buf, v_buf, s0, pos, wts):
    (an, w_in, qn, kn, sinks, rg, w_out, fn, w_gate, w_up, w_down, pn, w_ple, w_pg) = wts
    b, t = h.shape[0], h.shape[1]
    a = _rms_norm(h, an)
    z = a @ w_in
    aq, ak, av, rq, rk, rv, rgate = jnp.split(z, _SPLIT_POINTS, axis=-1)
    aq = _rms_norm(aq.reshape(b, t, N_ATTN_HEADS, HEAD_DIM), qn)
    ak = _rms_norm(ak.reshape(b, t, N_KV_HEADS, HEAD_DIM), kn)
    av = av.reshape(b, t, N_KV_HEADS, HEAD_DIM)
    if k_buf is None:
        o_attn = _swa_prompt(aq, ak, av, sinks)
        w = min(WINDOW, t)
        nk, nv = ak[:, -w:], av[:, -w:]
    else:
        o_attn, nk, nv = _swa_sample(aq, ak, av, k_buf, v_buf, sinks)
    rq = _rotary(rq.reshape(b, t, N_RET_HEADS, RET_QK_DIM), pos)
    rk = _rotary(rk.reshape(b, t, N_RET_HEADS, RET_QK_DIM), pos) * RET_K_SCALE
    rv = rv.reshape(b, t, N_RET_HEADS, RET_V_DIM).astype(jnp.float32)
    o_ret, s_new = _retention(rq, rk, rv, s0.astype(jnp.float32))
    o_ret = o_ret * lax.rsqrt(jnp.mean(o_ret * o_ret, axis=-1, keepdims=True) + NORM_EPS)
    o_ret = o_ret.reshape(b, t, RET_WIDTH) * rg.astype(jnp.float32) * jax.nn.silu(rgate.astype(jnp.float32))
    mix = jnp.concatenate([o_attn, o_ret.astype(h.dtype)], axis=-1)
    h = h + mix @ w_out
    f = _rms_norm(h, fn)
    h = h + (jax.nn.silu(f @ w_gate) * (f @ w_up)) @ w_down
    h = h + (pe @ w_ple) * jax.nn.sigmoid(_rms_norm(h, pn) @ w_pg)
    return h, nk, nv, s_new.astype(h.dtype)


def setup_inputs(seed: int = 0) -> dict:
    key = jax.random.key(seed)
    ks = jax.random.split(key, 24)
    f32 = jnp.float32
    n = lambda i, shape: jax.random.normal(ks[i], shape, f32)
    cache_win = min(WINDOW, PAST_LEN)
    return {
        'x_prompt': n(0, (BATCH, SEQ, D_MODEL)),
        'x_sample': n(1, (DEC_BATCH, DEC_SEQ, D_MODEL)),
        'cache_k_win': n(2, (DEPTH, DEC_BATCH, cache_win, N_KV_HEADS, HEAD_DIM)),
        'cache_v_win': n(3, (DEPTH, DEC_BATCH, cache_win, N_KV_HEADS, HEAD_DIM)),
        'state_ret': n(4, (DEPTH, DEC_BATCH, N_RET_HEADS, RET_QK_DIM, RET_V_DIM)),
        'p_prompt': n(5, (DEPTH, BATCH, SEQ, PLE_DIM)),
        'p_sample': n(6, (DEPTH, DEC_BATCH, DEC_SEQ, PLE_DIM)),
        'attn_norm_g': 1.0 + 0.05 * n(7, (DEPTH, D_MODEL)),
        'w_in': n(8, (DEPTH, D_MODEL, IN_WIDTH)) * D_MODEL ** -0.5,
        'q_norm_g': 1.0 + 0.05 * n(9, (DEPTH, HEAD_DIM)),
        'k_norm_g': 1.0 + 0.05 * n(10, (DEPTH, HEAD_DIM)),
        'attn_sinks': 0.5 * n(11, (DEPTH, N_ATTN_HEADS)),
        'ret_out_g': 1.0 + 0.05 * n(12, (DEPTH, RET_WIDTH)),
        'w_out': n(13, (DEPTH, MIX_WIDTH, D_MODEL)) * MIX_WIDTH ** -0.5,
        'ffn_norm_g': 1.0 + 0.05 * n(14, (DEPTH, D_MODEL)),
        'w_gate': n(15, (DEPTH, D_MODEL, D_FF)) * D_MODEL ** -0.5,
        'w_up': n(16, (DEPTH, D_MODEL, D_FF)) * D_MODEL ** -0.5,
        'w_down': n(17, (DEPTH, D_FF, D_MODEL)) * D_FF ** -0.5,
        'ple_norm_g': 1.0 + 0.05 * n(18, (DEPTH, D_MODEL)),
        'w_ple': n(19, (DEPTH, PLE_DIM, D_MODEL)) * PLE_DIM ** -0.5,
        'w_ple_gate': n(20, (DEPTH, D_MODEL, D_MODEL)) * D_MODEL ** -0.5,
    }


def reference(x_prompt, x_sample, cache_k_win, cache_v_win, state_ret, p_prompt, p_sample,
              attn_norm_g, w_in, q_norm_g, k_norm_g, attn_sinks, ret_out_g, w_out,
              ffn_norm_g, w_gate, w_up, w_down, ple_norm_g, w_ple, w_ple_gate):
    y_p, y_s = x_prompt, x_sample
    pos_p = jnp.arange(x_prompt.shape[1])
    pos_s = PAST_LEN + jnp.arange(x_sample.shape[1])
    kp_l, vp_l, sp_l, ks_l, vs_l, ss_l = [], [], [], [], [], []
    for l in range(DEPTH):
        wts = (attn_norm_g[l], w_in[l], q_norm_g[l], k_norm_g[l], attn_sinks[l], ret_out_g[l], w_out[l],
               ffn_norm_g[l], w_gate[l], w_up[l], w_down[l], ple_norm_g[l], w_ple[l], w_ple_gate[l])
        s0 = jnp.zeros((x_prompt.shape[0], N_RET_HEADS, RET_QK_DIM, RET_V_DIM), jnp.float32)
        y_p, kp, vp, sp = _layer(y_p, p_prompt[l], None, None, s0, pos_p, wts)
        y_s, kd, vd, sd = _layer(y_s, p_sample[l], cache_k_win[l], cache_v_win[l], state_ret[l], pos_s, wts)
        kp_l.append(kp); vp_l.append(vp); sp_l.append(sp)
        ks_l.append(kd); vs_l.append(vd); ss_l.append(sd)
    k_win_prompt = jnp.stack(kp_l)
    v_win_prompt = jnp.stack(vp_l)
    ret_state_prompt = jnp.stack(sp_l)
    k_win_sample = jnp.stack(ks_l)
    v_win_sample = jnp.stack(vs_l)
    ret_state_sample = jnp.stack(ss_l)
    return (y_p, y_s, k_win_prompt, v_win_prompt, ret_state_prompt, k_win_sample, v_win_sample, ret_state_sample)
```

```python
import functools

import numpy as np
import jax
import jax.numpy as jnp
from jax import lax
from jax.experimental import pallas as pl
from jax.experimental.pallas import tpu as pltpu

D_MODEL = 2048
HEAD_DIM = 128
N_ATTN_HEADS = 8
N_KV_HEADS = 2
GQA_GROUP = N_ATTN_HEADS // N_KV_HEADS
ATTN_WIDTH = N_ATTN_HEADS * HEAD_DIM
KV_WIDTH = N_KV_HEADS * HEAD_DIM
WINDOW = 128
BLK = 128
N_RET_HEADS = 8
RET_DIM = 128
RET_WIDTH = N_RET_HEADS * RET_DIM
D_FF = 5632
IN_WIDTH = 5632
PLE_DIM = 256
PAST_LEN = 16384
ROPE_BASE = 10000.0
NORM_EPS = 1e-6
ATTN_SCALE = HEAD_DIM ** -0.5
RET_K_SCALE = RET_DIM ** -0.5

COL_RQ, COL_RK, COL_RV, COL_RG = 0, RET_WIDTH, 2 * RET_WIDTH, 3 * RET_WIDTH
COL_AQ = 4 * RET_WIDTH
COL_AK = COL_AQ + ATTN_WIDTH
COL_AV = COL_AK + KV_WIDTH
IN_TILE = 512
IN_TILES = IN_WIDTH // IN_TILE
IN_ROT = (ATTN_WIDTH + 2 * KV_WIDTH) // IN_TILE

VMEM_LIMIT = 56 * 1024 * 1024

F32 = jnp.float32
BF16 = jnp.bfloat16


def _params(*sem):
    return pltpu.CompilerParams(dimension_semantics=sem, vmem_limit_bytes=VMEM_LIMIT)


def _rms(x, g):
    return x * lax.rsqrt(jnp.mean(x * x, axis=-1, keepdims=True) + NORM_EPS) * g


def _dot(a, b):
    return jnp.dot(a, b, preferred_element_type=F32)


def _dot_nt(a, b):
    return lax.dot_general(a, b, (((1,), (1,)), ((), ())), preferred_element_type=F32)


def _dot_tn(a, b):
    return lax.dot_general(a, b, (((0,), (0,)), ((), ())), preferred_element_type=F32)


def _in_proj_kernel(x_ref, g_ref, w_ref, o_ref, a_ref):
    @pl.when(pl.program_id(1) == 0)
    def _():
        a_ref[...] = _rms(x_ref[...], g_ref[...]).astype(BF16)

    o_ref[...] = _dot(a_ref[...], w_ref[...]).astype(o_ref.dtype)


def _in_proj(x, g, w_bf16, tm, out_dtype):
    m = x.shape[0]
    return pl.pallas_call(
        _in_proj_kernel,
        grid=(m // tm, IN_TILES),
        in_specs=[
            pl.BlockSpec((tm, D_MODEL), lambda i, j: (i, 0)),
            pl.BlockSpec((1, D_MODEL), lambda i, j: (0, 0)),
            pl.BlockSpec((D_MODEL, IN_TILE), lambda i, j: (0, (j + IN_ROT) % IN_TILES)),
        ],
        out_specs=pl.BlockSpec((tm, IN_TILE), lambda i, j: (i, j)),
        out_shape=jax.ShapeDtypeStruct((m, IN_WIDTH), out_dtype),
        scratch_shapes=[pltpu.VMEM((tm, D_MODEL), BF16)],
        compiler_params=_params("parallel", "arbitrary"),
        name="in_proj",
    )(x, g, w_bf16)


def _attn_prompt_kernel(q_ref, kc_ref, vc_ref, kp_ref, vp_ref, qg_ref, kg_ref, sink_ref,
                        o_ref, kwin_ref, vwin_ref):
    n = pl.program_id(1)
    qg = qg_ref[...]
    kg = kg_ref[...]
    row = lax.broadcasted_iota(jnp.int32, (BLK, 2 * BLK), 0)
    col = lax.broadcasted_iota(jnp.int32, (BLK, 2 * BLK), 1)
    mask = ((col < BLK) & (col >= row) & (n > 0)) | ((col >= BLK) & ((col - BLK) <= row))

    kc = kc_ref[...].astype(F32)
    kp = kp_ref[...].astype(F32)
    kc_n = [_rms(kc[:, h * HEAD_DIM:(h + 1) * HEAD_DIM], kg) for h in range(N_KV_HEADS)]
    kp_n = [_rms(kp[:, h * HEAD_DIM:(h + 1) * HEAD_DIM], kg) for h in range(N_KV_HEADS)]
    vc = vc_ref[...]
    vp = vp_ref[...]
    for kh in range(N_KV_HEADS):
        k_cat = jnp.concatenate([kp_n[kh], kc_n[kh]], axis=0).astype(BF16)
        sl = slice(kh * HEAD_DIM, (kh + 1) * HEAD_DIM)
        v_cat = jnp.concatenate([vp[:, sl], vc[:, sl]], axis=0)
        for g in range(GQA_GROUP):
            h = kh * GQA_GROUP + g
            hs = slice(h * HEAD_DIM, (h + 1) * HEAD_DIM)
            qn = (_rms(q_ref[:, hs].astype(F32), qg) * ATTN_SCALE).astype(BF16)
            s = jnp.where(mask, _dot_nt(qn, k_cat), -jnp.inf)
            sink = sink_ref[h]
            m = jnp.maximum(jnp.max(s, axis=-1, keepdims=True), sink)
            e = jnp.exp(s - m)
            denom = jnp.sum(e, axis=-1, keepdims=True) + jnp.exp(sink - m)
            o_ref[:, hs] = (_dot(e.astype(BF16), v_cat) / denom).astype(o_ref.dtype)

    @pl.when(n == pl.num_programs(1) - 1)
    def _():
        kwin_ref[0] = jnp.concatenate(kc_n, axis=1)
        vwin_ref[0] = vc.astype(F32)


def _attn_prompt(z, qg, kg, sinks, batch, seq):
    nb = seq // BLK
    qb, kb, vb = COL_AQ // ATTN_WIDTH, COL_AK // KV_WIDTH, COL_AV // KV_WIDTH
    prev = lambda b, n: b * nb + jnp.maximum(n - 1, 0)
    win = jax.ShapeDtypeStruct((batch, BLK, KV_WIDTH), F32)
    return pl.pallas_call(
        _attn_prompt_kernel,
        grid=(batch, nb),
        in_specs=[
            pl.BlockSpec((BLK, ATTN_WIDTH), lambda b, n: (b * nb + n, qb)),
            pl.BlockSpec((BLK, KV_WIDTH), lambda b, n: (b * nb + n, kb)),
            pl.BlockSpec((BLK, KV_WIDTH), lambda b, n: (b * nb + n, vb)),
            pl.BlockSpec((BLK, KV_WIDTH), lambda b, n: (prev(b, n), kb)),
            pl.BlockSpec((BLK, KV_WIDTH), lambda b, n: (prev(b, n), vb)),
            pl.BlockSpec((1, HEAD_DIM), lambda b, n: (0, 0)),
            pl.BlockSpec((1, HEAD_DIM), lambda b, n: (0, 0)),
            pl.BlockSpec(memory_space=pltpu.SMEM),
        ],
        out_specs=[
            pl.BlockSpec((BLK, ATTN_WIDTH), lambda b, n: (b * nb + n, 0)),
            pl.BlockSpec((1, BLK, KV_WIDTH), lambda b, n: (b, 0, 0)),
            pl.BlockSpec((1, BLK, KV_WIDTH), lambda b, n: (b, 0, 0)),
        ],
        out_shape=[jax.ShapeDtypeStruct((batch * seq, ATTN_WIDTH), BF16), win, win],
        compiler_params=_params("arbitrary", "arbitrary"),
        name="attn_prompt",
    )(z, z, z, z, z, qg, kg, sinks)


def _rotary(x, cos2, sin2):
    return x * cos2 + pltpu.roll(x, RET_DIM // 2, 1) * sin2


def _ret_prompt_kernel(q_ref, k_ref, v_ref, gate_ref, cos_ref, sin_ref, intra_ref, qdec_ref, kdec_ref,
                       cdec_ref, rg_ref, o_ref, s_out_ref, s_ref):
    n = pl.program_id(1)

    @pl.when(n == 0)
    def _():
        s_ref[...] = jnp.zeros_like(s_ref)

    cos2 = cos_ref[...]
    sin2 = sin_ref[...]
    for h in range(N_RET_HEADS):
        hs = slice(h * RET_DIM, (h + 1) * RET_DIM)
        q = _rotary(q_ref[:, hs].astype(F32), cos2, sin2)
        k = _rotary(k_ref[:, hs].astype(F32), cos2, sin2) * RET_K_SCALE
        v = v_ref[:, hs]
        state = s_ref[h]
        sc = _dot_nt(q.astype(BF16), k.astype(BF16)) * intra_ref[h]
        o = _dot(sc.astype(BF16), v) + _dot((q * qdec_ref[:, h:h + 1]).astype(BF16), state.astype(BF16))
        s_ref[h] = state * cdec_ref[h] + _dot_tn((k * kdec_ref[:, h:h + 1]).astype(BF16), v)
        o = o * lax.rsqrt(jnp.mean(o * o, axis=-1, keepdims=True) + NORM_EPS)
        gate = gate_ref[:, hs].astype(F32)
        o_ref[:, hs] = (o * rg_ref[:, hs] * (gate * jax.nn.sigmoid(gate))).astype(o_ref.dtype)

    @pl.when(n == pl.num_programs(1) - 1)
    def _():
        s_out_ref[0] = s_ref[...]


def _ret_tables(chunk):
    h = jnp.arange(N_RET_HEADS, dtype=F32)
    lg = jnp.log1p(-jnp.exp2(-5.0 - h))
    idx = jnp.arange(chunk, dtype=F32)
    diff = idx[:, None] - idx[None, :]
    intra = jnp.where(diff >= 0, jnp.exp(lg[:, None, None] * jnp.maximum(diff, 0.0)), 0.0)
    q_dec = jnp.exp(lg[None, :] * (idx[:, None] + 1.0))
    k_dec = jnp.exp(lg[None, :] * (chunk - 1.0 - idx[:, None]))
    chunk_dec = jnp.exp(lg * chunk)
    return intra, q_dec, k_dec, chunk_dec


def _rope_tables(pos):
    half = RET_DIM // 2
    inv = ROPE_BASE ** (-jnp.arange(half, dtype=F32) / half)
    ang = pos.astype(F32)[:, None] * inv[None, :]
    cos, sin = jnp.cos(ang), jnp.sin(ang)
    return jnp.concatenate([cos, cos], axis=-1), jnp.concatenate([-sin, sin], axis=-1)


def _ret_prompt(z, rg, batch, seq):
    nc = seq // BLK
    cos2, sin2 = _rope_tables(jnp.arange(seq))
    intra, q_dec, k_dec, chunk_dec = _ret_tables(BLK)
    tok = lambda c: pl.BlockSpec((BLK, RET_WIDTH), lambda b, n: (b * nc + n, c))
    const = lambda shape: pl.BlockSpec(shape, lambda b, n: (0,) * len(shape))
    return pl.pallas_call(
        _ret_prompt_kernel,
        grid=(batch, nc),
        in_specs=[
            tok(COL_RQ // RET_WIDTH), tok(COL_RK // RET_WIDTH), tok(COL_RV // RET_WIDTH), tok(COL_RG // RET_WIDTH),
            pl.BlockSpec((BLK, RET_DIM), lambda b, n: (n, 0)),
            pl.BlockSpec((BLK, RET_DIM), lambda b, n: (n, 0)),
            const((N_RET_HEADS, BLK, BLK)),
            const((BLK, N_RET_HEADS)),
            const((BLK, N_RET_HEADS)),
            pl.BlockSpec(memory_space=pltpu.SMEM),
            const((1, RET_WIDTH)),
        ],
        out_specs=[
            pl.BlockSpec((BLK, RET_WIDTH), lambda b, n: (b * nc + n, 0)),
            pl.BlockSpec((1, N_RET_HEADS, RET_DIM, RET_DIM), lambda b, n: (b, 0, 0, 0)),
        ],
        out_shape=[jax.ShapeDtypeStruct((batch * seq, RET_WIDTH), BF16),
                   jax.ShapeDtypeStruct((batch, N_RET_HEADS, RET_DIM, RET_DIM), F32)],
        scratch_shapes=[pltpu.VMEM((N_RET_HEADS, RET_DIM, RET_DIM), F32)],
        compiler_params=_params("arbitrary", "arbitrary"),
        name="ret_prompt",
    )(z, z, z, z, cos2, sin2, intra, q_dec, k_dec, chunk_dec, rg)


SAMPLE_GROUP = 8


def _attn_sample_kernel(q_ref, kn_ref, vn_ref, kbuf_ref, vbuf_ref, qg_ref, kg_ref, sink_ref,
                        o_ref, kout_ref, vout_ref):
    qg = qg_ref[...]
    kg = kg_ref[...]
    sink = sink_ref[...]
    head = lax.broadcasted_iota(jnp.int32, (N_ATTN_HEADS, HEAD_DIM), 0)
    first = head < GQA_GROUP
    last_row = lax.broadcasted_iota(jnp.int32, (WINDOW, KV_WIDTH), 0) == WINDOW - 1
    for b in range(SAMPLE_GROUP):
        qn = _rms(q_ref[b], qg) * ATTN_SCALE
        kn = _rms(kn_ref[b], kg)
        vn = vn_ref[b]
        kbuf = kbuf_ref[b]
        vbuf = vbuf_ref[b]
        qb = qn.astype(BF16)
        s0 = _dot_nt(qb, kbuf[:, :HEAD_DIM].astype(BF16))
        s1 = _dot_nt(qb, kbuf[:, HEAD_DIM:].astype(BF16))
        s = jnp.where(first[:, :WINDOW], s0, s1)
        k_sel = jnp.where(first, kn[0:1], kn[1:2])
        v_sel = jnp.where(first, vn[0:1], vn[1:2])
        s_new = jnp.sum(qn * k_sel, axis=-1, keepdims=True)
        m = jnp.maximum(jnp.maximum(jnp.max(s, axis=-1, keepdims=True), s_new), sink)
        e = jnp.exp(s - m)
        e_new = jnp.exp(s_new - m)
        denom = jnp.sum(e, axis=-1, keepdims=True) + e_new + jnp.exp(sink - m)
        eb = e.astype(BF16)
        o0 = _dot(eb, vbuf[:, :HEAD_DIM].astype(BF16))
        o1 = _dot(eb, vbuf[:, HEAD_DIM:].astype(BF16))
        o_ref[b] = (jnp.where(first, o0, o1) + e_new * v_sel) / denom
        k_row = jnp.concatenate([kn[0:1], kn[1:2]], axis=1)
        v_row = jnp.concatenate([vn[0:1], vn[1:2]], axis=1)
        kout_ref[b] = jnp.where(last_row, k_row, pltpu.roll(kbuf, WINDOW - 1, 0))
        vout_ref[b] = jnp.where(last_row, v_row, pltpu.roll(vbuf, WINDOW - 1, 0))


def _attn_sample(q, kn, vn, kbuf, vbuf, qg, kg, sinks):
    nb = q.shape[0]
    blk = lambda *shape: pl.BlockSpec((SAMPLE_GROUP,) + shape, lambda i: (i,) + (0,) * len(shape))
    const = lambda shape: pl.BlockSpec(shape, lambda i: (0,) * len(shape))
    return pl.pallas_call(
        _attn_sample_kernel,
        grid=(nb // SAMPLE_GROUP,),
        in_specs=[
            blk(N_ATTN_HEADS, HEAD_DIM), blk(N_KV_HEADS, HEAD_DIM), blk(N_KV_HEADS, HEAD_DIM),
            blk(WINDOW, KV_WIDTH), blk(WINDOW, KV_WIDTH),
            const((1, HEAD_DIM)), const((1, HEAD_DIM)), const((N_ATTN_HEADS, 1)),
        ],
        out_specs=[blk(N_ATTN_HEADS, HEAD_DIM), blk(WINDOW, KV_WIDTH), blk(WINDOW, KV_WIDTH)],
        out_shape=[jax.ShapeDtypeStruct((nb, N_ATTN_HEADS, HEAD_DIM), F32),
                   jax.ShapeDtypeStruct((nb, WINDOW, KV_WIDTH), F32),
                   jax.ShapeDtypeStruct((nb, WINDOW, KV_WIDTH), F32)],
        compiler_params=_params("parallel"),
        name="attn_sample",
    )(q, kn, vn, kbuf, vbuf, qg, kg, sinks)


def _ret_sample_kernel(qt_ref, kt_ref, v_ref, gate_ref, s0_ref, cos_ref, sin_ref, gamma_ref, rg_ref,
                       o_ref, s_out_ref):
    cos2 = cos_ref[...]
    sin2 = sin_ref[...]
    rot = lambda x: x * cos2 + pltpu.roll(x, RET_DIM // 2, 0) * sin2
    for b in range(SAMPLE_GROUP):
        qt = rot(qt_ref[b])
        kt = rot(kt_ref[b]) * RET_K_SCALE
        v = v_ref[b]
        gate = gate_ref[b]
        rows = []
        for h in range(N_RET_HEADS):
            s_new = s0_ref[b, h] * gamma_ref[h] + kt[:, h:h + 1] * v[h:h + 1, :]
            s_out_ref[b, h] = s_new
            o = jnp.sum(qt[:, h:h + 1] * s_new, axis=0, keepdims=True)
            rows.append(o * lax.rsqrt(jnp.mean(o * o, axis=-1, keepdims=True) + NORM_EPS))
        o = jnp.concatenate(rows, axis=0)
        o_ref[b] = o * rg_ref[...] * (gate * jax.nn.sigmoid(gate))


def _ret_sample(qt, kt, v, gate, s0, rg):
    nb = qt.shape[0]
    cos2, sin2 = _rope_tables(PAST_LEN + jnp.arange(1))
    h = jnp.arange(N_RET_HEADS, dtype=F32)
    gamma = jnp.exp(jnp.log1p(-jnp.exp2(-5.0 - h)))
    blk = lambda *shape: pl.BlockSpec((SAMPLE_GROUP,) + shape, lambda i: (i,) + (0,) * len(shape))
    const = lambda shape: pl.BlockSpec(shape, lambda i: (0,) * len(shape))
    return pl.pallas_call(
        _ret_sample_kernel,
        grid=(nb // SAMPLE_GROUP,),
        in_specs=[
            blk(RET_DIM, N_RET_HEADS), blk(RET_DIM, N_RET_HEADS), blk(N_RET_HEADS, RET_DIM),
            blk(N_RET_HEADS, RET_DIM), blk(N_RET_HEADS, RET_DIM, RET_DIM),
            const((RET_DIM, 1)), const((RET_DIM, 1)),
            pl.BlockSpec(memory_space=pltpu.SMEM),
            const((N_RET_HEADS, RET_DIM)),
        ],
        out_specs=[blk(N_RET_HEADS, RET_DIM), blk(N_RET_HEADS, RET_DIM, RET_DIM)],
        out_shape=[jax.ShapeDtypeStruct((nb, N_RET_HEADS, RET_DIM), F32),
                   jax.ShapeDtypeStruct((nb, N_RET_HEADS, RET_DIM, RET_DIM), F32)],
        compiler_params=_params("parallel"),
        name="ret_sample",
    )(qt, kt, v, gate, s0, cos2.reshape(RET_DIM, 1), sin2.reshape(RET_DIM, 1), gamma,
      rg.reshape(N_RET_HEADS, RET_DIM))


def _out_proj_kernel(h_ref, a_ref, r_ref, wa_ref, wr_ref, o_ref):
    o_ref[...] = h_ref[...] + _dot(a_ref[...], wa_ref[...]) + _dot(r_ref[...], wr_ref[...])


def _out_proj(h, o_attn, o_ret, w_bf16, tm):
    m = h.shape[0]
    return pl.pallas_call(
        _out_proj_kernel,
        grid=(m // tm,),
        in_specs=[
            pl.BlockSpec((tm, D_MODEL), lambda i: (i, 0)),
            pl.BlockSpec((tm, ATTN_WIDTH), lambda i: (i, 0)),
            pl.BlockSpec((tm, RET_WIDTH), lambda i: (i, 0)),
            pl.BlockSpec((ATTN_WIDTH, D_MODEL), lambda i: (0, 0)),
            pl.BlockSpec((RET_WIDTH, D_MODEL), lambda i: (1, 0)),
        ],
        out_specs=pl.BlockSpec((tm, D_MODEL), lambda i: (i, 0)),
        out_shape=jax.ShapeDtypeStruct((m, D_MODEL), F32),
        compiler_params=_params("parallel"),
        name="out_proj",
    )(h, o_attn, o_ret, w_bf16, w_bf16)


FF_TILE = 512


def _ffn_kernel(h_ref, g_ref, wg_ref, wu_ref, wd_ref, o_ref, f_ref):
    @pl.when(pl.program_id(1) == 0)
    def _():
        h = h_ref[...]
        f_ref[...] = _rms(h, g_ref[...]).astype(BF16)
        o_ref[...] = h

    f = f_ref[...]
    gate = _dot(f, wg_ref[...])
    act = (gate * jax.nn.sigmoid(gate) * _dot(f, wu_ref[...])).astype(BF16)
    o_ref[...] += _dot(act, wd_ref[...])


def _ffn(h, g, wg, wu, wd, tm):
    m = h.shape[0]
    return pl.pallas_call(
        _ffn_kernel,
        grid=(m // tm, D_FF // FF_TILE),
        in_specs=[
            pl.BlockSpec((tm, D_MODEL), lambda i, j: (i, 0)),
            pl.BlockSpec((1, D_MODEL), lambda i, j: (0, 0)),
            pl.BlockSpec((D_MODEL, FF_TILE), lambda i, j: (0, j)),
            pl.BlockSpec((D_MODEL, FF_TILE), lambda i, j: (0, j)),
            pl.BlockSpec((FF_TILE, D_MODEL), lambda i, j: (j, 0)),
        ],
        out_specs=pl.BlockSpec((tm, D_MODEL), lambda i, j: (i, 0)),
        out_shape=jax.ShapeDtypeStruct((m, D_MODEL), F32),
        scratch_shapes=[pltpu.VMEM((tm, D_MODEL), BF16)],
        compiler_params=_params("parallel", "arbitrary"),
        name="ffn",
    )(h, g, wg, wu, wd)


def _ple_kernel(h_ref, p_ref, g_ref, wp_ref, wg_ref, o_ref):
    h = h_ref[...]
    gate = jax.nn.sigmoid(_dot(_rms(h, g_ref[...]).astype(BF16), wg_ref[...]))
    o_ref[...] = h + _dot(p_ref[...].astype(BF16), wp_ref[...]) * gate


def _ple(h, p, g, w_ple, w_pg, tm):
    m = h.shape[0]
    return pl.pallas_call(
        _ple_kernel,
        grid=(m // tm,),
        in_specs=[
            pl.BlockSpec((tm, D_MODEL), lambda i: (i, 0)),
            pl.BlockSpec((tm, PLE_DIM), lambda i: (i, 0)),
            pl.BlockSpec((1, D_MODEL), lambda i: (0, 0)),
            pl.BlockSpec((PLE_DIM, D_MODEL), lambda i: (0, 0)),
            pl.BlockSpec((D_MODEL, D_MODEL), lambda i: (0, 0)),
        ],
        out_specs=pl.BlockSpec((tm, D_MODEL), lambda i: (i, 0)),
        out_shape=jax.ShapeDtypeStruct((m, D_MODEL), F32),
        compiler_params=_params("parallel"),
        name="ple",
    )(h, p, g, w_ple, w_pg)


def _dense_tail(h, o_attn, o_ret, p, w, tm):
    h = _out_proj(h, o_attn, o_ret, w["w_out"], tm)
    h = _ffn(h, w["ffn_norm_g"], w["w_gate"], w["w_up"], w["w_down"], tm)
    return _ple(h, p, w["ple_norm_g"], w["w_ple"], w["w_ple_gate"], tm)


def _layer_prompt(x, p, w):
    batch, seq, _ = x.shape
    h = x.reshape(batch * seq, D_MODEL)
    z = _in_proj(h, w["attn_norm_g"], w["w_in"], 1024, BF16)
    o_attn, k_win, v_win = _attn_prompt(z, w["q_norm_g"], w["k_norm_g"], w["attn_sinks"], batch, seq)
    o_ret, state = _ret_prompt(z, w["ret_out_g"], batch, seq)
    y = _dense_tail(h, o_attn, o_ret, p.reshape(batch * seq, PLE_DIM), w, 512)
    return (y.reshape(batch, seq, D_MODEL),
            k_win.reshape(batch, BLK, N_KV_HEADS, HEAD_DIM), v_win.reshape(batch, BLK, N_KV_HEADS, HEAD_DIM), state)


def _layer_sample(x, p, k_buf, v_buf, s0, w):
    nb = x.shape[0]
    h = x.reshape(nb, D_MODEL)
    z = _in_proj(h, w["attn_norm_g"], w["w_in"], nb, F32)
    heads = lambda c, n: z[:, c:c + n * HEAD_DIM].reshape(nb, n, HEAD_DIM)
    o_attn, k_win, v_win = _attn_sample(
        heads(COL_AQ, N_ATTN_HEADS), heads(COL_AK, N_KV_HEADS), heads(COL_AV, N_KV_HEADS),
        k_buf.reshape(nb, WINDOW, KV_WIDTH), v_buf.reshape(nb, WINDOW, KV_WIDTH),
        w["q_norm_g"], w["k_norm_g"], w["attn_sinks"].reshape(N_ATTN_HEADS, 1))
    o_ret, state = _ret_sample(
        heads(COL_RQ, N_RET_HEADS).transpose(0, 2, 1), heads(COL_RK, N_RET_HEADS).transpose(0, 2, 1),
        heads(COL_RV, N_RET_HEADS), heads(COL_RG, N_RET_HEADS), s0, w["ret_out_g"])
    y = _dense_tail(h, o_attn.reshape(nb, ATTN_WIDTH).astype(BF16), o_ret.reshape(nb, RET_WIDTH).astype(BF16),
                    p.reshape(nb, PLE_DIM), w, nb)
    return (y.reshape(nb, 1, D_MODEL),
            k_win.reshape(nb, WINDOW, N_KV_HEADS, HEAD_DIM), v_win.reshape(nb, WINDOW, N_KV_HEADS, HEAD_DIM), state)


def kernel(x_prompt, x_sample, cache_k_win, cache_v_win, state_ret, p_prompt, p_sample,
           attn_norm_g, w_in, q_norm_g, k_norm_g, attn_sinks, ret_out_g, w_out,
           ffn_norm_g, w_gate, w_up, w_down, ple_norm_g, w_ple, w_ple_gate):
    assert w_in.shape[0] == 1 and x_sample.shape[1] == 1 and cache_k_win.shape[2] == WINDOW
    y_p, y_s = x_prompt, x_sample
    outs = [[] for _ in range(6)]
    for l in range(w_in.shape[0]):
        w = {
            "attn_norm_g": attn_norm_g[l].reshape(1, D_MODEL),
            "w_in": w_in[l].astype(BF16),
            "q_norm_g": q_norm_g[l].reshape(1, HEAD_DIM),
            "k_norm_g": k_norm_g[l].reshape(1, HEAD_DIM),
            "attn_sinks": attn_sinks[l],
            "ret_out_g": ret_out_g[l].reshape(1, RET_WIDTH),
            "w_out": w_out[l].astype(BF16),
            "ffn_norm_g": ffn_norm_g[l].reshape(1, D_MODEL),
            "w_gate": w_gate[l].astype(BF16),
            "w_up": w_up[l].astype(BF16),
            "w_down": w_down[l].astype(BF16),
            "ple_norm_g": ple_norm_g[l].reshape(1, D_MODEL),
            "w_ple": w_ple[l].astype(BF16),
            "w_ple_gate": w_ple_gate[l].astype(BF16),
        }
        y_p, kp, vp, sp = _layer_prompt(y_p, p_prompt[l], w)
        y_s, ks, vs, ss = _layer_sample(y_s, p_sample[l], cache_k_win[l], cache_v_win[l], state_ret[l], w)
        for acc, val in zip(outs, (kp, vp, sp, ks, vs, ss)):
            acc.append(val)
    kp, vp, sp, ks, vs, ss = (jnp.stack(o) for o in outs)
    return (y_p, y_s, kp, vp, sp, ks, vs, ss)
```

```python
import functools

import numpy as np
import jax
import jax.numpy as jnp
from jax import lax
from jax.experimental import pallas as pl
from jax.experimental.pallas import tpu as pltpu

D_MODEL = 2048
HEAD_DIM = 128
N_ATTN_HEADS = 8
N_KV_HEADS = 2
GQA_GROUP = N_ATTN_HEADS // N_KV_HEADS
ATTN_WIDTH = N_ATTN_HEADS * HEAD_DIM
KV_WIDTH = N_KV_HEADS * HEAD_DIM
WINDOW = 128
BLK = 128
N_RET_HEADS = 8
RET_DIM = 128
RET_WIDTH = N_RET_HEADS * RET_DIM
D_FF = 5632
IN_WIDTH = 5632
PLE_DIM = 256
PAST_LEN = 16384
ROPE_BASE = 10000.0
NORM_EPS = 1e-6
ATTN_SCALE = HEAD_DIM ** -0.5
RET_K_SCALE = RET_DIM ** -0.5

COL_RQ, COL_RK, COL_RV, COL_RG = 0, RET_WIDTH, 2 * RET_WIDTH, 3 * RET_WIDTH
COL_AQ = 4 * RET_WIDTH
COL_AK = COL_AQ + ATTN_WIDTH
COL_AV = COL_AK + KV_WIDTH
IN_TILE = 512
IN_TILES = IN_WIDTH // IN_TILE
IN_ROT = (ATTN_WIDTH + 2 * KV_WIDTH) // IN_TILE

VMEM_LIMIT = 56 * 1024 * 1024

F32 = jnp.float32
BF16 = jnp.bfloat16


def _params(*sem):
    return pltpu.CompilerParams(dimension_semantics=sem, vmem_limit_bytes=VMEM_LIMIT)


def _rms(x, g):
    return x * lax.rsqrt(jnp.mean(x * x, axis=-1, keepdims=True) + NORM_EPS) * g


def _dot(a, b):
    return jnp.dot(a, b, preferred_element_type=F32)


def _dot_nt(a, b):
    return lax.dot_general(a, b, (((1,), (1,)), ((), ())), preferred_element_type=F32)


def _dot_tn(a, b):
    return lax.dot_general(a, b, (((0,), (0,)), ((), ())), preferred_element_type=F32)


def _in_proj_kernel(x_ref, g_ref, w_ref, o_ref, a_ref):
    @pl.when(pl.program_id(1) == 0)
    def _():
        a_ref[...] = _rms(x_ref[...], g_ref[...]).astype(BF16)

    o_ref[...] = _dot(a_ref[...], w_ref[...]).astype(o_ref.dtype)


def _in_proj(x, g, w_bf16, tm, out_dtype):
    m = x.shape[0]
    return pl.pallas_call(
        _in_proj_kernel,
        grid=(m // tm, IN_TILES),
        in_specs=[
            pl.BlockSpec((tm, D_MODEL), lambda i, j: (i, 0)),
            pl.BlockSpec((1, D_MODEL), lambda i, j: (0, 0)),
            pl.BlockSpec((D_MODEL, IN_TILE), lambda i, j: (0, (j + IN_ROT) % IN_TILES)),
        ],
        out_specs=pl.BlockSpec((tm, IN_TILE), lambda i, j: (i, j)),
        out_shape=jax.ShapeDtypeStruct((m, IN_WIDTH), out_dtype),
        scratch_shapes=[pltpu.VMEM((tm, D_MODEL), BF16)],
        compiler_params=_params("parallel", "arbitrary"),
        name="in_proj",
    )(x, g, w_bf16)


def _attn_prompt_kernel(q_ref, kc_ref, vc_ref, kp_ref, vp_ref, qg_ref, kg_ref, sink_ref,
                        o_ref, kwin_ref, vwin_ref):
    n = pl.program_id(1)
    qg = qg_ref[...]
    kg = kg_ref[...]
    row = lax.broadcasted_iota(jnp.int32, (BLK, 2 * BLK), 0)
    col = lax.broadcasted_iota(jnp.int32, (BLK, 2 * BLK), 1)
    mask = ((col < BLK) & (col >= row) & (n > 0)) | ((col >= BLK) & ((col - BLK) <= row))

    kc = kc_ref[...].astype(F32)
    kp = kp_ref[...].astype(F32)
    kc_n = [_rms(kc[:, h * HEAD_DIM:(h + 1) * HEAD_DIM], kg) for h in range(N_KV_HEADS)]
    kp_n = [_rms(kp[:, h * HEAD_DIM:(h + 1) * HEAD_DIM], kg) for h in range(N_KV_HEADS)]
    vc = vc_ref[...]
    vp = vp_ref[...]
    for kh in range(N_KV_HEADS):
        k_cat = jnp.concatenate([kp_n[kh], kc_n[kh]], axis=0).astype(BF16)
        sl = slice(kh * HEAD_DIM, (kh + 1) * HEAD_DIM)
        v_cat = jnp.concatenate([vp[:, sl], vc[:, sl]], axis=0)
        for g in range(GQA_GROUP):
            h = kh * GQA_GROUP + g
            hs = slice(h * HEAD_DIM, (h + 1) * HEAD_DIM)
            qn = (_rms(q_ref[:, hs].astype(F32), qg) * ATTN_SCALE).astype(BF16)
            s = jnp.where(mask, _dot_nt(qn, k_cat), -jnp.inf)
            sink = sink_ref[h]
            m = jnp.maximum(jnp.max(s, axis=-1, keepdims=True), sink)
            e = jnp.exp(s - m)
            denom = jnp.sum(e, axis=-1, keepdims=True) + jnp.exp(sink - m)
            o_ref[:, hs] = (_dot(e.astype(BF16), v_cat) / denom).astype(o_ref.dtype)

    @pl.when(n == pl.num_programs(1) - 1)
    def _():
        kwin_ref[0] = jnp.concatenate(kc_n, axis=1)
        vwin_ref[0] = vc.astype(F32)


def _attn_prompt(z, qg, kg, sinks, batch, seq):
    nb = seq // BLK
    qb, kb, vb = COL_AQ // ATTN_WIDTH, COL_AK // KV_WIDTH, COL_AV // KV_WIDTH
    prev = lambda b, n: b * nb + jnp.maximum(n - 1, 0)
    win = jax.ShapeDtypeStruct((batch, BLK, KV_WIDTH), F32)
    return pl.pallas_call(
        _attn_prompt_kernel,
        grid=(batch, nb),
        in_specs=[
            pl.BlockSpec((BLK, ATTN_WIDTH), lambda b, n: (b * nb + n, qb)),
            pl.BlockSpec((BLK, KV_WIDTH), lambda b, n: (b * nb + n, kb)),
            pl.BlockSpec((BLK, KV_WIDTH), lambda b, n: (b * nb + n, vb)),
            pl.BlockSpec((BLK, KV_WIDTH), lambda b, n: (prev(b, n), kb)),
            pl.BlockSpec((BLK, KV_WIDTH), lambda b, n: (prev(b, n), vb)),
            pl.BlockSpec((1, HEAD_DIM), lambda b, n: (0, 0)),
            pl.BlockSpec((1, HEAD_DIM), lambda b, n: (0, 0)),
            pl.BlockSpec(memory_space=pltpu.SMEM),
        ],
        out_specs=[
            pl.BlockSpec((BLK, ATTN_WIDTH), lambda b, n: (b * nb + n, 0)),
            pl.BlockSpec((1, BLK, KV_WIDTH), lambda b, n: (b, 0, 0)),
            pl.BlockSpec((1, BLK, KV_WIDTH), lambda b, n: (b, 0, 0)),
        ],
        out_shape=[jax.ShapeDtypeStruct((batch * seq, ATTN_WIDTH), BF16), win, win],
        compiler_params=_params("arbitrary", "arbitrary"),
        name="attn_prompt",
    )(z, z, z, z, z, qg, kg, sinks)


def _rotary(x, cos2, sin2):
    return x * cos2 + pltpu.roll(x, RET_DIM // 2, 1) * sin2


def _ret_prompt_kernel(q_ref, k_ref, v_ref, gate_ref, cos_ref, sin_ref, intra_ref, qdec_ref, kdec_ref,
                       cdec_ref, rg_ref, o_ref, s_out_ref, s_ref):
    n = pl.program_id(1)

    @pl.when(n == 0)
    def _():
        s_ref[...] = jnp.zeros_like(s_ref)

    cos2 = cos_ref[...]
    sin2 = sin_ref[...]
    for h in range(N_RET_HEADS):
        hs = slice(h * RET_DIM, (h + 1) * RET_DIM)
        q = _rotary(q_ref[:, hs].astype(F32), cos2, sin2)
        k = _rotary(k_ref[:, hs].astype(F32), cos2, sin2) * RET_K_SCALE
        v = v_ref[:, hs]
        state = s_ref[h]
        sc = _dot_nt(q.astype(BF16), k.astype(BF16)) * intra_ref[h]
        o = _dot(sc.astype(BF16), v) + _dot((q * qdec_ref[:, h:h + 1]).astype(BF16), state.astype(BF16))
        s_ref[h] = state * cdec_ref[h] + _dot_tn((k * kdec_ref[:, h:h + 1]).astype(BF16), v)
        o = o * lax.rsqrt(jnp.mean(o * o, axis=-1, keepdims=True) + NORM_EPS)
        gate = gate_ref[:, hs].astype(F32)
        o_ref[:, hs] = (o * rg_ref[:, hs] * (gate * jax.nn.sigmoid(gate))).astype(o_ref.dtype)

    @pl.when(n == pl.num_programs(1) - 1)
    def _():
        s_out_ref[0] = s_ref[...]


def _ret_tables(chunk):
    h = jnp.arange(N_RET_HEADS, dtype=F32)
    lg = jnp.log1p(-jnp.exp2(-5.0 - h))
    idx = jnp.arange(chunk, dtype=F32)
    diff = idx[:, None] - idx[None, :]
    intra = jnp.where(diff >= 0, jnp.exp(lg[:, None, None] * jnp.maximum(diff, 0.0)), 0.0)
    q_dec = jnp.exp(lg[None, :] * (idx[:, None] + 1.0))
    k_dec = jnp.exp(lg[None, :] * (chunk - 1.0 - idx[:, None]))
    chunk_dec = jnp.exp(lg * chunk)
    return intra, q_dec, k_dec, chunk_dec


def _rope_tables(pos):
    half = RET_DIM // 2
    inv = ROPE_BASE ** (-jnp.arange(half, dtype=F32) / half)
    ang = pos.astype(F32)[:, None] * inv[None, :]
    cos, sin = jnp.cos(ang), jnp.sin(ang)
    return jnp.concatenate([cos, cos], axis=-1), jnp.concatenate([-sin, sin], axis=-1)


def _ret_prompt(z, rg, batch, seq):
    nc = seq // BLK
    cos2, sin2 = _rope_tables(jnp.arange(seq))
    intra, q_dec, k_dec, chunk_dec = _ret_tables(BLK)
    tok = lambda c: pl.BlockSpec((BLK, RET_WIDTH), lambda b, n: (b * nc + n, c))
    const = lambda shape: pl.BlockSpec(shape, lambda b, n: (0,) * len(shape))
    return pl.pallas_call(
        _ret_prompt_kernel,
        grid=(batch, nc),
        in_specs=[
            tok(COL_RQ // RET_WIDTH), tok(COL_RK // RET_WIDTH), tok(COL_RV // RET_WIDTH), tok(COL_RG // RET_WIDTH),
            pl.BlockSpec((BLK, RET_DIM), lambda b, n: (n, 0)),
            pl.BlockSpec((BLK, RET_DIM), lambda b, n: (n, 0)),
            const((N_RET_HEADS, BLK, BLK)),
            const((BLK, N_RET_HEADS)),
            const((BLK, N_RET_HEADS)),
            pl.BlockSpec(memory_space=pltpu.SMEM),
            const((1, RET_WIDTH)),
        ],
        out_specs=[
            pl.BlockSpec((BLK, RET_WIDTH), lambda b, n: (b * nc + n, 0)),
            pl.BlockSpec((1, N_RET_HEADS, RET_DIM, RET_DIM), lambda b, n: (b, 0, 0, 0)),
        ],
        out_shape=[jax.ShapeDtypeStruct((batch * seq, RET_WIDTH), BF16),
                   jax.ShapeDtypeStruct((batch, N_RET_HEADS, RET_DIM, RET_DIM), F32)],
        scratch_shapes=[pltpu.VMEM((N_RET_HEADS, RET_DIM, RET_DIM), F32)],
        compiler_params=_params("arbitrary", "arbitrary"),
        name="ret_prompt",
    )(z, z, z, z, cos2, sin2, intra, q_dec, k_dec, chunk_dec, rg)


SAMPLE_GROUP = 8


def _attn_sample_kernel(q_ref, kn_ref, vn_ref, kbuf_ref, vbuf_ref, qg_ref, kg_ref, sink_ref,
                        o_ref, kout_ref, vout_ref):
    qg = qg_ref[...]
    kg = kg_ref[...]
    sink = sink_ref[...]
    head = lax.broadcasted_iota(jnp.int32, (N_ATTN_HEADS, HEAD_DIM), 0)
    first = head < GQA_GROUP
    last_row = lax.broadcasted_iota(jnp.int32, (WINDOW, KV_WIDTH), 0) == WINDOW - 1
    for b in range(SAMPLE_GROUP):
        qn = _rms(q_ref[b], qg) * ATTN_SCALE
        kn = _rms(kn_ref[b], kg)
        vn = vn_ref[b]
        kbuf = kbuf_ref[b]
        vbuf = vbuf_ref[b]
        qb = qn.astype(BF16)
        s0 = _dot_nt(qb, kbuf[:, :HEAD_DIM].astype(BF16))
        s1 = _dot_nt(qb, kbuf[:, HEAD_DIM:].astype(BF16))
        s = jnp.where(first[:, :WINDOW], s0, s1)
        k_sel = jnp.where(first, kn[0:1], kn[1:2])
        v_sel = jnp.where(first, vn[0:1], vn[1:2])
        s_new = jnp.sum(qn * k_sel, axis=-1, keepdims=True)
        m = jnp.maximum(jnp.maximum(jnp.max(s, axis=-1, keepdims=True), s_new), sink)
        e = jnp.exp(s - m)
        e_new = jnp.exp(s_new - m)
        denom = jnp.sum(e, axis=-1, keepdims=True) + e_new + jnp.exp(sink - m)
        eb = e.astype(BF16)
        o0 = _dot(eb, vbuf[:, :HEAD_DIM].astype(BF16))
        o1 = _dot(eb, vbuf[:, HEAD_DIM:].astype(BF16))
        o_ref[b] = (jnp.where(first, o0, o1) + e_new * v_sel) / denom
        k_row = jnp.concatenate([kn[0:1], kn[1:2]], axis=1)
        v_row = jnp.concatenate([vn[0:1], vn[1:2]], axis=1)
        kout_ref[b] = jnp.where(last_row, k_row, pltpu.roll(kbuf, WINDOW - 1, 0))
        vout_ref[b] = jnp.where(last_row, v_row, pltpu.roll(vbuf, WINDOW - 1, 0))


def _attn_sample(q, kn, vn, kbuf, vbuf, qg, kg, sinks):
    nb = q.shape[0]
    blk = lambda *shape: pl.BlockSpec((SAMPLE_GROUP,) + shape, lambda i: (i,) + (0,) * len(shape))
    const = lambda shape: pl.BlockSpec(shape, lambda i: (0,) * len(shape))
    return pl.pallas_call(
        _attn_sample_kernel,
        grid=(nb // SAMPLE_GROUP,),
        in_specs=[
            blk(N_ATTN_HEADS, HEAD_DIM), blk(N_KV_HEADS, HEAD_DIM), blk(N_KV_HEADS, HEAD_DIM),
            blk(WINDOW, KV_WIDTH), blk(WINDOW, KV_WIDTH),
            const((1, HEAD_DIM)), const((1, HEAD_DIM)), const((N_ATTN_HEADS, 1)),
        ],
        out_specs=[blk(N_ATTN_HEADS, HEAD_DIM), blk(WINDOW, KV_WIDTH), blk(WINDOW, KV_WIDTH)],
        out_shape=[jax.ShapeDtypeStruct((nb, N_ATTN_HEADS, HEAD_DIM), F32),
                   jax.ShapeDtypeStruct((nb, WINDOW, KV_WIDTH), F32),
                   jax.ShapeDtypeStruct((nb, WINDOW, KV_WIDTH), F32)],
        compiler_params=_params("parallel"),
        name="attn_sample",
    )(q, kn, vn, kbuf, vbuf, qg, kg, sinks)


def _ret_sample_kernel(qt_ref, kt_ref, v_ref, gate_ref, s0_ref, cos_ref, sin_ref, gamma_ref, rg_ref,
                       o_ref, s_out_ref):
    cos2 = cos_ref[...]
    sin2 = sin_ref[...]
    rot = lambda x: x * cos2 + pltpu.roll(x, RET_DIM // 2, 0) * sin2
    for b in range(SAMPLE_GROUP):
        qt = rot(qt_ref[b])
        kt = rot(kt_ref[b]) * RET_K_SCALE
        v = v_ref[b]
        gate = gate_ref[b]
        rows = []
        for h in range(N_RET_HEADS):
            s_new = s0_ref[b, h] * gamma_ref[h] + kt[:, h:h + 1] * v[h:h + 1, :]
            s_out_ref[b, h] = s_new
            o = jnp.sum(qt[:, h:h + 1] * s_new, axis=0, keepdims=True)
            rows.append(o * lax.rsqrt(jnp.mean(o * o, axis=-1, keepdims=True) + NORM_EPS))
        o = jnp.concatenate(rows, axis=0)
        o_ref[b] = o * rg_ref[...] * (gate * jax.nn.sigmoid(gate))


def _ret_sample(qt, kt, v, gate, s0, rg):
    nb = qt.shape[0]
    cos2, sin2 = _rope_tables(PAST_LEN + jnp.arange(1))
    h = jnp.arange(N_RET_HEADS, dtype=F32)
    gamma = jnp.exp(jnp.log1p(-jnp.exp2(-5.0 - h)))
    blk = lambda *shape: pl.BlockSpec((SAMPLE_GROUP,) + shape, lambda i: (i,) + (0,) * len(shape))
    const = lambda shape: pl.BlockSpec(shape, lambda i: (0,) * len(shape))
    return pl.pallas_call(
        _ret_sample_kernel,
        grid=(nb // SAMPLE_GROUP,),
        in_specs=[
            blk(RET_DIM, N_RET_HEADS), blk(RET_DIM, N_RET_HEADS), blk(N_RET_HEADS, RET_DIM),
            blk(N_RET_HEADS, RET_DIM), blk(N_RET_HEADS, RET_DIM, RET_DIM),
            const((RET_DIM, 1)), const((RET_DIM, 1)),
            pl.BlockSpec(memory_space=pltpu.SMEM),
            const((N_RET_HEADS, RET_DIM)),
        ],
        out_specs=[blk(N_RET_HEADS, RET_DIM), blk(N_RET_HEADS, RET_DIM, RET_DIM)],
        out_shape=[jax.ShapeDtypeStruct((nb, N_RET_HEADS, RET_DIM), F32),
                   jax.ShapeDtypeStruct((nb, N_RET_HEADS, RET_DIM, RET_DIM), F32)],
        compiler_params=_params("parallel"),
        name="ret_sample",
    )(qt, kt, v, gate, s0, cos2.reshape(RET_DIM, 1), sin2.reshape(RET_DIM, 1), gamma,
      rg.reshape(N_RET_HEADS, RET_DIM))


def _out_proj_kernel(h_ref, a_ref, r_ref, wa_ref, wr_ref, o_ref):
    o_ref[...] = h_ref[...] + _dot(a_ref[...], wa_ref[...]) + _dot(r_ref[...], wr_ref[...])


def _out_proj(h, o_attn, o_ret, w_bf16, tm):
    m = h.shape[0]
    return pl.pallas_call(
        _out_proj_kernel,
        grid=(m // tm,),
        in_specs=[
            pl.BlockSpec((tm, D_MODEL), lambda i: (i, 0)),
            pl.BlockSpec((tm, ATTN_WIDTH), lambda i: (i, 0)),
            pl.BlockSpec((tm, RET_WIDTH), lambda i: (i, 0)),
            pl.BlockSpec((ATTN_WIDTH, D_MODEL), lambda i: (0, 0)),
            pl.BlockSpec((RET_WIDTH, D_MODEL), lambda i: (1, 0)),
        ],
        out_specs=pl.BlockSpec((tm, D_MODEL), lambda i: (i, 0)),
        out_shape=jax.ShapeDtypeStruct((m, D_MODEL), F32),
        compiler_params=_params("parallel"),
        name="out_proj",
    )(h, o_attn, o_ret, w_bf16, w_bf16)


FF_TILE = 512


def _ffn_kernel(h_ref, g_ref, wg_ref, wu_ref, wd_ref, o_ref, f_ref):
    @pl.when(pl.program_id(1) == 0)
    def _():
        h = h_ref[...]
        f_ref[...] = _rms(h, g_ref[...]).astype(BF16)
        o_ref[...] = h

    f = f_ref[...]
    gate = _dot(f, wg_ref[...])
    act = (gate * jax.nn.sigmoid(gate) * _dot(f, wu_ref[...])).astype(BF16)
    o_ref[...] += _dot(act, wd_ref[...])


def _ffn(h, g, wg, wu, wd, tm):
    m = h.shape[0]
    return pl.pallas_call(
        _ffn_kernel,
        grid=(m // tm, D_FF // FF_TILE),
        in_specs=[
            pl.BlockSpec((tm, D_MODEL), lambda i, j: (i, 0)),
            pl.BlockSpec((1, D_MODEL), lambda i, j: (0, 0)),
            pl.BlockSpec((D_MODEL, FF_TILE), lambda i, j: (0, j)),
            pl.BlockSpec((D_MODEL, FF_TILE), lambda i, j: (0, j)),
            pl.BlockSpec((FF_TILE, D_MODEL), lambda i, j: (j, 0)),
        ],
        out_specs=pl.BlockSpec((tm, D_MODEL), lambda i, j: (i, 0)),
        out_shape=jax.ShapeDtypeStruct((m, D_MODEL), F32),
        scratch_shapes=[pltpu.VMEM((tm, D_MODEL), BF16)],
        compiler_params=_params("parallel", "arbitrary"),
        name="ffn",
    )(h, g, wg, wu, wd)


def _ple_kernel(h_ref, p_ref, g_ref, wp_ref, wg_ref, o_ref):
    h = h_ref[...]
    gate = jax.nn.sigmoid(_dot(_rms(h, g_ref[...]).astype(BF16), wg_ref[...]))
    o_ref[...] = h + _dot(p_ref[...].astype(BF16), wp_ref[...]) * gate


def _ple(h, p, g, w_ple, w_pg, tm):
    m = h.shape[0]
    return pl.pallas_call(
        _ple_kernel,
        grid=(m // tm,),
        in_specs=[
            pl.BlockSpec((tm, D_MODEL), lambda i: (i, 0)),
            pl.BlockSpec((tm, PLE_DIM), lambda i: (i, 0)),
            pl.BlockSpec((1, D_MODEL), lambda i: (0, 0)),
            pl.BlockSpec((PLE_DIM, D_MODEL), lambda i: (0, 0)),
            pl.BlockSpec((D_MODEL, D_MODEL), lambda i: (0, 0)),
        ],
        out_specs=pl.BlockSpec((tm, D_MODEL), lambda i: (i, 0)),
        out_shape=jax.ShapeDtypeStruct((m, D_MODEL), F32),
        compiler_params=_params("parallel"),
        name="ple",
    )(h, p, g, w_ple, w_pg)


def _dense_tail(h, o_attn, o_ret, p, w, tm):
    h = _out_proj(h, o_attn, o_ret, w["w_out"], tm)
    h = _ffn(h, w["ffn_norm_g"], w["w_gate"], w["w_up"], w["w_down"], min(2 * tm, h.shape[0]))
    return _ple(h, p, w["ple_norm_g"], w["w_ple"], w["w_ple_gate"], tm)


def _layer_prompt(x, p, w):
    batch, seq, _ = x.shape
    h = x.reshape(batch * seq, D_MODEL)
    z = _in_proj(h, w["attn_norm_g"], w["w_in"], 1024, BF16)
    o_attn, k_win, v_win = _attn_prompt(z, w["q_norm_g"], w["k_norm_g"], w["attn_sinks"], batch, seq)
    o_ret, state = _ret_prompt(z, w["ret_out_g"], batch, seq)
    y = _dense_tail(h, o_attn, o_ret, p.reshape(batch * seq, PLE_DIM), w, 512)
    return (y.reshape(batch, seq, D_MODEL),
            k_win.reshape(batch, BLK, N_KV_HEADS, HEAD_DIM), v_win.reshape(batch, BLK, N_KV_HEADS, HEAD_DIM), state)


def _layer_sample(x, p, k_buf, v_buf, s0, w):
    nb = x.shape[0]
    h = x.reshape(nb, D_MODEL)
    z = _in_proj(h, w["attn_norm_g"], w["w_in"], nb, F32)
    heads = lambda c, n: z[:, c:c + n * HEAD_DIM].reshape(nb, n, HEAD_DIM)
    o_attn, k_win, v_win = _attn_sample(
        heads(COL_AQ, N_ATTN_HEADS), heads(COL_AK, N_KV_HEADS), heads(COL_AV, N_KV_HEADS),
        k_buf.reshape(nb, WINDOW, KV_WIDTH), v_buf.reshape(nb, WINDOW, KV_WIDTH),
        w["q_norm_g"], w["k_norm_g"], w["attn_sinks"].reshape(N_ATTN_HEADS, 1))
    o_ret, state = _ret_sample(
        heads(COL_RQ, N_RET_HEADS).transpose(0, 2, 1), heads(COL_RK, N_RET_HEADS).transpose(0, 2, 1),
        heads(COL_RV, N_RET_HEADS), heads(COL_RG, N_RET_HEADS), s0, w["ret_out_g"])
    y = _dense_tail(h, o_attn.reshape(nb, ATTN_WIDTH).astype(BF16), o_ret.reshape(nb, RET_WIDTH).astype(BF16),
                    p.reshape(nb, PLE_DIM), w, nb)
    return (y.reshape(nb, 1, D_MODEL),
            k_win.reshape(nb, WINDOW, N_KV_HEADS, HEAD_DIM), v_win.reshape(nb, WINDOW, N_KV_HEADS, HEAD_DIM), state)


def kernel(x_prompt, x_sample, cache_k_win, cache_v_win, state_ret, p_prompt, p_sample,
           attn_norm_g, w_in, q_norm_g, k_norm_g, attn_sinks, ret_out_g, w_out,
           ffn_norm_g, w_gate, w_up, w_down, ple_norm_g, w_ple, w_ple_gate):
    assert w_in.shape[0] == 1 and x_sample.shape[1] == 1 and cache_k_win.shape[2] == WINDOW
    y_p, y_s = x_prompt, x_sample
    outs = [[] for _ in range(6)]
    for l in range(w_in.shape[0]):
        w = {
            "attn_norm_g": attn_norm_g[l].reshape(1, D_MODEL),
            "w_in": w_in[l].astype(BF16),
            "q_norm_g": q_norm_g[l].reshape(1, HEAD_DIM),
            "k_norm_g": k_norm_g[l].reshape(1, HEAD_DIM),
            "attn_sinks": attn_sinks[l],
            "ret_out_g": ret_out_g[l].reshape(1, RET_WIDTH),
            "w_out": w_out[l].astype(BF16),
            "ffn_norm_g": ffn_norm_g[l].reshape(1, D_MODEL),
            "w_gate": w_gate[l].astype(BF16),
            "w_up": w_up[l].astype(BF16),
            "w_down": w_down[l].astype(BF16),
            "ple_norm_g": ple_norm_g[l].reshape(1, D_MODEL),
            "w_ple": w_ple[l].astype(BF16),
            "w_ple_gate": w_ple_gate[l].astype(BF16),
        }
        y_p, kp, vp, sp = _layer_prompt(y_p, p_prompt[l], w)
        y_s, ks, vs, ss = _layer_sample(y_s, p_sample[l], cache_k_win[l], cache_v_win[l], state_ret[l], w)
        for acc, val in zip(outs, (kp, vp, sp, ks, vs, ss)):
            acc.append(val)
    kp, vp, sp, ks, vs, ss = (jnp.stack(o) for o in outs)
    return (y_p, y_s, kp, vp, sp, ks, vs, ss)
```

```python
import functools

import numpy as np
import jax
import jax.numpy as jnp
from jax import lax
from jax.experimental import pallas as pl
from jax.experimental.pallas import tpu as pltpu

D_MODEL = 2048
HEAD_DIM = 128
N_ATTN_HEADS = 8
N_KV_HEADS = 2
GQA_GROUP = N_ATTN_HEADS // N_KV_HEADS
ATTN_WIDTH = N_ATTN_HEADS * HEAD_DIM
KV_WIDTH = N_KV_HEADS * HEAD_DIM
WINDOW = 128
BLK = 128
N_RET_HEADS = 8
RET_DIM = 128
RET_WIDTH = N_RET_HEADS * RET_DIM
D_FF = 5632
IN_WIDTH = 5632
PLE_DIM = 256
PAST_LEN = 16384
ROPE_BASE = 10000.0
NORM_EPS = 1e-6
ATTN_SCALE = HEAD_DIM ** -0.5
RET_K_SCALE = RET_DIM ** -0.5

COL_RQ, COL_RK, COL_RV, COL_RG = 0, RET_WIDTH, 2 * RET_WIDTH, 3 * RET_WIDTH
COL_AQ = 4 * RET_WIDTH
COL_AK = COL_AQ + ATTN_WIDTH
COL_AV = COL_AK + KV_WIDTH
IN_TILE = 512
IN_TILES = IN_WIDTH // IN_TILE
IN_ROT = (ATTN_WIDTH + 2 * KV_WIDTH) // IN_TILE

VMEM_LIMIT = 56 * 1024 * 1024

F32 = jnp.float32
BF16 = jnp.bfloat16


def _params(*sem):
    return pltpu.CompilerParams(dimension_semantics=sem, vmem_limit_bytes=VMEM_LIMIT)


def _rms(x, g):
    return x * lax.rsqrt(jnp.mean(x * x, axis=-1, keepdims=True) + NORM_EPS) * g


def _dot(a, b):
    return jnp.dot(a, b, preferred_element_type=F32)


def _dot_nt(a, b):
    return lax.dot_general(a, b, (((1,), (1,)), ((), ())), preferred_element_type=F32)


def _dot_tn(a, b):
    return lax.dot_general(a, b, (((0,), (0,)), ((), ())), preferred_element_type=F32)


def _in_proj_kernel(x_ref, g_ref, w_ref, o_ref, a_ref):
    @pl.when(pl.program_id(1) == 0)
    def _():
        a_ref[...] = _rms(x_ref[...], g_ref[...]).astype(BF16)

    o_ref[...] = _dot(a_ref[...], w_ref[...]).astype(o_ref.dtype)


def _in_proj_cast_kernel(x_ref, g_ref, w_ref, wg_ref, wu_ref, wd_ref, wo_ref, wpg_ref, wple_ref,
                         o_ref, wg_o, wu_o, wd_o, wo_o, wpg_o, wple_o, a_ref):
    j = pl.program_id(1)
    _in_proj_kernel(x_ref, g_ref, w_ref, o_ref, a_ref)
    wg_o[...] = wg_ref[...].astype(BF16)
    wu_o[...] = wu_ref[...].astype(BF16)
    wd_o[...] = wd_ref[...].astype(BF16)

    @pl.when(j < CAST_SPLIT)
    def _():
        wo_o[...] = wo_ref[...].astype(BF16)
        wpg_o[...] = wpg_ref[...].astype(BF16)

    @pl.when(j == 0)
    def _():
        wple_o[...] = wple_ref[...].astype(BF16)


IN_ROWS = 1024
CAST_SPLIT = 2
FF_TILE = 512


def _in_w_spec():
    return pl.BlockSpec((None, D_MODEL, IN_TILE), lambda i, j: ((j + IN_ROT) % IN_TILES, 0, 0))


def _in_proj(x, g, w_tiles, tm, out_dtype):
    m = x.shape[0]
    return pl.pallas_call(
        _in_proj_kernel,
        grid=(m // tm, IN_TILES),
        in_specs=[
            pl.BlockSpec((tm, D_MODEL), lambda i, j: (i, 0)),
            pl.BlockSpec((1, D_MODEL), lambda i, j: (0, 0)),
            _in_w_spec(),
        ],
        out_specs=pl.BlockSpec((tm, IN_TILE), lambda i, j: (i, j)),
        out_shape=jax.ShapeDtypeStruct((m, IN_WIDTH), out_dtype),
        scratch_shapes=[pltpu.VMEM((tm, D_MODEL), BF16)],
        compiler_params=_params("parallel", "arbitrary"),
        name="in_proj",
    )(x, g, w_tiles)


def _in_proj_cast(x, g, w_tiles, w_gate, w_up, w_down, w_out, w_pg, w_ple):
    m = x.shape[0]
    ni = m // IN_ROWS
    steps = ni * IN_TILES
    assert m % IN_ROWS == 0 and FF_TILE == IN_TILE and D_FF == IN_WIDTH
    gu_rows, d_rows = D_MODEL // ni, D_FF // steps
    sq_rows, ple_rows = D_MODEL // (ni * CAST_SPLIT), PLE_DIM // ni
    assert gu_rows * ni == D_MODEL and d_rows * steps == D_FF and sq_rows % 16 == 0 and ple_rows % 16 == 0
    sq_idx = lambda i, j: (i * CAST_SPLIT + jnp.minimum(j, CAST_SPLIT - 1), 0)
    return pl.pallas_call(
        _in_proj_cast_kernel,
        grid=(ni, IN_TILES),
        in_specs=[
            pl.BlockSpec((IN_ROWS, D_MODEL), lambda i, j: (i, 0)),
            pl.BlockSpec((1, D_MODEL), lambda i, j: (0, 0)),
            _in_w_spec(),
            pl.BlockSpec((gu_rows, FF_TILE), lambda i, j: (i, j)),
            pl.BlockSpec((gu_rows, FF_TILE), lambda i, j: (i, j)),
            pl.BlockSpec((d_rows, D_MODEL), lambda i, j: (i * IN_TILES + j, 0)),
            pl.BlockSpec((sq_rows, D_MODEL), sq_idx),
            pl.BlockSpec((sq_rows, D_MODEL), sq_idx),
            pl.BlockSpec((ple_rows, D_MODEL), lambda i, j: (i, 0)),
        ],
        out_specs=[
            pl.BlockSpec((IN_ROWS, IN_TILE), lambda i, j: (i, j)),
            pl.BlockSpec((None, gu_rows, FF_TILE), lambda i, j: (j, i, 0)),
            pl.BlockSpec((None, gu_rows, FF_TILE), lambda i, j: (j, i, 0)),
            pl.BlockSpec((d_rows, D_MODEL), lambda i, j: (i * IN_TILES + j, 0)),
            pl.BlockSpec((sq_rows, D_MODEL), sq_idx),
            pl.BlockSpec((sq_rows, D_MODEL), sq_idx),
            pl.BlockSpec((ple_rows, D_MODEL), lambda i, j: (i, 0)),
        ],
        out_shape=[
            jax.ShapeDtypeStruct((m, IN_WIDTH), BF16),
            jax.ShapeDtypeStruct((D_FF // FF_TILE, D_MODEL, FF_TILE), BF16),
            jax.ShapeDtypeStruct((D_FF // FF_TILE, D_MODEL, FF_TILE), BF16),
            jax.ShapeDtypeStruct((D_FF, D_MODEL), BF16),
            jax.ShapeDtypeStruct((D_MODEL, D_MODEL), BF16),
            jax.ShapeDtypeStruct((D_MODEL, D_MODEL), BF16),
            jax.ShapeDtypeStruct((PLE_DIM, D_MODEL), BF16),
        ],
        scratch_shapes=[pltpu.VMEM((IN_ROWS, D_MODEL), BF16)],
        compiler_params=_params("arbitrary", "arbitrary"),
        name="in_proj_cast",
    )(x, g, w_tiles, w_gate, w_up, w_down, w_out, w_pg, w_ple)


def _attn_prompt_kernel(q_ref, kc_ref, vc_ref, kp_ref, vp_ref, qg_ref, kg_ref, sink_ref,
                        o_ref, kwin_ref, vwin_ref):
    n = pl.program_id(1)
    qg = qg_ref[...]
    kg = kg_ref[...]
    row = lax.broadcasted_iota(jnp.int32, (BLK, 2 * BLK), 0)
    col = lax.broadcasted_iota(jnp.int32, (BLK, 2 * BLK), 1)
    mask = ((col < BLK) & (col >= row) & (n > 0)) | ((col >= BLK) & ((col - BLK) <= row))

    kc = kc_ref[...].astype(F32)
    kp = kp_ref[...].astype(F32)
    kc_n = [_rms(kc[:, h * HEAD_DIM:(h + 1) * HEAD_DIM], kg) for h in range(N_KV_HEADS)]
    kp_n = [_rms(kp[:, h * HEAD_DIM:(h + 1) * HEAD_DIM], kg) for h in range(N_KV_HEADS)]
    vc = vc_ref[...]
    vp = vp_ref[...]
    for kh in range(N_KV_HEADS):
        k_cat = jnp.concatenate([kp_n[kh], kc_n[kh]], axis=0).astype(BF16)
        sl = slice(kh * HEAD_DIM, (kh + 1) * HEAD_DIM)
        v_cat = jnp.concatenate([vp[:, sl], vc[:, sl]], axis=0)
        for g in range(GQA_GROUP):
            h = kh * GQA_GROUP + g
            hs = slice(h * HEAD_DIM, (h + 1) * HEAD_DIM)
            qn = (_rms(q_ref[:, hs].astype(F32), qg) * ATTN_SCALE).astype(BF16)
            s = jnp.where(mask, _dot_nt(qn, k_cat), -jnp.inf)
            sink = sink_ref[h]
            m = jnp.maximum(jnp.max(s, axis=-1, keepdims=True), sink)
            e = jnp.exp(s - m)
            denom = jnp.sum(e, axis=-1, keepdims=True) + jnp.exp(sink - m)
            o_ref[:, hs] = (_dot(e.astype(BF16), v_cat) / denom).astype(o_ref.dtype)

    @pl.when(n == pl.num_programs(1) - 1)
    def _():
        kwin_ref[0] = jnp.concatenate(kc_n, axis=1)
        vwin_ref[0] = vc.astype(F32)


def _attn_prompt(z, qg, kg, sinks, batch, seq):
    nb = seq // BLK
    qb, kb, vb = COL_AQ // ATTN_WIDTH, COL_AK // KV_WIDTH, COL_AV // KV_WIDTH
    prev = lambda b, n: b * nb + jnp.maximum(n - 1, 0)
    win = jax.ShapeDtypeStruct((batch, BLK, KV_WIDTH), F32)
    return pl.pallas_call(
        _attn_prompt_kernel,
        grid=(batch, nb),
        in_specs=[
            pl.BlockSpec((BLK, ATTN_WIDTH), lambda b, n: (b * nb + n, qb)),
            pl.BlockSpec((BLK, KV_WIDTH), lambda b, n: (b * nb + n, kb)),
            pl.BlockSpec((BLK, KV_WIDTH), lambda b, n: (b * nb + n, vb)),
            pl.BlockSpec((BLK, KV_WIDTH), lambda b, n: (prev(b, n), kb)),
            pl.BlockSpec((BLK, KV_WIDTH), lambda b, n: (prev(b, n), vb)),
            pl.BlockSpec((1, HEAD_DIM), lambda b, n: (0, 0)),
            pl.BlockSpec((1, HEAD_DIM), lambda b, n: (0, 0)),
            pl.BlockSpec(memory_space=pltpu.SMEM),
        ],
        out_specs=[
            pl.BlockSpec((BLK, ATTN_WIDTH), lambda b, n: (b * nb + n, 0)),
            pl.BlockSpec((1, BLK, KV_WIDTH), lambda b, n: (b, 0, 0)),
            pl.BlockSpec((1, BLK, KV_WIDTH), lambda b, n: (b, 0, 0)),
        ],
        out_shape=[jax.ShapeDtypeStruct((batch * seq, ATTN_WIDTH), BF16), win, win],
        compiler_params=_params("arbitrary", "arbitrary"),
        name="attn_prompt",
    )(z, z, z, z, z, qg, kg, sinks)


def _rotary(x, cos2, sin2):
    return x * cos2 + pltpu.roll(x, RET_DIM // 2, 1) * sin2


def _ret_prompt_kernel(q_ref, k_ref, v_ref, gate_ref, cos_ref, sin_ref, intra_ref, qdec_ref, kdec_ref,
                       cdec_ref, rg_ref, o_ref, s_out_ref, s_ref):
    n = pl.program_id(1)

    @pl.when(n == 0)
    def _():
        s_ref[...] = jnp.zeros_like(s_ref)

    cos2 = cos_ref[...]
    sin2 = sin_ref[...]
    for h in range(N_RET_HEADS):
        hs = slice(h * RET_DIM, (h + 1) * RET_DIM)
        q = _rotary(q_ref[:, hs].astype(F32), cos2, sin2)
        k = _rotary(k_ref[:, hs].astype(F32), cos2, sin2) * RET_K_SCALE
        v = v_ref[:, hs]
        state = s_ref[h]
        sc = _dot_nt(q.astype(BF16), k.astype(BF16)) * intra_ref[h]
        o = _dot(sc.astype(BF16), v) + _dot((q * qdec_ref[:, h:h + 1]).astype(BF16), state.astype(BF16))
        s_ref[h] = state * cdec_ref[h] + _dot_tn((k * kdec_ref[:, h:h + 1]).astype(BF16), v)
        o = o * lax.rsqrt(jnp.mean(o * o, axis=-1, keepdims=True) + NORM_EPS)
        gate = gate_ref[:, hs].astype(F32)
        o_ref[:, hs] = (o * rg_ref[:, hs] * (gate * jax.nn.sigmoid(gate))).astype(o_ref.dtype)

    @pl.when(n == pl.num_programs(1) - 1)
    def _():
        s_out_ref[0] = s_ref[...]


def _ret_tables(chunk):
    h = jnp.arange(N_RET_HEADS, dtype=F32)
    lg = jnp.log1p(-jnp.exp2(-5.0 - h))
    idx = jnp.arange(chunk, dtype=F32)
    diff = idx[:, None] - idx[None, :]
    intra = jnp.where(diff >= 0, jnp.exp(lg[:, None, None] * jnp.maximum(diff, 0.0)), 0.0)
    q_dec = jnp.exp(lg[None, :] * (idx[:, None] + 1.0))
    k_dec = jnp.exp(lg[None, :] * (chunk - 1.0 - idx[:, None]))
    chunk_dec = jnp.exp(lg * chunk)
    return intra, q_dec, k_dec, chunk_dec


def _rope_tables(pos):
    half = RET_DIM // 2
    inv = ROPE_BASE ** (-jnp.arange(half, dtype=F32) / half)
    ang = pos.astype(F32)[:, None] * inv[None, :]
    cos, sin = jnp.cos(ang), jnp.sin(ang)
    return jnp.concatenate([cos, cos], axis=-1), jnp.concatenate([-sin, sin], axis=-1)


def _ret_prompt(z, rg, batch, seq):
    nc = seq // BLK
    cos2, sin2 = _rope_tables(jnp.arange(seq))
    intra, q_dec, k_dec, chunk_dec = _ret_tables(BLK)
    tok = lambda c: pl.BlockSpec((BLK, RET_WIDTH), lambda b, n: (b * nc + n, c))
    const = lambda shape: pl.BlockSpec(shape, lambda b, n: (0,) * len(shape))
    return pl.pallas_call(
        _ret_prompt_kernel,
        grid=(batch, nc),
        in_specs=[
            tok(COL_RQ // RET_WIDTH), tok(COL_RK // RET_WIDTH), tok(COL_RV // RET_WIDTH), tok(COL_RG // RET_WIDTH),
            pl.BlockSpec((BLK, RET_DIM), lambda b, n: (n, 0)),
            pl.BlockSpec((BLK, RET_DIM), lambda b, n: (n, 0)),
            const((N_RET_HEADS, BLK, BLK)),
            const((BLK, N_RET_HEADS)),
            const((BLK, N_RET_HEADS)),
            pl.BlockSpec(memory_space=pltpu.SMEM),
            const((1, RET_WIDTH)),
        ],
        out_specs=[
            pl.BlockSpec((BLK, RET_WIDTH), lambda b, n: (b * nc + n, 0)),
            pl.BlockSpec((1, N_RET_HEADS, RET_DIM, RET_DIM), lambda b, n: (b, 0, 0, 0)),
        ],
        out_shape=[jax.ShapeDtypeStruct((batch * seq, RET_WIDTH), BF16),
                   jax.ShapeDtypeStruct((batch, N_RET_HEADS, RET_DIM, RET_DIM), F32)],
        scratch_shapes=[pltpu.VMEM((N_RET_HEADS, RET_DIM, RET_DIM), F32)],
        compiler_params=_params("arbitrary", "arbitrary"),
        name="ret_prompt",
    )(z, z, z, z, cos2, sin2, intra, q_dec, k_dec, chunk_dec, rg)


SAMPLE_GROUP = 8


def _attn_sample_kernel(q_ref, kn_ref, vn_ref, kbuf_ref, vbuf_ref, qg_ref, kg_ref, sink_ref,
                        o_ref, kout_ref, vout_ref):
    qg = qg_ref[...]
    kg = kg_ref[...]
    sink = sink_ref[...]
    head = lax.broadcasted_iota(jnp.int32, (N_ATTN_HEADS, HEAD_DIM), 0)
    first = head < GQA_GROUP
    last_row = lax.broadcasted_iota(jnp.int32, (WINDOW, KV_WIDTH), 0) == WINDOW - 1
    for b in range(SAMPLE_GROUP):
        qn = _rms(q_ref[b], qg) * ATTN_SCALE
        kn = _rms(kn_ref[b], kg)
        vn = vn_ref[b]
        kbuf = kbuf_ref[b]
        vbuf = vbuf_ref[b]
        qb = qn.astype(BF16)
        s0 = _dot_nt(qb, kbuf[:, :HEAD_DIM].astype(BF16))
        s1 = _dot_nt(qb, kbuf[:, HEAD_DIM:].astype(BF16))
        s = jnp.where(first[:, :WINDOW], s0, s1)
        k_sel = jnp.where(first, kn[0:1], kn[1:2])
        v_sel = jnp.where(first, vn[0:1], vn[1:2])
        s_new = jnp.sum(qn * k_sel, axis=-1, keepdims=True)
        m = jnp.maximum(jnp.maximum(jnp.max(s, axis=-1, keepdims=True), s_new), sink)
        e = jnp.exp(s - m)
        e_new = jnp.exp(s_new - m)
        denom = jnp.sum(e, axis=-1, keepdims=True) + e_new + jnp.exp(sink - m)
        eb = e.astype(BF16)
        o0 = _dot(eb, vbuf[:, :HEAD_DIM].astype(BF16))
        o1 = _dot(eb, vbuf[:, HEAD_DIM:].astype(BF16))
        o_ref[b] = (jnp.where(first, o0, o1) + e_new * v_sel) / denom
        k_row = jnp.concatenate([kn[0:1], kn[1:2]], axis=1)
        v_row = jnp.concatenate([vn[0:1], vn[1:2]], axis=1)
        kout_ref[b] = jnp.where(last_row, k_row, pltpu.roll(kbuf, WINDOW - 1, 0))
        vout_ref[b] = jnp.where(last_row, v_row, pltpu.roll(vbuf, WINDOW - 1, 0))


def _attn_sample(q, kn, vn, kbuf, vbuf, qg, kg, sinks):
    nb = q.shape[0]
    blk = lambda *shape: pl.BlockSpec((SAMPLE_GROUP,) + shape, lambda i: (i,) + (0,) * len(shape))
    const = lambda shape: pl.BlockSpec(shape, lambda i: (0,) * len(shape))
    return pl.pallas_call(
        _attn_sample_kernel,
        grid=(nb // SAMPLE_GROUP,),
        in_specs=[
            blk(N_ATTN_HEADS, HEAD_DIM), blk(N_KV_HEADS, HEAD_DIM), blk(N_KV_HEADS, HEAD_DIM),
            blk(WINDOW, KV_WIDTH), blk(WINDOW, KV_WIDTH),
            const((1, HEAD_DIM)), const((1, HEAD_DIM)), const((N_ATTN_HEADS, 1)),
        ],
        out_specs=[blk(N_ATTN_HEADS, HEAD_DIM), blk(WINDOW, KV_WIDTH), blk(WINDOW, KV_WIDTH)],
        out_shape=[jax.ShapeDtypeStruct((nb, N_ATTN_HEADS, HEAD_DIM), F32),
                   jax.ShapeDtypeStruct((nb, WINDOW, KV_WIDTH), F32),
                   jax.ShapeDtypeStruct((nb, WINDOW, KV_WIDTH), F32)],
        compiler_params=_params("parallel"),
        name="attn_sample",
    )(q, kn, vn, kbuf, vbuf, qg, kg, sinks)


def _ret_sample_kernel(qt_ref, kt_ref, v_ref, gate_ref, s0_ref, cos_ref, sin_ref, gamma_ref, rg_ref,
                       o_ref, s_out_ref):
    cos2 = cos_ref[...]
    sin2 = sin_ref[...]
    rot = lambda x: x * cos2 + pltpu.roll(x, RET_DIM // 2, 0) * sin2
    for b in range(SAMPLE_GROUP):
        qt = rot(qt_ref[b])
        kt = rot(kt_ref[b]) * RET_K_SCALE
        v = v_ref[b]
        gate = gate_ref[b]
        rows = []
        for h in range(N_RET_HEADS):
            s_new = s0_ref[b, h] * gamma_ref[h] + kt[:, h:h + 1] * v[h:h + 1, :]
            s_out_ref[b, h] = s_new
            o = jnp.sum(qt[:, h:h + 1] * s_new, axis=0, keepdims=True)
            rows.append(o * lax.rsqrt(jnp.mean(o * o, axis=-1, keepdims=True) + NORM_EPS))
        o = jnp.concatenate(rows, axis=0)
        o_ref[b] = o * rg_ref[...] * (gate * jax.nn.sigmoid(gate))


def _ret_sample(qt, kt, v, gate, s0, rg):
    nb = qt.shape[0]
    cos2, sin2 = _rope_tables(PAST_LEN + jnp.arange(1))
    h = jnp.arange(N_RET_HEADS, dtype=F32)
    gamma = jnp.exp(jnp.log1p(-jnp.exp2(-5.0 - h)))
    blk = lambda *shape: pl.BlockSpec((SAMPLE_GROUP,) + shape, lambda i: (i,) + (0,) * len(shape))
    const = lambda shape: pl.BlockSpec(shape, lambda i: (0,) * len(shape))
    return pl.pallas_call(
        _ret_sample_kernel,
        grid=(nb // SAMPLE_GROUP,),
        in_specs=[
            blk(RET_DIM, N_RET_HEADS), blk(RET_DIM, N_RET_HEADS), blk(N_RET_HEADS, RET_DIM),
            blk(N_RET_HEADS, RET_DIM), blk(N_RET_HEADS, RET_DIM, RET_DIM),
            const((RET_DIM, 1)), const((RET_DIM, 1)),
            pl.BlockSpec(memory_space=pltpu.SMEM),
            const((N_RET_HEADS, RET_DIM)),
        ],
        out_specs=[blk(N_RET_HEADS, RET_DIM), blk(N_RET_HEADS, RET_DIM, RET_DIM)],
        out_shape=[jax.ShapeDtypeStruct((nb, N_RET_HEADS, RET_DIM), F32),
                   jax.ShapeDtypeStruct((nb, N_RET_HEADS, RET_DIM, RET_DIM), F32)],
        compiler_params=_params("parallel"),
        name="ret_sample",
    )(qt, kt, v, gate, s0, cos2.reshape(RET_DIM, 1), sin2.reshape(RET_DIM, 1), gamma,
      rg.reshape(N_RET_HEADS, RET_DIM))


def _out_proj_kernel(h_ref, a_ref, r_ref, wa_ref, wr_ref, o_ref):
    o_ref[...] = h_ref[...] + _dot(a_ref[...], wa_ref[...]) + _dot(r_ref[...], wr_ref[...])


def _out_proj(h, o_attn, o_ret, w_bf16, tm):
    m = h.shape[0]
    return pl.pallas_call(
        _out_proj_kernel,
        grid=(m // tm,),
        in_specs=[
            pl.BlockSpec((tm, D_MODEL), lambda i: (i, 0)),
            pl.BlockSpec((tm, ATTN_WIDTH), lambda i: (i, 0)),
            pl.BlockSpec((tm, RET_WIDTH), lambda i: (i, 0)),
            pl.BlockSpec((ATTN_WIDTH, D_MODEL), lambda i: (0, 0)),
            pl.BlockSpec((RET_WIDTH, D_MODEL), lambda i: (1, 0)),
        ],
        out_specs=pl.BlockSpec((tm, D_MODEL), lambda i: (i, 0)),
        out_shape=jax.ShapeDtypeStruct((m, D_MODEL), F32),
        compiler_params=_params("parallel"),
        name="out_proj",
    )(h, o_attn, o_ret, w_bf16, w_bf16)


def _ffn_kernel(h_ref, g_ref, wg_ref, wu_ref, wd_ref, o_ref, f_ref):
    @pl.when(pl.program_id(1) == 0)
    def _():
        h = h_ref[...]
        f_ref[...] = _rms(h, g_ref[...]).astype(BF16)
        o_ref[...] = h

    f = f_ref[...]
    gate = _dot(f, wg_ref[...])
    act = (gate * jax.nn.sigmoid(gate) * _dot(f, wu_ref[...])).astype(BF16)
    o_ref[...] += _dot(act, wd_ref[...])


def _ffn(h, g, wg, wu, wd, tm):
    m = h.shape[0]
    return pl.pallas_call(
        _ffn_kernel,
        grid=(m // tm, D_FF // FF_TILE),
        in_specs=[
            pl.BlockSpec((tm, D_MODEL), lambda i, j: (i, 0)),
            pl.BlockSpec((1, D_MODEL), lambda i, j: (0, 0)),
            pl.BlockSpec((None, D_MODEL, FF_TILE), lambda i, j: (j, 0, 0)),
            pl.BlockSpec((None, D_MODEL, FF_TILE), lambda i, j: (j, 0, 0)),
            pl.BlockSpec((FF_TILE, D_MODEL), lambda i, j: (j, 0)),
        ],
        out_specs=pl.BlockSpec((tm, D_MODEL), lambda i, j: (i, 0)),
        out_shape=jax.ShapeDtypeStruct((m, D_MODEL), F32),
        scratch_shapes=[pltpu.VMEM((tm, D_MODEL), BF16)],
        compiler_params=_params("parallel", "arbitrary"),
        name="ffn",
    )(h, g, wg, wu, wd)


def _ple_kernel(h_ref, p_ref, g_ref, wp_ref, wg_ref, o_ref):
    h = h_ref[...]
    gate = jax.nn.sigmoid(_dot(_rms(h, g_ref[...]).astype(BF16), wg_ref[...]))
    o_ref[...] = h + _dot(p_ref[...].astype(BF16), wp_ref[...]) * gate


def _ple(h, p, g, w_ple, w_pg, tm):
    m = h.shape[0]
    return pl.pallas_call(
        _ple_kernel,
        grid=(m // tm,),
        in_specs=[
            pl.BlockSpec((tm, D_MODEL), lambda i: (i, 0)),
            pl.BlockSpec((tm, PLE_DIM), lambda i: (i, 0)),
            pl.BlockSpec((1, D_MODEL), lambda i: (0, 0)),
            pl.BlockSpec((PLE_DIM, D_MODEL), lambda i: (0, 0)),
            pl.BlockSpec((D_MODEL, D_MODEL), lambda i: (0, 0)),
        ],
        out_specs=pl.BlockSpec((tm, D_MODEL), lambda i: (i, 0)),
        out_shape=jax.ShapeDtypeStruct((m, D_MODEL), F32),
        compiler_params=_params("parallel"),
        name="ple",
    )(h, p, g, w_ple, w_pg)


def _dense_tail(h, o_attn, o_ret, p, w, tm):
    h = _out_proj(h, o_attn, o_ret, w["w_out"], tm)
    h = _ffn(h, w["ffn_norm_g"], w["w_gate"], w["w_up"], w["w_down"], min(2 * tm, h.shape[0]))
    return _ple(h, p, w["ple_norm_g"], w["w_ple"], w["w_ple_gate"], tm)


def _layer_prompt(x, p, w):
    batch, seq, _ = x.shape
    h = x.reshape(batch * seq, D_MODEL)
    z, *cast = _in_proj_cast(h, w["attn_norm_g"], w["w_in"], w["w_gate"], w["w_up"], w["w_down"],
                             w["w_out"], w["w_ple_gate"], w["w_ple"])
    w = dict(w, **dict(zip(("w_gate", "w_up", "w_down", "w_out", "w_ple_gate", "w_ple"), cast)))
    o_attn, k_win, v_win = _attn_prompt(z, w["q_norm_g"], w["k_norm_g"], w["attn_sinks"], batch, seq)
    o_ret, state = _ret_prompt(z, w["ret_out_g"], batch, seq)
    y = _dense_tail(h, o_attn, o_ret, p.reshape(batch * seq, PLE_DIM), w, 512)
    outs = (y.reshape(batch, seq, D_MODEL),
            k_win.reshape(batch, BLK, N_KV_HEADS, HEAD_DIM), v_win.reshape(batch, BLK, N_KV_HEADS, HEAD_DIM), state)
    return outs, w


def _layer_sample(x, p, k_buf, v_buf, s0, w):
    nb = x.shape[0]
    h = x.reshape(nb, D_MODEL)
    z = _in_proj(h, w["attn_norm_g"], w["w_in"], nb, F32)
    heads = lambda c, n: z[:, c:c + n * HEAD_DIM].reshape(nb, n, HEAD_DIM)
    o_attn, k_win, v_win = _attn_sample(
        heads(COL_AQ, N_ATTN_HEADS), heads(COL_AK, N_KV_HEADS), heads(COL_AV, N_KV_HEADS),
        k_buf.reshape(nb, WINDOW, KV_WIDTH), v_buf.reshape(nb, WINDOW, KV_WIDTH),
        w["q_norm_g"], w["k_norm_g"], w["attn_sinks"].reshape(N_ATTN_HEADS, 1))
    o_ret, state = _ret_sample(
        heads(COL_RQ, N_RET_HEADS).transpose(0, 2, 1), heads(COL_RK, N_RET_HEADS).transpose(0, 2, 1),
        heads(COL_RV, N_RET_HEADS), heads(COL_RG, N_RET_HEADS), s0, w["ret_out_g"])
    y = _dense_tail(h, o_attn.reshape(nb, ATTN_WIDTH).astype(BF16), o_ret.reshape(nb, RET_WIDTH).astype(BF16),
                    p.reshape(nb, PLE_DIM), w, nb)
    return (y.reshape(nb, 1, D_MODEL),
            k_win.reshape(nb, WINDOW, N_KV_HEADS, HEAD_DIM), v_win.reshape(nb, WINDOW, N_KV_HEADS, HEAD_DIM), state)


def kernel(x_prompt, x_sample, cache_k_win, cache_v_win, state_ret, p_prompt, p_sample,
           attn_norm_g, w_in, q_norm_g, k_norm_g, attn_sinks, ret_out_g, w_out,
           ffn_norm_g, w_gate, w_up, w_down, ple_norm_g, w_ple, w_ple_gate):
    assert w_in.shape[0] == 1 and x_sample.shape[1] == 1 and cache_k_win.shape[2] == WINDOW
    y_p, y_s = x_prompt, x_sample
    outs = [[] for _ in range(6)]
    for l in range(w_in.shape[0]):
        w = {
            "attn_norm_g": attn_norm_g[l].reshape(1, D_MODEL),
            "w_in": w_in[l].reshape(D_MODEL, IN_TILES, IN_TILE).transpose(1, 0, 2).astype(BF16),
            "q_norm_g": q_norm_g[l].reshape(1, HEAD_DIM),
            "k_norm_g": k_norm_g[l].reshape(1, HEAD_DIM),
            "attn_sinks": attn_sinks[l],
            "ret_out_g": ret_out_g[l].reshape(1, RET_WIDTH),
            "w_out": w_out[l],
            "ffn_norm_g": ffn_norm_g[l].reshape(1, D_MODEL),
            "w_gate": w_gate[l],
            "w_up": w_up[l],
            "w_down": w_down[l],
            "ple_norm_g": ple_norm_g[l].reshape(1, D_MODEL),
            "w_ple": w_ple[l],
            "w_ple_gate": w_ple_gate[l],
        }
        (y_p, kp, vp, sp), w = _layer_prompt(y_p, p_prompt[l], w)
        y_s, ks, vs, ss = _layer_sample(y_s, p_sample[l], cache_k_win[l], cache_v_win[l], state_ret[l], w)
        for acc, val in zip(outs, (kp, vp, sp, ks, vs, ss)):
            acc.append(val)
    kp, vp, sp, ks, vs, ss = (jnp.stack(o) for o in outs)
    return (y_p, y_s, kp, vp, sp, ks, vs, ss)
```

```python
import functools

import numpy as np
import jax
import jax.numpy as jnp
from jax import lax
from jax.experimental import pallas as pl
from jax.experimental.pallas import tpu as pltpu

D_MODEL = 2048
HEAD_DIM = 128
N_ATTN_HEADS = 8
N_KV_HEADS = 2
GQA_GROUP = N_ATTN_HEADS // N_KV_HEADS
ATTN_WIDTH = N_ATTN_HEADS * HEAD_DIM
KV_WIDTH = N_KV_HEADS * HEAD_DIM
WINDOW = 128
BLK = 128
N_RET_HEADS = 8
RET_DIM = 128
RET_WIDTH = N_RET_HEADS * RET_DIM
D_FF = 5632
IN_WIDTH = 5632
PLE_DIM = 256
PAST_LEN = 16384
ROPE_BASE = 10000.0
NORM_EPS = 1e-6
ATTN_SCALE = HEAD_DIM ** -0.5
RET_K_SCALE = RET_DIM ** -0.5

COL_RQ, COL_RK, COL_RV, COL_RG = 0, RET_WIDTH, 2 * RET_WIDTH, 3 * RET_WIDTH
COL_AQ = 4 * RET_WIDTH
COL_AK = COL_AQ + ATTN_WIDTH
COL_AV = COL_AK + KV_WIDTH
IN_TILE = 512
IN_TILES = IN_WIDTH // IN_TILE
IN_ROT = (ATTN_WIDTH + 2 * KV_WIDTH) // IN_TILE

VMEM_LIMIT = 56 * 1024 * 1024

F32 = jnp.float32
BF16 = jnp.bfloat16


def _params(*sem):
    return pltpu.CompilerParams(dimension_semantics=sem, vmem_limit_bytes=VMEM_LIMIT)


def _rms(x, g):
    return x * lax.rsqrt(jnp.mean(x * x, axis=-1, keepdims=True) + NORM_EPS) * g


def _dot(a, b):
    return jnp.dot(a, b, preferred_element_type=F32)


def _dot_nt(a, b):
    return lax.dot_general(a, b, (((1,), (1,)), ((), ())), preferred_element_type=F32)


def _dot_tn(a, b):
    return lax.dot_general(a, b, (((0,), (0,)), ((), ())), preferred_element_type=F32)


def _in_proj_kernel(x_hbm, xs_ref, g_ref, w_ref, wg_ref, wu_ref, wd_ref, wo_ref, wpg_ref, wple_ref,
                    z_ref, zs_ref, wg_o, wu_o, wd_o, wo_o, wpg_o, wple_o,
                    xbuf, xsem, a_ref, as_ref, wres):
    i, j = pl.program_id(0), pl.program_id(1)

    def x_copy(tile):
        return pltpu.make_async_copy(x_hbm.at[pl.ds(tile * IN_ROWS, IN_ROWS)], xbuf, xsem)

    @pl.when((i == 0) & (j == 0))
    def _():
        x_copy(0).start()
        as_ref[...] = _rms(xs_ref[...], g_ref[...]).astype(BF16)

    @pl.when(j == 0)
    def _():
        x_copy(i).wait()
        a_ref[...] = _rms(xbuf[...], g_ref[...]).astype(BF16)

        @pl.when(i + 1 < pl.num_programs(0))
        def _():
            x_copy(i + 1).start()

    @pl.when(i == 0)
    def _():
        wres[j] = w_ref[...].astype(BF16)
        zs_ref[...] = _dot(as_ref[...], wres[j])

    z_ref[...] = _dot(a_ref[...], wres[j]).astype(z_ref.dtype)
    wg_o[...] = wg_ref[...].astype(BF16)
    wu_o[...] = wu_ref[...].astype(BF16)
    wd_o[...] = wd_ref[...].astype(BF16)

    @pl.when(j < CAST_SPLIT)
    def _():
        wo_o[...] = wo_ref[...].astype(BF16)
        wpg_o[...] = wpg_ref[...].astype(BF16)

    @pl.when(j == 0)
    def _():
        wple_o[...] = wple_ref[...].astype(BF16)


IN_ROWS = 1024
CAST_SPLIT = 8
FF_TILE = 512


def _in_proj(x, xs, g, w_in, w_gate, w_up, w_down, w_out, w_pg, w_ple):
    m = x.shape[0]
    ni = m // IN_ROWS
    steps = ni * IN_TILES
    assert m % IN_ROWS == 0 and FF_TILE == IN_TILE and D_FF == IN_WIDTH
    gu_rows, d_rows = D_MODEL // ni, D_FF // steps
    sq_rows, ple_rows = D_MODEL // (ni * CAST_SPLIT), PLE_DIM // ni
    assert gu_rows * ni == D_MODEL and d_rows * steps == D_FF and sq_rows % 16 == 0 and ple_rows % 16 == 0
    sq_idx = lambda i, j: (i * CAST_SPLIT + jnp.minimum(j, CAST_SPLIT - 1), 0)
    first = lambda i, j: jnp.where(i == 0, j, IN_TILES - 1)
    return pl.pallas_call(
        _in_proj_kernel,
        grid=(ni, IN_TILES),
        in_specs=[
            pl.BlockSpec(memory_space=pl.ANY),
            pl.BlockSpec(xs.shape, lambda i, j: (0, 0)),
            pl.BlockSpec((1, D_MODEL), lambda i, j: (0, 0)),
            pl.BlockSpec((D_MODEL, IN_TILE), lambda i, j: (0, (first(i, j) + IN_ROT) % IN_TILES)),
            pl.BlockSpec((gu_rows, FF_TILE), lambda i, j: (i, j)),
            pl.BlockSpec((gu_rows, FF_TILE), lambda i, j: (i, j)),
            pl.BlockSpec((d_rows, D_MODEL), lambda i, j: (i * IN_TILES + j, 0)),
            pl.BlockSpec((sq_rows, D_MODEL), sq_idx),
            pl.BlockSpec((sq_rows, D_MODEL), sq_idx),
            pl.BlockSpec((ple_rows, D_MODEL), lambda i, j: (i, 0)),
        ],
        out_specs=[
            pl.BlockSpec((IN_ROWS, IN_TILE), lambda i, j: (i, j)),
            pl.BlockSpec((xs.shape[0], IN_TILE), lambda i, j: (0, first(i, j))),
            pl.BlockSpec((None, gu_rows, FF_TILE), lambda i, j: (j, i, 0)),
            pl.BlockSpec((None, gu_rows, FF_TILE), lambda i, j: (j, i, 0)),
            pl.BlockSpec((d_rows, D_MODEL), lambda i, j: (i * IN_TILES + j, 0)),
            pl.BlockSpec((sq_rows, D_MODEL), sq_idx),
            pl.BlockSpec((sq_rows, D_MODEL), sq_idx),
            pl.BlockSpec((ple_rows, D_MODEL), lambda i, j: (i, 0)),
        ],
        out_shape=[
            jax.ShapeDtypeStruct((m, IN_WIDTH), BF16),
            jax.ShapeDtypeStruct((xs.shape[0], IN_WIDTH), F32),
            jax.ShapeDtypeStruct((D_FF // FF_TILE, D_MODEL, FF_TILE), BF16),
            jax.ShapeDtypeStruct((D_FF // FF_TILE, D_MODEL, FF_TILE), BF16),
            jax.ShapeDtypeStruct((D_FF, D_MODEL), BF16),
            jax.ShapeDtypeStruct((D_MODEL, D_MODEL), BF16),
            jax.ShapeDtypeStruct((D_MODEL, D_MODEL), BF16),
            jax.ShapeDtypeStruct((PLE_DIM, D_MODEL), BF16),
        ],
        scratch_shapes=[
            pltpu.VMEM((IN_ROWS, D_MODEL), F32),
            pltpu.SemaphoreType.DMA(()),
            pltpu.VMEM((IN_ROWS, D_MODEL), BF16),
            pltpu.VMEM((xs.shape[0], D_MODEL), BF16),
            pltpu.VMEM((IN_TILES, D_MODEL, IN_TILE), BF16),
        ],
        compiler_params=_params("arbitrary", "arbitrary"),
        name="in_proj",
    )(x, xs, g, w_in, w_gate, w_up, w_down, w_out, w_pg, w_ple)


def _attn_prompt_kernel(q_ref, kc_ref, vc_ref, kp_ref, vp_ref, qg_ref, kg_ref, sink_ref,
                        o_ref, kwin_ref, vwin_ref):
    n = pl.program_id(1)
    qg = qg_ref[...]
    kg = kg_ref[...]
    row = lax.broadcasted_iota(jnp.int32, (BLK, 2 * BLK), 0)
    col = lax.broadcasted_iota(jnp.int32, (BLK, 2 * BLK), 1)
    mask = ((col < BLK) & (col >= row) & (n > 0)) | ((col >= BLK) & ((col - BLK) <= row))

    kc = kc_ref[...].astype(F32)
    kp = kp_ref[...].astype(F32)
    kc_n = [_rms(kc[:, h * HEAD_DIM:(h + 1) * HEAD_DIM], kg) for h in range(N_KV_HEADS)]
    kp_n = [_rms(kp[:, h * HEAD_DIM:(h + 1) * HEAD_DIM], kg) for h in range(N_KV_HEADS)]
    vc = vc_ref[...]
    vp = vp_ref[...]
    for kh in range(N_KV_HEADS):
        k_cat = jnp.concatenate([kp_n[kh], kc_n[kh]], axis=0).astype(BF16)
        sl = slice(kh * HEAD_DIM, (kh + 1) * HEAD_DIM)
        v_cat = jnp.concatenate([vp[:, sl], vc[:, sl]], axis=0)
        for g in range(GQA_GROUP):
            h = kh * GQA_GROUP + g
            hs = slice(h * HEAD_DIM, (h + 1) * HEAD_DIM)
            qn = (_rms(q_ref[:, hs].astype(F32), qg) * ATTN_SCALE).astype(BF16)
            s = jnp.where(mask, _dot_nt(qn, k_cat), -jnp.inf)
            sink = sink_ref[h]
            m = jnp.maximum(jnp.max(s, axis=-1, keepdims=True), sink)
            e = jnp.exp(s - m)
            denom = jnp.sum(e, axis=-1, keepdims=True) + jnp.exp(sink - m)
            o_ref[:, hs] = (_dot(e.astype(BF16), v_cat) / denom).astype(o_ref.dtype)

    @pl.when(n == pl.num_programs(1) - 1)
    def _():
        kwin_ref[0] = jnp.concatenate(kc_n, axis=1)
        vwin_ref[0] = vc.astype(F32)


def _attn_prompt(z, qg, kg, sinks, batch, seq):
    nb = seq // BLK
    qb, kb, vb = COL_AQ // ATTN_WIDTH, COL_AK // KV_WIDTH, COL_AV // KV_WIDTH
    prev = lambda b, n: b * nb + jnp.maximum(n - 1, 0)
    win = jax.ShapeDtypeStruct((batch, BLK, KV_WIDTH), F32)
    return pl.pallas_call(
        _attn_prompt_kernel,
        grid=(batch, nb),
        in_specs=[
            pl.BlockSpec((BLK, ATTN_WIDTH), lambda b, n: (b * nb + n, qb)),
            pl.BlockSpec((BLK, KV_WIDTH), lambda b, n: (b * nb + n, kb)),
            pl.BlockSpec((BLK, KV_WIDTH), lambda b, n: (b * nb + n, vb)),
            pl.BlockSpec((BLK, KV_WIDTH), lambda b, n: (prev(b, n), kb)),
            pl.BlockSpec((BLK, KV_WIDTH), lambda b, n: (prev(b, n), vb)),
            pl.BlockSpec((1, HEAD_DIM), lambda b, n: (0, 0)),
            pl.BlockSpec((1, HEAD_DIM), lambda b, n: (0, 0)),
            pl.BlockSpec(memory_space=pltpu.SMEM),
        ],
        out_specs=[
            pl.BlockSpec((BLK, ATTN_WIDTH), lambda b, n: (b * nb + n, 0)),
            pl.BlockSpec((1, BLK, KV_WIDTH), lambda b, n: (b, 0, 0)),
            pl.BlockSpec((1, BLK, KV_WIDTH), lambda b, n: (b, 0, 0)),
        ],
        out_shape=[jax.ShapeDtypeStruct((batch * seq, ATTN_WIDTH), BF16), win, win],
        compiler_params=_params("arbitrary", "arbitrary"),
        name="attn_prompt",
    )(z, z, z, z, z, qg, kg, sinks)


def _rotary(x, cos2, sin2):
    return x * cos2 + pltpu.roll(x, RET_DIM // 2, 1) * sin2


def _ret_prompt_kernel(q_ref, k_ref, v_ref, gate_ref, cos_ref, sin_ref, intra_ref, qdec_ref, kdec_ref,
                       cdec_ref, rg_ref, o_ref, s_out_ref, s_ref):
    n = pl.program_id(1)

    @pl.when(n == 0)
    def _():
        s_ref[...] = jnp.zeros_like(s_ref)

    cos2 = cos_ref[...]
    sin2 = sin_ref[...]
    for h in range(N_RET_HEADS):
        hs = slice(h * RET_DIM, (h + 1) * RET_DIM)
        q = _rotary(q_ref[:, hs].astype(F32), cos2, sin2)
        k = _rotary(k_ref[:, hs].astype(F32), cos2, sin2) * RET_K_SCALE
        v = v_ref[:, hs]
        state = s_ref[h]
        sc = _dot_nt(q.astype(BF16), k.astype(BF16)) * intra_ref[h]
        o = _dot(sc.astype(BF16), v) + _dot((q * qdec_ref[:, h:h + 1]).astype(BF16), state.astype(BF16))
        s_ref[h] = state * cdec_ref[h] + _dot_tn((k * kdec_ref[:, h:h + 1]).astype(BF16), v)
        o = o * lax.rsqrt(jnp.mean(o * o, axis=-1, keepdims=True) + NORM_EPS)
        gate = gate_ref[:, hs].astype(F32)
        o_ref[:, hs] = (o * rg_ref[:, hs] * (gate * jax.nn.sigmoid(gate))).astype(o_ref.dtype)

    @pl.when(n == pl.num_programs(1) - 1)
    def _():
        s_out_ref[0] = s_ref[...]


def _ret_tables(chunk):
    h = jnp.arange(N_RET_HEADS, dtype=F32)
    lg = jnp.log1p(-jnp.exp2(-5.0 - h))
    idx = jnp.arange(chunk, dtype=F32)
    diff = idx[:, None] - idx[None, :]
    intra = jnp.where(diff >= 0, jnp.exp(lg[:, None, None] * jnp.maximum(diff, 0.0)), 0.0)
    q_dec = jnp.exp(lg[None, :] * (idx[:, None] + 1.0))
    k_dec = jnp.exp(lg[None, :] * (chunk - 1.0 - idx[:, None]))
    chunk_dec = jnp.exp(lg * chunk)
    return intra, q_dec, k_dec, chunk_dec


def _rope_tables(pos):
    half = RET_DIM // 2
    inv = ROPE_BASE ** (-jnp.arange(half, dtype=F32) / half)
    ang = pos.astype(F32)[:, None] * inv[None, :]
    cos, sin = jnp.cos(ang), jnp.sin(ang)
    return jnp.concatenate([cos, cos], axis=-1), jnp.concatenate([-sin, sin], axis=-1)


def _ret_prompt(z, rg, batch, seq):
    nc = seq // BLK
    cos2, sin2 = _rope_tables(jnp.arange(seq))
    intra, q_dec, k_dec, chunk_dec = _ret_tables(BLK)
    tok = lambda c: pl.BlockSpec((BLK, RET_WIDTH), lambda b, n: (b * nc + n, c))
    const = lambda shape: pl.BlockSpec(shape, lambda b, n: (0,) * len(shape))
    return pl.pallas_call(
        _ret_prompt_kernel,
        grid=(batch, nc),
        in_specs=[
            tok(COL_RQ // RET_WIDTH), tok(COL_RK // RET_WIDTH), tok(COL_RV // RET_WIDTH), tok(COL_RG // RET_WIDTH),
            pl.BlockSpec((BLK, RET_DIM), lambda b, n: (n, 0)),
            pl.BlockSpec((BLK, RET_DIM), lambda b, n: (n, 0)),
            const((N_RET_HEADS, BLK, BLK)),
            const((BLK, N_RET_HEADS)),
            const((BLK, N_RET_HEADS)),
            pl.BlockSpec(memory_space=pltpu.SMEM),
            const((1, RET_WIDTH)),
        ],
        out_specs=[
            pl.BlockSpec((BLK, RET_WIDTH), lambda b, n: (b * nc + n, 0)),
            pl.BlockSpec((1, N_RET_HEADS, RET_DIM, RET_DIM), lambda b, n: (b, 0, 0, 0)),
        ],
        out_shape=[jax.ShapeDtypeStruct((batch * seq, RET_WIDTH), BF16),
                   jax.ShapeDtypeStruct((batch, N_RET_HEADS, RET_DIM, RET_DIM), F32)],
        scratch_shapes=[pltpu.VMEM((N_RET_HEADS, RET_DIM, RET_DIM), F32)],
        compiler_params=_params("arbitrary", "arbitrary"),
        name="ret_prompt",
    )(z, z, z, z, cos2, sin2, intra, q_dec, k_dec, chunk_dec, rg)


SAMPLE_GROUP = 8


def _attn_sample_kernel(q_ref, kn_ref, vn_ref, kbuf_ref, vbuf_ref, qg_ref, kg_ref, sink_ref,
                        o_ref, kout_ref, vout_ref):
    qg = qg_ref[...]
    kg = kg_ref[...]
    sink = sink_ref[...]
    head = lax.broadcasted_iota(jnp.int32, (N_ATTN_HEADS, HEAD_DIM), 0)
    first = head < GQA_GROUP
    last_row = lax.broadcasted_iota(jnp.int32, (WINDOW, KV_WIDTH), 0) == WINDOW - 1
    for b in range(SAMPLE_GROUP):
        qn = _rms(q_ref[b], qg) * ATTN_SCALE
        kn = _rms(kn_ref[b], kg)
        vn = vn_ref[b]
        kbuf = kbuf_ref[b]
        vbuf = vbuf_ref[b]
        qb = qn.astype(BF16)
        s0 = _dot_nt(qb, kbuf[:, :HEAD_DIM].astype(BF16))
        s1 = _dot_nt(qb, kbuf[:, HEAD_DIM:].astype(BF16))
        s = jnp.where(first[:, :WINDOW], s0, s1)
        k_sel = jnp.where(first, kn[0:1], kn[1:2])
        v_sel = jnp.where(first, vn[0:1], vn[1:2])
        s_new = jnp.sum(qn * k_sel, axis=-1, keepdims=True)
        m = jnp.maximum(jnp.maximum(jnp.max(s, axis=-1, keepdims=True), s_new), sink)
        e = jnp.exp(s - m)
        e_new = jnp.exp(s_new - m)
        denom = jnp.sum(e, axis=-1, keepdims=True) + e_new + jnp.exp(sink - m)
        eb = e.astype(BF16)
        o0 = _dot(eb, vbuf[:, :HEAD_DIM].astype(BF16))
        o1 = _dot(eb, vbuf[:, HEAD_DIM:].astype(BF16))
        o_ref[b] = (jnp.where(first, o0, o1) + e_new * v_sel) / denom
        k_row = jnp.concatenate([kn[0:1], kn[1:2]], axis=1)
        v_row = jnp.concatenate([vn[0:1], vn[1:2]], axis=1)
        kout_ref[b] = jnp.where(last_row, k_row, pltpu.roll(kbuf, WINDOW - 1, 0))
        vout_ref[b] = jnp.where(last_row, v_row, pltpu.roll(vbuf, WINDOW - 1, 0))


def _attn_sample(q, kn, vn, kbuf, vbuf, qg, kg, sinks):
    nb = q.shape[0]
    blk = lambda *shape: pl.BlockSpec((SAMPLE_GROUP,) + shape, lambda i: (i,) + (0,) * len(shape))
    const = lambda shape: pl.BlockSpec(shape, lambda i: (0,) * len(shape))
    return pl.pallas_call(
        _attn_sample_kernel,
        grid=(nb // SAMPLE_GROUP,),
        in_specs=[
            blk(N_ATTN_HEADS, HEAD_DIM), blk(N_KV_HEADS, HEAD_DIM), blk(N_KV_HEADS, HEAD_DIM),
            blk(WINDOW, KV_WIDTH), blk(WINDOW, KV_WIDTH),
            const((1, HEAD_DIM)), const((1, HEAD_DIM)), const((N_ATTN_HEADS, 1)),
        ],
        out_specs=[blk(N_ATTN_HEADS, HEAD_DIM), blk(WINDOW, KV_WIDTH), blk(WINDOW, KV_WIDTH)],
        out_shape=[jax.ShapeDtypeStruct((nb, N_ATTN_HEADS, HEAD_DIM), F32),
                   jax.ShapeDtypeStruct((nb, WINDOW, KV_WIDTH), F32),
                   jax.ShapeDtypeStruct((nb, WINDOW, KV_WIDTH), F32)],
        compiler_params=_params("parallel"),
        name="attn_sample",
    )(q, kn, vn, kbuf, vbuf, qg, kg, sinks)


def _ret_sample_kernel(qt_ref, kt_ref, v_ref, gate_ref, s0_ref, cos_ref, sin_ref, gamma_ref, rg_ref,
                       o_ref, s_out_ref):
    cos2 = cos_ref[...]
    sin2 = sin_ref[...]
    rot = lambda x: x * cos2 + pltpu.roll(x, RET_DIM // 2, 0) * sin2
    for b in range(SAMPLE_GROUP):
        qt = rot(qt_ref[b])
        kt = rot(kt_ref[b]) * RET_K_SCALE
        v = v_ref[b]
        gate = gate_ref[b]
        rows = []
        for h in range(N_RET_HEADS):
            s_new = s0_ref[b, h] * gamma_ref[h] + kt[:, h:h + 1] * v[h:h + 1, :]
            s_out_ref[b, h] = s_new
            o = jnp.sum(qt[:, h:h + 1] * s_new, axis=0, keepdims=True)
            rows.append(o * lax.rsqrt(jnp.mean(o * o, axis=-1, keepdims=True) + NORM_EPS))
        o = jnp.concatenate(rows, axis=0)
        o_ref[b] = o * rg_ref[...] * (gate * jax.nn.sigmoid(gate))


def _ret_sample(qt, kt, v, gate, s0, rg):
    nb = qt.shape[0]
    cos2, sin2 = _rope_tables(PAST_LEN + jnp.arange(1))
    h = jnp.arange(N_RET_HEADS, dtype=F32)
    gamma = jnp.exp(jnp.log1p(-jnp.exp2(-5.0 - h)))
    blk = lambda *shape: pl.BlockSpec((SAMPLE_GROUP,) + shape, lambda i: (i,) + (0,) * len(shape))
    const = lambda shape: pl.BlockSpec(shape, lambda i: (0,) * len(shape))
    return pl.pallas_call(
        _ret_sample_kernel,
        grid=(nb // SAMPLE_GROUP,),
        in_specs=[
            blk(RET_DIM, N_RET_HEADS), blk(RET_DIM, N_RET_HEADS), blk(N_RET_HEADS, RET_DIM),
            blk(N_RET_HEADS, RET_DIM), blk(N_RET_HEADS, RET_DIM, RET_DIM),
            const((RET_DIM, 1)), const((RET_DIM, 1)),
            pl.BlockSpec(memory_space=pltpu.SMEM),
            const((N_RET_HEADS, RET_DIM)),
        ],
        out_specs=[blk(N_RET_HEADS, RET_DIM), blk(N_RET_HEADS, RET_DIM, RET_DIM)],
        out_shape=[jax.ShapeDtypeStruct((nb, N_RET_HEADS, RET_DIM), F32),
                   jax.ShapeDtypeStruct((nb, N_RET_HEADS, RET_DIM, RET_DIM), F32)],
        compiler_params=_params("parallel"),
        name="ret_sample",
    )(qt, kt, v, gate, s0, cos2.reshape(RET_DIM, 1), sin2.reshape(RET_DIM, 1), gamma,
      rg.reshape(N_RET_HEADS, RET_DIM))


def _out_proj_kernel(h_ref, a_ref, r_ref, wa_ref, wr_ref, o_ref):
    o_ref[...] = h_ref[...] + _dot(a_ref[...], wa_ref[...]) + _dot(r_ref[...], wr_ref[...])


def _out_proj(h, o_attn, o_ret, w_bf16, tm):
    m = h.shape[0]
    return pl.pallas_call(
        _out_proj_kernel,
        grid=(m // tm,),
        in_specs=[
            pl.BlockSpec((tm, D_MODEL), lambda i: (i, 0)),
            pl.BlockSpec((tm, ATTN_WIDTH), lambda i: (i, 0)),
            pl.BlockSpec((tm, RET_WIDTH), lambda i: (i, 0)),
            pl.BlockSpec((ATTN_WIDTH, D_MODEL), lambda i: (0, 0)),
            pl.BlockSpec((RET_WIDTH, D_MODEL), lambda i: (1, 0)),
        ],
        out_specs=pl.BlockSpec((tm, D_MODEL), lambda i: (i, 0)),
        out_shape=jax.ShapeDtypeStruct((m, D_MODEL), F32),
        compiler_params=_params("parallel"),
        name="out_proj",
    )(h, o_attn, o_ret, w_bf16, w_bf16)


def _ffn_kernel(h_ref, g_ref, wg_ref, wu_ref, wd_ref, o_ref, f_ref):
    @pl.when(pl.program_id(1) == 0)
    def _():
        h = h_ref[...]
        f_ref[...] = _rms(h, g_ref[...]).astype(BF16)
        o_ref[...] = h

    f = f_ref[...]
    gate = _dot(f, wg_ref[...])
    act = (gate * jax.nn.sigmoid(gate) * _dot(f, wu_ref[...])).astype(BF16)
    o_ref[...] += _dot(act, wd_ref[...])


def _ffn(h, g, wg, wu, wd, tm):
    m = h.shape[0]
    return pl.pallas_call(
        _ffn_kernel,
        grid=(m // tm, D_FF // FF_TILE),
        in_specs=[
            pl.BlockSpec((tm, D_MODEL), lambda i, j: (i, 0)),
            pl.BlockSpec((1, D_MODEL), lambda i, j: (0, 0)),
            pl.BlockSpec((None, D_MODEL, FF_TILE), lambda i, j: (j, 0, 0)),
            pl.BlockSpec((None, D_MODEL, FF_TILE), lambda i, j: (j, 0, 0)),
            pl.BlockSpec((FF_TILE, D_MODEL), lambda i, j: (j, 0)),
        ],
        out_specs=pl.BlockSpec((tm, D_MODEL), lambda i, j: (i, 0)),
        out_shape=jax.ShapeDtypeStruct((m, D_MODEL), F32),
        scratch_shapes=[pltpu.VMEM((tm, D_MODEL), BF16)],
        compiler_params=_params("parallel", "arbitrary"),
        name="ffn",
    )(h, g, wg, wu, wd)


def _ple_kernel(h_ref, p_ref, g_ref, wp_ref, wg_ref, o_ref):
    h = h_ref[...]
    gate = jax.nn.sigmoid(_dot(_rms(h, g_ref[...]).astype(BF16), wg_ref[...]))
    o_ref[...] = h + _dot(p_ref[...].astype(BF16), wp_ref[...]) * gate


def _ple(h, p, g, w_ple, w_pg, tm):
    m = h.shape[0]
    return pl.pallas_call(
        _ple_kernel,
        grid=(m // tm,),
        in_specs=[
            pl.BlockSpec((tm, D_MODEL), lambda i: (i, 0)),
            pl.BlockSpec((tm, PLE_DIM), lambda i: (i, 0)),
            pl.BlockSpec((1, D_MODEL), lambda i: (0, 0)),
            pl.BlockSpec((PLE_DIM, D_MODEL), lambda i: (0, 0)),
            pl.BlockSpec((D_MODEL, D_MODEL), lambda i: (0, 0)),
        ],
        out_specs=pl.BlockSpec((tm, D_MODEL), lambda i: (i, 0)),
        out_shape=jax.ShapeDtypeStruct((m, D_MODEL), F32),
        compiler_params=_params("parallel"),
        name="ple",
    )(h, p, g, w_ple, w_pg)


def _dense_tail(h, o_attn, o_ret, p, w, tm):
    h = _out_proj(h, o_attn, o_ret, w["w_out"], tm)
    h = _ffn(h, w["ffn_norm_g"], w["w_gate"], w["w_up"], w["w_down"], min(2 * tm, h.shape[0]))
    return _ple(h, p, w["ple_norm_g"], w["w_ple"], w["w_ple_gate"], tm)


def _layer_prompt(h, z, p, w, batch, seq):
    o_attn, k_win, v_win = _attn_prompt(z, w["q_norm_g"], w["k_norm_g"], w["attn_sinks"], batch, seq)
    o_ret, state = _ret_prompt(z, w["ret_out_g"], batch, seq)
    y = _dense_tail(h, o_attn, o_ret, p.reshape(batch * seq, PLE_DIM), w, 512)
    return (y.reshape(batch, seq, D_MODEL),
            k_win.reshape(batch, BLK, N_KV_HEADS, HEAD_DIM), v_win.reshape(batch, BLK, N_KV_HEADS, HEAD_DIM), state)


def _layer_sample(h, z, p, k_buf, v_buf, s0, w):
    nb = h.shape[0]
    heads = lambda c, n: z[:, c:c + n * HEAD_DIM].reshape(nb, n, HEAD_DIM)
    o_attn, k_win, v_win = _attn_sample(
        heads(COL_AQ, N_ATTN_HEADS), heads(COL_AK, N_KV_HEADS), heads(COL_AV, N_KV_HEADS),
        k_buf.reshape(nb, WINDOW, KV_WIDTH), v_buf.reshape(nb, WINDOW, KV_WIDTH),
        w["q_norm_g"], w["k_norm_g"], w["attn_sinks"].reshape(N_ATTN_HEADS, 1))
    o_ret, state = _ret_sample(
        heads(COL_RQ, N_RET_HEADS).transpose(0, 2, 1), heads(COL_RK, N_RET_HEADS).transpose(0, 2, 1),
        heads(COL_RV, N_RET_HEADS), heads(COL_RG, N_RET_HEADS), s0, w["ret_out_g"])
    y = _dense_tail(h, o_attn.reshape(nb, ATTN_WIDTH).astype(BF16), o_ret.reshape(nb, RET_WIDTH).astype(BF16),
                    p.reshape(nb, PLE_DIM), w, nb)
    return (y.reshape(nb, 1, D_MODEL),
            k_win.reshape(nb, WINDOW, N_KV_HEADS, HEAD_DIM), v_win.reshape(nb, WINDOW, N_KV_HEADS, HEAD_DIM), state)


def kernel(x_prompt, x_sample, cache_k_win, cache_v_win, state_ret, p_prompt, p_sample,
           attn_norm_g, w_in, q_norm_g, k_norm_g, attn_sinks, ret_out_g, w_out,
           ffn_norm_g, w_gate, w_up, w_down, ple_norm_g, w_ple, w_ple_gate):
    assert w_in.shape[0] == 1 and x_sample.shape[1] == 1 and cache_k_win.shape[2] == WINDOW
    y_p, y_s = x_prompt, x_sample
    outs = [[] for _ in range(6)]
    for l in range(w_in.shape[0]):
        w = {
            "attn_norm_g": attn_norm_g[l].reshape(1, D_MODEL),
            "q_norm_g": q_norm_g[l].reshape(1, HEAD_DIM),
            "k_norm_g": k_norm_g[l].reshape(1, HEAD_DIM),
            "attn_sinks": attn_sinks[l],
            "ret_out_g": ret_out_g[l].reshape(1, RET_WIDTH),
            "ffn_norm_g": ffn_norm_g[l].reshape(1, D_MODEL),
            "ple_norm_g": ple_norm_g[l].reshape(1, D_MODEL),
        }
        batch, seq, _ = y_p.shape
        h_p = y_p.reshape(batch * seq, D_MODEL)
        h_s = y_s.reshape(y_s.shape[0], D_MODEL)
        z_p, z_s, *cast = _in_proj(h_p, h_s, w["attn_norm_g"], w_in[l], w_gate[l], w_up[l], w_down[l],
                                   w_out[l], w_ple_gate[l], w_ple[l])
        w.update(zip(("w_gate", "w_up", "w_down", "w_out", "w_ple_gate", "w_ple"), cast))
        y_p, kp, vp, sp = _layer_prompt(h_p, z_p, p_prompt[l], w, batch, seq)
        y_s, ks, vs, ss = _layer_sample(h_s, z_s, p_sample[l], cache_k_win[l], cache_v_win[l], state_ret[l], w)
        for acc, val in zip(outs, (kp, vp, sp, ks, vs, ss)):
            acc.append(val)
    kp, vp, sp, ks, vs, ss = (jnp.stack(o) for o in outs)
    return (y_p, y_s, kp, vp, sp, ks, vs, ss)
```

```python
import jax
import jax.numpy as jnp
from jax import lax
from jax.experimental import pallas as pl
from jax.experimental.pallas import tpu as pltpu

D_MODEL = 2048
HEAD_DIM = 128
N_ATTN_HEADS = 8
N_KV_HEADS = 2
GQA_GROUP = N_ATTN_HEADS // N_KV_HEADS
ATTN_WIDTH = N_ATTN_HEADS * HEAD_DIM
KV_WIDTH = N_KV_HEADS * HEAD_DIM
WINDOW = 128
BLK = 128
N_RET_HEADS = 8
RET_DIM = 128
RET_WIDTH = N_RET_HEADS * RET_DIM
D_FF = 5632
IN_WIDTH = 5632
PLE_DIM = 256
PAST_LEN = 16384
ROPE_BASE = 10000.0
NORM_EPS = 1e-6
ATTN_SCALE = HEAD_DIM ** -0.5
RET_K_SCALE = RET_DIM ** -0.5

COL_RQ, COL_RK, COL_RV, COL_RG = 0, RET_WIDTH, 2 * RET_WIDTH, 3 * RET_WIDTH
COL_AQ = 4 * RET_WIDTH
COL_AK = COL_AQ + ATTN_WIDTH
COL_AV = COL_AK + KV_WIDTH
IN_TILE = 512
IN_TILES = IN_WIDTH // IN_TILE
IN_ROT = (ATTN_WIDTH + 2 * KV_WIDTH) // IN_TILE

VMEM_LIMIT = 56 * 1024 * 1024

F32 = jnp.float32
BF16 = jnp.bfloat16


def _params(*sem):
    return pltpu.CompilerParams(dimension_semantics=sem, vmem_limit_bytes=VMEM_LIMIT)


def _rms(x, g):
    return x * lax.rsqrt(jnp.mean(x * x, axis=-1, keepdims=True) + NORM_EPS) * g


def _dot(a, b):
    return jnp.dot(a, b, preferred_element_type=F32)


def _dot_nt(a, b):
    return lax.dot_general(a, b, (((1,), (1,)), ((), ())), preferred_element_type=F32)


def _dot_tn(a, b):
    return lax.dot_general(a, b, (((0,), (0,)), ((), ())), preferred_element_type=F32)


def _in_proj_kernel(x_hbm, xs_ref, g_ref, w_ref, wg_ref, wu_ref, wd_ref, wo_ref, wpg_ref, wple_ref,
                    z_ref, zs_ref, wg_o, wu_o, wd_o, wo_o, wpg_o, wple_o,
                    xbuf, xsem, a_ref, as_ref, wres):
    i, j = pl.program_id(0), pl.program_id(1)

    def x_copy(tile):
        return pltpu.make_async_copy(x_hbm.at[pl.ds(tile * IN_ROWS, IN_ROWS)], xbuf, xsem)

    @pl.when((i == 0) & (j == 0))
    def _():
        x_copy(0).start()
        as_ref[...] = _rms(xs_ref[...], g_ref[...]).astype(BF16)

    @pl.when(j == 0)
    def _():
        x_copy(i).wait()
        a_ref[...] = _rms(xbuf[...], g_ref[...]).astype(BF16)

        @pl.when(i + 1 < pl.num_programs(0))
        def _():
            x_copy(i + 1).start()

    @pl.when(i == 0)
    def _():
        wres[j] = w_ref[...].astype(BF16)
        zs_ref[...] = _dot(as_ref[...], wres[j])

    z_ref[...] = _dot(a_ref[...], wres[j]).astype(z_ref.dtype)
    wg_o[...] = wg_ref[...].astype(BF16)
    wu_o[...] = wu_ref[...].astype(BF16)
    wd_o[...] = wd_ref[...].astype(BF16)

    @pl.when(j < CAST_SPLIT)
    def _():
        wo_o[...] = wo_ref[...].astype(BF16)
        wpg_o[...] = wpg_ref[...].astype(BF16)

    @pl.when(j == 0)
    def _():
        wple_o[...] = wple_ref[...].astype(BF16)


IN_ROWS = 1024
CAST_SPLIT = 8
FF_TILE = 512


def _in_proj(x, xs, g, w_in, w_gate, w_up, w_down, w_out, w_pg, w_ple):
    m = x.shape[0]
    ni = m // IN_ROWS
    steps = ni * IN_TILES
    assert m % IN_ROWS == 0 and FF_TILE == IN_TILE and D_FF == IN_WIDTH
    gu_rows, d_rows = D_MODEL // ni, D_FF // steps
    sq_rows, ple_rows = D_MODEL // (ni * CAST_SPLIT), PLE_DIM // ni
    assert gu_rows * ni == D_MODEL and d_rows * steps == D_FF and sq_rows % 16 == 0 and ple_rows % 16 == 0
    sq_idx = lambda i, j: (i * CAST_SPLIT + jnp.minimum(j, CAST_SPLIT - 1), 0)
    first = lambda i, j: jnp.where(i == 0, j, IN_TILES - 1)
    return pl.pallas_call(
        _in_proj_kernel,
        grid=(ni, IN_TILES),
        in_specs=[
            pl.BlockSpec(memory_space=pl.ANY),
            pl.BlockSpec(xs.shape, lambda i, j: (0, 0)),
            pl.BlockSpec((1, D_MODEL), lambda i, j: (0, 0)),
            pl.BlockSpec((D_MODEL, IN_TILE), lambda i, j: (0, (first(i, j) + IN_ROT) % IN_TILES)),
            pl.BlockSpec((gu_rows, FF_TILE), lambda i, j: (i, j)),
            pl.BlockSpec((gu_rows, FF_TILE), lambda i, j: (i, j)),
            pl.BlockSpec((d_rows, D_MODEL), lambda i, j: (i * IN_TILES + j, 0)),
            pl.BlockSpec((sq_rows, D_MODEL), sq_idx),
            pl.BlockSpec((sq_rows, D_MODEL), sq_idx),
            pl.BlockSpec((ple_rows, D_MODEL), lambda i, j: (i, 0)),
        ],
        out_specs=[
            pl.BlockSpec((IN_ROWS, IN_TILE), lambda i, j: (i, j)),
            pl.BlockSpec((xs.shape[0], IN_TILE), lambda i, j: (0, first(i, j))),
            pl.BlockSpec((None, gu_rows, FF_TILE), lambda i, j: (j, i, 0)),
            pl.BlockSpec((None, gu_rows, FF_TILE), lambda i, j: (j, i, 0)),
            pl.BlockSpec((d_rows, D_MODEL), lambda i, j: (i * IN_TILES + j, 0)),
            pl.BlockSpec((sq_rows, D_MODEL), sq_idx),
            pl.BlockSpec((sq_rows, D_MODEL), sq_idx),
            pl.BlockSpec((ple_rows, D_MODEL), lambda i, j: (i, 0)),
        ],
        out_shape=[
            jax.ShapeDtypeStruct((m, IN_WIDTH), BF16),
            jax.ShapeDtypeStruct((xs.shape[0], IN_WIDTH), F32),
            jax.ShapeDtypeStruct((D_FF // FF_TILE, D_MODEL, FF_TILE), BF16),
            jax.ShapeDtypeStruct((D_FF // FF_TILE, D_MODEL, FF_TILE), BF16),
            jax.ShapeDtypeStruct((D_FF, D_MODEL), BF16),
            jax.ShapeDtypeStruct((D_MODEL, D_MODEL), BF16),
            jax.ShapeDtypeStruct((D_MODEL, D_MODEL), BF16),
            jax.ShapeDtypeStruct((PLE_DIM, D_MODEL), BF16),
        ],
        scratch_shapes=[
            pltpu.VMEM((IN_ROWS, D_MODEL), F32),
            pltpu.SemaphoreType.DMA(()),
            pltpu.VMEM((IN_ROWS, D_MODEL), BF16),
            pltpu.VMEM((xs.shape[0], D_MODEL), BF16),
            pltpu.VMEM((IN_TILES, D_MODEL, IN_TILE), BF16),
        ],
        compiler_params=_params("arbitrary", "arbitrary"),
        name="in_proj",
    )(x, xs, g, w_in, w_gate, w_up, w_down, w_out, w_pg, w_ple)


MIX_ROWS = 256
LOG2E = 1.4426950408889634


def _rotary(x, cos2, sin2):
    return x * cos2 + pltpu.roll(x, RET_DIM // 2, 1) * sin2


def _mixer_prompt_kernel(z_ref, kp_ref, vp_ref, qg_ref, kg_ref, sink_ref, cos_ref, sin_ref, qdec_ref, kdec_ref,
                         cdec_ref, rg_ref, mix_ref, kwin_ref, vwin_ref, sout_ref, s_ref):
    n = pl.program_id(1)
    last = pl.num_programs(1) - 1

    @pl.when(n == 0)
    def _():
        s_ref[...] = jnp.zeros_like(s_ref)

    qg = qg_ref[...] * (ATTN_SCALE * LOG2E)
    kg = kg_ref[...]
    row = lax.broadcasted_iota(jnp.int32, (BLK, 2 * BLK), 0)
    col = lax.broadcasted_iota(jnp.int32, (BLK, 2 * BLK), 1)
    cur_ok = (col >= BLK) & ((col - BLK) <= row)
    prev_ok = (col < BLK) & (col >= row)
    kp = kp_ref[...].astype(F32)
    kprev = jnp.concatenate([_rms(kp[:, h * HEAD_DIM:(h + 1) * HEAD_DIM], kg) for h in range(N_KV_HEADS)],
                            axis=1).astype(BF16)
    vprev = vp_ref[...]
    for blk in range(MIX_ROWS // BLK):
        rows = slice(blk * BLK, (blk + 1) * BLK)
        mask = cur_ok | (prev_ok & (n > 0)) if blk == 0 else cur_ok | prev_ok
        k = z_ref[rows, COL_AK:COL_AK + KV_WIDTH].astype(F32)
        kn = [_rms(k[:, h * HEAD_DIM:(h + 1) * HEAD_DIM], kg) for h in range(N_KV_HEADS)]
        kcur = jnp.concatenate(kn, axis=1).astype(BF16)
        vcur = z_ref[rows, COL_AV:COL_AV + KV_WIDTH]
        for kh in range(N_KV_HEADS):
            sl = slice(kh * HEAD_DIM, (kh + 1) * HEAD_DIM)
            k_cat = jnp.concatenate([kprev[:, sl], kcur[:, sl]], axis=0)
            v_cat = jnp.concatenate([vprev[:, sl], vcur[:, sl]], axis=0)
            for g in range(GQA_GROUP):
                h = kh * GQA_GROUP + g
                q = z_ref[rows, COL_AQ + h * HEAD_DIM:COL_AQ + (h + 1) * HEAD_DIM].astype(F32)
                s = jnp.where(mask, _dot_nt(_rms(q, qg).astype(BF16), k_cat), -jnp.inf)
                sink = sink_ref[h] * LOG2E
                m = jnp.maximum(jnp.max(s, axis=-1, keepdims=True), sink)
                e = jnp.exp2(s - m)
                denom = jnp.sum(e, axis=-1, keepdims=True) + jnp.exp2(sink - m)
                mix_ref[rows, h * HEAD_DIM:(h + 1) * HEAD_DIM] = (
                    _dot(e.astype(BF16), v_cat) / denom).astype(mix_ref.dtype)
        kprev, vprev = kcur, vcur

    @pl.when(n == last)
    def _():
        kwin_ref[0] = jnp.concatenate(kn, axis=1)
        vwin_ref[0] = vcur.astype(F32)

    tril = (lax.broadcasted_iota(jnp.int32, (MIX_ROWS, MIX_ROWS), 0)
            >= lax.broadcasted_iota(jnp.int32, (MIX_ROWS, MIX_ROWS), 1))
    cos2 = cos_ref[...]
    sin2 = sin_ref[...]
    for h in range(N_RET_HEADS):
        hs = slice(h * RET_DIM, (h + 1) * RET_DIM)
        col_of = lambda c: slice(c + h * RET_DIM, c + (h + 1) * RET_DIM)
        q = (_rotary(z_ref[:, col_of(COL_RQ)].astype(F32), cos2, sin2) * qdec_ref[:, hs]).astype(BF16)
        k = (_rotary(z_ref[:, col_of(COL_RK)].astype(F32), cos2, sin2) * kdec_ref[:, hs]).astype(BF16)
        v = z_ref[:, col_of(COL_RV)]
        state = s_ref[h]
        sc = jnp.where(tril, _dot_nt(q, k), 0.0)
        o = _dot(sc.astype(BF16), v) + _dot(q, state.astype(BF16))
        s_ref[h] = (state + _dot_tn(k, v)) * cdec_ref[h]
        o = o * lax.rsqrt(jnp.mean(o * o, axis=-1, keepdims=True) + NORM_EPS)
        gate = z_ref[:, col_of(COL_RG)].astype(F32)
        mix_ref[:, ATTN_WIDTH + h * RET_DIM:ATTN_WIDTH + (h + 1) * RET_DIM] = (
            o * rg_ref[:, hs] * (gate * jax.nn.sigmoid(gate))).astype(mix_ref.dtype)

    @pl.when(n == last)
    def _():
        sout_ref[0] = s_ref[...]


def _rope_tables(pos):
    half = RET_DIM // 2
    inv = ROPE_BASE ** (-jnp.arange(half, dtype=F32) / half)
    ang = pos.astype(F32)[:, None] * inv[None, :]
    cos, sin = jnp.cos(ang), jnp.sin(ang)
    return jnp.concatenate([cos, cos], axis=-1), jnp.concatenate([-sin, sin], axis=-1)


def _ret_log_decay():
    return jnp.log1p(-jnp.exp2(-5.0 - jnp.arange(N_RET_HEADS, dtype=F32)))


def _mixer_prompt(z, qg, kg, sinks, rg, batch, seq):
    nsteps = seq // MIX_ROWS
    cos2, sin2 = _rope_tables(jnp.arange(seq))
    lg = _ret_log_decay()
    steps = jnp.arange(MIX_ROWS, dtype=F32)[:, None, None] + 1.0
    lanes = lambda t: jnp.broadcast_to(t, (MIX_ROWS, N_RET_HEADS, RET_DIM)).reshape(MIX_ROWS, RET_WIDTH)
    q_dec = lanes(jnp.exp(lg[None, :, None] * steps))
    k_dec = lanes(jnp.exp(-lg[None, :, None] * steps) * RET_K_SCALE)
    chunk_dec = jnp.exp(lg * MIX_ROWS)
    const = lambda shape: pl.BlockSpec(shape, lambda b, n: (0,) * len(shape))
    smem = pl.BlockSpec(memory_space=pltpu.SMEM)
    win = jax.ShapeDtypeStruct((batch, BLK, KV_WIDTH), F32)
    blks = MIX_ROWS // BLK
    prev_blk = lambda b, n: (b * nsteps + n) * blks - jnp.minimum(n, 1)
    return pl.pallas_call(
        _mixer_prompt_kernel,
        grid=(batch, nsteps),
        in_specs=[
            pl.BlockSpec((MIX_ROWS, IN_WIDTH), lambda b, n: (b * nsteps + n, 0)),
            pl.BlockSpec((BLK, KV_WIDTH), lambda b, n: (prev_blk(b, n), COL_AK // KV_WIDTH)),
            pl.BlockSpec((BLK, KV_WIDTH), lambda b, n: (prev_blk(b, n), COL_AV // KV_WIDTH)),
            const((1, HEAD_DIM)), const((1, HEAD_DIM)), smem,
            pl.BlockSpec((MIX_ROWS, RET_DIM), lambda b, n: (n, 0)),
            pl.BlockSpec((MIX_ROWS, RET_DIM), lambda b, n: (n, 0)),
            const((MIX_ROWS, RET_WIDTH)), const((MIX_ROWS, RET_WIDTH)), smem,
            const((1, RET_WIDTH)),
        ],
        out_specs=[
            pl.BlockSpec((MIX_ROWS, D_MODEL), lambda b, n: (b * nsteps + n, 0)),
            pl.BlockSpec((1, BLK, KV_WIDTH), lambda b, n: (b, 0, 0)),
            pl.BlockSpec((1, BLK, KV_WIDTH), lambda b, n: (b, 0, 0)),
            pl.BlockSpec((1, N_RET_HEADS, RET_DIM, RET_DIM), lambda b, n: (b, 0, 0, 0)),
        ],
        out_shape=[jax.ShapeDtypeStruct((batch * seq, D_MODEL), BF16), win, win,
                   jax.ShapeDtypeStruct((batch, N_RET_HEADS, RET_DIM, RET_DIM), F32)],
        scratch_shapes=[pltpu.VMEM((N_RET_HEADS, RET_DIM, RET_DIM), F32)],
        compiler_params=_params("arbitrary", "arbitrary"),
        name="mixer_prompt",
    )(z, z, z, qg, kg, sinks, cos2, sin2, q_dec, k_dec, chunk_dec, rg)


SAMPLE_GROUP = 8


def _attn_sample_kernel(q_ref, kn_ref, vn_ref, kbuf_ref, vbuf_ref, qg_ref, kg_ref, sink_ref,
                        o_ref, kout_ref, vout_ref):
    qg = qg_ref[...]
    kg = kg_ref[...]
    sink = sink_ref[...]
    head = lax.broadcasted_iota(jnp.int32, (N_ATTN_HEADS, HEAD_DIM), 0)
    first = head < GQA_GROUP
    last_row = lax.broadcasted_iota(jnp.int32, (WINDOW, KV_WIDTH), 0) == WINDOW - 1
    for b in range(SAMPLE_GROUP):
        qn = _rms(q_ref[b], qg) * ATTN_SCALE
        kn = _rms(kn_ref[b], kg)
        vn = vn_ref[b]
        kbuf = kbuf_ref[b]
        vbuf = vbuf_ref[b]
        qb = qn.astype(BF16)
        s0 = _dot_nt(qb, kbuf[:, :HEAD_DIM].astype(BF16))
        s1 = _dot_nt(qb, kbuf[:, HEAD_DIM:].astype(BF16))
        s = jnp.where(first[:, :WINDOW], s0, s1)
        k_sel = jnp.where(first, kn[0:1], kn[1:2])
        v_sel = jnp.where(first, vn[0:1], vn[1:2])
        s_new = jnp.sum(qn * k_sel, axis=-1, keepdims=True)
        m = jnp.maximum(jnp.maximum(jnp.max(s, axis=-1, keepdims=True), s_new), sink)
        e = jnp.exp(s - m)
        e_new = jnp.exp(s_new - m)
        denom = jnp.sum(e, axis=-1, keepdims=True) + e_new + jnp.exp(sink - m)
        eb = e.astype(BF16)
        o0 = _dot(eb, vbuf[:, :HEAD_DIM].astype(BF16))
        o1 = _dot(eb, vbuf[:, HEAD_DIM:].astype(BF16))
        o_ref[b] = (jnp.where(first, o0, o1) + e_new * v_sel) / denom
        k_row = jnp.concatenate([kn[0:1], kn[1:2]], axis=1)
        v_row = jnp.concatenate([vn[0:1], vn[1:2]], axis=1)
        kout_ref[b] = jnp.where(last_row, k_row, pltpu.roll(kbuf, WINDOW - 1, 0))
        vout_ref[b] = jnp.where(last_row, v_row, pltpu.roll(vbuf, WINDOW - 1, 0))


def _attn_sample(q, kn, vn, kbuf, vbuf, qg, kg, sinks):
    nb = q.shape[0]
    blk = lambda *shape: pl.BlockSpec((SAMPLE_GROUP,) + shape, lambda i: (i,) + (0,) * len(shape))
    const = lambda shape: pl.BlockSpec(shape, lambda i: (0,) * len(shape))
    return pl.pallas_call(
        _attn_sample_kernel,
        grid=(nb // SAMPLE_GROUP,),
        in_specs=[
            blk(N_ATTN_HEADS, HEAD_DIM), blk(N_KV_HEADS, HEAD_DIM), blk(N_KV_HEADS, HEAD_DIM),
            blk(WINDOW, KV_WIDTH), blk(WINDOW, KV_WIDTH),
            const((1, HEAD_DIM)), const((1, HEAD_DIM)), const((N_ATTN_HEADS, 1)),
        ],
        out_specs=[blk(N_ATTN_HEADS, HEAD_DIM), blk(WINDOW, KV_WIDTH), blk(WINDOW, KV_WIDTH)],
        out_shape=[jax.ShapeDtypeStruct((nb, N_ATTN_HEADS, HEAD_DIM), F32),
                   jax.ShapeDtypeStruct((nb, WINDOW, KV_WIDTH), F32),
                   jax.ShapeDtypeStruct((nb, WINDOW, KV_WIDTH), F32)],
        compiler_params=_params("parallel"),
        name="attn_sample",
    )(q, kn, vn, kbuf, vbuf, qg, kg, sinks)


def _ret_sample_kernel(qt_ref, kt_ref, v_ref, gate_ref, s0_ref, cos_ref, sin_ref, gamma_ref, rg_ref,
                       o_ref, s_out_ref):
    cos2 = cos_ref[...]
    sin2 = sin_ref[...]
    rot = lambda x: x * cos2 + pltpu.roll(x, RET_DIM // 2, 0) * sin2
    for b in range(SAMPLE_GROUP):
        qt = rot(qt_ref[b])
        kt = rot(kt_ref[b]) * RET_K_SCALE
        v = v_ref[b]
        gate = gate_ref[b]
        rows = []
        for h in range(N_RET_HEADS):
            s_new = s0_ref[b, h] * gamma_ref[h] + kt[:, h:h + 1] * v[h:h + 1, :]
            s_out_ref[b, h] = s_new
            o = jnp.sum(qt[:, h:h + 1] * s_new, axis=0, keepdims=True)
            rows.append(o * lax.rsqrt(jnp.mean(o * o, axis=-1, keepdims=True) + NORM_EPS))
        o = jnp.concatenate(rows, axis=0)
        o_ref[b] = o * rg_ref[...] * (gate * jax.nn.sigmoid(gate))


def _ret_sample(qt, kt, v, gate, s0, rg):
    nb = qt.shape[0]
    cos2, sin2 = _rope_tables(PAST_LEN + jnp.arange(1))
    gamma = jnp.exp(_ret_log_decay())
    blk = lambda *shape: pl.BlockSpec((SAMPLE_GROUP,) + shape, lambda i: (i,) + (0,) * len(shape))
    const = lambda shape: pl.BlockSpec(shape, lambda i: (0,) * len(shape))
    return pl.pallas_call(
        _ret_sample_kernel,
        grid=(nb // SAMPLE_GROUP,),
        in_specs=[
            blk(RET_DIM, N_RET_HEADS), blk(RET_DIM, N_RET_HEADS), blk(N_RET_HEADS, RET_DIM),
            blk(N_RET_HEADS, RET_DIM), blk(N_RET_HEADS, RET_DIM, RET_DIM),
            const((RET_DIM, 1)), const((RET_DIM, 1)),
            pl.BlockSpec(memory_space=pltpu.SMEM),
            const((N_RET_HEADS, RET_DIM)),
        ],
        out_specs=[blk(N_RET_HEADS, RET_DIM), blk(N_RET_HEADS, RET_DIM, RET_DIM)],
        out_shape=[jax.ShapeDtypeStruct((nb, N_RET_HEADS, RET_DIM), F32),
                   jax.ShapeDtypeStruct((nb, N_RET_HEADS, RET_DIM, RET_DIM), F32)],
        compiler_params=_params("parallel"),
        name="ret_sample",
    )(qt, kt, v, gate, s0, cos2.reshape(RET_DIM, 1), sin2.reshape(RET_DIM, 1), gamma,
      rg.reshape(N_RET_HEADS, RET_DIM))


def _out_proj_kernel(h_ref, mix_ref, w_ref, o_ref):
    o_ref[...] = h_ref[...] + _dot(mix_ref[...], w_ref[...])


def _out_proj(h, mix, w_bf16, tm):
    m = h.shape[0]
    return pl.pallas_call(
        _out_proj_kernel,
        grid=(m // tm,),
        in_specs=[
            pl.BlockSpec((tm, D_MODEL), lambda i: (i, 0)),
            pl.BlockSpec((tm, D_MODEL), lambda i: (i, 0)),
            pl.BlockSpec((D_MODEL, D_MODEL), lambda i: (0, 0)),
        ],
        out_specs=pl.BlockSpec((tm, D_MODEL), lambda i: (i, 0)),
        out_shape=jax.ShapeDtypeStruct((m, D_MODEL), F32),
        compiler_params=_params("parallel"),
        name="out_proj",
    )(h, mix, w_bf16)


def _ffn_kernel(h_ref, g_ref, wg_ref, wu_ref, wd_ref, o_ref, f_ref):
    @pl.when(pl.program_id(1) == 0)
    def _():
        h = h_ref[...]
        f_ref[...] = _rms(h, g_ref[...]).astype(BF16)
        o_ref[...] = h

    f = f_ref[...]
    gate = _dot(f, wg_ref[...])
    act = (gate * jax.nn.sigmoid(gate) * _dot(f, wu_ref[...])).astype(BF16)
    o_ref[...] += _dot(act, wd_ref[...])


def _ffn(h, g, wg, wu, wd, tm):
    m = h.shape[0]
    return pl.pallas_call(
        _ffn_kernel,
        grid=(m // tm, D_FF // FF_TILE),
        in_specs=[
            pl.BlockSpec((tm, D_MODEL), lambda i, j: (i, 0)),
            pl.BlockSpec((1, D_MODEL), lambda i, j: (0, 0)),
            pl.BlockSpec((None, D_MODEL, FF_TILE), lambda i, j: (j, 0, 0)),
            pl.BlockSpec((None, D_MODEL, FF_TILE), lambda i, j: (j, 0, 0)),
            pl.BlockSpec((FF_TILE, D_MODEL), lambda i, j: (j, 0)),
        ],
        out_specs=pl.BlockSpec((tm, D_MODEL), lambda i, j: (i, 0)),
        out_shape=jax.ShapeDtypeStruct((m, D_MODEL), F32),
        scratch_shapes=[pltpu.VMEM((tm, D_MODEL), BF16)],
        compiler_params=_params("parallel", "arbitrary"),
        name="ffn",
    )(h, g, wg, wu, wd)


def _ple_kernel(h_ref, p_ref, g_ref, wp_ref, wg_ref, o_ref):
    h = h_ref[...]
    gate = jax.nn.sigmoid(_dot(_rms(h, g_ref[...]).astype(BF16), wg_ref[...]))
    o_ref[...] = h + _dot(p_ref[...].astype(BF16), wp_ref[...]) * gate


def _ple(h, p, g, w_ple, w_pg, tm):
    m = h.shape[0]
    return pl.pallas_call(
        _ple_kernel,
        grid=(m // tm,),
        in_specs=[
            pl.BlockSpec((tm, D_MODEL), lambda i: (i, 0)),
            pl.BlockSpec((tm, PLE_DIM), lambda i: (i, 0)),
            pl.BlockSpec((1, D_MODEL), lambda i: (0, 0)),
            pl.BlockSpec((PLE_DIM, D_MODEL), lambda i: (0, 0)),
            pl.BlockSpec((D_MODEL, D_MODEL), lambda i: (0, 0)),
        ],
        out_specs=pl.BlockSpec((tm, D_MODEL), lambda i: (i, 0)),
        out_shape=jax.ShapeDtypeStruct((m, D_MODEL), F32),
        compiler_params=_params("parallel"),
        name="ple",
    )(h, p, g, w_ple, w_pg)


def _dense_tail(h, mix, p, w, tm):
    h = _out_proj(h, mix, w["w_out"], tm)
    h = _ffn(h, w["ffn_norm_g"], w["w_gate"], w["w_up"], w["w_down"], min(2 * tm, h.shape[0]))
    return _ple(h, p, w["ple_norm_g"], w["w_ple"], w["w_ple_gate"], tm)


def _layer_prompt(h, z, p, w, batch, seq):
    mix, k_win, v_win, state = _mixer_prompt(z, w["q_norm_g"], w["k_norm_g"], w["attn_sinks"], w["ret_out_g"],
                                             batch, seq)
    y = _dense_tail(h, mix, p.reshape(batch * seq, PLE_DIM), w, 512)
    return (y.reshape(batch, seq, D_MODEL),
            k_win.reshape(batch, BLK, N_KV_HEADS, HEAD_DIM), v_win.reshape(batch, BLK, N_KV_HEADS, HEAD_DIM), state)


def _layer_sample(h, z, p, k_buf, v_buf, s0, w):
    nb = h.shape[0]
    heads = lambda c, n: z[:, c:c + n * HEAD_DIM].reshape(nb, n, HEAD_DIM)
    o_attn, k_win, v_win = _attn_sample(
        heads(COL_AQ, N_ATTN_HEADS), heads(COL_AK, N_KV_HEADS), heads(COL_AV, N_KV_HEADS),
        k_buf.reshape(nb, WINDOW, KV_WIDTH), v_buf.reshape(nb, WINDOW, KV_WIDTH),
        w["q_norm_g"], w["k_norm_g"], w["attn_sinks"].reshape(N_ATTN_HEADS, 1))
    o_ret, state = _ret_sample(
        heads(COL_RQ, N_RET_HEADS).transpose(0, 2, 1), heads(COL_RK, N_RET_HEADS).transpose(0, 2, 1),
        heads(COL_RV, N_RET_HEADS), heads(COL_RG, N_RET_HEADS), s0, w["ret_out_g"])
    mix = jnp.concatenate([o_attn.reshape(nb, ATTN_WIDTH), o_ret.reshape(nb, RET_WIDTH)], axis=1).astype(BF16)
    y = _dense_tail(h, mix, p.reshape(nb, PLE_DIM), w, nb)
    return (y.reshape(nb, 1, D_MODEL),
            k_win.reshape(nb, WINDOW, N_KV_HEADS, HEAD_DIM), v_win.reshape(nb, WINDOW, N_KV_HEADS, HEAD_DIM), state)


def kernel(x_prompt, x_sample, cache_k_win, cache_v_win, state_ret, p_prompt, p_sample,
           attn_norm_g, w_in, q_norm_g, k_norm_g, attn_sinks, ret_out_g, w_out,
           ffn_norm_g, w_gate, w_up, w_down, ple_norm_g, w_ple, w_ple_gate):
    assert w_in.shape[0] == 1 and x_sample.shape[1] == 1 and cache_k_win.shape[2] == WINDOW
    y_p, y_s = x_prompt, x_sample
    outs = [[] for _ in range(6)]
    for l in range(w_in.shape[0]):
        w = {
            "attn_norm_g": attn_norm_g[l].reshape(1, D_MODEL),
            "q_norm_g": q_norm_g[l].reshape(1, HEAD_DIM),
            "k_norm_g": k_norm_g[l].reshape(1, HEAD_DIM),
            "attn_sinks": attn_sinks[l],
            "ret_out_g": ret_out_g[l].reshape(1, RET_WIDTH),
            "ffn_norm_g": ffn_norm_g[l].reshape(1, D_MODEL),
            "ple_norm_g": ple_norm_g[l].reshape(1, D_MODEL),
        }
        batch, seq, _ = y_p.shape
        h_p = y_p.reshape(batch * seq, D_MODEL)
        h_s = y_s.reshape(y_s.shape[0], D_MODEL)
        z_p, z_s, *cast = _in_proj(h_p, h_s, w["attn_norm_g"], w_in[l], w_gate[l], w_up[l], w_down[l],
                                   w_out[l], w_ple_gate[l], w_ple[l])
        w.update(zip(("w_gate", "w_up", "w_down", "w_out", "w_ple_gate", "w_ple"), cast))
        y_p, kp, vp, sp = _layer_prompt(h_p, z_p, p_prompt[l], w, batch, seq)
        y_s, ks, vs, ss = _layer_sample(h_s, z_s, p_sample[l], cache_k_win[l], cache_v_win[l], state_ret[l], w)
        for acc, val in zip(outs, (kp, vp, sp, ks, vs, ss)):
            acc.append(val)
    kp, vp, sp, ks, vs, ss = (jnp.stack(o) for o in outs)
    return (y_p, y_s, kp, vp, sp, ks, vs, ss)
```

```python
import numpy as np
import jax
import jax.numpy as jnp
from jax import lax
from jax.experimental import pallas as pl
from jax.experimental.pallas import tpu as pltpu

D_MODEL = 2048
HEAD_DIM = 128
N_ATTN_HEADS = 8
N_KV_HEADS = 2
GQA_GROUP = N_ATTN_HEADS // N_KV_HEADS
ATTN_WIDTH = N_ATTN_HEADS * HEAD_DIM
KV_WIDTH = N_KV_HEADS * HEAD_DIM
WINDOW = 128
BLK = 128
N_RET_HEADS = 8
RET_DIM = 128
RET_WIDTH = N_RET_HEADS * RET_DIM
D_FF = 5632
IN_WIDTH = 5632
PLE_DIM = 256
PAST_LEN = 16384
ROPE_BASE = 10000.0
NORM_EPS = 1e-6
ATTN_SCALE = HEAD_DIM ** -0.5
RET_K_SCALE = RET_DIM ** -0.5

COL_RQ, COL_RK, COL_RV, COL_RG = 0, RET_WIDTH, 2 * RET_WIDTH, 3 * RET_WIDTH
COL_AQ = 4 * RET_WIDTH
COL_AK = COL_AQ + ATTN_WIDTH
COL_AV = COL_AK + KV_WIDTH
IN_TILE = 512
IN_TILES = IN_WIDTH // IN_TILE
IN_ROT = (ATTN_WIDTH + 2 * KV_WIDTH) // IN_TILE

VMEM_LIMIT = 56 * 1024 * 1024

F32 = jnp.float32
BF16 = jnp.bfloat16


def _params(*sem):
    return pltpu.CompilerParams(dimension_semantics=sem, vmem_limit_bytes=VMEM_LIMIT)


def _rms(x, g):
    return x * lax.rsqrt(jnp.mean(x * x, axis=-1, keepdims=True) + NORM_EPS) * g


def _dot(a, b):
    return jnp.dot(a, b, preferred_element_type=F32)


def _dot_nt(a, b):
    return lax.dot_general(a, b, (((1,), (1,)), ((), ())), preferred_element_type=F32)


def _dot_tn(a, b):
    return lax.dot_general(a, b, (((0,), (0,)), ((), ())), preferred_element_type=F32)


def _in_proj_kernel(x_hbm, xs_ref, g_ref, w_ref, wg_ref, wu_ref, wd_ref, wo_ref, wpg_ref, wple_ref,
                    z_ref, zs_ref, wg_o, wu_o, wd_o, wo_o, wpg_o, wple_o,
                    xbuf, xsem, a_ref, as_ref, wres):
    i, j = pl.program_id(0), pl.program_id(1)

    def x_copy(tile):
        return pltpu.make_async_copy(x_hbm.at[pl.ds(tile * IN_ROWS, IN_ROWS)], xbuf, xsem)

    @pl.when((i == 0) & (j == 0))
    def _():
        x_copy(0).start()
        as_ref[...] = _rms(xs_ref[...], g_ref[...]).astype(BF16)

    @pl.when(j == 0)
    def _():
        x_copy(i).wait()
        a_ref[...] = _rms(xbuf[...], g_ref[...]).astype(BF16)

        @pl.when(i + 1 < pl.num_programs(0))
        def _():
            x_copy(i + 1).start()

    @pl.when(i == 0)
    def _():
        wres[j] = w_ref[...].astype(BF16)
        zs_ref[...] = _dot(as_ref[...], wres[j])

    z_ref[...] = _dot(a_ref[...], wres[j]).astype(z_ref.dtype)
    wg_o[...] = wg_ref[...].astype(BF16)
    wu_o[...] = wu_ref[...].astype(BF16)
    wd_o[...] = wd_ref[...].astype(BF16)

    @pl.when(j < CAST_SPLIT)
    def _():
        wo_o[...] = wo_ref[...].astype(BF16)
        wpg_o[...] = wpg_ref[...].astype(BF16)

    @pl.when(j == 0)
    def _():
        wple_o[...] = wple_ref[...].astype(BF16)


IN_ROWS = 1024
CAST_SPLIT = 8
FF_TILE = 512


def _in_proj(x, xs, g, w_in, w_gate, w_up, w_down, w_out, w_pg, w_ple):
    m = x.shape[0]
    ni = m // IN_ROWS
    steps = ni * IN_TILES
    assert m % IN_ROWS == 0 and FF_TILE == IN_TILE and D_FF == IN_WIDTH
    gu_rows, d_rows = D_MODEL // ni, D_FF // steps
    sq_rows, ple_rows = D_MODEL // (ni * CAST_SPLIT), PLE_DIM // ni
    assert gu_rows * ni == D_MODEL and d_rows * steps == D_FF and sq_rows % 16 == 0 and ple_rows % 16 == 0
    sq_idx = lambda i, j: (i * CAST_SPLIT + jnp.minimum(j, CAST_SPLIT - 1), 0)
    first = lambda i, j: jnp.where(i == 0, j, IN_TILES - 1)
    return pl.pallas_call(
        _in_proj_kernel,
        grid=(ni, IN_TILES),
        in_specs=[
            pl.BlockSpec(memory_space=pl.ANY),
            pl.BlockSpec(xs.shape, lambda i, j: (0, 0)),
            pl.BlockSpec((1, D_MODEL), lambda i, j: (0, 0)),
            pl.BlockSpec((D_MODEL, IN_TILE), lambda i, j: (0, (first(i, j) + IN_ROT) % IN_TILES)),
            pl.BlockSpec((gu_rows, FF_TILE), lambda i, j: (i, j)),
            pl.BlockSpec((gu_rows, FF_TILE), lambda i, j: (i, j)),
            pl.BlockSpec((d_rows, D_MODEL), lambda i, j: (i * IN_TILES + j, 0)),
            pl.BlockSpec((sq_rows, D_MODEL), sq_idx),
            pl.BlockSpec((sq_rows, D_MODEL), sq_idx),
            pl.BlockSpec((ple_rows, D_MODEL), lambda i, j: (i, 0)),
        ],
        out_specs=[
            pl.BlockSpec((IN_ROWS, IN_TILE), lambda i, j: (i, j)),
            pl.BlockSpec((xs.shape[0], IN_TILE), lambda i, j: (0, first(i, j))),
            pl.BlockSpec((None, gu_rows, FF_TILE), lambda i, j: (j, i, 0)),
            pl.BlockSpec((None, gu_rows, FF_TILE), lambda i, j: (j, i, 0)),
            pl.BlockSpec((d_rows, D_MODEL), lambda i, j: (i * IN_TILES + j, 0)),
            pl.BlockSpec((sq_rows, D_MODEL), sq_idx),
            pl.BlockSpec((sq_rows, D_MODEL), sq_idx),
            pl.BlockSpec((ple_rows, D_MODEL), lambda i, j: (i, 0)),
        ],
        out_shape=[
            jax.ShapeDtypeStruct((m, IN_WIDTH), BF16),
            jax.ShapeDtypeStruct((xs.shape[0], IN_WIDTH), F32),
            jax.ShapeDtypeStruct((D_FF // FF_TILE, D_MODEL, FF_TILE), BF16),
            jax.ShapeDtypeStruct((D_FF // FF_TILE, D_MODEL, FF_TILE), BF16),
            jax.ShapeDtypeStruct((D_FF, D_MODEL), BF16),
            jax.ShapeDtypeStruct((D_MODEL, D_MODEL), BF16),
            jax.ShapeDtypeStruct((D_MODEL, D_MODEL), BF16),
            jax.ShapeDtypeStruct((PLE_DIM, D_MODEL), BF16),
        ],
        scratch_shapes=[
            pltpu.VMEM((IN_ROWS, D_MODEL), F32),
            pltpu.SemaphoreType.DMA(()),
            pltpu.VMEM((IN_ROWS, D_MODEL), BF16),
            pltpu.VMEM((xs.shape[0], D_MODEL), BF16),
            pltpu.VMEM((IN_TILES, D_MODEL, IN_TILE), BF16),
        ],
        compiler_params=_params("arbitrary", "arbitrary"),
        name="in_proj",
    )(x, xs, g, w_in, w_gate, w_up, w_down, w_out, w_pg, w_ple)


MIX_ROWS = 256
LOG2E = 1.4426950408889634


def _rotary(x, cos2, sin2):
    return x * cos2 + pltpu.roll(x, RET_DIM // 2, 1) * sin2


def _mixer_prompt_kernel(z_ref, kp_ref, vp_ref, qg_ref, kg_ref, sink_ref, cos_ref, sin_ref, qdec_ref, kdec_ref,
                         cdec_ref, rg_ref, mix_ref, kwin_ref, vwin_ref, sout_ref, s_ref):
    n = pl.program_id(1)
    last = pl.num_programs(1) - 1

    @pl.when(n == 0)
    def _():
        s_ref[...] = jnp.zeros_like(s_ref)

    qg = qg_ref[...] * (ATTN_SCALE * LOG2E)
    kg = kg_ref[...]
    row = lax.broadcasted_iota(jnp.int32, (BLK, 2 * BLK), 0)
    col = lax.broadcasted_iota(jnp.int32, (BLK, 2 * BLK), 1)
    cur_ok = (col >= BLK) & ((col - BLK) <= row)
    prev_ok = (col < BLK) & (col >= row)
    kp = kp_ref[...].astype(F32)
    kprev = jnp.concatenate([_rms(kp[:, h * HEAD_DIM:(h + 1) * HEAD_DIM], kg) for h in range(N_KV_HEADS)],
                            axis=1).astype(BF16)
    vprev = vp_ref[...]
    for blk in range(MIX_ROWS // BLK):
        rows = slice(blk * BLK, (blk + 1) * BLK)
        mask = cur_ok | (prev_ok & (n > 0)) if blk == 0 else cur_ok | prev_ok
        k = z_ref[rows, COL_AK:COL_AK + KV_WIDTH].astype(F32)
        kn = [_rms(k[:, h * HEAD_DIM:(h + 1) * HEAD_DIM], kg) for h in range(N_KV_HEADS)]
        kcur = jnp.concatenate(kn, axis=1).astype(BF16)
        vcur = z_ref[rows, COL_AV:COL_AV + KV_WIDTH]
        for kh in range(N_KV_HEADS):
            sl = slice(kh * HEAD_DIM, (kh + 1) * HEAD_DIM)
            k_cat = jnp.concatenate([kprev[:, sl], kcur[:, sl]], axis=0)
            v_cat = jnp.concatenate([vprev[:, sl], vcur[:, sl]], axis=0)
            for g in range(GQA_GROUP):
                h = kh * GQA_GROUP + g
                q = z_ref[rows, COL_AQ + h * HEAD_DIM:COL_AQ + (h + 1) * HEAD_DIM].astype(F32)
                s = jnp.where(mask, _dot_nt(_rms(q, qg).astype(BF16), k_cat), -jnp.inf)
                sink = sink_ref[h] * LOG2E
                m = jnp.maximum(jnp.max(s, axis=-1, keepdims=True), sink)
                e = jnp.exp2(s - m)
                denom = jnp.sum(e, axis=-1, keepdims=True) + jnp.exp2(sink - m)
                mix_ref[rows, h * HEAD_DIM:(h + 1) * HEAD_DIM] = (
                    _dot(e.astype(BF16), v_cat) / denom).astype(mix_ref.dtype)
        kprev, vprev = kcur, vcur

    @pl.when(n == last)
    def _():
        kwin_ref[0] = jnp.concatenate(kn, axis=1)
        vwin_ref[0] = vcur.astype(F32)

    tril = (lax.broadcasted_iota(jnp.int32, (MIX_ROWS, MIX_ROWS), 0)
            >= lax.broadcasted_iota(jnp.int32, (MIX_ROWS, MIX_ROWS), 1))
    cos2 = cos_ref[...]
    sin2 = sin_ref[...]
    for h in range(N_RET_HEADS):
        hs = slice(h * RET_DIM, (h + 1) * RET_DIM)
        col_of = lambda c: slice(c + h * RET_DIM, c + (h + 1) * RET_DIM)
        q = (_rotary(z_ref[:, col_of(COL_RQ)].astype(F32), cos2, sin2) * qdec_ref[:, hs]).astype(BF16)
        k = (_rotary(z_ref[:, col_of(COL_RK)].astype(F32), cos2, sin2) * kdec_ref[:, hs]).astype(BF16)
        v = z_ref[:, col_of(COL_RV)]
        state = s_ref[h]
        sc = jnp.where(tril, _dot_nt(q, k), 0.0)
        o = _dot(sc.astype(BF16), v) + _dot(q, state.astype(BF16))
        s_ref[h] = (state + _dot_tn(k, v)) * cdec_ref[h]
        o = o * lax.rsqrt(jnp.mean(o * o, axis=-1, keepdims=True) + NORM_EPS)
        gate = z_ref[:, col_of(COL_RG)].astype(F32)
        mix_ref[:, ATTN_WIDTH + h * RET_DIM:ATTN_WIDTH + (h + 1) * RET_DIM] = (
            o * rg_ref[:, hs] * (gate * jax.nn.sigmoid(gate))).astype(mix_ref.dtype)

    @pl.when(n == last)
    def _():
        sout_ref[0] = s_ref[...]


def _rope_tables(pos):
    half = RET_DIM // 2
    inv = ROPE_BASE ** (-np.arange(half, dtype=np.float64) / half)
    ang = np.asarray(pos, np.float64)[:, None] * inv[None, :]
    cos, sin = np.cos(ang), np.sin(ang)
    return (np.concatenate([cos, cos], axis=-1).astype(np.float32),
            np.concatenate([-sin, sin], axis=-1).astype(np.float32))


def _ret_log_decay():
    return np.log1p(-np.exp2(-5.0 - np.arange(N_RET_HEADS, dtype=np.float64)))


def _mixer_prompt(z, qg, kg, sinks, rg, batch, seq):
    nsteps = seq // MIX_ROWS
    cos2, sin2 = _rope_tables(np.arange(seq))
    lg = _ret_log_decay()
    steps = np.arange(MIX_ROWS, dtype=np.float64)[:, None, None] + 1.0
    lanes = lambda t: np.broadcast_to(t, (MIX_ROWS, N_RET_HEADS, RET_DIM)).reshape(MIX_ROWS, RET_WIDTH).astype(np.float32)
    q_dec = lanes(np.exp(lg[None, :, None] * steps))
    k_dec = lanes(np.exp(-lg[None, :, None] * steps) * RET_K_SCALE)
    chunk_dec = np.exp(lg * MIX_ROWS).astype(np.float32)
    const = lambda shape: pl.BlockSpec(shape, lambda b, n: (0,) * len(shape))
    smem = pl.BlockSpec(memory_space=pltpu.SMEM)
    win = jax.ShapeDtypeStruct((batch, BLK, KV_WIDTH), F32)
    blks = MIX_ROWS // BLK
    prev_blk = lambda b, n: (b * nsteps + n) * blks - jnp.minimum(n, 1)
    return pl.pallas_call(
        _mixer_prompt_kernel,
        grid=(batch, nsteps),
        in_specs=[
            pl.BlockSpec((MIX_ROWS, IN_WIDTH), lambda b, n: (b * nsteps + n, 0)),
            pl.BlockSpec((BLK, KV_WIDTH), lambda b, n: (prev_blk(b, n), COL_AK // KV_WIDTH)),
            pl.BlockSpec((BLK, KV_WIDTH), lambda b, n: (prev_blk(b, n), COL_AV // KV_WIDTH)),
            const((1, HEAD_DIM)), const((1, HEAD_DIM)), smem,
            pl.BlockSpec((MIX_ROWS, RET_DIM), lambda b, n: (n, 0)),
            pl.BlockSpec((MIX_ROWS, RET_DIM), lambda b, n: (n, 0)),
            const((MIX_ROWS, RET_WIDTH)), const((MIX_ROWS, RET_WIDTH)), smem,
            const((1, RET_WIDTH)),
        ],
        out_specs=[
            pl.BlockSpec((MIX_ROWS, D_MODEL), lambda b, n: (b * nsteps + n, 0)),
            pl.BlockSpec((1, BLK, KV_WIDTH), lambda b, n: (b, 0, 0)),
            pl.BlockSpec((1, BLK, KV_WIDTH), lambda b, n: (b, 0, 0)),
            pl.BlockSpec((1, N_RET_HEADS, RET_DIM, RET_DIM), lambda b, n: (b, 0, 0, 0)),
        ],
        out_shape=[jax.ShapeDtypeStruct((batch * seq, D_MODEL), BF16), win, win,
                   jax.ShapeDtypeStruct((batch, N_RET_HEADS, RET_DIM, RET_DIM), F32)],
        scratch_shapes=[pltpu.VMEM((N_RET_HEADS, RET_DIM, RET_DIM), F32)],
        compiler_params=_params("arbitrary", "arbitrary"),
        name="mixer_prompt",
    )(z, z, z, qg, kg, sinks, cos2, sin2, q_dec, k_dec, chunk_dec, rg)


SAMPLE_GROUP = 8


def _attn_sample_kernel(q_ref, kn_ref, vn_ref, kbuf_ref, vbuf_ref, qg_ref, kg_ref, sink_ref,
                        o_ref, kout_ref, vout_ref):
    rows = WINDOW * N_KV_HEADS
    qg = qg_ref[...]
    kg = kg_ref[...]
    sink = sink_ref[...]
    head = lax.broadcasted_iota(jnp.int32, (N_ATTN_HEADS, HEAD_DIM), 0)
    first = head < GQA_GROUP
    own = ((lax.broadcasted_iota(jnp.int32, (N_ATTN_HEADS, rows), 1) % N_KV_HEADS)
           == (lax.broadcasted_iota(jnp.int32, (N_ATTN_HEADS, rows), 0) // GQA_GROUP))
    row = lax.broadcasted_iota(jnp.int32, (rows, HEAD_DIM), 0)
    for b in range(SAMPLE_GROUP):
        qn = _rms(q_ref[b], qg) * ATTN_SCALE
        kn = _rms(kn_ref[b], kg)
        vn = vn_ref[b]
        kbuf = kbuf_ref[b]
        vbuf = vbuf_ref[b]
        s = jnp.where(own, _dot_nt(qn.astype(BF16), kbuf.astype(BF16)), -jnp.inf)
        k_sel = jnp.where(first, kn[0:1], kn[1:2])
        v_sel = jnp.where(first, vn[0:1], vn[1:2])
        s_new = jnp.sum(qn * k_sel, axis=-1, keepdims=True)
        m = jnp.maximum(jnp.maximum(jnp.max(s, axis=-1, keepdims=True), s_new), sink)
        e = jnp.exp(s - m)
        e_new = jnp.exp(s_new - m)
        denom = jnp.sum(e, axis=-1, keepdims=True) + e_new + jnp.exp(sink - m)
        o_ref[b] = (_dot(e.astype(BF16), vbuf.astype(BF16)) + e_new * v_sel) / denom
        for buf, new, out in ((kbuf, kn, kout_ref), (vbuf, vn, vout_ref)):
            shifted = pltpu.roll(buf, rows - N_KV_HEADS, 0)
            out[b] = jnp.where(row == rows - 2, new[0:1], jnp.where(row == rows - 1, new[1:2], shifted))


def _attn_sample(q, kn, vn, kbuf, vbuf, qg, kg, sinks):
    nb = q.shape[0]
    rows = WINDOW * N_KV_HEADS
    blk = lambda *shape: pl.BlockSpec((SAMPLE_GROUP,) + shape, lambda i: (i,) + (0,) * len(shape))
    const = lambda shape: pl.BlockSpec(shape, lambda i: (0,) * len(shape))
    return pl.pallas_call(
        _attn_sample_kernel,
        grid=(nb // SAMPLE_GROUP,),
        in_specs=[
            blk(N_ATTN_HEADS, HEAD_DIM), blk(N_KV_HEADS, HEAD_DIM), blk(N_KV_HEADS, HEAD_DIM),
            blk(rows, HEAD_DIM), blk(rows, HEAD_DIM),
            const((1, HEAD_DIM)), const((1, HEAD_DIM)), const((N_ATTN_HEADS, 1)),
        ],
        out_specs=[blk(N_ATTN_HEADS, HEAD_DIM), blk(rows, HEAD_DIM), blk(rows, HEAD_DIM)],
        out_shape=[jax.ShapeDtypeStruct((nb, N_ATTN_HEADS, HEAD_DIM), F32),
                   jax.ShapeDtypeStruct((nb, rows, HEAD_DIM), F32),
                   jax.ShapeDtypeStruct((nb, rows, HEAD_DIM), F32)],
        compiler_params=_params("parallel"),
        name="attn_sample",
    )(q, kn, vn, kbuf, vbuf, qg, kg, sinks)


def _ret_sample_kernel(qt_ref, kt_ref, v_ref, gate_ref, s0_ref, cos_ref, sin_ref, gamma_ref, rg_ref,
                       o_ref, s_out_ref):
    cos2 = cos_ref[...]
    sin2 = sin_ref[...]
    rot = lambda x: x * cos2 + pltpu.roll(x, RET_DIM // 2, 0) * sin2
    for b in range(SAMPLE_GROUP):
        qt = rot(qt_ref[b])
        kt = rot(kt_ref[b]) * RET_K_SCALE
        v = v_ref[b]
        gate = gate_ref[b]
        rows = []
        for h in range(N_RET_HEADS):
            s_new = s0_ref[b, h] * gamma_ref[h] + kt[:, h:h + 1] * v[h:h + 1, :]
            s_out_ref[b, h] = s_new
            o = jnp.sum(qt[:, h:h + 1] * s_new, axis=0, keepdims=True)
            rows.append(o * lax.rsqrt(jnp.mean(o * o, axis=-1, keepdims=True) + NORM_EPS))
        o = jnp.concatenate(rows, axis=0)
        o_ref[b] = o * rg_ref[...] * (gate * jax.nn.sigmoid(gate))


def _ret_sample(qt, kt, v, gate, s0, rg):
    nb = qt.shape[0]
    cos2, sin2 = _rope_tables(PAST_LEN + np.arange(1))
    gamma = np.exp(_ret_log_decay()).astype(np.float32)
    blk = lambda *shape: pl.BlockSpec((SAMPLE_GROUP,) + shape, lambda i: (i,) + (0,) * len(shape))
    const = lambda shape: pl.BlockSpec(shape, lambda i: (0,) * len(shape))
    return pl.pallas_call(
        _ret_sample_kernel,
        grid=(nb // SAMPLE_GROUP,),
        in_specs=[
            blk(RET_DIM, N_RET_HEADS), blk(RET_DIM, N_RET_HEADS), blk(N_RET_HEADS, RET_DIM),
            blk(N_RET_HEADS, RET_DIM), blk(N_RET_HEADS, RET_DIM, RET_DIM),
            const((RET_DIM, 1)), const((RET_DIM, 1)),
            pl.BlockSpec(memory_space=pltpu.SMEM),
            const((N_RET_HEADS, RET_DIM)),
        ],
        out_specs=[blk(N_RET_HEADS, RET_DIM), blk(N_RET_HEADS, RET_DIM, RET_DIM)],
        out_shape=[jax.ShapeDtypeStruct((nb, N_RET_HEADS, RET_DIM), F32),
                   jax.ShapeDtypeStruct((nb, N_RET_HEADS, RET_DIM, RET_DIM), F32)],
        compiler_params=_params("parallel"),
        name="ret_sample",
    )(qt, kt, v, gate, s0, cos2.reshape(RET_DIM, 1), sin2.reshape(RET_DIM, 1), gamma,
      rg.reshape(N_RET_HEADS, RET_DIM))


def _whole(a):
    return pl.BlockSpec(a.shape, lambda *_: (0,) * a.ndim)


def _out_proj_kernel(h_ref, mix_ref, hs_ref, mixs_ref, w_ref, o_ref, os_ref):
    tm = h_ref.shape[0]

    @pl.when(pl.program_id(0) == 0)
    def _():
        y = _dot(jnp.concatenate([mix_ref[...], mixs_ref[...]], axis=0), w_ref[...])
        o_ref[...] = h_ref[...] + y[:tm]
        os_ref[...] = hs_ref[...] + y[tm:]

    @pl.when(pl.program_id(0) > 0)
    def _():
        o_ref[...] = h_ref[...] + _dot(mix_ref[...], w_ref[...])


def _out_proj(h, mix, hs, mixs, w_bf16, tm):
    m = h.shape[0]
    row = pl.BlockSpec((tm, D_MODEL), lambda i: (i, 0))
    return pl.pallas_call(
        _out_proj_kernel,
        grid=(m // tm,),
        in_specs=[row, row, _whole(hs), _whole(mixs), _whole(w_bf16)],
        out_specs=[row, _whole(hs)],
        out_shape=[jax.ShapeDtypeStruct((m, D_MODEL), F32), jax.ShapeDtypeStruct(hs.shape, F32)],
        compiler_params=_params("arbitrary"),
        name="out_proj",
    )(h, mix, hs, mixs, w_bf16)


def _ffn_kernel(h_ref, hs_ref, g_ref, wg_ref, wu_ref, wd_ref, o_ref, os_ref, f_ref):
    i, j = pl.program_id(0), pl.program_id(1)
    tm = h_ref.shape[0]

    @pl.when(j == 0)
    def _():
        h = h_ref[...]
        f_ref[:tm] = _rms(h, g_ref[...]).astype(BF16)
        o_ref[...] = h

    @pl.when((i == 0) & (j == 0))
    def _():
        hs = hs_ref[...]
        f_ref[tm:] = _rms(hs, g_ref[...]).astype(BF16)
        os_ref[...] = hs

    def swiglu(f):
        gate = _dot(f, wg_ref[...])
        act = (gate * jax.nn.sigmoid(gate) * _dot(f, wu_ref[...])).astype(BF16)
        return _dot(act, wd_ref[...])

    @pl.when(i == 0)
    def _():
        y = swiglu(f_ref[...])
        o_ref[...] += y[:tm]
        os_ref[...] += y[tm:]

    @pl.when(i > 0)
    def _():
        o_ref[...] += swiglu(f_ref[:tm])


def _ffn(h, hs, g, wg, wu, wd, tm):
    m = h.shape[0]
    row = pl.BlockSpec((tm, D_MODEL), lambda i, j: (i, 0))
    return pl.pallas_call(
        _ffn_kernel,
        grid=(m // tm, D_FF // FF_TILE),
        in_specs=[
            row, _whole(hs), _whole(g),
            pl.BlockSpec((None, D_MODEL, FF_TILE), lambda i, j: (j, 0, 0)),
            pl.BlockSpec((None, D_MODEL, FF_TILE), lambda i, j: (j, 0, 0)),
            pl.BlockSpec((FF_TILE, D_MODEL), lambda i, j: (j, 0)),
        ],
        out_specs=[row, _whole(hs)],
        out_shape=[jax.ShapeDtypeStruct((m, D_MODEL), F32), jax.ShapeDtypeStruct(hs.shape, F32)],
        scratch_shapes=[pltpu.VMEM((tm + hs.shape[0], D_MODEL), BF16)],
        compiler_params=_params("arbitrary", "arbitrary"),
        name="ffn",
    )(h, hs, g, wg, wu, wd)


def _ple_kernel(h_ref, p_ref, hs_ref, ps_ref, g_ref, wp_ref, wg_ref, o_ref, os_ref):
    tm = h_ref.shape[0]

    def embed(h, p):
        gate = jax.nn.sigmoid(_dot(_rms(h, g_ref[...]).astype(BF16), wg_ref[...]))
        return h + _dot(p.astype(BF16), wp_ref[...]) * gate

    @pl.when(pl.program_id(0) == 0)
    def _():
        y = embed(jnp.concatenate([h_ref[...], hs_ref[...]], axis=0),
                  jnp.concatenate([p_ref[...], ps_ref[...]], axis=0))
        o_ref[...] = y[:tm]
        os_ref[...] = y[tm:]

    @pl.when(pl.program_id(0) > 0)
    def _():
        o_ref[...] = embed(h_ref[...], p_ref[...])


def _ple(h, p, hs, ps, g, w_ple, w_pg, tm):
    m = h.shape[0]
    row = pl.BlockSpec((tm, D_MODEL), lambda i: (i, 0))
    return pl.pallas_call(
        _ple_kernel,
        grid=(m // tm,),
        in_specs=[row, pl.BlockSpec((tm, PLE_DIM), lambda i: (i, 0)), _whole(hs), _whole(ps), _whole(g),
                  _whole(w_ple), _whole(w_pg)],
        out_specs=[row, _whole(hs)],
        out_shape=[jax.ShapeDtypeStruct((m, D_MODEL), F32), jax.ShapeDtypeStruct(hs.shape, F32)],
        compiler_params=_params("arbitrary"),
        name="ple",
    )(h, p, hs, ps, g, w_ple, w_pg)


DENSE_ROWS = 512
FFN_ROWS = 1024


def _dense_tail(h, mix, p, hs, mixs, ps, w):
    h, hs = _out_proj(h, mix, hs, mixs, w["w_out"], DENSE_ROWS)
    h, hs = _ffn(h, hs, w["ffn_norm_g"], w["w_gate"], w["w_up"], w["w_down"], FFN_ROWS)
    return _ple(h, p, hs, ps, w["ple_norm_g"], w["w_ple"], w["w_ple_gate"], DENSE_ROWS)


def _mix_sample(z, k_buf, v_buf, s0, w):
    nb = z.shape[0]
    heads = lambda c, n: z[:, c:c + n * HEAD_DIM].reshape(nb, n, HEAD_DIM)
    o_attn, k_win, v_win = _attn_sample(
        heads(COL_AQ, N_ATTN_HEADS), heads(COL_AK, N_KV_HEADS), heads(COL_AV, N_KV_HEADS),
        k_buf.reshape(nb, WINDOW * N_KV_HEADS, HEAD_DIM), v_buf.reshape(nb, WINDOW * N_KV_HEADS, HEAD_DIM),
        w["q_norm_g"], w["k_norm_g"], w["attn_sinks"].reshape(N_ATTN_HEADS, 1))
    o_ret, state = _ret_sample(
        heads(COL_RQ, N_RET_HEADS).transpose(0, 2, 1), heads(COL_RK, N_RET_HEADS).transpose(0, 2, 1),
        heads(COL_RV, N_RET_HEADS), heads(COL_RG, N_RET_HEADS), s0, w["ret_out_g"])
    mix = jnp.concatenate([o_attn.reshape(nb, ATTN_WIDTH), o_ret.reshape(nb, RET_WIDTH)], axis=1).astype(BF16)
    return (mix, k_win.reshape(nb, WINDOW, N_KV_HEADS, HEAD_DIM), v_win.reshape(nb, WINDOW, N_KV_HEADS, HEAD_DIM),
            state)


def kernel(x_prompt, x_sample, cache_k_win, cache_v_win, state_ret, p_prompt, p_sample,
           attn_norm_g, w_in, q_norm_g, k_norm_g, attn_sinks, ret_out_g, w_out,
           ffn_norm_g, w_gate, w_up, w_down, ple_norm_g, w_ple, w_ple_gate):
    assert w_in.shape[0] == 1 and x_sample.shape[1] == 1 and cache_k_win.shape[2] == WINDOW
    y_p, y_s = x_prompt, x_sample
    outs = [[] for _ in range(6)]
    for l in range(w_in.shape[0]):
        w = {
            "attn_norm_g": attn_norm_g[l].reshape(1, D_MODEL),
            "q_norm_g": q_norm_g[l].reshape(1, HEAD_DIM),
            "k_norm_g": k_norm_g[l].reshape(1, HEAD_DIM),
            "attn_sinks": attn_sinks[l],
            "ret_out_g": ret_out_g[l].reshape(1, RET_WIDTH),
            "ffn_norm_g": ffn_norm_g[l].reshape(1, D_MODEL),
            "ple_norm_g": ple_norm_g[l].reshape(1, D_MODEL),
        }
        batch, seq, _ = y_p.shape
        nb = y_s.shape[0]
        h_p = y_p.reshape(batch * seq, D_MODEL)
        h_s = y_s.reshape(nb, D_MODEL)
        z_p, z_s, *cast = _in_proj(h_p, h_s, w["attn_norm_g"], w_in[l], w_gate[l], w_up[l], w_down[l],
                                   w_out[l], w_ple_gate[l], w_ple[l])
        w.update(zip(("w_gate", "w_up", "w_down", "w_out", "w_ple_gate", "w_ple"), cast))
        mix_p, kp, vp, sp = _mixer_prompt(z_p, w["q_norm_g"], w["k_norm_g"], w["attn_sinks"], w["ret_out_g"],
                                          batch, seq)
        mix_s, ks, vs, ss = _mix_sample(z_s, cache_k_win[l], cache_v_win[l], state_ret[l], w)
        h_p, h_s = _dense_tail(h_p, mix_p, p_prompt[l].reshape(batch * seq, PLE_DIM),
                               h_s, mix_s, p_sample[l].reshape(nb, PLE_DIM), w)
        y_p, y_s = h_p.reshape(batch, seq, D_MODEL), h_s.reshape(nb, 1, D_MODEL)
        kp = kp.reshape(batch, BLK, N_KV_HEADS, HEAD_DIM)
        vp = vp.reshape(batch, BLK, N_KV_HEADS, HEAD_DIM)
        for acc, val in zip(outs, (kp, vp, sp, ks, vs, ss)):
            acc.append(val)
    kp, vp, sp, ks, vs, ss = (jnp.stack(o) for o in outs)
    return (y_p, y_s, kp, vp, sp, ks, vs, ss)
```

```python
import numpy as np
import jax
import jax.numpy as jnp
from jax import lax
from jax.experimental import pallas as pl
from jax.experimental.pallas import tpu as pltpu

D_MODEL = 2048
HEAD_DIM = 128
N_ATTN_HEADS = 8
N_KV_HEADS = 2
GQA_GROUP = N_ATTN_HEADS // N_KV_HEADS
ATTN_WIDTH = N_ATTN_HEADS * HEAD_DIM
KV_WIDTH = N_KV_HEADS * HEAD_DIM
WINDOW = 128
BLK = 128
N_RET_HEADS = 8
RET_DIM = 128
RET_WIDTH = N_RET_HEADS * RET_DIM
D_FF = 5632
IN_WIDTH = 5632
PLE_DIM = 256
PAST_LEN = 16384
ROPE_BASE = 10000.0
NORM_EPS = 1e-6
ATTN_SCALE = HEAD_DIM ** -0.5
RET_K_SCALE = RET_DIM ** -0.5

COL_RQ, COL_RK, COL_RV, COL_RG = 0, RET_WIDTH, 2 * RET_WIDTH, 3 * RET_WIDTH
COL_AQ = 4 * RET_WIDTH
COL_AK = COL_AQ + ATTN_WIDTH
COL_AV = COL_AK + KV_WIDTH
IN_TILE = 512
IN_TILES = IN_WIDTH // IN_TILE
IN_ROT = (ATTN_WIDTH + 2 * KV_WIDTH) // IN_TILE

VMEM_LIMIT = 56 * 1024 * 1024

F32 = jnp.float32
BF16 = jnp.bfloat16


def _params(*sem):
    return pltpu.CompilerParams(dimension_semantics=sem, vmem_limit_bytes=VMEM_LIMIT)


def _rms(x, g):
    return x * lax.rsqrt(jnp.mean(x * x, axis=-1, keepdims=True) + NORM_EPS) * g


def _dot(a, b):
    return jnp.dot(a, b, preferred_element_type=F32)


def _dot_nt(a, b):
    return lax.dot_general(a, b, (((1,), (1,)), ((), ())), preferred_element_type=F32)


def _dot_tn(a, b):
    return lax.dot_general(a, b, (((0,), (0,)), ((), ())), preferred_element_type=F32)


def _in_proj_kernel(x_hbm, xs_ref, g_ref, w_ref, wg_ref, wu_ref, wd_ref, wo_ref, wpg_ref, wple_ref,
                    z_ref, zs_ref, wg_o, wu_o, wd_o, wo_o, wpg_o, wple_o,
                    xbuf, xsem, a_ref, as_ref, wres):
    i, j = pl.program_id(0), pl.program_id(1)

    def x_copy(tile):
        return pltpu.make_async_copy(x_hbm.at[pl.ds(tile * IN_ROWS, IN_ROWS)], xbuf, xsem)

    @pl.when((i == 0) & (j == 0))
    def _():
        x_copy(0).start()
        as_ref[...] = _rms(xs_ref[...], g_ref[...]).astype(BF16)

    @pl.when(j == 0)
    def _():
        x_copy(i).wait()
        a_ref[...] = _rms(xbuf[...], g_ref[...]).astype(BF16)

        @pl.when(i + 1 < pl.num_programs(0))
        def _():
            x_copy(i + 1).start()

    @pl.when(i == 0)
    def _():
        wres[j] = w_ref[...].astype(BF16)
        zs_ref[...] = _dot(as_ref[...], wres[j])

    z_ref[...] = _dot(a_ref[...], wres[j]).astype(z_ref.dtype)
    wg_o[...] = wg_ref[...].astype(BF16)
    wu_o[...] = wu_ref[...].astype(BF16)
    wd_o[...] = wd_ref[...].astype(BF16)

    @pl.when(j < CAST_SPLIT)
    def _():
        wo_o[...] = wo_ref[...].astype(BF16)
        wpg_o[...] = wpg_ref[...].astype(BF16)

    @pl.when(j == 0)
    def _():
        wple_o[...] = wple_ref[...].astype(BF16)


IN_ROWS = 1024
CAST_SPLIT = 8
FF_TILE = 512


def _in_proj(x, xs, g, w_in, w_gate, w_up, w_down, w_out, w_pg, w_ple):
    m = x.shape[0]
    ni = m // IN_ROWS
    steps = ni * IN_TILES
    assert m % IN_ROWS == 0 and FF_TILE == IN_TILE and D_FF == IN_WIDTH
    gu_rows, d_rows = D_MODEL // ni, D_FF // steps
    sq_rows, ple_rows = D_MODEL // (ni * CAST_SPLIT), PLE_DIM // ni
    assert gu_rows * ni == D_MODEL and d_rows * steps == D_FF and sq_rows % 16 == 0 and ple_rows % 16 == 0
    sq_idx = lambda i, j: (i * CAST_SPLIT + jnp.minimum(j, CAST_SPLIT - 1), 0)
    first = lambda i, j: jnp.where(i == 0, j, IN_TILES - 1)
    return pl.pallas_call(
        _in_proj_kernel,
        grid=(ni, IN_TILES),
        in_specs=[
            pl.BlockSpec(memory_space=pl.ANY),
            pl.BlockSpec(xs.shape, lambda i, j: (0, 0)),
            pl.BlockSpec((1, D_MODEL), lambda i, j: (0, 0)),
            pl.BlockSpec((D_MODEL, IN_TILE), lambda i, j: (0, (first(i, j) + IN_ROT) % IN_TILES)),
            pl.BlockSpec((gu_rows, FF_TILE), lambda i, j: (i, j)),
            pl.BlockSpec((gu_rows, FF_TILE), lambda i, j: (i, j)),
            pl.BlockSpec((d_rows, D_MODEL), lambda i, j: (i * IN_TILES + j, 0)),
            pl.BlockSpec((sq_rows, D_MODEL), sq_idx),
            pl.BlockSpec((sq_rows, D_MODEL), sq_idx),
            pl.BlockSpec((ple_rows, D_MODEL), lambda i, j: (i, 0)),
        ],
        out_specs=[
            pl.BlockSpec((IN_ROWS, IN_TILE), lambda i, j: (i, j)),
            pl.BlockSpec((xs.shape[0], IN_TILE), lambda i, j: (0, first(i, j))),
            pl.BlockSpec((None, gu_rows, FF_TILE), lambda i, j: (j, i, 0)),
            pl.BlockSpec((None, gu_rows, FF_TILE), lambda i, j: (j, i, 0)),
            pl.BlockSpec((d_rows, D_MODEL), lambda i, j: (i * IN_TILES + j, 0)),
            pl.BlockSpec((sq_rows, D_MODEL), sq_idx),
            pl.BlockSpec((sq_rows, D_MODEL), sq_idx),
            pl.BlockSpec((ple_rows, D_MODEL), lambda i, j: (i, 0)),
        ],
        out_shape=[
            jax.ShapeDtypeStruct((m, IN_WIDTH), BF16),
            jax.ShapeDtypeStruct((xs.shape[0], IN_WIDTH), F32),
            jax.ShapeDtypeStruct((D_FF // FF_TILE, D_MODEL, FF_TILE), BF16),
            jax.ShapeDtypeStruct((D_FF // FF_TILE, D_MODEL, FF_TILE), BF16),
            jax.ShapeDtypeStruct((D_FF, D_MODEL), BF16),
            jax.ShapeDtypeStruct((D_MODEL, D_MODEL), BF16),
            jax.ShapeDtypeStruct((D_MODEL, D_MODEL), BF16),
            jax.ShapeDtypeStruct((PLE_DIM, D_MODEL), BF16),
        ],
        scratch_shapes=[
            pltpu.VMEM((IN_ROWS, D_MODEL), F32),
            pltpu.SemaphoreType.DMA(()),
            pltpu.VMEM((IN_ROWS, D_MODEL), BF16),
            pltpu.VMEM((xs.shape[0], D_MODEL), BF16),
            pltpu.VMEM((IN_TILES, D_MODEL, IN_TILE), BF16),
        ],
        compiler_params=_params("arbitrary", "arbitrary"),
        name="in_proj",
    )(x, xs, g, w_in, w_gate, w_up, w_down, w_out, w_pg, w_ple)


MIX_ROWS = 256
LOG2E = 1.4426950408889634


def _rotary(x, cos2, sin2):
    return x * cos2 + pltpu.roll(x, RET_DIM // 2, 1) * sin2


def _mixer_prompt_kernel(z_ref, kp_ref, vp_ref, qg_ref, kg_ref, sink_ref, cos_ref, sin_ref, qdec_ref, kdec_ref,
                         cdec_ref, rg_ref, mix_ref, kwin_ref, vwin_ref, sout_ref, s_ref):
    n = pl.program_id(1)
    last = pl.num_programs(1) - 1

    @pl.when(n == 0)
    def _():
        s_ref[...] = jnp.zeros_like(s_ref)

    qg = qg_ref[...] * (ATTN_SCALE * LOG2E)
    kg = kg_ref[...]
    row = lax.broadcasted_iota(jnp.int32, (BLK, 2 * BLK), 0)
    col = lax.broadcasted_iota(jnp.int32, (BLK, 2 * BLK), 1)
    cur_ok = (col >= BLK) & ((col - BLK) <= row)
    prev_ok = (col < BLK) & (col >= row)
    kp = kp_ref[...].astype(F32)
    kprev = jnp.concatenate([_rms(kp[:, h * HEAD_DIM:(h + 1) * HEAD_DIM], kg) for h in range(N_KV_HEADS)],
                            axis=1).astype(BF16)
    vprev = vp_ref[...]
    for blk in range(MIX_ROWS // BLK):
        rows = slice(blk * BLK, (blk + 1) * BLK)
        mask = cur_ok | (prev_ok & (n > 0)) if blk == 0 else cur_ok | prev_ok
        k = z_ref[rows, COL_AK:COL_AK + KV_WIDTH].astype(F32)
        kn = [_rms(k[:, h * HEAD_DIM:(h + 1) * HEAD_DIM], kg) for h in range(N_KV_HEADS)]
        kcur = jnp.concatenate(kn, axis=1).astype(BF16)
        vcur = z_ref[rows, COL_AV:COL_AV + KV_WIDTH]
        for kh in range(N_KV_HEADS):
            sl = slice(kh * HEAD_DIM, (kh + 1) * HEAD_DIM)
            k_cat = jnp.concatenate([kprev[:, sl], kcur[:, sl]], axis=0)
            v_cat = jnp.concatenate([vprev[:, sl], vcur[:, sl]], axis=0)
            for g in range(GQA_GROUP):
                h = kh * GQA_GROUP + g
                q = z_ref[rows, COL_AQ + h * HEAD_DIM:COL_AQ + (h + 1) * HEAD_DIM].astype(F32)
                s = jnp.where(mask, _dot_nt(_rms(q, qg).astype(BF16), k_cat), -jnp.inf)
                sink = sink_ref[h] * LOG2E
                m = jnp.maximum(jnp.max(s, axis=-1, keepdims=True), sink)
                e = jnp.exp2(s - m)
                denom = jnp.sum(e, axis=-1, keepdims=True) + jnp.exp2(sink - m)
                mix_ref[rows, h * HEAD_DIM:(h + 1) * HEAD_DIM] = (
                    _dot(e.astype(BF16), v_cat) / denom).astype(mix_ref.dtype)
        kprev, vprev = kcur, vcur

    @pl.when(n == last)
    def _():
        kwin_ref[0] = jnp.concatenate(kn, axis=1)
        vwin_ref[0] = vcur.astype(F32)

    tril = (lax.broadcasted_iota(jnp.int32, (MIX_ROWS, MIX_ROWS), 0)
            >= lax.broadcasted_iota(jnp.int32, (MIX_ROWS, MIX_ROWS), 1))
    cos2 = cos_ref[...]
    sin2 = sin_ref[...]
    for h in range(N_RET_HEADS):
        hs = slice(h * RET_DIM, (h + 1) * RET_DIM)
        col_of = lambda c: slice(c + h * RET_DIM, c + (h + 1) * RET_DIM)
        q = (_rotary(z_ref[:, col_of(COL_RQ)].astype(F32), cos2, sin2) * qdec_ref[:, hs]).astype(BF16)
        k = (_rotary(z_ref[:, col_of(COL_RK)].astype(F32), cos2, sin2) * kdec_ref[:, hs]).astype(BF16)
        v = z_ref[:, col_of(COL_RV)]
        state = s_ref[h]
        sc = jnp.where(tril, _dot_nt(q, k), 0.0)
        o = _dot(sc.astype(BF16), v) + _dot(q, state.astype(BF16))
        s_ref[h] = (state + _dot_tn(k, v)) * cdec_ref[h]
        o = o * lax.rsqrt(jnp.mean(o * o, axis=-1, keepdims=True) + NORM_EPS)
        gate = z_ref[:, col_of(COL_RG)].astype(F32)
        mix_ref[:, ATTN_WIDTH + h * RET_DIM:ATTN_WIDTH + (h + 1) * RET_DIM] = (
            o * rg_ref[:, hs] * (gate * jax.nn.sigmoid(gate))).astype(mix_ref.dtype)

    @pl.when(n == last)
    def _():
        sout_ref[0] = s_ref[...]


def _rope_tables(pos):
    half = RET_DIM // 2
    inv = ROPE_BASE ** (-np.arange(half, dtype=np.float64) / half)
    ang = np.asarray(pos, np.float64)[:, None] * inv[None, :]
    cos, sin = np.cos(ang), np.sin(ang)
    return (np.concatenate([cos, cos], axis=-1).astype(np.float32),
            np.concatenate([-sin, sin], axis=-1).astype(np.float32))


def _ret_log_decay():
    return np.log1p(-np.exp2(-5.0 - np.arange(N_RET_HEADS, dtype=np.float64)))


def _mixer_prompt(z, qg, kg, sinks, rg, batch, seq):
    nsteps = seq // MIX_ROWS
    cos2, sin2 = _rope_tables(np.arange(seq))
    lg = _ret_log_decay()
    steps = np.arange(MIX_ROWS, dtype=np.float64)[:, None, None] + 1.0
    lanes = lambda t: np.broadcast_to(t, (MIX_ROWS, N_RET_HEADS, RET_DIM)).reshape(MIX_ROWS, RET_WIDTH).astype(np.float32)
    q_dec = lanes(np.exp(lg[None, :, None] * steps))
    k_dec = lanes(np.exp(-lg[None, :, None] * steps) * RET_K_SCALE)
    chunk_dec = np.exp(lg * MIX_ROWS).astype(np.float32)
    const = lambda shape: pl.BlockSpec(shape, lambda b, n: (0,) * len(shape))
    smem = pl.BlockSpec(memory_space=pltpu.SMEM)
    win = jax.ShapeDtypeStruct((batch, BLK, KV_WIDTH), F32)
    blks = MIX_ROWS // BLK
    prev_blk = lambda b, n: (b * nsteps + n) * blks - jnp.minimum(n, 1)
    return pl.pallas_call(
        _mixer_prompt_kernel,
        grid=(batch, nsteps),
        in_specs=[
            pl.BlockSpec((MIX_ROWS, IN_WIDTH), lambda b, n: (b * nsteps + n, 0)),
            pl.BlockSpec((BLK, KV_WIDTH), lambda b, n: (prev_blk(b, n), COL_AK // KV_WIDTH)),
            pl.BlockSpec((BLK, KV_WIDTH), lambda b, n: (prev_blk(b, n), COL_AV // KV_WIDTH)),
            const((1, HEAD_DIM)), const((1, HEAD_DIM)), smem,
            pl.BlockSpec((MIX_ROWS, RET_DIM), lambda b, n: (n, 0)),
            pl.BlockSpec((MIX_ROWS, RET_DIM), lambda b, n: (n, 0)),
            const((MIX_ROWS, RET_WIDTH)), const((MIX_ROWS, RET_WIDTH)), smem,
            const((1, RET_WIDTH)),
        ],
        out_specs=[
            pl.BlockSpec((MIX_ROWS, D_MODEL), lambda b, n: (b * nsteps + n, 0)),
            pl.BlockSpec((1, BLK, KV_WIDTH), lambda b, n: (b, 0, 0)),
            pl.BlockSpec((1, BLK, KV_WIDTH), lambda b, n: (b, 0, 0)),
            pl.BlockSpec((1, N_RET_HEADS, RET_DIM, RET_DIM), lambda b, n: (b, 0, 0, 0)),
        ],
        out_shape=[jax.ShapeDtypeStruct((batch * seq, D_MODEL), BF16), win, win,
                   jax.ShapeDtypeStruct((batch, N_RET_HEADS, RET_DIM, RET_DIM), F32)],
        scratch_shapes=[pltpu.VMEM((N_RET_HEADS, RET_DIM, RET_DIM), F32)],
        compiler_params=_params("arbitrary", "arbitrary"),
        name="mixer_prompt",
    )(z, z, z, qg, kg, sinks, cos2, sin2, q_dec, k_dec, chunk_dec, rg)


SAMPLE_GROUP = 8


def _attn_sample_kernel(q_ref, kn_ref, vn_ref, kbuf_ref, vbuf_ref, qg_ref, kg_ref, sink_ref,
                        o_ref, kout_ref, vout_ref):
    rows = WINDOW * N_KV_HEADS
    qg = qg_ref[...]
    kg = kg_ref[...]
    sink = sink_ref[...]
    head = lax.broadcasted_iota(jnp.int32, (N_ATTN_HEADS, HEAD_DIM), 0)
    first = head < GQA_GROUP
    own = ((lax.broadcasted_iota(jnp.int32, (N_ATTN_HEADS, rows), 1) % N_KV_HEADS)
           == (lax.broadcasted_iota(jnp.int32, (N_ATTN_HEADS, rows), 0) // GQA_GROUP))
    row = lax.broadcasted_iota(jnp.int32, (rows, HEAD_DIM), 0)
    for b in range(SAMPLE_GROUP):
        qn = _rms(q_ref[b], qg) * ATTN_SCALE
        kn = _rms(kn_ref[b], kg)
        vn = vn_ref[b]
        kbuf = kbuf_ref[b]
        vbuf = vbuf_ref[b]
        s = jnp.where(own, _dot_nt(qn.astype(BF16), kbuf.astype(BF16)), -jnp.inf)
        k_sel = jnp.where(first, kn[0:1], kn[1:2])
        v_sel = jnp.where(first, vn[0:1], vn[1:2])
        s_new = jnp.sum(qn * k_sel, axis=-1, keepdims=True)
        m = jnp.maximum(jnp.maximum(jnp.max(s, axis=-1, keepdims=True), s_new), sink)
        e = jnp.exp(s - m)
        e_new = jnp.exp(s_new - m)
        denom = jnp.sum(e, axis=-1, keepdims=True) + e_new + jnp.exp(sink - m)
        o_ref[b] = (_dot(e.astype(BF16), vbuf.astype(BF16)) + e_new * v_sel) / denom
        for buf, new, out in ((kbuf, kn, kout_ref), (vbuf, vn, vout_ref)):
            shifted = pltpu.roll(buf, rows - N_KV_HEADS, 0)
            out[b] = jnp.where(row == rows - 2, new[0:1], jnp.where(row == rows - 1, new[1:2], shifted))


def _attn_sample(q, kn, vn, kbuf, vbuf, qg, kg, sinks):
    nb = q.shape[0]
    rows = WINDOW * N_KV_HEADS
    blk = lambda *shape: pl.BlockSpec((SAMPLE_GROUP,) + shape, lambda i: (i,) + (0,) * len(shape))
    const = lambda shape: pl.BlockSpec(shape, lambda i: (0,) * len(shape))
    return pl.pallas_call(
        _attn_sample_kernel,
        grid=(nb // SAMPLE_GROUP,),
        in_specs=[
            blk(N_ATTN_HEADS, HEAD_DIM), blk(N_KV_HEADS, HEAD_DIM), blk(N_KV_HEADS, HEAD_DIM),
            blk(rows, HEAD_DIM), blk(rows, HEAD_DIM),
            const((1, HEAD_DIM)), const((1, HEAD_DIM)), const((N_ATTN_HEADS, 1)),
        ],
        out_specs=[blk(N_ATTN_HEADS, HEAD_DIM), blk(rows, HEAD_DIM), blk(rows, HEAD_DIM)],
        out_shape=[jax.ShapeDtypeStruct((nb, N_ATTN_HEADS, HEAD_DIM), F32),
                   jax.ShapeDtypeStruct((nb, rows, HEAD_DIM), F32),
                   jax.ShapeDtypeStruct((nb, rows, HEAD_DIM), F32)],
        compiler_params=_params("parallel"),
        name="attn_sample",
    )(q, kn, vn, kbuf, vbuf, qg, kg, sinks)


def _ret_sample_kernel(q_ref, k_ref, v_ref, gate_ref, s0_ref, cos_ref, sin_ref, spread_ref, gamma_ref, rg_ref,
                       o_ref, s_out_ref):
    cos2 = cos_ref[...]
    sin2 = sin_ref[...]
    spread = spread_ref[...]
    for b in range(SAMPLE_GROUP):
        q_cols = _dot_tn(_rotary(q_ref[b], cos2, sin2), spread)
        k_cols = _dot_tn(_rotary(k_ref[b], cos2, sin2) * RET_K_SCALE, spread)
        v = v_ref[b]
        gate = gate_ref[b]
        rows = []
        for h in range(N_RET_HEADS):
            hs = slice(h * RET_DIM, (h + 1) * RET_DIM)
            s_new = s0_ref[b, h] * gamma_ref[h] + k_cols[:, hs] * v[h:h + 1, :]
            s_out_ref[b, h] = s_new
            o = jnp.sum(q_cols[:, hs] * s_new, axis=0, keepdims=True)
            rows.append(o * lax.rsqrt(jnp.mean(o * o, axis=-1, keepdims=True) + NORM_EPS))
        o = jnp.concatenate(rows, axis=0)
        o_ref[b] = o * rg_ref[...] * (gate * jax.nn.sigmoid(gate))


def _ret_sample(q, k, v, gate, s0, rg):
    nb = q.shape[0]
    cos2, sin2 = _rope_tables(PAST_LEN + np.arange(1))
    gamma = np.exp(_ret_log_decay()).astype(np.float32)
    spread = np.kron(np.eye(N_RET_HEADS), np.ones((1, RET_DIM))).astype(np.float32)
    blk = lambda *shape: pl.BlockSpec((SAMPLE_GROUP,) + shape, lambda i: (i,) + (0,) * len(shape))
    const = lambda shape: pl.BlockSpec(shape, lambda i: (0,) * len(shape))
    vec = blk(N_RET_HEADS, RET_DIM)
    return pl.pallas_call(
        _ret_sample_kernel,
        grid=(nb // SAMPLE_GROUP,),
        in_specs=[
            vec, vec, vec, vec, blk(N_RET_HEADS, RET_DIM, RET_DIM),
            const((1, RET_DIM)), const((1, RET_DIM)), const((N_RET_HEADS, RET_WIDTH)),
            pl.BlockSpec(memory_space=pltpu.SMEM),
            const((N_RET_HEADS, RET_DIM)),
        ],
        out_specs=[vec, blk(N_RET_HEADS, RET_DIM, RET_DIM)],
        out_shape=[jax.ShapeDtypeStruct((nb, N_RET_HEADS, RET_DIM), F32),
                   jax.ShapeDtypeStruct((nb, N_RET_HEADS, RET_DIM, RET_DIM), F32)],
        compiler_params=_params("parallel"),
        name="ret_sample",
    )(q, k, v, gate, s0, cos2, sin2, spread, gamma, rg.reshape(N_RET_HEADS, RET_DIM))


def _whole(a):
    return pl.BlockSpec(a.shape, lambda *_: (0,) * a.ndim)


def _out_proj_kernel(h_ref, mix_ref, hs_ref, mixs_ref, w_ref, o_ref, os_ref):
    tm = h_ref.shape[0]

    @pl.when(pl.program_id(0) == 0)
    def _():
        y = _dot(jnp.concatenate([mix_ref[...], mixs_ref[...]], axis=0), w_ref[...])
        o_ref[...] = h_ref[...] + y[:tm]
        os_ref[...] = hs_ref[...] + y[tm:]

    @pl.when(pl.program_id(0) > 0)
    def _():
        o_ref[...] = h_ref[...] + _dot(mix_ref[...], w_ref[...])


def _out_proj(h, mix, hs, mixs, w_bf16, tm):
    m = h.shape[0]
    row = pl.BlockSpec((tm, D_MODEL), lambda i: (i, 0))
    return pl.pallas_call(
        _out_proj_kernel,
        grid=(m // tm,),
        in_specs=[row, row, _whole(hs), _whole(mixs), _whole(w_bf16)],
        out_specs=[row, _whole(hs)],
        out_shape=[jax.ShapeDtypeStruct((m, D_MODEL), F32), jax.ShapeDtypeStruct(hs.shape, F32)],
        compiler_params=_params("arbitrary"),
        name="out_proj",
    )(h, mix, hs, mixs, w_bf16)


def _ffn_kernel(h_ref, hs_ref, g_ref, wg_ref, wu_ref, wd_ref, o_ref, os_ref, f_ref):
    i, j = pl.program_id(0), pl.program_id(1)
    tm = h_ref.shape[0]

    @pl.when(j == 0)
    def _():
        h = h_ref[...]
        f_ref[:tm] = _rms(h, g_ref[...]).astype(BF16)
        o_ref[...] = h

    @pl.when((i == 0) & (j == 0))
    def _():
        hs = hs_ref[...]
        f_ref[tm:] = _rms(hs, g_ref[...]).astype(BF16)
        os_ref[...] = hs

    def swiglu(f):
        gate = _dot(f, wg_ref[...])
        act = (gate * jax.nn.sigmoid(gate) * _dot(f, wu_ref[...])).astype(BF16)
        return _dot(act, wd_ref[...])

    @pl.when(i == 0)
    def _():
        y = swiglu(f_ref[...])
        o_ref[...] += y[:tm]
        os_ref[...] += y[tm:]

    @pl.when(i > 0)
    def _():
        o_ref[...] += swiglu(f_ref[:tm])


def _ffn(h, hs, g, wg, wu, wd, tm):
    m = h.shape[0]
    row = pl.BlockSpec((tm, D_MODEL), lambda i, j: (i, 0))
    return pl.pallas_call(
        _ffn_kernel,
        grid=(m // tm, D_FF // FF_TILE),
        in_specs=[
            row, _whole(hs), _whole(g),
            pl.BlockSpec((None, D_MODEL, FF_TILE), lambda i, j: (j, 0, 0)),
            pl.BlockSpec((None, D_MODEL, FF_TILE), lambda i, j: (j, 0, 0)),
            pl.BlockSpec((FF_TILE, D_MODEL), lambda i, j: (j, 0)),
        ],
        out_specs=[row, _whole(hs)],
        out_shape=[jax.ShapeDtypeStruct((m, D_MODEL), F32), jax.ShapeDtypeStruct(hs.shape, F32)],
        scratch_shapes=[pltpu.VMEM((tm + hs.shape[0], D_MODEL), BF16)],
        compiler_params=_params("arbitrary", "arbitrary"),
        name="ffn",
    )(h, hs, g, wg, wu, wd)


def _ple_kernel(h_ref, p_ref, hs_ref, ps_ref, g_ref, wp_ref, wg_ref, o_ref, os_ref):
    tm = h_ref.shape[0]

    def embed(h, p):
        gate = jax.nn.sigmoid(_dot(_rms(h, g_ref[...]).astype(BF16), wg_ref[...]))
        return h + _dot(p.astype(BF16), wp_ref[...]) * gate

    @pl.when(pl.program_id(0) == 0)
    def _():
        y = embed(jnp.concatenate([h_ref[...], hs_ref[...]], axis=0),
                  jnp.concatenate([p_ref[...], ps_ref[...]], axis=0))
        o_ref[...] = y[:tm]
        os_ref[...] = y[tm:]

    @pl.when(pl.program_id(0) > 0)
    def _():
        o_ref[...] = embed(h_ref[...], p_ref[...])


def _ple(h, p, hs, ps, g, w_ple, w_pg, tm):
    m = h.shape[0]
    row = pl.BlockSpec((tm, D_MODEL), lambda i: (i, 0))
    return pl.pallas_call(
        _ple_kernel,
        grid=(m // tm,),
        in_specs=[row, pl.BlockSpec((tm, PLE_DIM), lambda i: (i, 0)), _whole(hs), _whole(ps), _whole(g),
                  _whole(w_ple), _whole(w_pg)],
        out_specs=[row, _whole(hs)],
        out_shape=[jax.ShapeDtypeStruct((m, D_MODEL), F32), jax.ShapeDtypeStruct(hs.shape, F32)],
        compiler_params=_params("arbitrary"),
        name="ple",
    )(h, p, hs, ps, g, w_ple, w_pg)


DENSE_ROWS = 512
FFN_ROWS = 1024


def _dense_tail(h, mix, p, hs, mixs, ps, w):
    h, hs = _out_proj(h, mix, hs, mixs, w["w_out"], DENSE_ROWS)
    h, hs = _ffn(h, hs, w["ffn_norm_g"], w["w_gate"], w["w_up"], w["w_down"], FFN_ROWS)
    return _ple(h, p, hs, ps, w["ple_norm_g"], w["w_ple"], w["w_ple_gate"], DENSE_ROWS)


def _mix_sample(z, k_buf, v_buf, s0, w):
    nb = z.shape[0]
    heads = lambda c, n: z[:, c:c + n * HEAD_DIM].reshape(nb, n, HEAD_DIM)
    o_attn, k_win, v_win = _attn_sample(
        heads(COL_AQ, N_ATTN_HEADS), heads(COL_AK, N_KV_HEADS), heads(COL_AV, N_KV_HEADS),
        k_buf.reshape(nb, WINDOW * N_KV_HEADS, HEAD_DIM), v_buf.reshape(nb, WINDOW * N_KV_HEADS, HEAD_DIM),
        w["q_norm_g"], w["k_norm_g"], w["attn_sinks"].reshape(N_ATTN_HEADS, 1))
    o_ret, state = _ret_sample(
        heads(COL_RQ, N_RET_HEADS), heads(COL_RK, N_RET_HEADS), heads(COL_RV, N_RET_HEADS), heads(COL_RG, N_RET_HEADS), s0, w["ret_out_g"])
    mix = jnp.concatenate([o_attn.reshape(nb, ATTN_WIDTH), o_ret.reshape(nb, RET_WIDTH)], axis=1).astype(BF16)
    return (mix, k_win.reshape(nb, WINDOW, N_KV_HEADS, HEAD_DIM), v_win.reshape(nb, WINDOW, N_KV_HEADS, HEAD_DIM),
            state)


def kernel(x_prompt, x_sample, cache_k_win, cache_v_win, state_ret, p_prompt, p_sample,
           attn_norm_g, w_in, q_norm_g, k_norm_g, attn_sinks, ret_out_g, w_out,
           ffn_norm_g, w_gate, w_up, w_down, ple_norm_g, w_ple, w_ple_gate):
    assert w_in.shape[0] == 1 and x_sample.shape[1] == 1 and cache_k_win.shape[2] == WINDOW
    y_p, y_s = x_prompt, x_sample
    outs = [[] for _ in range(6)]
    for l in range(w_in.shape[0]):
        w = {
            "attn_norm_g": attn_norm_g[l].reshape(1, D_MODEL),
            "q_norm_g": q_norm_g[l].reshape(1, HEAD_DIM),
            "k_norm_g": k_norm_g[l].reshape(1, HEAD_DIM),
            "attn_sinks": attn_sinks[l],
            "ret_out_g": ret_out_g[l].reshape(1, RET_WIDTH),
            "ffn_norm_g": ffn_norm_g[l].reshape(1, D_MODEL),
            "ple_norm_g": ple_norm_g[l].reshape(1, D_MODEL),
        }
        batch, seq, _ = y_p.shape
        nb = y_s.shape[0]
        h_p = y_p.reshape(batch * seq, D_MODEL)
        h_s = y_s.reshape(nb, D_MODEL)
        z_p, z_s, *cast = _in_proj(h_p, h_s, w["attn_norm_g"], w_in[l], w_gate[l], w_up[l], w_down[l],
                                   w_out[l], w_ple_gate[l], w_ple[l])
        w.update(zip(("w_gate", "w_up", "w_down", "w_out", "w_ple_gate", "w_ple"), cast))
        mix_p, kp, vp, sp = _mixer_prompt(z_p, w["q_norm_g"], w["k_norm_g"], w["attn_sinks"], w["ret_out_g"],
                                          batch, seq)
        mix_s, ks, vs, ss = _mix_sample(z_s, cache_k_win[l], cache_v_win[l], state_ret[l], w)
        h_p, h_s = _dense_tail(h_p, mix_p, p_prompt[l].reshape(batch * seq, PLE_DIM),
                               h_s, mix_s, p_sample[l].reshape(nb, PLE_DIM), w)
        y_p, y_s = h_p.reshape(batch, seq, D_MODEL), h_s.reshape(nb, 1, D_MODEL)
        kp = kp.reshape(batch, BLK, N_KV_HEADS, HEAD_DIM)
        vp = vp.reshape(batch, BLK, N_KV_HEADS, HEAD_DIM)
        for acc, val in zip(outs, (kp, vp, sp, ks, vs, ss)):
            acc.append(val)
    kp, vp, sp, ks, vs, ss = (jnp.stack(o) for o in outs)
    return (y_p, y_s, kp, vp, sp, ks, vs, ss)
```

```python
import functools

import numpy as np
import jax
import jax.numpy as jnp
from jax import lax
from jax.experimental import pallas as pl
from jax.experimental.pallas import tpu as pltpu

D_MODEL = 2048
HEAD_DIM = 128
N_ATTN_HEADS = 8
N_KV_HEADS = 2
GQA_GROUP = N_ATTN_HEADS // N_KV_HEADS
ATTN_WIDTH = N_ATTN_HEADS * HEAD_DIM
KV_WIDTH = N_KV_HEADS * HEAD_DIM
WINDOW = 128
BLK = 128
N_RET_HEADS = 8
RET_DIM = 128
RET_WIDTH = N_RET_HEADS * RET_DIM
D_FF = 5632
IN_WIDTH = 5632
PLE_DIM = 256
PAST_LEN = 16384
ROPE_BASE = 10000.0
NORM_EPS = 1e-6
ATTN_SCALE = HEAD_DIM ** -0.5
RET_K_SCALE = RET_DIM ** -0.5

COL_RQ, COL_RK, COL_RV, COL_RG = 0, RET_WIDTH, 2 * RET_WIDTH, 3 * RET_WIDTH
COL_AQ = 4 * RET_WIDTH
COL_AK = COL_AQ + ATTN_WIDTH
COL_AV = COL_AK + KV_WIDTH
IN_TILE = 512
IN_TILES = IN_WIDTH // IN_TILE
IN_ROT = (ATTN_WIDTH + 2 * KV_WIDTH) // IN_TILE

VMEM_LIMIT = 56 * 1024 * 1024

F32 = jnp.float32
BF16 = jnp.bfloat16


def _params(*sem):
    return pltpu.CompilerParams(dimension_semantics=sem, vmem_limit_bytes=VMEM_LIMIT)


def _rms(x, g):
    return x * lax.rsqrt(jnp.mean(x * x, axis=-1, keepdims=True) + NORM_EPS) * g


def _dot(a, b):
    return jnp.dot(a, b, preferred_element_type=F32)


def _dot_nt(a, b):
    return lax.dot_general(a, b, (((1,), (1,)), ((), ())), preferred_element_type=F32)


def _dot_tn(a, b):
    return lax.dot_general(a, b, (((0,), (0,)), ((), ())), preferred_element_type=F32)


def _whole(a):
    return pl.BlockSpec(a.shape, lambda *_: (0,) * a.ndim)


def _in_proj_kernel(x_hbm, xs_ref, g_ref, w_ref, wg_ref, wu_ref, wd_ref, wo_ref, wpg_ref, wple_ref,
                    z_ref, zs_ref, wg_o, wu_o, wd_o, wo_o, wpg_o, wple_o,
                    xbuf, xsem, a_ref, as_ref, wres):
    i, j = pl.program_id(0), pl.program_id(1)

    def x_copy(tile):
        return pltpu.make_async_copy(x_hbm.at[pl.ds(tile * IN_ROWS, IN_ROWS)], xbuf, xsem)

    @pl.when((i == 0) & (j == 0))
    def _():
        x_copy(0).start()
        as_ref[...] = _rms(xs_ref[...], g_ref[...]).astype(BF16)

    @pl.when(j == 0)
    def _():
        x_copy(i).wait()
        a_ref[...] = _rms(xbuf[...], g_ref[...]).astype(BF16)

        @pl.when(i + 1 < pl.num_programs(0))
        def _():
            x_copy(i + 1).start()

    @pl.when(i == 0)
    def _():
        wres[j] = w_ref[...].astype(BF16)
        zs_ref[...] = _dot(as_ref[...], wres[j])

    z_ref[...] = _dot(a_ref[...], wres[j]).astype(z_ref.dtype)
    wg_o[...] = wg_ref[...].astype(BF16)
    wu_o[...] = wu_ref[...].astype(BF16)
    wd_o[...] = wd_ref[...].astype(BF16)

    @pl.when(j < CAST_SPLIT)
    def _():
        wo_o[...] = wo_ref[...].astype(BF16)
        wpg_o[...] = wpg_ref[...].astype(BF16)

    @pl.when(j == 0)
    def _():
        wple_o[...] = wple_ref[...].astype(BF16)


IN_ROWS = 1024
CAST_SPLIT = 8
FF_TILE = 512


def _in_proj(x, xs, g, w_in, w_gate, w_up, w_down, w_out, w_pg, w_ple):
    m = x.shape[0]
    ni = m // IN_ROWS
    steps = ni * IN_TILES
    assert m % IN_ROWS == 0 and FF_TILE == IN_TILE and D_FF == IN_WIDTH
    gu_rows, d_rows = D_MODEL // ni, D_FF // steps
    sq_rows, ple_rows = D_MODEL // (ni * CAST_SPLIT), PLE_DIM // ni
    assert gu_rows * ni == D_MODEL and d_rows * steps == D_FF and sq_rows % 16 == 0 and ple_rows % 16 == 0
    sq_idx = lambda i, j: (i * CAST_SPLIT + jnp.minimum(j, CAST_SPLIT - 1), 0)
    first = lambda i, j: jnp.where(i == 0, j, IN_TILES - 1)
    return pl.pallas_call(
        _in_proj_kernel,
        grid=(ni, IN_TILES),
        in_specs=[
            pl.BlockSpec(memory_space=pl.ANY),
            pl.BlockSpec(xs.shape, lambda i, j: (0, 0)),
            pl.BlockSpec((1, D_MODEL), lambda i, j: (0, 0)),
            pl.BlockSpec((D_MODEL, IN_TILE), lambda i, j: (0, (first(i, j) + IN_ROT) % IN_TILES)),
            pl.BlockSpec((gu_rows, FF_TILE), lambda i, j: (i, j)),
            pl.BlockSpec((gu_rows, FF_TILE), lambda i, j: (i, j)),
            pl.BlockSpec((d_rows, D_MODEL), lambda i, j: (i * IN_TILES + j, 0)),
            pl.BlockSpec((sq_rows, D_MODEL), sq_idx),
            pl.BlockSpec((sq_rows, D_MODEL), sq_idx),
            pl.BlockSpec((ple_rows, D_MODEL), lambda i, j: (i, 0)),
        ],
        out_specs=[
            pl.BlockSpec((IN_ROWS, IN_TILE), lambda i, j: (i, j)),
            pl.BlockSpec((xs.shape[0], IN_TILE), lambda i, j: (0, first(i, j))),
            pl.BlockSpec((None, gu_rows, FF_TILE), lambda i, j: (j, i, 0)),
            pl.BlockSpec((None, gu_rows, FF_TILE), lambda i, j: (j, i, 0)),
            pl.BlockSpec((d_rows, D_MODEL), lambda i, j: (i * IN_TILES + j, 0)),
            pl.BlockSpec((sq_rows, D_MODEL), sq_idx),
            pl.BlockSpec((sq_rows, D_MODEL), sq_idx),
            pl.BlockSpec((ple_rows, D_MODEL), lambda i, j: (i, 0)),
        ],
        out_shape=[
            jax.ShapeDtypeStruct((m, IN_WIDTH), BF16),
            jax.ShapeDtypeStruct((xs.shape[0], IN_WIDTH), F32),
            jax.ShapeDtypeStruct((D_FF // FF_TILE, D_MODEL, FF_TILE), BF16),
            jax.ShapeDtypeStruct((D_FF // FF_TILE, D_MODEL, FF_TILE), BF16),
            jax.ShapeDtypeStruct((D_FF, D_MODEL), BF16),
            jax.ShapeDtypeStruct((D_MODEL, D_MODEL), BF16),
            jax.ShapeDtypeStruct((D_MODEL, D_MODEL), BF16),
            jax.ShapeDtypeStruct((PLE_DIM, D_MODEL), BF16),
        ],
        scratch_shapes=[
            pltpu.VMEM((IN_ROWS, D_MODEL), F32),
            pltpu.SemaphoreType.DMA(()),
            pltpu.VMEM((IN_ROWS, D_MODEL), BF16),
            pltpu.VMEM((xs.shape[0], D_MODEL), BF16),
            pltpu.VMEM((IN_TILES, D_MODEL, IN_TILE), BF16),
        ],
        compiler_params=_params("arbitrary", "arbitrary"),
        name="in_proj",
    )(x, xs, g, w_in, w_gate, w_up, w_down, w_out, w_pg, w_ple)


MIX_ROWS = 256
LOG2E = 1.4426950408889634


def _rotary(x, cos2, sin2):
    return x * cos2 + pltpu.roll(x, RET_DIM // 2, 1) * sin2


def _mix_tile(z_ref, kp_ref, vp_ref, qg_ref, kg_ref, sink_ref, cos_ref, sin_ref, qdec_ref, kdec_ref,
              cdec_ref, rg_ref, mix_ref, kwin_ref, vwin_ref, sout_ref, s_ref, first, live):
    qg = qg_ref[...] * (ATTN_SCALE * LOG2E)
    kg = kg_ref[...]
    row = lax.broadcasted_iota(jnp.int32, (BLK, 2 * BLK), 0)
    col = lax.broadcasted_iota(jnp.int32, (BLK, 2 * BLK), 1)
    cur_ok = (col >= BLK) & ((col - BLK) <= row)
    prev_ok = (col < BLK) & (col >= row)
    kp = kp_ref[...].astype(F32)
    kprev = jnp.concatenate([_rms(kp[:, h * HEAD_DIM:(h + 1) * HEAD_DIM], kg) for h in range(N_KV_HEADS)],
                            axis=1).astype(BF16)
    vprev = vp_ref[...]
    for blk in range(MIX_ROWS // BLK):
        rows = slice(blk * BLK, (blk + 1) * BLK)
        mask = cur_ok | (prev_ok & jnp.logical_not(first)) if blk == 0 else cur_ok | prev_ok
        k = z_ref[rows, COL_AK:COL_AK + KV_WIDTH].astype(F32)
        kn = [_rms(k[:, h * HEAD_DIM:(h + 1) * HEAD_DIM], kg) for h in range(N_KV_HEADS)]
        kcur = jnp.concatenate(kn, axis=1).astype(BF16)
        vcur = z_ref[rows, COL_AV:COL_AV + KV_WIDTH]
        for kh in range(N_KV_HEADS):
            sl = slice(kh * HEAD_DIM, (kh + 1) * HEAD_DIM)
            k_cat = jnp.concatenate([kprev[:, sl], kcur[:, sl]], axis=0)
            v_cat = jnp.concatenate([vprev[:, sl], vcur[:, sl]], axis=0)
            for g in range(GQA_GROUP):
                h = kh * GQA_GROUP + g
                q = z_ref[rows, COL_AQ + h * HEAD_DIM:COL_AQ + (h + 1) * HEAD_DIM].astype(F32)
                s = jnp.where(mask, _dot_nt(_rms(q, qg).astype(BF16), k_cat), -jnp.inf)
                sink = sink_ref[h] * LOG2E
                m = jnp.maximum(jnp.max(s, axis=-1, keepdims=True), sink)
                e = jnp.exp2(s - m)
                denom = jnp.sum(e, axis=-1, keepdims=True) + jnp.exp2(sink - m)
                mix_ref[rows, h * HEAD_DIM:(h + 1) * HEAD_DIM] = (
                    _dot(e.astype(BF16), v_cat) / denom).astype(mix_ref.dtype)
        kprev, vprev = kcur, vcur
    kwin_ref[0] = jnp.concatenate(kn, axis=1)
    vwin_ref[0] = vcur.astype(F32)

    tril = (lax.broadcasted_iota(jnp.int32, (MIX_ROWS, MIX_ROWS), 0)
            >= lax.broadcasted_iota(jnp.int32, (MIX_ROWS, MIX_ROWS), 1))
    cos2 = cos_ref[...]
    sin2 = sin_ref[...]
    for h in range(N_RET_HEADS):
        hs = slice(h * RET_DIM, (h + 1) * RET_DIM)
        col_of = lambda c: slice(c + h * RET_DIM, c + (h + 1) * RET_DIM)
        q = (_rotary(z_ref[:, col_of(COL_RQ)].astype(F32), cos2, sin2) * qdec_ref[:, hs]).astype(BF16)
        k = (_rotary(z_ref[:, col_of(COL_RK)].astype(F32), cos2, sin2) * kdec_ref[:, hs]).astype(BF16)
        v = z_ref[:, col_of(COL_RV)]
        state = s_ref[h]
        sc = jnp.where(tril, _dot_nt(q, k), 0.0)
        o = _dot(sc.astype(BF16), v) + _dot(q, state.astype(BF16))
        state = jnp.where(live, (state + _dot_tn(k, v)) * cdec_ref[h], state)
        s_ref[h] = state
        sout_ref[0, h] = state
        o = o * lax.rsqrt(jnp.mean(o * o, axis=-1, keepdims=True) + NORM_EPS)
        gate = z_ref[:, col_of(COL_RG)].astype(F32)
        mix_ref[:, ATTN_WIDTH + h * RET_DIM:ATTN_WIDTH + (h + 1) * RET_DIM] = (
            o * rg_ref[:, hs] * (gate * jax.nn.sigmoid(gate))).astype(mix_ref.dtype)


def _mixer_out_kernel(nsteps, z_ref, kp_ref, vp_ref, qg_ref, kg_ref, sink_ref, cos_ref, sin_ref, qdec_ref,
                      kdec_ref, cdec_ref, rg_ref, h_ref, hs_ref, mixs_ref, w_ref,
                      o_ref, os_ref, kwin_ref, vwin_ref, sout_ref, mix_a, mix_b, s_ref):
    t = pl.program_id(0)
    tiles = pl.num_programs(0) - 1
    n = jnp.minimum(t, tiles - 1) % nsteps
    live = t < tiles

    @pl.when(t == 0)
    def _():
        mix_b[...] = jnp.zeros_like(mix_b)

    @pl.when(n == 0)
    def _():
        s_ref[...] = jnp.zeros_like(s_ref)

    def step(dst, src):
        o_ref[...] = h_ref[...] + _dot(src[...], w_ref[...])
        _mix_tile(z_ref, kp_ref, vp_ref, qg_ref, kg_ref, sink_ref, cos_ref, sin_ref, qdec_ref, kdec_ref,
                  cdec_ref, rg_ref, dst, kwin_ref, vwin_ref, sout_ref, s_ref, n == 0, live)

    @pl.when(t % 2 == 0)
    def _():
        step(mix_a, mix_b)

    @pl.when(t % 2 == 1)
    def _():
        step(mix_b, mix_a)

    @pl.when(t == 1)
    def _():
        os_ref[...] = hs_ref[...] + _dot(mixs_ref[...], w_ref[...])


def _rope_tables(pos):
    half = RET_DIM // 2
    inv = ROPE_BASE ** (-np.arange(half, dtype=np.float64) / half)
    ang = np.asarray(pos, np.float64)[:, None] * inv[None, :]
    cos, sin = np.cos(ang), np.sin(ang)
    return (np.concatenate([cos, cos], axis=-1).astype(np.float32),
            np.concatenate([-sin, sin], axis=-1).astype(np.float32))


def _ret_log_decay():
    return np.log1p(-np.exp2(-5.0 - np.arange(N_RET_HEADS, dtype=np.float64)))


def _mixer_out(z, h, hs, mixs, w_out, qg, kg, sinks, rg, batch, seq):
    nsteps = seq // MIX_ROWS
    tiles = batch * nsteps
    cos2, sin2 = _rope_tables(np.arange(seq))
    lg = _ret_log_decay()
    steps = np.arange(MIX_ROWS, dtype=np.float64)[:, None, None] + 1.0
    lanes = lambda t: np.broadcast_to(t, (MIX_ROWS, N_RET_HEADS, RET_DIM)).reshape(MIX_ROWS, RET_WIDTH).astype(np.float32)
    q_dec = lanes(np.exp(lg[None, :, None] * steps))
    k_dec = lanes(np.exp(-lg[None, :, None] * steps) * RET_K_SCALE)
    chunk_dec = np.exp(lg * MIX_ROWS).astype(np.float32)
    smem = pl.BlockSpec(memory_space=pltpu.SMEM)
    win = jax.ShapeDtypeStruct((batch, BLK, KV_WIDTH), F32)
    blks = MIX_ROWS // BLK
    tile = lambda t: jnp.minimum(t, tiles - 1)
    prev_blk = lambda t: tile(t) * blks - jnp.minimum(tile(t) % nsteps, 1)
    behind = lambda t: (jnp.maximum(t - 1, 0), 0)
    seq_of = lambda t: tile(t) // nsteps
    return pl.pallas_call(
        functools.partial(_mixer_out_kernel, nsteps),
        grid=(tiles + 1,),
        in_specs=[
            pl.BlockSpec((MIX_ROWS, IN_WIDTH), lambda t: (tile(t), 0)),
            pl.BlockSpec((BLK, KV_WIDTH), lambda t: (prev_blk(t), COL_AK // KV_WIDTH)),
            pl.BlockSpec((BLK, KV_WIDTH), lambda t: (prev_blk(t), COL_AV // KV_WIDTH)),
            _whole(qg), _whole(kg), smem,
            pl.BlockSpec((MIX_ROWS, RET_DIM), lambda t: (tile(t) % nsteps, 0)),
            pl.BlockSpec((MIX_ROWS, RET_DIM), lambda t: (tile(t) % nsteps, 0)),
            _whole(q_dec), _whole(k_dec), smem, _whole(rg),
            pl.BlockSpec((MIX_ROWS, D_MODEL), behind), _whole(hs), _whole(mixs), _whole(w_out),
        ],
        out_specs=[
            pl.BlockSpec((MIX_ROWS, D_MODEL), behind), _whole(hs),
            pl.BlockSpec((1, BLK, KV_WIDTH), lambda t: (seq_of(t), 0, 0)),
            pl.BlockSpec((1, BLK, KV_WIDTH), lambda t: (seq_of(t), 0, 0)),
            pl.BlockSpec((1, N_RET_HEADS, RET_DIM, RET_DIM), lambda t: (seq_of(t), 0, 0, 0)),
        ],
        out_shape=[jax.ShapeDtypeStruct(h.shape, F32), jax.ShapeDtypeStruct(hs.shape, F32), win, win,
                   jax.ShapeDtypeStruct((batch, N_RET_HEADS, RET_DIM, RET_DIM), F32)],
        scratch_shapes=[pltpu.VMEM((MIX_ROWS, D_MODEL), BF16), pltpu.VMEM((MIX_ROWS, D_MODEL), BF16),
                        pltpu.VMEM((N_RET_HEADS, RET_DIM, RET_DIM), F32)],
        compiler_params=_params("arbitrary"),
        name="mixer_out",
    )(z, z, z, qg, kg, sinks, cos2, sin2, q_dec, k_dec, chunk_dec, rg, h, hs, mixs, w_out)


SAMPLE_GROUP = 8


def _attn_sample_kernel(q_ref, kn_ref, vn_ref, kbuf_ref, vbuf_ref, qg_ref, kg_ref, sink_ref,
                        o_ref, kout_ref, vout_ref):
    rows = WINDOW * N_KV_HEADS
    qg = qg_ref[...]
    kg = kg_ref[...]
    sink = sink_ref[...]
    head = lax.broadcasted_iota(jnp.int32, (N_ATTN_HEADS, HEAD_DIM), 0)
    first = head < GQA_GROUP
    own = ((lax.broadcasted_iota(jnp.int32, (N_ATTN_HEADS, rows), 1) % N_KV_HEADS)
           == (lax.broadcasted_iota(jnp.int32, (N_ATTN_HEADS, rows), 0) // GQA_GROUP))
    row = lax.broadcasted_iota(jnp.int32, (rows, HEAD_DIM), 0)
    for b in range(SAMPLE_GROUP):
        qn = _rms(q_ref[b], qg) * ATTN_SCALE
        kn = _rms(kn_ref[b], kg)
        vn = vn_ref[b]
        kbuf = kbuf_ref[b]
        vbuf = vbuf_ref[b]
        s = jnp.where(own, _dot_nt(qn.astype(BF16), kbuf.astype(BF16)), -jnp.inf)
        k_sel = jnp.where(first, kn[0:1], kn[1:2])
        v_sel = jnp.where(first, vn[0:1], vn[1:2])
        s_new = jnp.sum(qn * k_sel, axis=-1, keepdims=True)
        m = jnp.maximum(jnp.maximum(jnp.max(s, axis=-1, keepdims=True), s_new), sink)
        e = jnp.exp(s - m)
        e_new = jnp.exp(s_new - m)
        denom = jnp.sum(e, axis=-1, keepdims=True) + e_new + jnp.exp(sink - m)
        o_ref[b] = (_dot(e.astype(BF16), vbuf.astype(BF16)) + e_new * v_sel) / denom
        for buf, new, out in ((kbuf, kn, kout_ref), (vbuf, vn, vout_ref)):
            shifted = pltpu.roll(buf, rows - N_KV_HEADS, 0)
            out[b] = jnp.where(row == rows - 2, new[0:1], jnp.where(row == rows - 1, new[1:2], shifted))


def _attn_sample(q, kn, vn, kbuf, vbuf, qg, kg, sinks):
    nb = q.shape[0]
    rows = WINDOW * N_KV_HEADS
    blk = lambda *shape: pl.BlockSpec((SAMPLE_GROUP,) + shape, lambda i: (i,) + (0,) * len(shape))
    const = lambda shape: pl.BlockSpec(shape, lambda i: (0,) * len(shape))
    return pl.pallas_call(
        _attn_sample_kernel,
        grid=(nb // SAMPLE_GROUP,),
        in_specs=[
            blk(N_ATTN_HEADS, HEAD_DIM), blk(N_KV_HEADS, HEAD_DIM), blk(N_KV_HEADS, HEAD_DIM),
            blk(rows, HEAD_DIM), blk(rows, HEAD_DIM),
            const((1, HEAD_DIM)), const((1, HEAD_DIM)), const((N_ATTN_HEADS, 1)),
        ],
        out_specs=[blk(N_ATTN_HEADS, HEAD_DIM), blk(rows, HEAD_DIM), blk(rows, HEAD_DIM)],
        out_shape=[jax.ShapeDtypeStruct((nb, N_ATTN_HEADS, HEAD_DIM), F32),
                   jax.ShapeDtypeStruct((nb, rows, HEAD_DIM), F32),
                   jax.ShapeDtypeStruct((nb, rows, HEAD_DIM), F32)],
        compiler_params=_params("parallel"),
        name="attn_sample",
    )(q, kn, vn, kbuf, vbuf, qg, kg, sinks)


def _ret_sample_kernel(q_ref, k_ref, v_ref, gate_ref, s0_ref, cos_ref, sin_ref, spread_ref, gamma_ref, rg_ref,
                       o_ref, s_out_ref):
    cos2 = cos_ref[...]
    sin2 = sin_ref[...]
    spread = spread_ref[...]
    for b in range(SAMPLE_GROUP):
        q_cols = _dot_tn(_rotary(q_ref[b], cos2, sin2), spread)
        k_cols = _dot_tn(_rotary(k_ref[b], cos2, sin2) * RET_K_SCALE, spread)
        v = v_ref[b]
        gate = gate_ref[b]
        rows = []
        for h in range(N_RET_HEADS):
            hs = slice(h * RET_DIM, (h + 1) * RET_DIM)
            s_new = s0_ref[b, h] * gamma_ref[h] + k_cols[:, hs] * v[h:h + 1, :]
            s_out_ref[b, h] = s_new
            o = jnp.sum(q_cols[:, hs] * s_new, axis=0, keepdims=True)
            rows.append(o * lax.rsqrt(jnp.mean(o * o, axis=-1, keepdims=True) + NORM_EPS))
        o = jnp.concatenate(rows, axis=0)
        o_ref[b] = o * rg_ref[...] * (gate * jax.nn.sigmoid(gate))


def _ret_sample(q, k, v, gate, s0, rg):
    nb = q.shape[0]
    cos2, sin2 = _rope_tables(PAST_LEN + np.arange(1))
    gamma = np.exp(_ret_log_decay()).astype(np.float32)
    spread = np.kron(np.eye(N_RET_HEADS), np.ones((1, RET_DIM))).astype(np.float32)
    blk = lambda *shape: pl.BlockSpec((SAMPLE_GROUP,) + shape, lambda i: (i,) + (0,) * len(shape))
    const = lambda shape: pl.BlockSpec(shape, lambda i: (0,) * len(shape))
    vec = blk(N_RET_HEADS, RET_DIM)
    return pl.pallas_call(
        _ret_sample_kernel,
        grid=(nb // SAMPLE_GROUP,),
        in_specs=[
            vec, vec, vec, vec, blk(N_RET_HEADS, RET_DIM, RET_DIM),
            const((1, RET_DIM)), const((1, RET_DIM)), const((N_RET_HEADS, RET_WIDTH)),
            pl.BlockSpec(memory_space=pltpu.SMEM),
            const((N_RET_HEADS, RET_DIM)),
        ],
        out_specs=[vec, blk(N_RET_HEADS, RET_DIM, RET_DIM)],
        out_shape=[jax.ShapeDtypeStruct((nb, N_RET_HEADS, RET_DIM), F32),
                   jax.ShapeDtypeStruct((nb, N_RET_HEADS, RET_DIM, RET_DIM), F32)],
        compiler_params=_params("parallel"),
        name="ret_sample",
    )(q, k, v, gate, s0, cos2, sin2, spread, gamma, rg.reshape(N_RET_HEADS, RET_DIM))


def _ffn_kernel(h_ref, hs_ref, g_ref, wg_ref, wu_ref, wd_ref, o_ref, os_ref, f_ref):
    i, j = pl.program_id(0), pl.program_id(1)
    tm = h_ref.shape[0]

    @pl.when(j == 0)
    def _():
        h = h_ref[...]
        f_ref[:tm] = _rms(h, g_ref[...]).astype(BF16)
        o_ref[...] = h

    @pl.when((i == 0) & (j == 0))
    def _():
        hs = hs_ref[...]
        f_ref[tm:] = _rms(hs, g_ref[...]).astype(BF16)
        os_ref[...] = hs

    def swiglu(f):
        gate = _dot(f, wg_ref[...])
        act = (gate * jax.nn.sigmoid(gate) * _dot(f, wu_ref[...])).astype(BF16)
        return _dot(act, wd_ref[...])

    @pl.when(i == 0)
    def _():
        y = swiglu(f_ref[...])
        o_ref[...] += y[:tm]
        os_ref[...] += y[tm:]

    @pl.when(i > 0)
    def _():
        o_ref[...] += swiglu(f_ref[:tm])


def _ffn(h, hs, g, wg, wu, wd, tm):
    m = h.shape[0]
    row = pl.BlockSpec((tm, D_MODEL), lambda i, j: (i, 0))
    return pl.pallas_call(
        _ffn_kernel,
        grid=(m // tm, D_FF // FF_TILE),
        in_specs=[
            row, _whole(hs), _whole(g),
            pl.BlockSpec((None, D_MODEL, FF_TILE), lambda i, j: (j, 0, 0)),
            pl.BlockSpec((None, D_MODEL, FF_TILE), lambda i, j: (j, 0, 0)),
            pl.BlockSpec((FF_TILE, D_MODEL), lambda i, j: (j, 0)),
        ],
        out_specs=[row, _whole(hs)],
        out_shape=[jax.ShapeDtypeStruct((m, D_MODEL), F32), jax.ShapeDtypeStruct(hs.shape, F32)],
        scratch_shapes=[pltpu.VMEM((tm + hs.shape[0], D_MODEL), BF16)],
        compiler_params=_params("arbitrary", "arbitrary"),
        name="ffn",
    )(h, hs, g, wg, wu, wd)


def _ple_kernel(h_ref, p_ref, hs_ref, ps_ref, g_ref, wp_ref, wg_ref, o_ref, os_ref):
    tm = h_ref.shape[0]

    def embed(h, p):
        gate = jax.nn.sigmoid(_dot(_rms(h, g_ref[...]).astype(BF16), wg_ref[...]))
        return h + _dot(p.astype(BF16), wp_ref[...]) * gate

    @pl.when(pl.program_id(0) == 0)
    def _():
        y = embed(jnp.concatenate([h_ref[...], hs_ref[...]], axis=0),
                  jnp.concatenate([p_ref[...], ps_ref[...]], axis=0))
        o_ref[...] = y[:tm]
        os_ref[...] = y[tm:]

    @pl.when(pl.program_id(0) > 0)
    def _():
        o_ref[...] = embed(h_ref[...], p_ref[...])


def _ple(h, p, hs, ps, g, w_ple, w_pg, tm):
    m = h.shape[0]
    row = pl.BlockSpec((tm, D_MODEL), lambda i: (i, 0))
    return pl.pallas_call(
        _ple_kernel,
        grid=(m // tm,),
        in_specs=[row, pl.BlockSpec((tm, PLE_DIM), lambda i: (i, 0)), _whole(hs), _whole(ps), _whole(g),
                  _whole(w_ple), _whole(w_pg)],
        out_specs=[row, _whole(hs)],
        out_shape=[jax.ShapeDtypeStruct((m, D_MODEL), F32), jax.ShapeDtypeStruct(hs.shape, F32)],
        compiler_params=_params("arbitrary"),
        name="ple",
    )(h, p, hs, ps, g, w_ple, w_pg)


PLE_ROWS = 512
FFN_ROWS = 1024


def _dense_tail(h, p, hs, ps, w):
    h, hs = _ffn(h, hs, w["ffn_norm_g"], w["w_gate"], w["w_up"], w["w_down"], FFN_ROWS)
    return _ple(h, p, hs, ps, w["ple_norm_g"], w["w_ple"], w["w_ple_gate"], PLE_ROWS)


def _mix_sample(z, k_buf, v_buf, s0, w):
    nb = z.shape[0]
    heads = lambda c, n: z[:, c:c + n * HEAD_DIM].reshape(nb, n, HEAD_DIM)
    o_attn, k_win, v_win = _attn_sample(
        heads(COL_AQ, N_ATTN_HEADS), heads(COL_AK, N_KV_HEADS), heads(COL_AV, N_KV_HEADS),
        k_buf.reshape(nb, WINDOW * N_KV_HEADS, HEAD_DIM), v_buf.reshape(nb, WINDOW * N_KV_HEADS, HEAD_DIM),
        w["q_norm_g"], w["k_norm_g"], w["attn_sinks"].reshape(N_ATTN_HEADS, 1))
    o_ret, state = _ret_sample(
        heads(COL_RQ, N_RET_HEADS), heads(COL_RK, N_RET_HEADS), heads(COL_RV, N_RET_HEADS), heads(COL_RG, N_RET_HEADS), s0, w["ret_out_g"])
    mix = jnp.concatenate([o_attn.reshape(nb, ATTN_WIDTH), o_ret.reshape(nb, RET_WIDTH)], axis=1).astype(BF16)
    return (mix, k_win.reshape(nb, WINDOW, N_KV_HEADS, HEAD_DIM), v_win.reshape(nb, WINDOW, N_KV_HEADS, HEAD_DIM),
            state)


def kernel(x_prompt, x_sample, cache_k_win, cache_v_win, state_ret, p_prompt, p_sample,
           attn_norm_g, w_in, q_norm_g, k_norm_g, attn_sinks, ret_out_g, w_out,
           ffn_norm_g, w_gate, w_up, w_down, ple_norm_g, w_ple, w_ple_gate):
    assert w_in.shape[0] == 1 and x_sample.shape[1] == 1 and cache_k_win.shape[2] == WINDOW
    y_p, y_s = x_prompt, x_sample
    outs = [[] for _ in range(6)]
    for l in range(w_in.shape[0]):
        w = {
            "attn_norm_g": attn_norm_g[l].reshape(1, D_MODEL),
            "q_norm_g": q_norm_g[l].reshape(1, HEAD_DIM),
            "k_norm_g": k_norm_g[l].reshape(1, HEAD_DIM),
            "attn_sinks": attn_sinks[l],
            "ret_out_g": ret_out_g[l].reshape(1, RET_WIDTH),
            "ffn_norm_g": ffn_norm_g[l].reshape(1, D_MODEL),
            "ple_norm_g": ple_norm_g[l].reshape(1, D_MODEL),
        }
        batch, seq, _ = y_p.shape
        nb = y_s.shape[0]
        h_p = y_p.reshape(batch * seq, D_MODEL)
        h_s = y_s.reshape(nb, D_MODEL)
        z_p, z_s, *cast = _in_proj(h_p, h_s, w["attn_norm_g"], w_in[l], w_gate[l], w_up[l], w_down[l],
                                   w_out[l], w_ple_gate[l], w_ple[l])
        w.update(zip(("w_gate", "w_up", "w_down", "w_out", "w_ple_gate", "w_ple"), cast))
        mix_s, ks, vs, ss = _mix_sample(z_s, cache_k_win[l], cache_v_win[l], state_ret[l], w)
        h_p, h_s, kp, vp, sp = _mixer_out(z_p, h_p, h_s, mix_s, w["w_out"], w["q_norm_g"], w["k_norm_g"],
                                          w["attn_sinks"], w["ret_out_g"], batch, seq)
        h_p, h_s = _dense_tail(h_p, p_prompt[l].reshape(batch * seq, PLE_DIM),
                               h_s, p_sample[l].reshape(nb, PLE_DIM), w)
        y_p, y_s = h_p.reshape(batch, seq, D_MODEL), h_s.reshape(nb, 1, D_MODEL)
        kp = kp.reshape(batch, BLK, N_KV_HEADS, HEAD_DIM)
        vp = vp.reshape(batch, BLK, N_KV_HEADS, HEAD_DIM)
        for acc, val in zip(outs, (kp, vp, sp, ks, vs, ss)):
            acc.append(val)
    kp, vp, sp, ks, vs, ss = (jnp.stack(o) for o in outs)
    return (y_p, y_s, kp, vp, sp, ks, vs, ss)
```

```python
import functools

import numpy as np
import jax
import jax.numpy as jnp
from jax import lax
from jax.experimental import pallas as pl
from jax.experimental.pallas import tpu as pltpu

D_MODEL = 2048
HEAD_DIM = 128
N_ATTN_HEADS = 8
N_KV_HEADS = 2
GQA_GROUP = N_ATTN_HEADS // N_KV_HEADS
ATTN_WIDTH = N_ATTN_HEADS * HEAD_DIM
KV_WIDTH = N_KV_HEADS * HEAD_DIM
WINDOW = 128
BLK = 128
N_RET_HEADS = 8
RET_DIM = 128
RET_WIDTH = N_RET_HEADS * RET_DIM
D_FF = 5632
IN_WIDTH = 5632
PLE_DIM = 256
PAST_LEN = 16384
ROPE_BASE = 10000.0
NORM_EPS = 1e-6
ATTN_SCALE = HEAD_DIM ** -0.5
RET_K_SCALE = RET_DIM ** -0.5

COL_RQ, COL_RK, COL_RV, COL_RG = 0, RET_WIDTH, 2 * RET_WIDTH, 3 * RET_WIDTH
COL_AQ = 4 * RET_WIDTH
COL_AK = COL_AQ + ATTN_WIDTH
COL_AV = COL_AK + KV_WIDTH
IN_TILE = 512
IN_TILES = IN_WIDTH // IN_TILE
IN_ROT = (ATTN_WIDTH + 2 * KV_WIDTH) // IN_TILE

VMEM_LIMIT = 56 * 1024 * 1024

F32 = jnp.float32
BF16 = jnp.bfloat16


def _params(*sem):
    return pltpu.CompilerParams(dimension_semantics=sem, vmem_limit_bytes=VMEM_LIMIT)


def _rms(x, g):
    return x * lax.rsqrt(jnp.mean(x * x, axis=-1, keepdims=True) + NORM_EPS) * g


def _dot(a, b):
    return jnp.dot(a, b, preferred_element_type=F32)


def _dot_nt(a, b):
    return lax.dot_general(a, b, (((1,), (1,)), ((), ())), preferred_element_type=F32)


def _dot_tn(a, b):
    return lax.dot_general(a, b, (((0,), (0,)), ((), ())), preferred_element_type=F32)


def _whole(a):
    return pl.BlockSpec(a.shape, lambda *_: (0,) * a.ndim)


def _in_proj_kernel(x_hbm, xs_ref, g_ref, w_ref, wg_ref, wu_ref, wd_ref, wo_ref, wpg_ref, wple_ref,
                    z_ref, zs_ref, wg_o, wu_o, wd_o, wo_o, wpg_o, wple_o,
                    xbuf, xsem, a_ref, as_ref, wres):
    i, j = pl.program_id(0), pl.program_id(1)

    def x_copy(tile):
        return pltpu.make_async_copy(x_hbm.at[pl.ds(tile * IN_ROWS, IN_ROWS)], xbuf, xsem)

    @pl.when((i == 0) & (j == 0))
    def _():
        x_copy(0).start()
        as_ref[...] = _rms(xs_ref[...], g_ref[...]).astype(BF16)

    @pl.when(j == 0)
    def _():
        x_copy(i).wait()
        a_ref[...] = _rms(xbuf[...], g_ref[...]).astype(BF16)

        @pl.when(i + 1 < pl.num_programs(0))
        def _():
            x_copy(i + 1).start()

    @pl.when(i == 0)
    def _():
        wres[j] = w_ref[...].astype(BF16)
        zs_ref[...] = _dot(as_ref[...], wres[j])

    z_ref[...] = _dot(a_ref[...], wres[j]).astype(z_ref.dtype)
    wg_o[...] = wg_ref[...].astype(BF16)
    wu_o[...] = wu_ref[...].astype(BF16)
    wd_o[...] = wd_ref[...].astype(BF16)

    @pl.when(j < CAST_SPLIT)
    def _():
        wo_o[...] = wo_ref[...].astype(BF16)
        wpg_o[...] = wpg_ref[...].astype(BF16)

    @pl.when(j == 0)
    def _():
        wple_o[...] = wple_ref[...].astype(BF16)


IN_ROWS = 1024
CAST_SPLIT = 8
FF_TILE = 512


def _in_proj(x, xs, g, w_in, w_gate, w_up, w_down, w_out, w_pg, w_ple):
    m = x.shape[0]
    ni = m // IN_ROWS
    steps = ni * IN_TILES
    assert m % IN_ROWS == 0 and FF_TILE == IN_TILE and D_FF == IN_WIDTH
    gu_rows, d_rows = D_MODEL // ni, D_FF // steps
    sq_rows, ple_rows = D_MODEL // (ni * CAST_SPLIT), PLE_DIM // ni
    assert gu_rows * ni == D_MODEL and d_rows * steps == D_FF and sq_rows % 16 == 0 and ple_rows % 16 == 0
    sq_idx = lambda i, j: (i * CAST_SPLIT + jnp.minimum(j, CAST_SPLIT - 1), 0)
    first = lambda i, j: jnp.where(i == 0, j, IN_TILES - 1)
    return pl.pallas_call(
        _in_proj_kernel,
        grid=(ni, IN_TILES),
        in_specs=[
            pl.BlockSpec(memory_space=pl.ANY),
            pl.BlockSpec(xs.shape, lambda i, j: (0, 0)),
            pl.BlockSpec((1, D_MODEL), lambda i, j: (0, 0)),
            pl.BlockSpec((D_MODEL, IN_TILE), lambda i, j: (0, (first(i, j) + IN_ROT) % IN_TILES)),
            pl.BlockSpec((gu_rows, FF_TILE), lambda i, j: (i, j)),
            pl.BlockSpec((gu_rows, FF_TILE), lambda i, j: (i, j)),
            pl.BlockSpec((d_rows, D_MODEL), lambda i, j: (i * IN_TILES + j, 0)),
            pl.BlockSpec((sq_rows, D_MODEL), sq_idx),
            pl.BlockSpec((sq_rows, D_MODEL), sq_idx),
            pl.BlockSpec((ple_rows, D_MODEL), lambda i, j: (i, 0)),
        ],
        out_specs=[
            pl.BlockSpec((IN_ROWS, IN_TILE), lambda i, j: (i, j)),
            pl.BlockSpec((xs.shape[0], IN_TILE), lambda i, j: (0, first(i, j))),
            pl.BlockSpec((None, gu_rows, FF_TILE), lambda i, j: (j, i, 0)),
            pl.BlockSpec((None, gu_rows, FF_TILE), lambda i, j: (j, i, 0)),
            pl.BlockSpec((d_rows, D_MODEL), lambda i, j: (i * IN_TILES + j, 0)),
            pl.BlockSpec((sq_rows, D_MODEL), sq_idx),
            pl.BlockSpec((sq_rows, D_MODEL), sq_idx),
            pl.BlockSpec((ple_rows, D_MODEL), lambda i, j: (i, 0)),
        ],
        out_shape=[
            jax.ShapeDtypeStruct((m, IN_WIDTH), BF16),
            jax.ShapeDtypeStruct((xs.shape[0], IN_WIDTH), F32),
            jax.ShapeDtypeStruct((D_FF // FF_TILE, D_MODEL, FF_TILE), BF16),
            jax.ShapeDtypeStruct((D_FF // FF_TILE, D_MODEL, FF_TILE), BF16),
            jax.ShapeDtypeStruct((D_FF, D_MODEL), BF16),
            jax.ShapeDtypeStruct((D_MODEL, D_MODEL), BF16),
            jax.ShapeDtypeStruct((D_MODEL, D_MODEL), BF16),
            jax.ShapeDtypeStruct((PLE_DIM, D_MODEL), BF16),
        ],
        scratch_shapes=[
            pltpu.VMEM((IN_ROWS, D_MODEL), F32),
            pltpu.SemaphoreType.DMA(()),
            pltpu.VMEM((IN_ROWS, D_MODEL), BF16),
            pltpu.VMEM((xs.shape[0], D_MODEL), BF16),
            pltpu.VMEM((IN_TILES, D_MODEL, IN_TILE), BF16),
        ],
        compiler_params=_params("arbitrary", "arbitrary"),
        name="in_proj",
    )(x, xs, g, w_in, w_gate, w_up, w_down, w_out, w_pg, w_ple)


MIX_ROWS = 512
RET_CHUNK = 256
LOG2E = 1.4426950408889634


def _rotary(x, cos2, sin2):
    return x * cos2 + pltpu.roll(x, RET_DIM // 2, 1) * sin2


def _mix_tile(z_ref, kp_ref, vp_ref, qg_ref, kg_ref, sink_ref, cos_ref, sin_ref, qdec_ref, kdec_ref,
              cdec_ref, rg_ref, mix_ref, kwin_ref, vwin_ref, sout_ref, s_ref, first, live):
    qg = qg_ref[...] * (ATTN_SCALE * LOG2E)
    kg = kg_ref[...]
    row = lax.broadcasted_iota(jnp.int32, (BLK, 2 * BLK), 0)
    col = lax.broadcasted_iota(jnp.int32, (BLK, 2 * BLK), 1)
    cur_ok = (col >= BLK) & ((col - BLK) <= row)
    prev_ok = (col < BLK) & (col >= row)
    kp = kp_ref[...].astype(F32)
    kprev = jnp.concatenate([_rms(kp[:, h * HEAD_DIM:(h + 1) * HEAD_DIM], kg) for h in range(N_KV_HEADS)],
                            axis=1).astype(BF16)
    vprev = vp_ref[...]
    for blk in range(MIX_ROWS // BLK):
        rows = slice(blk * BLK, (blk + 1) * BLK)
        mask = cur_ok | (prev_ok & jnp.logical_not(first)) if blk == 0 else cur_ok | prev_ok
        k = z_ref[rows, COL_AK:COL_AK + KV_WIDTH].astype(F32)
        kn = [_rms(k[:, h * HEAD_DIM:(h + 1) * HEAD_DIM], kg) for h in range(N_KV_HEADS)]
        kcur = jnp.concatenate(kn, axis=1).astype(BF16)
        vcur = z_ref[rows, COL_AV:COL_AV + KV_WIDTH]
        for kh in range(N_KV_HEADS):
            sl = slice(kh * HEAD_DIM, (kh + 1) * HEAD_DIM)
            k_cat = jnp.concatenate([kprev[:, sl], kcur[:, sl]], axis=0)
            v_cat = jnp.concatenate([vprev[:, sl], vcur[:, sl]], axis=0)
            for g in range(GQA_GROUP):
                h = kh * GQA_GROUP + g
                q = z_ref[rows, COL_AQ + h * HEAD_DIM:COL_AQ + (h + 1) * HEAD_DIM].astype(F32)
                s = jnp.where(mask, _dot_nt(_rms(q, qg).astype(BF16), k_cat), -jnp.inf)
                sink = sink_ref[h] * LOG2E
                m = jnp.maximum(jnp.max(s, axis=-1, keepdims=True), sink)
                e = jnp.exp2(s - m)
                denom = jnp.sum(e, axis=-1, keepdims=True) + jnp.exp2(sink - m)
                mix_ref[rows, h * HEAD_DIM:(h + 1) * HEAD_DIM] = (
                    _dot(e.astype(BF16), v_cat) / denom).astype(mix_ref.dtype)
        kprev, vprev = kcur, vcur
    kwin_ref[0] = jnp.concatenate(kn, axis=1)
    vwin_ref[0] = vcur.astype(F32)

    tril = (lax.broadcasted_iota(jnp.int32, (RET_CHUNK, RET_CHUNK), 0)
            >= lax.broadcasted_iota(jnp.int32, (RET_CHUNK, RET_CHUNK), 1))
    for h in range(N_RET_HEADS):
        hs = slice(h * RET_DIM, (h + 1) * RET_DIM)
        col_of = lambda c: slice(c + h * RET_DIM, c + (h + 1) * RET_DIM)
        state = s_ref[h]
        for c in range(MIX_ROWS // RET_CHUNK):
            rows = slice(c * RET_CHUNK, (c + 1) * RET_CHUNK)
            cos2, sin2 = cos_ref[rows, :], sin_ref[rows, :]
            q = (_rotary(z_ref[rows, col_of(COL_RQ)].astype(F32), cos2, sin2) * qdec_ref[:, hs]).astype(BF16)
            k = (_rotary(z_ref[rows, col_of(COL_RK)].astype(F32), cos2, sin2) * kdec_ref[:, hs]).astype(BF16)
            v = z_ref[rows, col_of(COL_RV)]
            sc = jnp.where(tril, _dot_nt(q, k), 0.0)
            o = _dot(sc.astype(BF16), v) + _dot(q, state.astype(BF16))
            state = jnp.where(live, (state + _dot_tn(k, v)) * cdec_ref[h], state)
            o = o * lax.rsqrt(jnp.mean(o * o, axis=-1, keepdims=True) + NORM_EPS)
            gate = z_ref[rows, col_of(COL_RG)].astype(F32)
            mix_ref[rows, ATTN_WIDTH + h * RET_DIM:ATTN_WIDTH + (h + 1) * RET_DIM] = (
                o * rg_ref[:, hs] * (gate * jax.nn.sigmoid(gate))).astype(mix_ref.dtype)
        s_ref[h] = state
        sout_ref[0, h] = state


def _mixer_out_kernel(nsteps, z_ref, kp_ref, vp_ref, qg_ref, kg_ref, sink_ref, cos_ref, sin_ref, qdec_ref,
                      kdec_ref, cdec_ref, rg_ref, h_ref, hs_ref, mixs_ref, w_ref,
                      o_ref, os_ref, kwin_ref, vwin_ref, sout_ref, mix_a, mix_b, s_ref):
    t = pl.program_id(0)
    tiles = pl.num_programs(0) - 1
    n = jnp.minimum(t, tiles - 1) % nsteps
    live = t < tiles

    @pl.when(t == 0)
    def _():
        mix_b[...] = jnp.zeros_like(mix_b)

    @pl.when(n == 0)
    def _():
        s_ref[...] = jnp.zeros_like(s_ref)

    def step(dst, src):
        o_ref[...] = h_ref[...] + _dot(src[...], w_ref[...])
        _mix_tile(z_ref, kp_ref, vp_ref, qg_ref, kg_ref, sink_ref, cos_ref, sin_ref, qdec_ref, kdec_ref,
                  cdec_ref, rg_ref, dst, kwin_ref, vwin_ref, sout_ref, s_ref, n == 0, live)

    @pl.when(t % 2 == 0)
    def _():
        step(mix_a, mix_b)

    @pl.when(t % 2 == 1)
    def _():
        step(mix_b, mix_a)

    @pl.when(t == 1)
    def _():
        os_ref[...] = hs_ref[...] + _dot(mixs_ref[...], w_ref[...])


def _rope_tables(pos):
    half = RET_DIM // 2
    inv = ROPE_BASE ** (-np.arange(half, dtype=np.float64) / half)
    ang = np.asarray(pos, np.float64)[:, None] * inv[None, :]
    cos, sin = np.cos(ang), np.sin(ang)
    return (np.concatenate([cos, cos], axis=-1).astype(np.float32),
            np.concatenate([-sin, sin], axis=-1).astype(np.float32))


def _ret_log_decay():
    return np.log1p(-np.exp2(-5.0 - np.arange(N_RET_HEADS, dtype=np.float64)))


def _mixer_out(z, h, hs, mixs, w_out, qg, kg, sinks, rg, batch, seq):
    nsteps = seq // MIX_ROWS
    tiles = batch * nsteps
    cos2, sin2 = _rope_tables(np.arange(seq))
    lg = _ret_log_decay()
    steps = np.arange(RET_CHUNK, dtype=np.float64)[:, None, None] + 1.0
    lanes = lambda t: np.broadcast_to(t, (RET_CHUNK, N_RET_HEADS, RET_DIM)).reshape(RET_CHUNK, RET_WIDTH).astype(np.float32)
    q_dec = lanes(np.exp(lg[None, :, None] * steps))
    k_dec = lanes(np.exp(-lg[None, :, None] * steps) * RET_K_SCALE)
    chunk_dec = np.exp(lg * RET_CHUNK).astype(np.float32)
    smem = pl.BlockSpec(memory_space=pltpu.SMEM)
    win = jax.ShapeDtypeStruct((batch, BLK, KV_WIDTH), F32)
    blks = MIX_ROWS // BLK
    tile = lambda t: jnp.minimum(t, tiles - 1)
    prev_blk = lambda t: tile(t) * blks - jnp.minimum(tile(t) % nsteps, 1)
    behind = lambda t: (jnp.maximum(t - 1, 0), 0)
    seq_of = lambda t: tile(t) // nsteps
    return pl.pallas_call(
        functools.partial(_mixer_out_kernel, nsteps),
        grid=(tiles + 1,),
        in_specs=[
            pl.BlockSpec((MIX_ROWS, IN_WIDTH), lambda t: (tile(t), 0)),
            pl.BlockSpec((BLK, KV_WIDTH), lambda t: (prev_blk(t), COL_AK // KV_WIDTH)),
            pl.BlockSpec((BLK, KV_WIDTH), lambda t: (prev_blk(t), COL_AV // KV_WIDTH)),
            _whole(qg), _whole(kg), smem,
            pl.BlockSpec((MIX_ROWS, RET_DIM), lambda t: (tile(t) % nsteps, 0)),
            pl.BlockSpec((MIX_ROWS, RET_DIM), lambda t: (tile(t) % nsteps, 0)),
            _whole(q_dec), _whole(k_dec), smem, _whole(rg),
            pl.BlockSpec((MIX_ROWS, D_MODEL), behind), _whole(hs), _whole(mixs), _whole(w_out),
        ],
        out_specs=[
            pl.BlockSpec((MIX_ROWS, D_MODEL), behind), _whole(hs),
            pl.BlockSpec((1, BLK, KV_WIDTH), lambda t: (seq_of(t), 0, 0)),
            pl.BlockSpec((1, BLK, KV_WIDTH), lambda t: (seq_of(t), 0, 0)),
            pl.BlockSpec((1, N_RET_HEADS, RET_DIM, RET_DIM), lambda t: (seq_of(t), 0, 0, 0)),
        ],
        out_shape=[jax.ShapeDtypeStruct(h.shape, F32), jax.ShapeDtypeStruct(hs.shape, F32), win, win,
                   jax.ShapeDtypeStruct((batch, N_RET_HEADS, RET_DIM, RET_DIM), F32)],
        scratch_shapes=[pltpu.VMEM((MIX_ROWS, D_MODEL), BF16), pltpu.VMEM((MIX_ROWS, D_MODEL), BF16),
                        pltpu.VMEM((N_RET_HEADS, RET_DIM, RET_DIM), F32)],
        compiler_params=_params("arbitrary"),
        name="mixer_out",
    )(z, z, z, qg, kg, sinks, cos2, sin2, q_dec, k_dec, chunk_dec, rg, h, hs, mixs, w_out)


SAMPLE_GROUP = 8


def _attn_sample_kernel(q_ref, kn_ref, vn_ref, kbuf_ref, vbuf_ref, qg_ref, kg_ref, sink_ref,
                        o_ref, kout_ref, vout_ref):
    rows = WINDOW * N_KV_HEADS
    qg = qg_ref[...]
    kg = kg_ref[...]
    sink = sink_ref[...]
    head = lax.broadcasted_iota(jnp.int32, (N_ATTN_HEADS, HEAD_DIM), 0)
    first = head < GQA_GROUP
    own = ((lax.broadcasted_iota(jnp.int32, (N_ATTN_HEADS, rows), 1) % N_KV_HEADS)
           == (lax.broadcasted_iota(jnp.int32, (N_ATTN_HEADS, rows), 0) // GQA_GROUP))
    row = lax.broadcasted_iota(jnp.int32, (rows, HEAD_DIM), 0)
    for b in range(SAMPLE_GROUP):
        qn = _rms(q_ref[b], qg) * ATTN_SCALE
        kn = _rms(kn_ref[b], kg)
        vn = vn_ref[b]
        kbuf = kbuf_ref[b]
        vbuf = vbuf_ref[b]
        s = jnp.where(own, _dot_nt(qn.astype(BF16), kbuf.astype(BF16)), -jnp.inf)
        k_sel = jnp.where(first, kn[0:1], kn[1:2])
        v_sel = jnp.where(first, vn[0:1], vn[1:2])
        s_new = jnp.sum(qn * k_sel, axis=-1, keepdims=True)
        m = jnp.maximum(jnp.maximum(jnp.max(s, axis=-1, keepdims=True), s_new), sink)
        e = jnp.exp(s - m)
        e_new = jnp.exp(s_new - m)
        denom = jnp.sum(e, axis=-1, keepdims=True) + e_new + jnp.exp(sink - m)
        o_ref[b] = (_dot(e.astype(BF16), vbuf.astype(BF16)) + e_new * v_sel) / denom
        for buf, new, out in ((kbuf, kn, kout_ref), (vbuf, vn, vout_ref)):
            shifted = pltpu.roll(buf, rows - N_KV_HEADS, 0)
            out[b] = jnp.where(row == rows - 2, new[0:1], jnp.where(row == rows - 1, new[1:2], shifted))


def _attn_sample(q, kn, vn, kbuf, vbuf, qg, kg, sinks):
    nb = q.shape[0]
    rows = WINDOW * N_KV_HEADS
    blk = lambda *shape: pl.BlockSpec((SAMPLE_GROUP,) + shape, lambda i: (i,) + (0,) * len(shape))
    const = lambda shape: pl.BlockSpec(shape, lambda i: (0,) * len(shape))
    return pl.pallas_call(
        _attn_sample_kernel,
        grid=(nb // SAMPLE_GROUP,),
        in_specs=[
            blk(N_ATTN_HEADS, HEAD_DIM), blk(N_KV_HEADS, HEAD_DIM), blk(N_KV_HEADS, HEAD_DIM),
            blk(rows, HEAD_DIM), blk(rows, HEAD_DIM),
            const((1, HEAD_DIM)), const((1, HEAD_DIM)), const((N_ATTN_HEADS, 1)),
        ],
        out_specs=[blk(N_ATTN_HEADS, HEAD_DIM), blk(rows, HEAD_DIM), blk(rows, HEAD_DIM)],
        out_shape=[jax.ShapeDtypeStruct((nb, N_ATTN_HEADS, HEAD_DIM), F32),
                   jax.ShapeDtypeStruct((nb, rows, HEAD_DIM), F32),
                   jax.ShapeDtypeStruct((nb, rows, HEAD_DIM), F32)],
        compiler_params=_params("parallel"),
        name="attn_sample",
    )(q, kn, vn, kbuf, vbuf, qg, kg, sinks)


def _ret_sample_kernel(q_ref, k_ref, v_ref, gate_ref, s0_ref, cos_ref, sin_ref, spread_ref, gamma_ref, rg_ref,
                       o_ref, s_out_ref):
    cos2 = cos_ref[...]
    sin2 = sin_ref[...]
    spread = spread_ref[...]
    for b in range(SAMPLE_GROUP):
        q_cols = _dot_tn(_rotary(q_ref[b], cos2, sin2), spread)
        k_cols = _dot_tn(_rotary(k_ref[b], cos2, sin2) * RET_K_SCALE, spread)
        v = v_ref[b]
        gate = gate_ref[b]
        rows = []
        for h in range(N_RET_HEADS):
            hs = slice(h * RET_DIM, (h + 1) * RET_DIM)
            s_new = s0_ref[b, h] * gamma_ref[h] + k_cols[:, hs] * v[h:h + 1, :]
            s_out_ref[b, h] = s_new
            o = jnp.sum(q_cols[:, hs] * s_new, axis=0, keepdims=True)
            rows.append(o * lax.rsqrt(jnp.mean(o * o, axis=-1, keepdims=True) + NORM_EPS))
        o = jnp.concatenate(rows, axis=0)
        o_ref[b] = o * rg_ref[...] * (gate * jax.nn.sigmoid(gate))


def _ret_sample(q, k, v, gate, s0, rg):
    nb = q.shape[0]
    cos2, sin2 = _rope_tables(PAST_LEN + np.arange(1))
    gamma = np.exp(_ret_log_decay()).astype(np.float32)
    spread = np.kron(np.eye(N_RET_HEADS), np.ones((1, RET_DIM))).astype(np.float32)
    blk = lambda *shape: pl.BlockSpec((SAMPLE_GROUP,) + shape, lambda i: (i,) + (0,) * len(shape))
    const = lambda shape: pl.BlockSpec(shape, lambda i: (0,) * len(shape))
    vec = blk(N_RET_HEADS, RET_DIM)
    return pl.pallas_call(
        _ret_sample_kernel,
        grid=(nb // SAMPLE_GROUP,),
        in_specs=[
            vec, vec, vec, vec, blk(N_RET_HEADS, RET_DIM, RET_DIM),
            const((1, RET_DIM)), const((1, RET_DIM)), const((N_RET_HEADS, RET_WIDTH)),
            pl.BlockSpec(memory_space=pltpu.SMEM),
            const((N_RET_HEADS, RET_DIM)),
        ],
        out_specs=[vec, blk(N_RET_HEADS, RET_DIM, RET_DIM)],
        out_shape=[jax.ShapeDtypeStruct((nb, N_RET_HEADS, RET_DIM), F32),
                   jax.ShapeDtypeStruct((nb, N_RET_HEADS, RET_DIM, RET_DIM), F32)],
        compiler_params=_params("parallel"),
        name="ret_sample",
    )(q, k, v, gate, s0, cos2, sin2, spread, gamma, rg.reshape(N_RET_HEADS, RET_DIM))


def _ffn_kernel(h_ref, hs_ref, g_ref, wg_ref, wu_ref, wd_ref, o_ref, os_ref, f_ref):
    i, j = pl.program_id(0), pl.program_id(1)
    tm = h_ref.shape[0]

    @pl.when(j == 0)
    def _():
        h = h_ref[...]
        f_ref[:tm] = _rms(h, g_ref[...]).astype(BF16)
        o_ref[...] = h

    @pl.when((i == 0) & (j == 0))
    def _():
        hs = hs_ref[...]
        f_ref[tm:] = _rms(hs, g_ref[...]).astype(BF16)
        os_ref[...] = hs

    def swiglu(f):
        gate = _dot(f, wg_ref[...])
        act = (gate * jax.nn.sigmoid(gate) * _dot(f, wu_ref[...])).astype(BF16)
        return _dot(act, wd_ref[...])

    @pl.when(i == 0)
    def _():
        y = swiglu(f_ref[...])
        o_ref[...] += y[:tm]
        os_ref[...] += y[tm:]

    @pl.when(i > 0)
    def _():
        o_ref[...] += swiglu(f_ref[:tm])


def _ffn(h, hs, g, wg, wu, wd, tm):
    m = h.shape[0]
    row = pl.BlockSpec((tm, D_MODEL), lambda i, j: (i, 0))
    return pl.pallas_call(
        _ffn_kernel,
        grid=(m // tm, D_FF // FF_TILE),
        in_specs=[
            row, _whole(hs), _whole(g),
            pl.BlockSpec((None, D_MODEL, FF_TILE), lambda i, j: (j, 0, 0)),
            pl.BlockSpec((None, D_MODEL, FF_TILE), lambda i, j: (j, 0, 0)),
            pl.BlockSpec((FF_TILE, D_MODEL), lambda i, j: (j, 0)),
        ],
        out_specs=[row, _whole(hs)],
        out_shape=[jax.ShapeDtypeStruct((m, D_MODEL), F32), jax.ShapeDtypeStruct(hs.shape, F32)],
        scratch_shapes=[pltpu.VMEM((tm + hs.shape[0], D_MODEL), BF16)],
        compiler_params=_params("arbitrary", "arbitrary"),
        name="ffn",
    )(h, hs, g, wg, wu, wd)


def _ple_kernel(h_ref, p_ref, hs_ref, ps_ref, g_ref, wp_ref, wg_ref, o_ref, os_ref):
    tm = h_ref.shape[0]

    def embed(h, p):
        gate = jax.nn.sigmoid(_dot(_rms(h, g_ref[...]).astype(BF16), wg_ref[...]))
        return h + _dot(p.astype(BF16), wp_ref[...]) * gate

    @pl.when(pl.program_id(0) == 0)
    def _():
        y = embed(jnp.concatenate([h_ref[...], hs_ref[...]], axis=0),
                  jnp.concatenate([p_ref[...], ps_ref[...]], axis=0))
        o_ref[...] = y[:tm]
        os_ref[...] = y[tm:]

    @pl.when(pl.program_id(0) > 0)
    def _():
        o_ref[...] = embed(h_ref[...], p_ref[...])


def _ple(h, p, hs, ps, g, w_ple, w_pg, tm):
    m = h.shape[0]
    row = pl.BlockSpec((tm, D_MODEL), lambda i: (i, 0))
    return pl.pallas_call(
        _ple_kernel,
        grid=(m // tm,),
        in_specs=[row, pl.BlockSpec((tm, PLE_DIM), lambda i: (i, 0)), _whole(hs), _whole(ps), _whole(g),
                  _whole(w_ple), _whole(w_pg)],
        out_specs=[row, _whole(hs)],
        out_shape=[jax.ShapeDtypeStruct((m, D_MODEL), F32), jax.ShapeDtypeStruct(hs.shape, F32)],
        compiler_params=_params("arbitrary"),
        name="ple",
    )(h, p, hs, ps, g, w_ple, w_pg)


PLE_ROWS = 1024
FFN_ROWS = 1024


def _dense_tail(h, p, hs, ps, w):
    h, hs = _ffn(h, hs, w["ffn_norm_g"], w["w_gate"], w["w_up"], w["w_down"], FFN_ROWS)
    return _ple(h, p, hs, ps, w["ple_norm_g"], w["w_ple"], w["w_ple_gate"], PLE_ROWS)


def _mix_sample(z, k_buf, v_buf, s0, w):
    nb = z.shape[0]
    heads = lambda c, n: z[:, c:c + n * HEAD_DIM].reshape(nb, n, HEAD_DIM)
    o_attn, k_win, v_win = _attn_sample(
        heads(COL_AQ, N_ATTN_HEADS), heads(COL_AK, N_KV_HEADS), heads(COL_AV, N_KV_HEADS),
        k_buf.reshape(nb, WINDOW * N_KV_HEADS, HEAD_DIM), v_buf.reshape(nb, WINDOW * N_KV_HEADS, HEAD_DIM),
        w["q_norm_g"], w["k_norm_g"], w["attn_sinks"].reshape(N_ATTN_HEADS, 1))
    o_ret, state = _ret_sample(
        heads(COL_RQ, N_RET_HEADS), heads(COL_RK, N_RET_HEADS), heads(COL_RV, N_RET_HEADS), heads(COL_RG, N_RET_HEADS), s0, w["ret_out_g"])
    mix = jnp.concatenate([o_attn.reshape(nb, ATTN_WIDTH), o_ret.reshape(nb, RET_WIDTH)], axis=1).astype(BF16)
    return (mix, k_win.reshape(nb, WINDOW, N_KV_HEADS, HEAD_DIM), v_win.reshape(nb, WINDOW, N_KV_HEADS, HEAD_DIM),
            state)


def kernel(x_prompt, x_sample, cache_k_win, cache_v_win, state_ret, p_prompt, p_sample,
           attn_norm_g, w_in, q_norm_g, k_norm_g, attn_sinks, ret_out_g, w_out,
           ffn_norm_g, w_gate, w_up, w_down, ple_norm_g, w_ple, w_ple_gate):
    assert w_in.shape[0] == 1 and x_sample.shape[1] == 1 and cache_k_win.shape[2] == WINDOW
    y_p, y_s = x_prompt, x_sample
    outs = [[] for _ in range(6)]
    for l in range(w_in.shape[0]):
        w = {
            "attn_norm_g": attn_norm_g[l].reshape(1, D_MODEL),
            "q_norm_g": q_norm_g[l].reshape(1, HEAD_DIM),
            "k_norm_g": k_norm_g[l].reshape(1, HEAD_DIM),
            "attn_sinks": attn_sinks[l],
            "ret_out_g": ret_out_g[l].reshape(1, RET_WIDTH),
            "ffn_norm_g": ffn_norm_g[l].reshape(1, D_MODEL),
            "ple_norm_g": ple_norm_g[l].reshape(1, D_MODEL),
        }
        batch, seq, _ = y_p.shape
        nb = y_s.shape[0]
        h_p = y_p.reshape(batch * seq, D_MODEL)
        h_s = y_s.reshape(nb, D_MODEL)
        z_p, z_s, *cast = _in_proj(h_p, h_s, w["attn_norm_g"], w_in[l], w_gate[l], w_up[l], w_down[l],
                                   w_out[l], w_ple_gate[l], w_ple[l])
        w.update(zip(("w_gate", "w_up", "w_down", "w_out", "w_ple_gate", "w_ple"), cast))
        mix_s, ks, vs, ss = _mix_sample(z_s, cache_k_win[l], cache_v_win[l], state_ret[l], w)
        h_p, h_s, kp, vp, sp = _mixer_out(z_p, h_p, h_s, mix_s, w["w_out"], w["q_norm_g"], w["k_norm_g"],
                                          w["attn_sinks"], w["ret_out_g"], batch, seq)
        h_p, h_s = _dense_tail(h_p, p_prompt[l].reshape(batch * seq, PLE_DIM),
                               h_s, p_sample[l].reshape(nb, PLE_DIM), w)
        y_p, y_s = h_p.reshape(batch, seq, D_MODEL), h_s.reshape(nb, 1, D_MODEL)
        kp = kp.reshape(batch, BLK, N_KV_HEADS, HEAD_DIM)
        vp = vp.reshape(batch, BLK, N_KV_HEADS, HEAD_DIM)
        for acc, val in zip(outs, (kp, vp, sp, ks, vs, ss)):
            acc.append(val)
    kp, vp, sp, ks, vs, ss = (jnp.stack(o) for o in outs)
    return (y_p, y_s, kp, vp, sp, ks, vs, ss)
```

```python
import functools

import numpy as np
import jax
import jax.numpy as jnp
from jax import lax
from jax.experimental import pallas as pl
from jax.experimental.pallas import tpu as pltpu

D_MODEL = 2048
HEAD_DIM = 128
N_ATTN_HEADS = 8
N_KV_HEADS = 2
GQA_GROUP = N_ATTN_HEADS // N_KV_HEADS
ATTN_WIDTH = N_ATTN_HEADS * HEAD_DIM
KV_WIDTH = N_KV_HEADS * HEAD_DIM
WINDOW = 128
BLK = 128
N_RET_HEADS = 8
RET_DIM = 128
RET_WIDTH = N_RET_HEADS * RET_DIM
D_FF = 5632
IN_WIDTH = 5632
PLE_DIM = 256
PAST_LEN = 16384
ROPE_BASE = 10000.0
NORM_EPS = 1e-6
ATTN_SCALE = HEAD_DIM ** -0.5
RET_K_SCALE = RET_DIM ** -0.5

COL_RQ, COL_RK, COL_RV, COL_RG = 0, RET_WIDTH, 2 * RET_WIDTH, 3 * RET_WIDTH
COL_AQ = 4 * RET_WIDTH
COL_AK = COL_AQ + ATTN_WIDTH
COL_AV = COL_AK + KV_WIDTH
IN_TILE = 512
IN_TILES = IN_WIDTH // IN_TILE
IN_ROT = (ATTN_WIDTH + 2 * KV_WIDTH) // IN_TILE

VMEM_LIMIT = 56 * 1024 * 1024

F32 = jnp.float32
BF16 = jnp.bfloat16


def _params(*sem):
    return pltpu.CompilerParams(dimension_semantics=sem, vmem_limit_bytes=VMEM_LIMIT)


def _rms(x, g):
    return x * lax.rsqrt(jnp.mean(x * x, axis=-1, keepdims=True) + NORM_EPS) * g


def _dot(a, b):
    return jnp.dot(a, b, preferred_element_type=F32)


def _dot_nt(a, b):
    return lax.dot_general(a, b, (((1,), (1,)), ((), ())), preferred_element_type=F32)


def _dot_tn(a, b):
    return lax.dot_general(a, b, (((0,), (0,)), ((), ())), preferred_element_type=F32)


def _whole(a):
    return pl.BlockSpec(a.shape, lambda *_: (0,) * a.ndim)


def _in_proj_kernel(x_hbm, xs_ref, g_ref, w_ref, wg_ref, wu_ref, wd_ref, wo_ref, wpg_ref, wple_ref,
                    z_ref, zs_ref, wg_o, wu_o, wd_o, wo_o, wpg_o, wple_o,
                    xbuf, xsem, a_ref, wres):
    i, j = pl.program_id(0), pl.program_id(1)

    def x_copy(tile):
        return pltpu.make_async_copy(x_hbm.at[pl.ds(tile * IN_ROWS, IN_ROWS)], xbuf, xsem)

    @pl.when((i == 0) & (j == 0))
    def _():
        x_copy(0).start()
        a_ref[IN_ROWS:] = _rms(xs_ref[...], g_ref[...]).astype(BF16)

    @pl.when(j == 0)
    def _():
        x_copy(i).wait()
        a_ref[:IN_ROWS] = _rms(xbuf[...], g_ref[...]).astype(BF16)

        @pl.when(i + 1 < pl.num_programs(0))
        def _():
            x_copy(i + 1).start()

    def cast_weights():
        for src, dst in ((wg_ref, wg_o), (wu_ref, wu_o), (wd_ref, wd_o), (wo_ref, wo_o), (wpg_ref, wpg_o),
                         (wple_ref, wple_o)):
            dst[...] = src[...].astype(BF16)

    @pl.when(i == 0)
    def _():
        wres[j] = w_ref[...].astype(BF16)
        y = _dot(a_ref[...], wres[j])
        z_ref[...] = y[:IN_ROWS].astype(z_ref.dtype)
        zs_ref[...] = y[IN_ROWS:]
        cast_weights()

    @pl.when(i > 0)
    def _():
        z_ref[...] = _dot(a_ref[:IN_ROWS], wres[j]).astype(z_ref.dtype)
        cast_weights()


IN_ROWS = 1024
CAST_SPLIT = 8
FF_TILE = 512


def _in_proj(x, xs, g, w_in, w_gate, w_up, w_down, w_out, w_pg, w_ple):
    m = x.shape[0]
    ni = m // IN_ROWS
    steps = ni * IN_TILES
    assert m % IN_ROWS == 0 and FF_TILE == IN_TILE and D_FF == IN_WIDTH
    gu_rows, d_rows = D_MODEL // ni, D_FF // steps
    sq_rows, ple_rows = D_MODEL // (ni * CAST_SPLIT), PLE_DIM // ni
    assert gu_rows * ni == D_MODEL and d_rows * steps == D_FF and sq_rows % 16 == 0 and ple_rows % 16 == 0
    sq_idx = lambda i, j: (i * CAST_SPLIT + jnp.minimum(j, CAST_SPLIT - 1), 0)
    first = lambda i, j: jnp.where(i == 0, j, IN_TILES - 1)
    return pl.pallas_call(
        _in_proj_kernel,
        grid=(ni, IN_TILES),
        in_specs=[
            pl.BlockSpec(memory_space=pl.ANY),
            pl.BlockSpec(xs.shape, lambda i, j: (0, 0)),
            pl.BlockSpec((1, D_MODEL), lambda i, j: (0, 0)),
            pl.BlockSpec((D_MODEL, IN_TILE), lambda i, j: (0, (first(i, j) + IN_ROT) % IN_TILES)),
            pl.BlockSpec((gu_rows, FF_TILE), lambda i, j: (i, j)),
            pl.BlockSpec((gu_rows, FF_TILE), lambda i, j: (i, j)),
            pl.BlockSpec((d_rows, D_MODEL), lambda i, j: (i * IN_TILES + j, 0)),
            pl.BlockSpec((sq_rows, D_MODEL), sq_idx),
            pl.BlockSpec((sq_rows, D_MODEL), sq_idx),
            pl.BlockSpec((ple_rows, D_MODEL), lambda i, j: (i, 0)),
        ],
        out_specs=[
            pl.BlockSpec((IN_ROWS, IN_TILE), lambda i, j: (i, j)),
            pl.BlockSpec((xs.shape[0], IN_TILE), lambda i, j: (0, first(i, j))),
            pl.BlockSpec((None, gu_rows, FF_TILE), lambda i, j: (j, i, 0)),
            pl.BlockSpec((None, gu_rows, FF_TILE), lambda i, j: (j, i, 0)),
            pl.BlockSpec((d_rows, D_MODEL), lambda i, j: (i * IN_TILES + j, 0)),
            pl.BlockSpec((sq_rows, D_MODEL), sq_idx),
            pl.BlockSpec((sq_rows, D_MODEL), sq_idx),
            pl.BlockSpec((ple_rows, D_MODEL), lambda i, j: (i, 0)),
        ],
        out_shape=[
            jax.ShapeDtypeStruct((m, IN_WIDTH), BF16),
            jax.ShapeDtypeStruct((xs.shape[0], IN_WIDTH), F32),
            jax.ShapeDtypeStruct((D_FF // FF_TILE, D_MODEL, FF_TILE), BF16),
            jax.ShapeDtypeStruct((D_FF // FF_TILE, D_MODEL, FF_TILE), BF16),
            jax.ShapeDtypeStruct((D_FF, D_MODEL), BF16),
            jax.ShapeDtypeStruct((D_MODEL, D_MODEL), BF16),
            jax.ShapeDtypeStruct((D_MODEL, D_MODEL), BF16),
            jax.ShapeDtypeStruct((PLE_DIM, D_MODEL), BF16),
        ],
        scratch_shapes=[
            pltpu.VMEM((IN_ROWS, D_MODEL), F32),
            pltpu.SemaphoreType.DMA(()),
            pltpu.VMEM((IN_ROWS + xs.shape[0], D_MODEL), BF16),
            pltpu.VMEM((IN_TILES, D_MODEL, IN_TILE), BF16),
        ],
        compiler_params=_params("arbitrary", "arbitrary"),
        name="in_proj",
    )(x, xs, g, w_in, w_gate, w_up, w_down, w_out, w_pg, w_ple)


MIX_ROWS = 256
RET_CHUNK = 256
LOG2E = 1.4426950408889634


def _rotary(x, cos2, sin2):
    return x * cos2 + pltpu.roll(x, RET_DIM // 2, 1) * sin2


def _mix_tile(z_ref, kp_ref, vp_ref, qg_ref, kg_ref, sink_ref, cos_ref, sin_ref, qdec_ref, kdec_ref,
              cdec_ref, rg_ref, mix_ref, kwin_ref, vwin_ref, sout_ref, s_ref, first):
    qg = qg_ref[...] * (ATTN_SCALE * LOG2E)
    kg = kg_ref[...]
    row = lax.broadcasted_iota(jnp.int32, (BLK, 2 * BLK), 0)
    col = lax.broadcasted_iota(jnp.int32, (BLK, 2 * BLK), 1)
    cur_ok = (col >= BLK) & ((col - BLK) <= row)
    prev_ok = (col < BLK) & (col >= row)
    kp = kp_ref[...].astype(F32)
    kprev = jnp.concatenate([_rms(kp[:, h * HEAD_DIM:(h + 1) * HEAD_DIM], kg) for h in range(N_KV_HEADS)],
                            axis=1).astype(BF16)
    vprev = vp_ref[...]
    for blk in range(MIX_ROWS // BLK):
        rows = slice(blk * BLK, (blk + 1) * BLK)
        mask = cur_ok | (prev_ok & jnp.logical_not(first)) if blk == 0 else cur_ok | prev_ok
        k = z_ref[rows, COL_AK:COL_AK + KV_WIDTH].astype(F32)
        kn = [_rms(k[:, h * HEAD_DIM:(h + 1) * HEAD_DIM], kg) for h in range(N_KV_HEADS)]
        kcur = jnp.concatenate(kn, axis=1).astype(BF16)
        vcur = z_ref[rows, COL_AV:COL_AV + KV_WIDTH]
        for kh in range(N_KV_HEADS):
            sl = slice(kh * HEAD_DIM, (kh + 1) * HEAD_DIM)
            k_cat = jnp.concatenate([kprev[:, sl], kcur[:, sl]], axis=0)
            v_cat = jnp.concatenate([vprev[:, sl], vcur[:, sl]], axis=0)
            for g in range(GQA_GROUP):
                h = kh * GQA_GROUP + g
                q = z_ref[rows, COL_AQ + h * HEAD_DIM:COL_AQ + (h + 1) * HEAD_DIM].astype(F32)
                s = jnp.where(mask, _dot_nt(_rms(q, qg).astype(BF16), k_cat), -jnp.inf)
                sink = sink_ref[h] * LOG2E
                m = jnp.maximum(jnp.max(s, axis=-1, keepdims=True), sink)
                e = jnp.exp2(s - m)
                denom = jnp.sum(e, axis=-1, keepdims=True) + jnp.exp2(sink - m)
                mix_ref[rows, h * HEAD_DIM:(h + 1) * HEAD_DIM] = (
                    _dot(e.astype(BF16), v_cat) / denom).astype(mix_ref.dtype)
        kprev, vprev = kcur, vcur
    kwin_ref[0] = jnp.concatenate(kn, axis=1)
    vwin_ref[0] = vcur.astype(F32)

    tril = (lax.broadcasted_iota(jnp.int32, (RET_CHUNK, RET_CHUNK), 0)
            >= lax.broadcasted_iota(jnp.int32, (RET_CHUNK, RET_CHUNK), 1))
    for h in range(N_RET_HEADS):
        hs = slice(h * RET_DIM, (h + 1) * RET_DIM)
        col_of = lambda c: slice(c + h * RET_DIM, c + (h + 1) * RET_DIM)
        state = s_ref[h]
        for c in range(MIX_ROWS // RET_CHUNK):
            rows = slice(c * RET_CHUNK, (c + 1) * RET_CHUNK)
            cos2, sin2 = cos_ref[rows, :], sin_ref[rows, :]
            q = (_rotary(z_ref[rows, col_of(COL_RQ)].astype(F32), cos2, sin2) * qdec_ref[:, hs]).astype(BF16)
            k = (_rotary(z_ref[rows, col_of(COL_RK)].astype(F32), cos2, sin2) * kdec_ref[:, hs]).astype(BF16)
            v = z_ref[rows, col_of(COL_RV)]
            sc = jnp.where(tril, _dot_nt(q, k), 0.0)
            o = _dot(sc.astype(BF16), v) + _dot(q, state.astype(BF16))
            state = (state + _dot_tn(k, v)) * cdec_ref[h]
            o = o * lax.rsqrt(jnp.mean(o * o, axis=-1, keepdims=True) + NORM_EPS)
            gate = z_ref[rows, col_of(COL_RG)].astype(F32)
            mix_ref[rows, ATTN_WIDTH + h * RET_DIM:ATTN_WIDTH + (h + 1) * RET_DIM] = (
                o * rg_ref[:, hs] * (gate * jax.nn.sigmoid(gate))).astype(mix_ref.dtype)
        s_ref[h] = state
        sout_ref[0, h] = state


def _mixer_out_kernel(nsteps, tiles, z_ref, kp_ref, vp_ref, qg_ref, kg_ref, sink_ref, cos_ref, sin_ref,
                      qdec_ref, kdec_ref, cdec_ref, rg_ref, h_ref, hs_ref, mixs_ref, w_ref,
                      o_ref, os_ref, kwin_ref, vwin_ref, sout_ref, mix_a, mix_b, s_ref):
    t = pl.program_id(0)

    @pl.when(t % nsteps == 0)
    def _():
        s_ref[...] = jnp.zeros_like(s_ref)

    def mix(dst):
        _mix_tile(z_ref, kp_ref, vp_ref, qg_ref, kg_ref, sink_ref, cos_ref, sin_ref, qdec_ref, kdec_ref,
                  cdec_ref, rg_ref, dst, kwin_ref, vwin_ref, sout_ref, s_ref, t % nsteps == 0)

    def project(src):
        o_ref[...] = h_ref[...] + _dot(src[...], w_ref[...])

    @pl.when(t == 0)
    def _():
        mix(mix_a)

    @pl.when((t > 0) & (t < tiles) & (t % 2 == 0))
    def _():
        project(mix_b)
        mix(mix_a)

    @pl.when((t < tiles) & (t % 2 == 1))
    def _():
        project(mix_a)
        mix(mix_b)

    @pl.when(t == tiles)
    def _():
        project(mix_b if tiles % 2 == 0 else mix_a)

    @pl.when(t == 1)
    def _():
        os_ref[...] = hs_ref[...] + _dot(mixs_ref[...], w_ref[...])


def _rope_tables(pos):
    half = RET_DIM // 2
    inv = ROPE_BASE ** (-np.arange(half, dtype=np.float64) / half)
    ang = np.asarray(pos, np.float64)[:, None] * inv[None, :]
    cos, sin = np.cos(ang), np.sin(ang)
    return (np.concatenate([cos, cos], axis=-1).astype(np.float32),
            np.concatenate([-sin, sin], axis=-1).astype(np.float32))


def _ret_log_decay():
    return np.log1p(-np.exp2(-5.0 - np.arange(N_RET_HEADS, dtype=np.float64)))


def _mixer_out(z, h, hs, mixs, w_out, qg, kg, sinks, rg, batch, seq):
    nsteps = seq // MIX_ROWS
    tiles = batch * nsteps
    cos2, sin2 = _rope_tables(np.arange(seq))
    lg = _ret_log_decay()
    steps = np.arange(RET_CHUNK, dtype=np.float64)[:, None, None] + 1.0
    lanes = lambda t: np.broadcast_to(t, (RET_CHUNK, N_RET_HEADS, RET_DIM)).reshape(RET_CHUNK, RET_WIDTH).astype(np.float32)
    q_dec = lanes(np.exp(lg[None, :, None] * steps))
    k_dec = lanes(np.exp(-lg[None, :, None] * steps) * RET_K_SCALE)
    chunk_dec = np.exp(lg * RET_CHUNK).astype(np.float32)
    smem = pl.BlockSpec(memory_space=pltpu.SMEM)
    win = jax.ShapeDtypeStruct((batch, BLK, KV_WIDTH), F32)
    blks = MIX_ROWS // BLK
    tile = lambda t: jnp.minimum(t, tiles - 1)
    prev_blk = lambda t: tile(t) * blks - jnp.minimum(tile(t) % nsteps, 1)
    behind = lambda t: (jnp.maximum(t - 1, 0), 0)
    seq_of = lambda t: tile(t) // nsteps
    return pl.pallas_call(
        functools.partial(_mixer_out_kernel, nsteps, tiles),
        grid=(tiles + 1,),
        in_specs=[
            pl.BlockSpec((MIX_ROWS, IN_WIDTH), lambda t: (tile(t), 0)),
            pl.BlockSpec((BLK, KV_WIDTH), lambda t: (prev_blk(t), COL_AK // KV_WIDTH)),
            pl.BlockSpec((BLK, KV_WIDTH), lambda t: (prev_blk(t), COL_AV // KV_WIDTH)),
            _whole(qg), _whole(kg), smem,
            pl.BlockSpec((MIX_ROWS, RET_DIM), lambda t: (tile(t) % nsteps, 0)),
            pl.BlockSpec((MIX_ROWS, RET_DIM), lambda t: (tile(t) % nsteps, 0)),
            _whole(q_dec), _whole(k_dec), smem, _whole(rg),
            pl.BlockSpec((MIX_ROWS, D_MODEL), behind), _whole(hs), _whole(mixs), _whole(w_out),
        ],
        out_specs=[
            pl.BlockSpec((MIX_ROWS, D_MODEL), behind), _whole(hs),
            pl.BlockSpec((1, BLK, KV_WIDTH), lambda t: (seq_of(t), 0, 0)),
            pl.BlockSpec((1, BLK, KV_WIDTH), lambda t: (seq_of(t), 0, 0)),
            pl.BlockSpec((1, N_RET_HEADS, RET_DIM, RET_DIM), lambda t: (seq_of(t), 0, 0, 0)),
        ],
        out_shape=[jax.ShapeDtypeStruct(h.shape, F32), jax.ShapeDtypeStruct(hs.shape, F32), win, win,
                   jax.ShapeDtypeStruct((batch, N_RET_HEADS, RET_DIM, RET_DIM), F32)],
        scratch_shapes=[pltpu.VMEM((MIX_ROWS, D_MODEL), BF16), pltpu.VMEM((MIX_ROWS, D_MODEL), BF16),
                        pltpu.VMEM((N_RET_HEADS, RET_DIM, RET_DIM), F32)],
        compiler_params=_params("arbitrary"),
        name="mixer_out",
    )(z, z, z, qg, kg, sinks, cos2, sin2, q_dec, k_dec, chunk_dec, rg, h, hs, mixs, w_out)


SAMPLE_GROUP = 8


def _attn_sample_kernel(q_ref, kn_ref, vn_ref, kbuf_ref, vbuf_ref, qg_ref, kg_ref, sink_ref,
                        o_ref, kout_ref, vout_ref):
    rows = WINDOW * N_KV_HEADS
    qg = qg_ref[...]
    kg = kg_ref[...]
    sink = sink_ref[...]
    head = lax.broadcasted_iota(jnp.int32, (N_ATTN_HEADS, HEAD_DIM), 0)
    first = head < GQA_GROUP
    own = ((lax.broadcasted_iota(jnp.int32, (N_ATTN_HEADS, rows), 1) % N_KV_HEADS)
           == (lax.broadcasted_iota(jnp.int32, (N_ATTN_HEADS, rows), 0) // GQA_GROUP))
    row = lax.broadcasted_iota(jnp.int32, (rows, HEAD_DIM), 0)
    for b in range(SAMPLE_GROUP):
        qn = _rms(q_ref[b], qg) * ATTN_SCALE
        kn = _rms(kn_ref[b], kg)
        vn = vn_ref[b]
        kbuf = kbuf_ref[b]
        vbuf = vbuf_ref[b]
        s = jnp.where(own, _dot_nt(qn.astype(BF16), kbuf.astype(BF16)), -jnp.inf)
        k_sel = jnp.where(first, kn[0:1], kn[1:2])
        v_sel = jnp.where(first, vn[0:1], vn[1:2])
        s_new = jnp.sum(qn * k_sel, axis=-1, keepdims=True)
        m = jnp.maximum(jnp.maximum(jnp.max(s, axis=-1, keepdims=True), s_new), sink)
        e = jnp.exp(s - m)
        e_new = jnp.exp(s_new - m)
        denom = jnp.sum(e, axis=-1, keepdims=True) + e_new + jnp.exp(sink - m)
        o_ref[b] = (_dot(e.astype(BF16), vbuf.astype(BF16)) + e_new * v_sel) / denom
        for buf, new, out in ((kbuf, kn, kout_ref), (vbuf, vn, vout_ref)):
            shifted = pltpu.roll(buf, rows - N_KV_HEADS, 0)
            out[b] = jnp.where(row == rows - 2, new[0:1], jnp.where(row == rows - 1, new[1:2], shifted))


def _attn_sample(q, kn, vn, kbuf, vbuf, qg, kg, sinks):
    nb = q.shape[0]
    rows = WINDOW * N_KV_HEADS
    blk = lambda *shape: pl.BlockSpec((SAMPLE_GROUP,) + shape, lambda i: (i,) + (0,) * len(shape))
    const = lambda shape: pl.BlockSpec(shape, lambda i: (0,) * len(shape))
    return pl.pallas_call(
        _attn_sample_kernel,
        grid=(nb // SAMPLE_GROUP,),
        in_specs=[
            blk(N_ATTN_HEADS, HEAD_DIM), blk(N_KV_HEADS, HEAD_DIM), blk(N_KV_HEADS, HEAD_DIM),
            blk(rows, HEAD_DIM), blk(rows, HEAD_DIM),
            const((1, HEAD_DIM)), const((1, HEAD_DIM)), const((N_ATTN_HEADS, 1)),
        ],
        out_specs=[blk(N_ATTN_HEADS, HEAD_DIM), blk(rows, HEAD_DIM), blk(rows, HEAD_DIM)],
        out_shape=[jax.ShapeDtypeStruct((nb, N_ATTN_HEADS, HEAD_DIM), F32),
                   jax.ShapeDtypeStruct((nb, rows, HEAD_DIM), F32),
                   jax.ShapeDtypeStruct((nb, rows, HEAD_DIM), F32)],
        compiler_params=_params("parallel"),
        name="attn_sample",
    )(q, kn, vn, kbuf, vbuf, qg, kg, sinks)


def _ret_sample_kernel(q_ref, k_ref, v_ref, gate_ref, s0_ref, cos_ref, sin_ref, spread_ref, gamma_ref, rg_ref,
                       o_ref, s_out_ref):
    cos2 = cos_ref[...]
    sin2 = sin_ref[...]
    spread = spread_ref[...]
    for b in range(SAMPLE_GROUP):
        q_cols = _dot_tn(_rotary(q_ref[b], cos2, sin2), spread)
        k_cols = _dot_tn(_rotary(k_ref[b], cos2, sin2) * RET_K_SCALE, spread)
        v = v_ref[b]
        gate = gate_ref[b]
        rows = []
        for h in range(N_RET_HEADS):
            hs = slice(h * RET_DIM, (h + 1) * RET_DIM)
            s_new = s0_ref[b, h] * gamma_ref[h] + k_cols[:, hs] * v[h:h + 1, :]
            s_out_ref[b, h] = s_new
            o = jnp.sum(q_cols[:, hs] * s_new, axis=0, keepdims=True)
            rows.append(o * lax.rsqrt(jnp.mean(o * o, axis=-1, keepdims=True) + NORM_EPS))
        o = jnp.concatenate(rows, axis=0)
        o_ref[b] = o * rg_ref[...] * (gate * jax.nn.sigmoid(gate))


def _ret_sample(q, k, v, gate, s0, rg):
    nb = q.shape[0]
    cos2, sin2 = _rope_tables(PAST_LEN + np.arange(1))
    gamma = np.exp(_ret_log_decay()).astype(np.float32)
    spread = np.kron(np.eye(N_RET_HEADS), np.ones((1, RET_DIM))).astype(np.float32)
    blk = lambda *shape: pl.BlockSpec((SAMPLE_GROUP,) + shape, lambda i: (i,) + (0,) * len(shape))
    const = lambda shape: pl.BlockSpec(shape, lambda i: (0,) * len(shape))
    vec = blk(N_RET_HEADS, RET_DIM)
    return pl.pallas_call(
        _ret_sample_kernel,
        grid=(nb // SAMPLE_GROUP,),
        in_specs=[
            vec, vec, vec, vec, blk(N_RET_HEADS, RET_DIM, RET_DIM),
            const((1, RET_DIM)), const((1, RET_DIM)), const((N_RET_HEADS, RET_WIDTH)),
            pl.BlockSpec(memory_space=pltpu.SMEM),
            const((N_RET_HEADS, RET_DIM)),
        ],
        out_specs=[vec, blk(N_RET_HEADS, RET_DIM, RET_DIM)],
        out_shape=[jax.ShapeDtypeStruct((nb, N_RET_HEADS, RET_DIM), F32),
                   jax.ShapeDtypeStruct((nb, N_RET_HEADS, RET_DIM, RET_DIM), F32)],
        compiler_params=_params("parallel"),
        name="ret_sample",
    )(q, k, v, gate, s0, cos2, sin2, spread, gamma, rg.reshape(N_RET_HEADS, RET_DIM))


def _ffn_kernel(h_ref, hs_ref, g_ref, wg_ref, wu_ref, wd_ref, o_ref, os_ref, f_ref):
    i, j = pl.program_id(0), pl.program_id(1)
    tm = h_ref.shape[0]

    @pl.when(j == 0)
    def _():
        h = h_ref[...]
        f_ref[:tm] = _rms(h, g_ref[...]).astype(BF16)
        o_ref[...] = h

    @pl.when((i == 0) & (j == 0))
    def _():
        hs = hs_ref[...]
        f_ref[tm:] = _rms(hs, g_ref[...]).astype(BF16)
        os_ref[...] = hs

    def swiglu(f):
        gate = _dot(f, wg_ref[...])
        act = (gate * jax.nn.sigmoid(gate) * _dot(f, wu_ref[...])).astype(BF16)
        return _dot(act, wd_ref[...])

    @pl.when(i == 0)
    def _():
        y = swiglu(f_ref[...])
        o_ref[...] += y[:tm]
        os_ref[...] += y[tm:]

    @pl.when(i > 0)
    def _():
        o_ref[...] += swiglu(f_ref[:tm])


def _ffn(h, hs, g, wg, wu, wd, tm):
    m = h.shape[0]
    row = pl.BlockSpec((tm, D_MODEL), lambda i, j: (i, 0))
    return pl.pallas_call(
        _ffn_kernel,
        grid=(m // tm, D_FF // FF_TILE),
        in_specs=[
            row, _whole(hs), _whole(g),
            pl.BlockSpec((None, D_MODEL, FF_TILE), lambda i, j: (j, 0, 0)),
            pl.BlockSpec((None, D_MODEL, FF_TILE), lambda i, j: (j, 0, 0)),
            pl.BlockSpec((FF_TILE, D_MODEL), lambda i, j: (j, 0)),
        ],
        out_specs=[row, _whole(hs)],
        out_shape=[jax.ShapeDtypeStruct((m, D_MODEL), F32), jax.ShapeDtypeStruct(hs.shape, F32)],
        scratch_shapes=[pltpu.VMEM((tm + hs.shape[0], D_MODEL), BF16)],
        compiler_params=_params("arbitrary", "arbitrary"),
        name="ffn",
    )(h, hs, g, wg, wu, wd)


def _ple_kernel(h_ref, p_ref, hs_ref, ps_ref, g_ref, wp_ref, wg_ref, o_ref, os_ref):
    tm = h_ref.shape[0]

    def embed(h, p):
        gate = jax.nn.sigmoid(_dot(_rms(h, g_ref[...]).astype(BF16), wg_ref[...]))
        return h + _dot(p.astype(BF16), wp_ref[...]) * gate

    @pl.when(pl.program_id(0) == 0)
    def _():
        y = embed(jnp.concatenate([h_ref[...], hs_ref[...]], axis=0),
                  jnp.concatenate([p_ref[...], ps_ref[...]], axis=0))
        o_ref[...] = y[:tm]
        os_ref[...] = y[tm:]

    @pl.when(pl.program_id(0) > 0)
    def _():
        o_ref[...] = embed(h_ref[...], p_ref[...])


def _ple(h, p, hs, ps, g, w_ple, w_pg, tm):
    m = h.shape[0]
    row = pl.BlockSpec((tm, D_MODEL), lambda i: (i, 0))
    return pl.pallas_call(
        _ple_kernel,
        grid=(m // tm,),
        in_specs=[row, pl.BlockSpec((tm, PLE_DIM), lambda i: (i, 0)), _whole(hs), _whole(ps), _whole(g),
                  _whole(w_ple), _whole(w_pg)],
        out_specs=[row, _whole(hs)],
        out_shape=[jax.ShapeDtypeStruct((m, D_MODEL), F32), jax.ShapeDtypeStruct(hs.shape, F32)],
        compiler_params=_params("arbitrary"),
        name="ple",
    )(h, p, hs, ps, g, w_ple, w_pg)


PLE_ROWS = 512
FFN_ROWS = 1024


def _dense_tail(h, p, hs, ps, w):
    h, hs = _ffn(h, hs, w["ffn_norm_g"], w["w_gate"], w["w_up"], w["w_down"], FFN_ROWS)
    return _ple(h, p, hs, ps, w["ple_norm_g"], w["w_ple"], w["w_ple_gate"], PLE_ROWS)


def _mix_sample(z, k_buf, v_buf, s0, w):
    nb = z.shape[0]
    heads = lambda c, n: z[:, c:c + n * HEAD_DIM].reshape(nb, n, HEAD_DIM)
    o_attn, k_win, v_win = _attn_sample(
        heads(COL_AQ, N_ATTN_HEADS), heads(COL_AK, N_KV_HEADS), heads(COL_AV, N_KV_HEADS),
        k_buf.reshape(nb, WINDOW * N_KV_HEADS, HEAD_DIM), v_buf.reshape(nb, WINDOW * N_KV_HEADS, HEAD_DIM),
        w["q_norm_g"], w["k_norm_g"], w["attn_sinks"].reshape(N_ATTN_HEADS, 1))
    o_ret, state = _ret_sample(
        heads(COL_RQ, N_RET_HEADS), heads(COL_RK, N_RET_HEADS), heads(COL_RV, N_RET_HEADS), heads(COL_RG, N_RET_HEADS), s0, w["ret_out_g"])
    mix = jnp.concatenate([o_attn.reshape(nb, ATTN_WIDTH), o_ret.reshape(nb, RET_WIDTH)], axis=1).astype(BF16)
    return (mix, k_win.reshape(nb, WINDOW, N_KV_HEADS, HEAD_DIM), v_win.reshape(nb, WINDOW, N_KV_HEADS, HEAD_DIM),
            state)


def kernel(x_prompt, x_sample, cache_k_win, cache_v_win, state_ret, p_prompt, p_sample,
           attn_norm_g, w_in, q_norm_g, k_norm_g, attn_sinks, ret_out_g, w_out,
           ffn_norm_g, w_gate, w_up, w_down, ple_norm_g, w_ple, w_ple_gate):
    assert w_in.shape[0] == 1 and x_sample.shape[1] == 1 and cache_k_win.shape[2] == WINDOW
    y_p, y_s = x_prompt, x_sample
    outs = [[] for _ in range(6)]
    for l in range(w_in.shape[0]):
        w = {
            "attn_norm_g": attn_norm_g[l].reshape(1, D_MODEL),
            "q_norm_g": q_norm_g[l].reshape(1, HEAD_DIM),
            "k_norm_g": k_norm_g[l].reshape(1, HEAD_DIM),
            "attn_sinks": attn_sinks[l],
            "ret_out_g": ret_out_g[l].reshape(1, RET_WIDTH),
            "ffn_norm_g": ffn_norm_g[l].reshape(1, D_MODEL),
            "ple_norm_g": ple_norm_g[l].reshape(1, D_MODEL),
        }
        batch, seq, _ = y_p.shape
        nb = y_s.shape[0]
        h_p = y_p.reshape(batch * seq, D_MODEL)
        h_s = y_s.reshape(nb, D_MODEL)
        z_p, z_s, *cast = _in_proj(h_p, h_s, w["attn_norm_g"], w_in[l], w_gate[l], w_up[l], w_down[l],
                                   w_out[l], w_ple_gate[l], w_ple[l])
        w.update(zip(("w_gate", "w_up", "w_down", "w_out", "w_ple_gate", "w_ple"), cast))
        mix_s, ks, vs, ss = _mix_sample(z_s, cache_k_win[l], cache_v_win[l], state_ret[l], w)
        h_p, h_s, kp, vp, sp = _mixer_out(z_p, h_p, h_s, mix_s, w["w_out"], w["q_norm_g"], w["k_norm_g"],
                                          w["attn_sinks"], w["ret_out_g"], batch, seq)
        h_p, h_s = _dense_tail(h_p, p_prompt[l].reshape(batch * seq, PLE_DIM),
                               h_s, p_sample[l].reshape(nb, PLE_DIM), w)
        y_p, y_s = h_p.reshape(batch, seq, D_MODEL), h_s.reshape(nb, 1, D_MODEL)
        kp = kp.reshape(batch, BLK, N_KV_HEADS, HEAD_DIM)
        vp = vp.reshape(batch, BLK, N_KV_HEADS, HEAD_DIM)
        for acc, val in zip(outs, (kp, vp, sp, ks, vs, ss)):
            acc.append(val)
    kp, vp, sp, ks, vs, ss = (jnp.stack(o) for o in outs)
    return (y_p, y_s, kp, vp, sp, ks, vs, ss)
```

```python
import functools

import numpy as np
import jax
import jax.numpy as jnp
from jax import lax
from jax.experimental import pallas as pl
from jax.experimental.pallas import tpu as pltpu

D_MODEL = 2048
HEAD_DIM = 128
N_ATTN_HEADS = 8
N_KV_HEADS = 2
GQA_GROUP = N_ATTN_HEADS // N_KV_HEADS
ATTN_WIDTH = N_ATTN_HEADS * HEAD_DIM
KV_WIDTH = N_KV_HEADS * HEAD_DIM
WINDOW = 128
BLK = 128
N_RET_HEADS = 8
RET_DIM = 128
RET_WIDTH = N_RET_HEADS * RET_DIM
D_FF = 5632
IN_WIDTH = 5632
PLE_DIM = 256
PAST_LEN = 16384
ROPE_BASE = 10000.0
NORM_EPS = 1e-6
ATTN_SCALE = HEAD_DIM ** -0.5
RET_K_SCALE = RET_DIM ** -0.5

COL_RQ, COL_RK, COL_RV, COL_RG = 0, RET_WIDTH, 2 * RET_WIDTH, 3 * RET_WIDTH
COL_AQ = 4 * RET_WIDTH
COL_AK = COL_AQ + ATTN_WIDTH
COL_AV = COL_AK + KV_WIDTH
IN_TILE = 512
IN_TILES = IN_WIDTH // IN_TILE
IN_ROT = (ATTN_WIDTH + 2 * KV_WIDTH) // IN_TILE

VMEM_LIMIT = 56 * 1024 * 1024

F32 = jnp.float32
BF16 = jnp.bfloat16


def _params(*sem):
    return pltpu.CompilerParams(dimension_semantics=sem, vmem_limit_bytes=VMEM_LIMIT)


def _rms(x, g):
    return x * lax.rsqrt(jnp.mean(x * x, axis=-1, keepdims=True) + NORM_EPS) * g


def _dot(a, b):
    return jnp.dot(a, b, preferred_element_type=F32)


def _dot_nt(a, b):
    return lax.dot_general(a, b, (((1,), (1,)), ((), ())), preferred_element_type=F32)


def _dot_tn(a, b):
    return lax.dot_general(a, b, (((0,), (0,)), ((), ())), preferred_element_type=F32)


def _whole(a):
    return pl.BlockSpec(a.shape, lambda *_: (0,) * a.ndim)


def _in_proj_kernel(x_hbm, xs_ref, g_ref, w_ref, wg_ref, wu_ref, wd_ref, wo_ref, wpg_ref, wple_ref,
                    z_ref, zs_ref, wg_o, wu_o, wd_o, wo_o, wpg_o, wple_o,
                    xbuf, xsem, a_ref, wres):
    i, j = pl.program_id(0), pl.program_id(1)

    def x_copy(tile):
        return pltpu.make_async_copy(x_hbm.at[pl.ds(tile * IN_ROWS, IN_ROWS)], xbuf, xsem)

    @pl.when((i == 0) & (j == 0))
    def _():
        x_copy(0).start()
        a_ref[IN_ROWS:] = _rms(xs_ref[...], g_ref[...]).astype(BF16)

    @pl.when(j == 0)
    def _():
        x_copy(i).wait()
        a_ref[:IN_ROWS] = _rms(xbuf[...], g_ref[...]).astype(BF16)

        @pl.when(i + 1 < pl.num_programs(0))
        def _():
            x_copy(i + 1).start()

    def cast_weights():
        for src, dst in ((wg_ref, wg_o), (wu_ref, wu_o), (wd_ref, wd_o), (wo_ref, wo_o), (wpg_ref, wpg_o),
                         (wple_ref, wple_o)):
            dst[...] = src[...].astype(BF16)

    @pl.when(i == 0)
    def _():
        wres[j] = w_ref[...].astype(BF16)
        y = _dot(a_ref[...], wres[j])
        z_ref[...] = y[:IN_ROWS].astype(z_ref.dtype)
        zs_ref[...] = y[IN_ROWS:]
        cast_weights()

    @pl.when(i > 0)
    def _():
        z_ref[...] = _dot(a_ref[:IN_ROWS], wres[j]).astype(z_ref.dtype)
        cast_weights()


IN_ROWS = 1024
CAST_SPLIT = 8
FF_TILE = 512


def _in_proj(x, xs, g, w_in, w_gate, w_up, w_down, w_out, w_pg, w_ple):
    m = x.shape[0]
    ni = m // IN_ROWS
    steps = ni * IN_TILES
    assert m % IN_ROWS == 0 and FF_TILE == IN_TILE and D_FF == IN_WIDTH
    gu_rows, d_rows = D_MODEL // ni, D_FF // steps
    sq_rows, ple_rows = D_MODEL // (ni * CAST_SPLIT), PLE_DIM // ni
    assert gu_rows * ni == D_MODEL and d_rows * steps == D_FF and sq_rows % 16 == 0 and ple_rows % 16 == 0
    sq_idx = lambda i, j: (i * CAST_SPLIT + jnp.minimum(j, CAST_SPLIT - 1), 0)
    first = lambda i, j: jnp.where(i == 0, j, IN_TILES - 1)
    return pl.pallas_call(
        _in_proj_kernel,
        grid=(ni, IN_TILES),
        in_specs=[
            pl.BlockSpec(memory_space=pl.ANY),
            pl.BlockSpec(xs.shape, lambda i, j: (0, 0)),
            pl.BlockSpec((1, D_MODEL), lambda i, j: (0, 0)),
            pl.BlockSpec((D_MODEL, IN_TILE), lambda i, j: (0, (first(i, j) + IN_ROT) % IN_TILES)),
            pl.BlockSpec((gu_rows, FF_TILE), lambda i, j: (i, j)),
            pl.BlockSpec((gu_rows, FF_TILE), lambda i, j: (i, j)),
            pl.BlockSpec((d_rows, D_MODEL), lambda i, j: (i * IN_TILES + j, 0)),
            pl.BlockSpec((sq_rows, D_MODEL), sq_idx),
            pl.BlockSpec((sq_rows, D_MODEL), sq_idx),
            pl.BlockSpec((ple_rows, D_MODEL), lambda i, j: (i, 0)),
        ],
        out_specs=[
            pl.BlockSpec((IN_ROWS, IN_TILE), lambda i, j: (i, j)),
            pl.BlockSpec((xs.shape[0], IN_TILE), lambda i, j: (0, first(i, j))),
            pl.BlockSpec((None, gu_rows, FF_TILE), lambda i, j: (j, i, 0)),
            pl.BlockSpec((None, gu_rows, FF_TILE), lambda i, j: (j, i, 0)),
            pl.BlockSpec((d_rows, D_MODEL), lambda i, j: (i * IN_TILES + j, 0)),
            pl.BlockSpec((sq_rows, D_MODEL), sq_idx),
            pl.BlockSpec((sq_rows, D_MODEL), sq_idx),
            pl.BlockSpec((ple_rows, D_MODEL), lambda i, j: (i, 0)),
        ],
        out_shape=[
            jax.ShapeDtypeStruct((m, IN_WIDTH), BF16),
            jax.ShapeDtypeStruct((xs.shape[0], IN_WIDTH), F32),
            jax.ShapeDtypeStruct((D_FF // FF_TILE, D_MODEL, FF_TILE), BF16),
            jax.ShapeDtypeStruct((D_FF // FF_TILE, D_MODEL, FF_TILE), BF16),
            jax.ShapeDtypeStruct((D_FF, D_MODEL), BF16),
            jax.ShapeDtypeStruct((D_MODEL, D_MODEL), BF16),
            jax.ShapeDtypeStruct((D_MODEL, D_MODEL), BF16),
            jax.ShapeDtypeStruct((PLE_DIM, D_MODEL), BF16),
        ],
        scratch_shapes=[
            pltpu.VMEM((IN_ROWS, D_MODEL), F32),
            pltpu.SemaphoreType.DMA(()),
            pltpu.VMEM((IN_ROWS + xs.shape[0], D_MODEL), BF16),
            pltpu.VMEM((IN_TILES, D_MODEL, IN_TILE), BF16),
        ],
        compiler_params=_params("arbitrary", "arbitrary"),
        name="in_proj",
    )(x, xs, g, w_in, w_gate, w_up, w_down, w_out, w_pg, w_ple)


MIX_ROWS = 256
RET_CHUNK = 256
LOG2E = 1.4426950408889634


def _rotary(x, cos2, sin2):
    return x * cos2 + pltpu.roll(x, RET_DIM // 2, 1) * sin2


def _mix_tile(z_ref, kp_ref, vp_ref, qg_ref, kg_ref, sink_ref, cos_ref, sin_ref, qdec_ref, kdec_ref,
              cdec_ref, rg_ref, mix_ref, kwin_ref, vwin_ref, sout_ref, s_ref, first):
    qg = qg_ref[...] * (ATTN_SCALE * LOG2E)
    kg = kg_ref[...]
    row = lax.broadcasted_iota(jnp.int32, (BLK, 2 * BLK), 0)
    col = lax.broadcasted_iota(jnp.int32, (BLK, 2 * BLK), 1)
    cur_ok = (col >= BLK) & ((col - BLK) <= row)
    prev_ok = (col < BLK) & (col >= row)
    kp = kp_ref[...].astype(F32)
    kprev = jnp.concatenate([_rms(kp[:, h * HEAD_DIM:(h + 1) * HEAD_DIM], kg) for h in range(N_KV_HEADS)],
                            axis=1).astype(BF16)
    vprev = vp_ref[...]
    for blk in range(MIX_ROWS // BLK):
        rows = slice(blk * BLK, (blk + 1) * BLK)
        mask = cur_ok | (prev_ok & jnp.logical_not(first)) if blk == 0 else cur_ok | prev_ok
        k = z_ref[rows, COL_AK:COL_AK + KV_WIDTH].astype(F32)
        kn = [_rms(k[:, h * HEAD_DIM:(h + 1) * HEAD_DIM], kg) for h in range(N_KV_HEADS)]
        kcur = jnp.concatenate(kn, axis=1).astype(BF16)
        vcur = z_ref[rows, COL_AV:COL_AV + KV_WIDTH]
        for kh in range(N_KV_HEADS):
            sl = slice(kh * HEAD_DIM, (kh + 1) * HEAD_DIM)
            k_cat = jnp.concatenate([kprev[:, sl], kcur[:, sl]], axis=0)
            v_cat = jnp.concatenate([vprev[:, sl], vcur[:, sl]], axis=0)
            for g in range(GQA_GROUP):
                h = kh * GQA_GROUP + g
                q = z_ref[rows, COL_AQ + h * HEAD_DIM:COL_AQ + (h + 1) * HEAD_DIM].astype(F32)
                s = jnp.where(mask, _dot_nt(_rms(q, qg).astype(BF16), k_cat), -jnp.inf)
                sink = sink_ref[h] * LOG2E
                m = jnp.maximum(jnp.max(s, axis=-1, keepdims=True), sink)
                e = jnp.exp2(s - m)
                denom = jnp.sum(e, axis=-1, keepdims=True) + jnp.exp2(sink - m)
                mix_ref[rows, h * HEAD_DIM:(h + 1) * HEAD_DIM] = (
                    _dot(e.astype(BF16), v_cat) / denom).astype(mix_ref.dtype)
        kprev, vprev = kcur, vcur
    kwin_ref[0] = jnp.concatenate(kn, axis=1)
    vwin_ref[0] = vcur.astype(F32)

    tril = (lax.broadcasted_iota(jnp.int32, (RET_CHUNK, RET_CHUNK), 0)
            >= lax.broadcasted_iota(jnp.int32, (RET_CHUNK, RET_CHUNK), 1))
    for h in range(N_RET_HEADS):
        hs = slice(h * RET_DIM, (h + 1) * RET_DIM)
        col_of = lambda c: slice(c + h * RET_DIM, c + (h + 1) * RET_DIM)
        state = s_ref[h]
        for c in range(MIX_ROWS // RET_CHUNK):
            rows = slice(c * RET_CHUNK, (c + 1) * RET_CHUNK)
            cos2, sin2 = cos_ref[rows, :], sin_ref[rows, :]
            q = (_rotary(z_ref[rows, col_of(COL_RQ)].astype(F32), cos2, sin2) * qdec_ref[:, hs]).astype(BF16)
            k = (_rotary(z_ref[rows, col_of(COL_RK)].astype(F32), cos2, sin2) * kdec_ref[:, hs]).astype(BF16)
            v = z_ref[rows, col_of(COL_RV)]
            sc = jnp.where(tril, _dot_nt(q, k), 0.0)
            o = _dot(sc.astype(BF16), v) + _dot(q, state.astype(BF16))
            state = (state + _dot_tn(k, v)) * cdec_ref[h]
            o = o * lax.rsqrt(jnp.mean(o * o, axis=-1, keepdims=True) + NORM_EPS)
            gate = z_ref[rows, col_of(COL_RG)].astype(F32)
            mix_ref[rows, ATTN_WIDTH + h * RET_DIM:ATTN_WIDTH + (h + 1) * RET_DIM] = (
                o * rg_ref[:, hs] * (gate * jax.nn.sigmoid(gate))).astype(mix_ref.dtype)
        s_ref[h] = state
        sout_ref[0, h] = state


def _mixer_out_kernel(nsteps, tiles, z_ref, kp_ref, vp_ref, qg_ref, kg_ref, sink_ref, cos_ref, sin_ref,
                      qdec_ref, kdec_ref, cdec_ref, rg_ref, h_ref, hs_ref, mixs_ref, w_ref,
                      o_ref, os_ref, kwin_ref, vwin_ref, sout_ref, mix_a, mix_b, s_ref):
    t = pl.program_id(0)

    @pl.when(t % nsteps == 0)
    def _():
        s_ref[...] = jnp.zeros_like(s_ref)

    def mix(dst):
        _mix_tile(z_ref, kp_ref, vp_ref, qg_ref, kg_ref, sink_ref, cos_ref, sin_ref, qdec_ref, kdec_ref,
                  cdec_ref, rg_ref, dst, kwin_ref, vwin_ref, sout_ref, s_ref, t % nsteps == 0)

    def project(src):
        o_ref[...] = h_ref[...] + _dot(src[...], w_ref[...])

    @pl.when(t == 0)
    def _():
        mix(mix_a)

    @pl.when((t > 0) & (t < tiles) & (t % 2 == 0))
    def _():
        project(mix_b)
        mix(mix_a)

    @pl.when((t < tiles) & (t % 2 == 1))
    def _():
        project(mix_a)
        mix(mix_b)

    @pl.when(t == tiles)
    def _():
        project(mix_b if tiles % 2 == 0 else mix_a)

    @pl.when(t == 1)
    def _():
        os_ref[...] = hs_ref[...] + _dot(mixs_ref[...], w_ref[...])


def _rope_tables(pos):
    half = RET_DIM // 2
    inv = ROPE_BASE ** (-np.arange(half, dtype=np.float64) / half)
    ang = np.asarray(pos, np.float64)[:, None] * inv[None, :]
    cos, sin = np.cos(ang), np.sin(ang)
    return (np.concatenate([cos, cos], axis=-1).astype(np.float32),
            np.concatenate([-sin, sin], axis=-1).astype(np.float32))


def _ret_log_decay():
    return np.log1p(-np.exp2(-5.0 - np.arange(N_RET_HEADS, dtype=np.float64)))


def _mixer_out(z, h, hs, mixs, w_out, qg, kg, sinks, rg, batch, seq):
    nsteps = seq // MIX_ROWS
    tiles = batch * nsteps
    cos2, sin2 = _rope_tables(np.arange(seq))
    lg = _ret_log_decay()
    steps = np.arange(RET_CHUNK, dtype=np.float64)[:, None, None] + 1.0
    lanes = lambda t: np.broadcast_to(t, (RET_CHUNK, N_RET_HEADS, RET_DIM)).reshape(RET_CHUNK, RET_WIDTH).astype(np.float32)
    q_dec = lanes(np.exp(lg[None, :, None] * steps))
    k_dec = lanes(np.exp(-lg[None, :, None] * steps) * RET_K_SCALE)
    chunk_dec = np.exp(lg * RET_CHUNK).astype(np.float32)
    smem = pl.BlockSpec(memory_space=pltpu.SMEM)
    win = jax.ShapeDtypeStruct((batch, BLK, KV_WIDTH), F32)
    blks = MIX_ROWS // BLK
    tile = lambda t: jnp.minimum(t, tiles - 1)
    prev_blk = lambda t: tile(t) * blks - jnp.minimum(tile(t) % nsteps, 1)
    behind = lambda t: (jnp.maximum(t - 1, 0), 0)
    seq_of = lambda t: tile(t) // nsteps
    return pl.pallas_call(
        functools.partial(_mixer_out_kernel, nsteps, tiles),
        grid=(tiles + 1,),
        in_specs=[
            pl.BlockSpec((MIX_ROWS, IN_WIDTH), lambda t: (tile(t), 0)),
            pl.BlockSpec((BLK, KV_WIDTH), lambda t: (prev_blk(t), COL_AK // KV_WIDTH)),
            pl.BlockSpec((BLK, KV_WIDTH), lambda t: (prev_blk(t), COL_AV // KV_WIDTH)),
            _whole(qg), _whole(kg), smem,
            pl.BlockSpec((MIX_ROWS, RET_DIM), lambda t: (tile(t) % nsteps, 0)),
            pl.BlockSpec((MIX_ROWS, RET_DIM), lambda t: (tile(t) % nsteps, 0)),
            _whole(q_dec), _whole(k_dec), smem, _whole(rg),
            pl.BlockSpec((MIX_ROWS, D_MODEL), behind), _whole(hs), _whole(mixs), _whole(w_out),
        ],
        out_specs=[
            pl.BlockSpec((MIX_ROWS, D_MODEL), behind), _whole(hs),
            pl.BlockSpec((1, BLK, KV_WIDTH), lambda t: (seq_of(t), 0, 0)),
            pl.BlockSpec((1, BLK, KV_WIDTH), lambda t: (seq_of(t), 0, 0)),
            pl.BlockSpec((1, N_RET_HEADS, RET_DIM, RET_DIM), lambda t: (seq_of(t), 0, 0, 0)),
        ],
        out_shape=[jax.ShapeDtypeStruct(h.shape, F32), jax.ShapeDtypeStruct(hs.shape, F32), win, win,
                   jax.ShapeDtypeStruct((batch, N_RET_HEADS, RET_DIM, RET_DIM), F32)],
        scratch_shapes=[pltpu.VMEM((MIX_ROWS, D_MODEL), BF16), pltpu.VMEM((MIX_ROWS, D_MODEL), BF16),
                        pltpu.VMEM((N_RET_HEADS, RET_DIM, RET_DIM), F32)],
        compiler_params=_params("arbitrary"),
        name="mixer_out",
    )(z, z, z, qg, kg, sinks, cos2, sin2, q_dec, k_dec, chunk_dec, rg, h, hs, mixs, w_out)


SAMPLE_GROUP = 8
Z_HEADS = IN_WIDTH // HEAD_DIM
ROW_RQ, ROW_RK, ROW_RV, ROW_RG, ROW_AQ, ROW_AK, ROW_AV = (
    c // HEAD_DIM for c in (COL_RQ, COL_RK, COL_RV, COL_RG, COL_AQ, COL_AK, COL_AV))


def _mix_sample_kernel(z_ref, kbuf_ref, vbuf_ref, s0_ref, qg_ref, kg_ref, sink_ref, cos_ref, sin_ref,
                       spread_ref, gamma_ref, rg_ref, mix_ref, kout_ref, vout_ref, s_out_ref):
    rows = WINDOW * N_KV_HEADS
    qg = qg_ref[...]
    kg = kg_ref[...]
    sink = sink_ref[...]
    head = lax.broadcasted_iota(jnp.int32, (N_ATTN_HEADS, HEAD_DIM), 0)
    first = head < GQA_GROUP
    own = ((lax.broadcasted_iota(jnp.int32, (N_ATTN_HEADS, rows), 1) % N_KV_HEADS)
           == (lax.broadcasted_iota(jnp.int32, (N_ATTN_HEADS, rows), 0) // GQA_GROUP))
    row = lax.broadcasted_iota(jnp.int32, (rows, HEAD_DIM), 0)
    cos2 = cos_ref[...]
    sin2 = sin_ref[...]
    spread = spread_ref[...]
    for b in range(SAMPLE_GROUP):
        part = lambda r, n: z_ref[b, r:r + n, :]
        qn = _rms(part(ROW_AQ, N_ATTN_HEADS), qg) * ATTN_SCALE
        kn = _rms(part(ROW_AK, N_KV_HEADS), kg)
        vn = part(ROW_AV, N_KV_HEADS)
        kbuf = kbuf_ref[b]
        vbuf = vbuf_ref[b]
        s = jnp.where(own, _dot_nt(qn.astype(BF16), kbuf.astype(BF16)), -jnp.inf)
        k_sel = jnp.where(first, kn[0:1], kn[1:2])
        v_sel = jnp.where(first, vn[0:1], vn[1:2])
        s_new = jnp.sum(qn * k_sel, axis=-1, keepdims=True)
        m = jnp.maximum(jnp.maximum(jnp.max(s, axis=-1, keepdims=True), s_new), sink)
        e = jnp.exp(s - m)
        e_new = jnp.exp(s_new - m)
        denom = jnp.sum(e, axis=-1, keepdims=True) + e_new + jnp.exp(sink - m)
        o_attn = (_dot(e.astype(BF16), vbuf.astype(BF16)) + e_new * v_sel) / denom
        for buf, new, out in ((kbuf, kn, kout_ref), (vbuf, vn, vout_ref)):
            shifted = pltpu.roll(buf, rows - N_KV_HEADS, 0)
            out[b] = jnp.where(row == rows - 2, new[0:1], jnp.where(row == rows - 1, new[1:2], shifted))
        q_cols = _dot_tn(_rotary(part(ROW_RQ, N_RET_HEADS), cos2, sin2), spread)
        k_cols = _dot_tn(_rotary(part(ROW_RK, N_RET_HEADS), cos2, sin2) * RET_K_SCALE, spread)
        v = part(ROW_RV, N_RET_HEADS)
        gate = part(ROW_RG, N_RET_HEADS)
        heads = []
        for h in range(N_RET_HEADS):
            hs = slice(h * RET_DIM, (h + 1) * RET_DIM)
            s_next = s0_ref[b, h] * gamma_ref[h] + k_cols[:, hs] * v[h:h + 1, :]
            s_out_ref[b, h] = s_next
            o = jnp.sum(q_cols[:, hs] * s_next, axis=0, keepdims=True)
            heads.append(o * lax.rsqrt(jnp.mean(o * o, axis=-1, keepdims=True) + NORM_EPS) * rg_ref[:, hs])
        o_ret = jnp.concatenate(heads, axis=0) * (gate * jax.nn.sigmoid(gate))
        mix_ref[b] = jnp.concatenate([o_attn, o_ret], axis=0).astype(mix_ref.dtype)


def _mix_sample(z, k_buf, v_buf, s0, qg, kg, sinks, rg):
    nb = z.shape[0]
    rows = WINDOW * N_KV_HEADS
    cos2, sin2 = _rope_tables(PAST_LEN + np.arange(1))
    gamma = np.exp(_ret_log_decay()).astype(np.float32)
    spread = np.kron(np.eye(N_RET_HEADS), np.ones((1, RET_DIM))).astype(np.float32)
    sinks = sinks.reshape(N_ATTN_HEADS, 1)
    blk = lambda *shape: pl.BlockSpec((SAMPLE_GROUP,) + shape, lambda i: (i,) + (0,) * len(shape))
    cache = blk(rows, HEAD_DIM)
    state = blk(N_RET_HEADS, RET_DIM, RET_DIM)
    mix, k_win, v_win, s_new = pl.pallas_call(
        _mix_sample_kernel,
        grid=(nb // SAMPLE_GROUP,),
        in_specs=[blk(Z_HEADS, HEAD_DIM), cache, cache, state, _whole(qg), _whole(kg), _whole(sinks),
                  _whole(cos2), _whole(sin2), _whole(spread), pl.BlockSpec(memory_space=pltpu.SMEM), _whole(rg)],
        out_specs=[blk(2 * N_RET_HEADS, HEAD_DIM), cache, cache, state],
        out_shape=[jax.ShapeDtypeStruct((nb, 2 * N_RET_HEADS, HEAD_DIM), BF16),
                   jax.ShapeDtypeStruct((nb, rows, HEAD_DIM), F32),
                   jax.ShapeDtypeStruct((nb, rows, HEAD_DIM), F32),
                   jax.ShapeDtypeStruct(s0.shape, F32)],
        compiler_params=_params("parallel"),
        name="mix_sample",
    )(z.reshape(nb, Z_HEADS, HEAD_DIM), k_buf.reshape(nb, rows, HEAD_DIM), v_buf.reshape(nb, rows, HEAD_DIM),
      s0, qg, kg, sinks, cos2, sin2, spread, gamma, rg)
    window = (nb, WINDOW, N_KV_HEADS, HEAD_DIM)
    return mix.reshape(nb, D_MODEL), k_win.reshape(window), v_win.reshape(window), s_new


def _ffn_kernel(h_ref, hs_ref, g_ref, wg_ref, wu_ref, wd_ref, o_ref, os_ref, f_ref):
    i, j = pl.program_id(0), pl.program_id(1)
    tm = h_ref.shape[0]

    @pl.when(j == 0)
    def _():
        h = h_ref[...]
        f_ref[:tm] = _rms(h, g_ref[...]).astype(BF16)
        o_ref[...] = h

    @pl.when((i == 0) & (j == 0))
    def _():
        hs = hs_ref[...]
        f_ref[tm:] = _rms(hs, g_ref[...]).astype(BF16)
        os_ref[...] = hs

    def swiglu(f):
        gate = _dot(f, wg_ref[...])
        act = (gate * jax.nn.sigmoid(gate) * _dot(f, wu_ref[...])).astype(BF16)
        return _dot(act, wd_ref[...])

    @pl.when(i == 0)
    def _():
        y = swiglu(f_ref[...])
        o_ref[...] += y[:tm]
        os_ref[...] += y[tm:]

    @pl.when(i > 0)
    def _():
        o_ref[...] += swiglu(f_ref[:tm])


def _ffn(h, hs, g, wg, wu, wd, tm):
    m = h.shape[0]
    row = pl.BlockSpec((tm, D_MODEL), lambda i, j: (i, 0))
    return pl.pallas_call(
        _ffn_kernel,
        grid=(m // tm, D_FF // FF_TILE),
        in_specs=[
            row, _whole(hs), _whole(g),
            pl.BlockSpec((None, D_MODEL, FF_TILE), lambda i, j: (j, 0, 0)),
            pl.BlockSpec((None, D_MODEL, FF_TILE), lambda i, j: (j, 0, 0)),
            pl.BlockSpec((FF_TILE, D_MODEL), lambda i, j: (j, 0)),
        ],
        out_specs=[row, _whole(hs)],
        out_shape=[jax.ShapeDtypeStruct((m, D_MODEL), F32), jax.ShapeDtypeStruct(hs.shape, F32)],
        scratch_shapes=[pltpu.VMEM((tm + hs.shape[0], D_MODEL), BF16)],
        compiler_params=_params("arbitrary", "arbitrary"),
        name="ffn",
    )(h, hs, g, wg, wu, wd)


def _ple_kernel(h_ref, p_ref, hs_ref, ps_ref, g_ref, wp_ref, wg_ref, o_ref, os_ref):
    tm = h_ref.shape[0]

    def embed(h, p):
        gate = jax.nn.sigmoid(_dot(_rms(h, g_ref[...]).astype(BF16), wg_ref[...]))
        return h + _dot(p.astype(BF16), wp_ref[...]) * gate

    @pl.when(pl.program_id(0) == 0)
    def _():
        y = embed(jnp.concatenate([h_ref[...], hs_ref[...]], axis=0),
                  jnp.concatenate([p_ref[...], ps_ref[...]], axis=0))
        o_ref[...] = y[:tm]
        os_ref[...] = y[tm:]

    @pl.when(pl.program_id(0) > 0)
    def _():
        o_ref[...] = embed(h_ref[...], p_ref[...])


def _ple(h, p, hs, ps, g, w_ple, w_pg, tm):
    m = h.shape[0]
    row = pl.BlockSpec((tm, D_MODEL), lambda i: (i, 0))
    return pl.pallas_call(
        _ple_kernel,
        grid=(m // tm,),
        in_specs=[row, pl.BlockSpec((tm, PLE_DIM), lambda i: (i, 0)), _whole(hs), _whole(ps), _whole(g),
                  _whole(w_ple), _whole(w_pg)],
        out_specs=[row, _whole(hs)],
        out_shape=[jax.ShapeDtypeStruct((m, D_MODEL), F32), jax.ShapeDtypeStruct(hs.shape, F32)],
        compiler_params=_params("arbitrary"),
        name="ple",
    )(h, p, hs, ps, g, w_ple, w_pg)


PLE_ROWS = 512
FFN_ROWS = 1024


def _dense_tail(h, p, hs, ps, w):
    h, hs = _ffn(h, hs, w["ffn_norm_g"], w["w_gate"], w["w_up"], w["w_down"], FFN_ROWS)
    return _ple(h, p, hs, ps, w["ple_norm_g"], w["w_ple"], w["w_ple_gate"], PLE_ROWS)


def kernel(x_prompt, x_sample, cache_k_win, cache_v_win, state_ret, p_prompt, p_sample,
           attn_norm_g, w_in, q_norm_g, k_norm_g, attn_sinks, ret_out_g, w_out,
           ffn_norm_g, w_gate, w_up, w_down, ple_norm_g, w_ple, w_ple_gate):
    assert w_in.shape[0] == 1 and x_sample.shape[1] == 1 and cache_k_win.shape[2] == WINDOW
    y_p, y_s = x_prompt, x_sample
    outs = [[] for _ in range(6)]
    for l in range(w_in.shape[0]):
        w = {
            "attn_norm_g": attn_norm_g[l].reshape(1, D_MODEL),
            "q_norm_g": q_norm_g[l].reshape(1, HEAD_DIM),
            "k_norm_g": k_norm_g[l].reshape(1, HEAD_DIM),
            "attn_sinks": attn_sinks[l],
            "ret_out_g": ret_out_g[l].reshape(1, RET_WIDTH),
            "ffn_norm_g": ffn_norm_g[l].reshape(1, D_MODEL),
            "ple_norm_g": ple_norm_g[l].reshape(1, D_MODEL),
        }
        batch, seq, _ = y_p.shape
        nb = y_s.shape[0]
        h_p = y_p.reshape(batch * seq, D_MODEL)
        h_s = y_s.reshape(nb, D_MODEL)
        z_p, z_s, *cast = _in_proj(h_p, h_s, w["attn_norm_g"], w_in[l], w_gate[l], w_up[l], w_down[l],
                                   w_out[l], w_ple_gate[l], w_ple[l])
        w.update(zip(("w_gate", "w_up", "w_down", "w_out", "w_ple_gate", "w_ple"), cast))
        mix_s, ks, vs, ss = _mix_sample(z_s, cache_k_win[l], cache_v_win[l], state_ret[l], w["q_norm_g"],
                                        w["k_norm_g"], w["attn_sinks"], w["ret_out_g"])
        h_p, h_s, kp, vp, sp = _mixer_out(z_p, h_p, h_s, mix_s, w["w_out"], w["q_norm_g"], w["k_norm_g"],
                                          w["attn_sinks"], w["ret_out_g"], batch, seq)
        h_p, h_s = _dense_tail(h_p, p_prompt[l].reshape(batch * seq, PLE_DIM),
                               h_s, p_sample[l].reshape(nb, PLE_DIM), w)
        y_p, y_s = h_p.reshape(batch, seq, D_MODEL), h_s.reshape(nb, 1, D_MODEL)
        kp = kp.reshape(batch, BLK, N_KV_HEADS, HEAD_DIM)
        vp = vp.reshape(batch, BLK, N_KV_HEADS, HEAD_DIM)
        for acc, val in zip(outs, (kp, vp, sp, ks, vs, ss)):
            acc.append(val)
    kp, vp, sp, ks, vs, ss = (jnp.stack(o) for o in outs)
    return (y_p, y_s, kp, vp, sp, ks, vs, ss)
```

```python
import functools

import numpy as np
import jax
import jax.numpy as jnp
from jax import lax
from jax.experimental import pallas as pl
from jax.experimental.pallas import tpu as pltpu

D_MODEL = 2048
HEAD_DIM = 128
N_ATTN_HEADS = 8
N_KV_HEADS = 2
GQA_GROUP = N_ATTN_HEADS // N_KV_HEADS
ATTN_WIDTH = N_ATTN_HEADS * HEAD_DIM
KV_WIDTH = N_KV_HEADS * HEAD_DIM
WINDOW = 128
BLK = 128
N_RET_HEADS = 8
RET_DIM = 128
RET_WIDTH = N_RET_HEADS * RET_DIM
D_FF = 5632
IN_WIDTH = 5632
PLE_DIM = 256
PAST_LEN = 16384
ROPE_BASE = 10000.0
NORM_EPS = 1e-6
ATTN_SCALE = HEAD_DIM ** -0.5
RET_K_SCALE = RET_DIM ** -0.5

COL_RQ, COL_RK, COL_RV, COL_RG = 0, RET_WIDTH, 2 * RET_WIDTH, 3 * RET_WIDTH
COL_AQ = 4 * RET_WIDTH
COL_AK = COL_AQ + ATTN_WIDTH
COL_AV = COL_AK + KV_WIDTH
IN_TILE = 512
IN_TILES = IN_WIDTH // IN_TILE
IN_ROT = (ATTN_WIDTH + 2 * KV_WIDTH) // IN_TILE

VMEM_LIMIT = 56 * 1024 * 1024

F32 = jnp.float32
BF16 = jnp.bfloat16


def _params(*sem):
    return pltpu.CompilerParams(dimension_semantics=sem, vmem_limit_bytes=VMEM_LIMIT)


def _rms(x, g):
    return x * lax.rsqrt(jnp.mean(x * x, axis=-1, keepdims=True) + NORM_EPS) * g


def _dot(a, b):
    return jnp.dot(a, b, preferred_element_type=F32)


def _dot_nt(a, b):
    return lax.dot_general(a, b, (((1,), (1,)), ((), ())), preferred_element_type=F32)


def _dot_tn(a, b):
    return lax.dot_general(a, b, (((0,), (0,)), ((), ())), preferred_element_type=F32)


def _whole(a):
    return pl.BlockSpec(a.shape, lambda *_: (0,) * a.ndim)


def _in_proj_kernel(x_hbm, xs_ref, g_ref, w_ref, wg_ref, wu_ref, wd_ref, wo_ref, wpg_ref, wple_ref,
                    z_ref, zs_ref, wg_o, wu_o, wd_o, wo_o, wpg_o, wple_o,
                    xbuf, xsem, a_ref, wres):
    i, j = pl.program_id(0), pl.program_id(1)

    def x_copy(tile):
        return pltpu.make_async_copy(x_hbm.at[pl.ds(tile * IN_ROWS, IN_ROWS)], xbuf, xsem)

    @pl.when((i == 0) & (j == 0))
    def _():
        x_copy(0).start()
        a_ref[IN_ROWS:] = _rms(xs_ref[...], g_ref[...]).astype(BF16)

    @pl.when(j == 0)
    def _():
        x_copy(i).wait()
        a_ref[:IN_ROWS] = _rms(xbuf[...], g_ref[...]).astype(BF16)

        @pl.when(i + 1 < pl.num_programs(0))
        def _():
            x_copy(i + 1).start()

    def cast_weights():
        for src, dst in ((wg_ref, wg_o), (wu_ref, wu_o), (wd_ref, wd_o), (wo_ref, wo_o), (wpg_ref, wpg_o),
                         (wple_ref, wple_o)):
            dst[...] = src[...].astype(BF16)

    @pl.when(i == 0)
    def _():
        wres[j] = w_ref[...].astype(BF16)
        y = _dot(a_ref[...], wres[j])
        z_ref[...] = y[:IN_ROWS].astype(z_ref.dtype)
        zs_ref[...] = y[IN_ROWS:]
        cast_weights()

    @pl.when(i > 0)
    def _():
        z_ref[...] = _dot(a_ref[:IN_ROWS], wres[j]).astype(z_ref.dtype)
        cast_weights()


IN_ROWS = 1024
CAST_SPLIT = 8
FF_TILE = 512


def _in_proj(x, xs, g, w_in, w_gate, w_up, w_down, w_out, w_pg, w_ple):
    m = x.shape[0]
    ni = m // IN_ROWS
    steps = ni * IN_TILES
    assert m % IN_ROWS == 0 and FF_TILE == IN_TILE and D_FF == IN_WIDTH
    gu_rows, d_rows = D_MODEL // ni, D_FF // steps
    sq_rows, ple_rows = D_MODEL // (ni * CAST_SPLIT), PLE_DIM // ni
    assert gu_rows * ni == D_MODEL and d_rows * steps == D_FF and sq_rows % 16 == 0 and ple_rows % 16 == 0
    sq_idx = lambda i, j: (i * CAST_SPLIT + jnp.minimum(j, CAST_SPLIT - 1), 0)
    first = lambda i, j: jnp.where(i == 0, j, IN_TILES - 1)
    return pl.pallas_call(
        _in_proj_kernel,
        grid=(ni, IN_TILES),
        in_specs=[
            pl.BlockSpec(memory_space=pl.ANY),
            pl.BlockSpec(xs.shape, lambda i, j: (0, 0)),
            pl.BlockSpec((1, D_MODEL), lambda i, j: (0, 0)),
            pl.BlockSpec((D_MODEL, IN_TILE), lambda i, j: (0, (first(i, j) + IN_ROT) % IN_TILES)),
            pl.BlockSpec((gu_rows, FF_TILE), lambda i, j: (i, j)),
            pl.BlockSpec((gu_rows, FF_TILE), lambda i, j: (i, j)),
            pl.BlockSpec((d_rows, D_MODEL), lambda i, j: (i * IN_TILES + j, 0)),
            pl.BlockSpec((sq_rows, D_MODEL), sq_idx),
            pl.BlockSpec((sq_rows, D_MODEL), sq_idx),
            pl.BlockSpec((ple_rows, D_MODEL), lambda i, j: (i, 0)),
        ],
        out_specs=[
            pl.BlockSpec((IN_ROWS, IN_TILE), lambda i, j: (i, j)),
            pl.BlockSpec((xs.shape[0], IN_TILE), lambda i, j: (0, first(i, j))),
            pl.BlockSpec((None, gu_rows, FF_TILE), lambda i, j: (j, i, 0)),
            pl.BlockSpec((None, gu_rows, FF_TILE), lambda i, j: (j, i, 0)),
            pl.BlockSpec((d_rows, D_MODEL), lambda i, j: (i * IN_TILES + j, 0)),
            pl.BlockSpec((sq_rows, D_MODEL), sq_idx),
            pl.BlockSpec((sq_rows, D_MODEL), sq_idx),
            pl.BlockSpec((ple_rows, D_MODEL), lambda i, j: (i, 0)),
        ],
        out_shape=[
            jax.ShapeDtypeStruct((m, IN_WIDTH), BF16),
            jax.ShapeDtypeStruct((xs.shape[0], IN_WIDTH), F32),
            jax.ShapeDtypeStruct((D_FF // FF_TILE, D_MODEL, FF_TILE), BF16),
            jax.ShapeDtypeStruct((D_FF // FF_TILE, D_MODEL, FF_TILE), BF16),
            jax.ShapeDtypeStruct((D_FF, D_MODEL), BF16),
            jax.ShapeDtypeStruct((D_MODEL, D_MODEL), BF16),
            jax.ShapeDtypeStruct((D_MODEL, D_MODEL), BF16),
            jax.ShapeDtypeStruct((PLE_DIM, D_MODEL), BF16),
        ],
        scratch_shapes=[
            pltpu.VMEM((IN_ROWS, D_MODEL), F32),
            pltpu.SemaphoreType.DMA(()),
            pltpu.VMEM((IN_ROWS + xs.shape[0], D_MODEL), BF16),
            pltpu.VMEM((IN_TILES, D_MODEL, IN_TILE), BF16),
        ],
        compiler_params=_params("arbitrary", "arbitrary"),
        name="in_proj",
    )(x, xs, g, w_in, w_gate, w_up, w_down, w_out, w_pg, w_ple)


MIX_ROWS = 256
RET_CHUNK = 256
LOG2E = 1.4426950408889634


def _rotary(x, cos2, sin2):
    return x * cos2 + pltpu.roll(x, RET_DIM // 2, 1) * sin2


def _mix_tile(z_ref, kp_ref, vp_ref, qg_ref, kg_ref, sink_ref, cos_ref, sin_ref, qdec_ref, kdec_ref,
              cdec_ref, rg_ref, mix_ref, kwin_ref, vwin_ref, sout_ref, s_ref, first):
    qg = qg_ref[...] * (ATTN_SCALE * LOG2E)
    kg = kg_ref[...]
    row = lax.broadcasted_iota(jnp.int32, (BLK, 2 * BLK), 0)
    col = lax.broadcasted_iota(jnp.int32, (BLK, 2 * BLK), 1)
    cur_ok = (col >= BLK) & ((col - BLK) <= row)
    prev_ok = (col < BLK) & (col >= row)
    kp = kp_ref[...].astype(F32)
    kprev = jnp.concatenate([_rms(kp[:, h * HEAD_DIM:(h + 1) * HEAD_DIM], kg) for h in range(N_KV_HEADS)],
                            axis=1).astype(BF16)
    vprev = vp_ref[...]
    for blk in range(MIX_ROWS // BLK):
        rows = slice(blk * BLK, (blk + 1) * BLK)
        mask = cur_ok | (prev_ok & jnp.logical_not(first)) if blk == 0 else cur_ok | prev_ok
        k = z_ref[rows, COL_AK:COL_AK + KV_WIDTH].astype(F32)
        kn = [_rms(k[:, h * HEAD_DIM:(h + 1) * HEAD_DIM], kg) for h in range(N_KV_HEADS)]
        kcur = jnp.concatenate(kn, axis=1).astype(BF16)
        vcur = z_ref[rows, COL_AV:COL_AV + KV_WIDTH]
        for kh in range(N_KV_HEADS):
            sl = slice(kh * HEAD_DIM, (kh + 1) * HEAD_DIM)
            k_cat = jnp.concatenate([kprev[:, sl], kcur[:, sl]], axis=0)
            v_cat = jnp.concatenate([vprev[:, sl], vcur[:, sl]], axis=0)
            for g in range(GQA_GROUP):
                h = kh * GQA_GROUP + g
                q = z_ref[rows, COL_AQ + h * HEAD_DIM:COL_AQ + (h + 1) * HEAD_DIM].astype(F32)
                s = jnp.where(mask, _dot_nt(_rms(q, qg).astype(BF16), k_cat), -jnp.inf)
                sink = sink_ref[h] * LOG2E
                m = jnp.maximum(jnp.max(s, axis=-1, keepdims=True), sink)
                e = jnp.exp2(s - m)
                denom = jnp.sum(e, axis=-1, keepdims=True) + jnp.exp2(sink - m)
                mix_ref[rows, h * HEAD_DIM:(h + 1) * HEAD_DIM] = (
                    _dot(e.astype(BF16), v_cat) / denom).astype(mix_ref.dtype)
        kprev, vprev = kcur, vcur
    kwin_ref[0] = jnp.concatenate(kn, axis=1)
    vwin_ref[0] = vcur.astype(F32)

    tril = (lax.broadcasted_iota(jnp.int32, (RET_CHUNK, RET_CHUNK), 0)
            >= lax.broadcasted_iota(jnp.int32, (RET_CHUNK, RET_CHUNK), 1))
    for h in range(N_RET_HEADS):
        hs = slice(h * RET_DIM, (h + 1) * RET_DIM)
        col_of = lambda c: slice(c + h * RET_DIM, c + (h + 1) * RET_DIM)
        state = s_ref[h]
        for c in range(MIX_ROWS // RET_CHUNK):
            rows = slice(c * RET_CHUNK, (c + 1) * RET_CHUNK)
            cos2, sin2 = cos_ref[rows, :], sin_ref[rows, :]
            q = (_rotary(z_ref[rows, col_of(COL_RQ)].astype(F32), cos2, sin2) * qdec_ref[:, hs]).astype(BF16)
            k = (_rotary(z_ref[rows, col_of(COL_RK)].astype(F32), cos2, sin2) * kdec_ref[:, hs]).astype(BF16)
            v = z_ref[rows, col_of(COL_RV)]
            sc = jnp.where(tril, _dot_nt(q, k), 0.0)
            o = _dot(sc.astype(BF16), v) + _dot(q, state.astype(BF16))
            state = (state + _dot_tn(k, v)) * cdec_ref[h]
            o = o * lax.rsqrt(jnp.mean(o * o, axis=-1, keepdims=True) + NORM_EPS)
            gate = z_ref[rows, col_of(COL_RG)].astype(F32)
            mix_ref[rows, ATTN_WIDTH + h * RET_DIM:ATTN_WIDTH + (h + 1) * RET_DIM] = (
                o * rg_ref[:, hs] * (gate * jax.nn.sigmoid(gate))).astype(mix_ref.dtype)
        s_ref[h] = state
        sout_ref[0, h] = state


def _mixer_out_kernel(nsteps, tiles, groups, z_ref, kp_ref, vp_ref, qg_ref, kg_ref, sink_ref, cos_ref, sin_ref,
                      qdec_ref, kdec_ref, cdec_ref, rg_ref, h_ref, hs_ref, w_ref,
                      zs_ref, kbuf_ref, vbuf_ref, s0_ref, sinkc_ref, coss_ref, sins_ref, spread_ref, gamma_ref,
                      o_ref, os_ref, kwin_ref, vwin_ref, sout_ref, kout_ref, vout_ref, ssout_ref,
                      mix_a, mix_b, s_ref, mixs_scr):
    t = pl.program_id(0)
    per_group = tiles // groups

    @pl.when(t % nsteps == 0)
    def _():
        s_ref[...] = jnp.zeros_like(s_ref)

    def mix(dst):
        _mix_tile(z_ref, kp_ref, vp_ref, qg_ref, kg_ref, sink_ref, cos_ref, sin_ref, qdec_ref, kdec_ref,
                  cdec_ref, rg_ref, dst, kwin_ref, vwin_ref, sout_ref, s_ref, t % nsteps == 0)

    def project(src):
        o_ref[...] = h_ref[...] + _dot(src[...], w_ref[...])

    @pl.when(t == 0)
    def _():
        mix(mix_a)

    @pl.when((t > 0) & (t < tiles) & (t % 2 == 0))
    def _():
        project(mix_b)
        mix(mix_a)

    @pl.when((t < tiles) & (t % 2 == 1))
    def _():
        project(mix_a)
        mix(mix_b)

    @pl.when((t % per_group == per_group - 1) & (t < tiles))
    def _():
        g = t // per_group

        def put_mix(b, attn, ret):
            heads = [x[r:r + 1] for x in (attn, ret) for r in range(x.shape[0])]
            mixs_scr[g, b:b + 1, :] = jnp.concatenate(heads, axis=1)

        _sample_group(zs_ref, kbuf_ref, vbuf_ref, s0_ref, qg_ref, kg_ref, sinkc_ref, coss_ref, sins_ref,
                      spread_ref, gamma_ref, rg_ref, put_mix, kout_ref, vout_ref, ssout_ref)

    @pl.when(t == tiles)
    def _():
        project(mix_b if tiles % 2 == 0 else mix_a)
        mixs = jnp.concatenate([mixs_scr[g, :SAMPLE_GROUP, :] for g in range(groups)], axis=0)
        os_ref[...] = hs_ref[...] + _dot(mixs.astype(BF16), w_ref[...])


def _rope_tables(pos):
    half = RET_DIM // 2
    inv = ROPE_BASE ** (-np.arange(half, dtype=np.float64) / half)
    ang = np.asarray(pos, np.float64)[:, None] * inv[None, :]
    cos, sin = np.cos(ang), np.sin(ang)
    return (np.concatenate([cos, cos], axis=-1).astype(np.float32),
            np.concatenate([-sin, sin], axis=-1).astype(np.float32))


def _ret_log_decay():
    return np.log1p(-np.exp2(-5.0 - np.arange(N_RET_HEADS, dtype=np.float64)))


def _mixer_out(z, h, hs, zs, k_buf, v_buf, s0, w_out, qg, kg, sinks, rg, batch, seq):
    nsteps = seq // MIX_ROWS
    tiles = batch * nsteps
    nb = hs.shape[0]
    groups = nb // SAMPLE_GROUP
    assert nb % SAMPLE_GROUP == 0 and tiles % groups == 0 and SAMPLE_GROUP <= SUBLANES
    cache_rows = WINDOW * N_KV_HEADS
    cos2, sin2 = _rope_tables(np.arange(seq))
    cos_s, sin_s = _rope_tables(PAST_LEN + np.arange(1))
    lg = _ret_log_decay()
    steps = np.arange(RET_CHUNK, dtype=np.float64)[:, None, None] + 1.0
    lanes = lambda t: np.broadcast_to(t, (RET_CHUNK, N_RET_HEADS, RET_DIM)).reshape(RET_CHUNK, RET_WIDTH).astype(np.float32)
    q_dec = lanes(np.exp(lg[None, :, None] * steps))
    k_dec = lanes(np.exp(-lg[None, :, None] * steps) * RET_K_SCALE)
    chunk_dec = np.exp(lg * RET_CHUNK).astype(np.float32)
    gamma = np.exp(lg).astype(np.float32)
    spread = np.kron(np.eye(N_RET_HEADS), np.ones((1, RET_DIM))).astype(np.float32)
    sink_col = sinks.reshape(N_ATTN_HEADS, 1)
    smem = pl.BlockSpec(memory_space=pltpu.SMEM)
    win = jax.ShapeDtypeStruct((batch, BLK, KV_WIDTH), F32)
    blks = MIX_ROWS // BLK
    tile = lambda t: jnp.minimum(t, tiles - 1)
    prev_blk = lambda t: tile(t) * blks - jnp.minimum(tile(t) % nsteps, 1)
    behind = lambda t: (jnp.maximum(t - 1, 0), 0)
    seq_of = lambda t: tile(t) // nsteps
    group_of = lambda t: tile(t) // (tiles // groups)
    sample = lambda *shape: pl.BlockSpec((SAMPLE_GROUP,) + shape, lambda t: (group_of(t),) + (0,) * len(shape))
    cache, state = sample(cache_rows, HEAD_DIM), sample(N_RET_HEADS, RET_DIM, RET_DIM)
    cache_shape = jax.ShapeDtypeStruct((nb, cache_rows, HEAD_DIM), F32)
    y, ys, k_win, v_win, s_new, ks, vs, ss = pl.pallas_call(
        functools.partial(_mixer_out_kernel, nsteps, tiles, groups),
        grid=(tiles + 1,),
        in_specs=[
            pl.BlockSpec((MIX_ROWS, IN_WIDTH), lambda t: (tile(t), 0)),
            pl.BlockSpec((BLK, KV_WIDTH), lambda t: (prev_blk(t), COL_AK // KV_WIDTH)),
            pl.BlockSpec((BLK, KV_WIDTH), lambda t: (prev_blk(t), COL_AV // KV_WIDTH)),
            _whole(qg), _whole(kg), smem,
            pl.BlockSpec((MIX_ROWS, RET_DIM), lambda t: (tile(t) % nsteps, 0)),
            pl.BlockSpec((MIX_ROWS, RET_DIM), lambda t: (tile(t) % nsteps, 0)),
            _whole(q_dec), _whole(k_dec), smem, _whole(rg),
            pl.BlockSpec((MIX_ROWS, D_MODEL), behind), _whole(hs), _whole(w_out),
            sample(Z_HEADS, HEAD_DIM), cache, cache, state,
            _whole(sink_col), _whole(cos_s), _whole(sin_s), _whole(spread), smem,
        ],
        out_specs=[
            pl.BlockSpec((MIX_ROWS, D_MODEL), behind), _whole(hs),
            pl.BlockSpec((1, BLK, KV_WIDTH), lambda t: (seq_of(t), 0, 0)),
            pl.BlockSpec((1, BLK, KV_WIDTH), lambda t: (seq_of(t), 0, 0)),
            pl.BlockSpec((1, N_RET_HEADS, RET_DIM, RET_DIM), lambda t: (seq_of(t), 0, 0, 0)),
            cache, cache, state,
        ],
        out_shape=[jax.ShapeDtypeStruct(h.shape, F32), jax.ShapeDtypeStruct(hs.shape, F32), win, win,
                   jax.ShapeDtypeStruct((batch, N_RET_HEADS, RET_DIM, RET_DIM), F32),
                   cache_shape, cache_shape, jax.ShapeDtypeStruct(s0.shape, F32)],
        scratch_shapes=[pltpu.VMEM((MIX_ROWS, D_MODEL), BF16), pltpu.VMEM((MIX_ROWS, D_MODEL), BF16),
                        pltpu.VMEM((N_RET_HEADS, RET_DIM, RET_DIM), F32),
                        pltpu.VMEM((groups, SUBLANES, D_MODEL), F32)],
        compiler_params=_params("arbitrary"),
        name="mixer_out",
    )(z, z, z, qg, kg, sinks, cos2, sin2, q_dec, k_dec, chunk_dec, rg, h, hs, w_out,
      zs.reshape(nb, Z_HEADS, HEAD_DIM), k_buf.reshape(nb, cache_rows, HEAD_DIM),
      v_buf.reshape(nb, cache_rows, HEAD_DIM), s0, sink_col, cos_s, sin_s, spread, gamma)
    window = (nb, WINDOW, N_KV_HEADS, HEAD_DIM)
    return y, ys, k_win, v_win, s_new, ks.reshape(window), vs.reshape(window), ss


SAMPLE_GROUP = 4
Z_HEADS = IN_WIDTH // HEAD_DIM
ROW_RQ, ROW_RK, ROW_RV, ROW_RG, ROW_AQ, ROW_AK, ROW_AV = (
    c // HEAD_DIM for c in (COL_RQ, COL_RK, COL_RV, COL_RG, COL_AQ, COL_AK, COL_AV))
SUBLANES = 8


def _sample_group(z_ref, kbuf_ref, vbuf_ref, s0_ref, qg_ref, kg_ref, sink_ref, cos_ref, sin_ref,
                  spread_ref, gamma_ref, rg_ref, put_mix, kout_ref, vout_ref, s_out_ref):
    rows = WINDOW * N_KV_HEADS
    qg = qg_ref[...]
    kg = kg_ref[...]
    sink = sink_ref[...]
    head = lax.broadcasted_iota(jnp.int32, (N_ATTN_HEADS, HEAD_DIM), 0)
    first = head < GQA_GROUP
    own = ((lax.broadcasted_iota(jnp.int32, (N_ATTN_HEADS, rows), 1) % N_KV_HEADS)
           == (lax.broadcasted_iota(jnp.int32, (N_ATTN_HEADS, rows), 0) // GQA_GROUP))
    row = lax.broadcasted_iota(jnp.int32, (rows, HEAD_DIM), 0)
    cos2 = cos_ref[...]
    sin2 = sin_ref[...]
    spread = spread_ref[...]
    for b in range(SAMPLE_GROUP):
        part = lambda r, n: z_ref[b, r:r + n, :]
        qn = _rms(part(ROW_AQ, N_ATTN_HEADS), qg) * ATTN_SCALE
        kn = _rms(part(ROW_AK, N_KV_HEADS), kg)
        vn = part(ROW_AV, N_KV_HEADS)
        kbuf = kbuf_ref[b]
        vbuf = vbuf_ref[b]
        s = jnp.where(own, _dot_nt(qn.astype(BF16), kbuf.astype(BF16)), -jnp.inf)
        k_sel = jnp.where(first, kn[0:1], kn[1:2])
        v_sel = jnp.where(first, vn[0:1], vn[1:2])
        s_new = jnp.sum(qn * k_sel, axis=-1, keepdims=True)
        m = jnp.maximum(jnp.maximum(jnp.max(s, axis=-1, keepdims=True), s_new), sink)
        e = jnp.exp(s - m)
        e_new = jnp.exp(s_new - m)
        denom = jnp.sum(e, axis=-1, keepdims=True) + e_new + jnp.exp(sink - m)
        o_attn = (_dot(e.astype(BF16), vbuf.astype(BF16)) + e_new * v_sel) / denom
        for buf, new, out in ((kbuf, kn, kout_ref), (vbuf, vn, vout_ref)):
            shifted = pltpu.roll(buf, rows - N_KV_HEADS, 0)
            out[b] = jnp.where(row == rows - 2, new[0:1], jnp.where(row == rows - 1, new[1:2], shifted))
        q_cols = _dot_tn(_rotary(part(ROW_RQ, N_RET_HEADS), cos2, sin2), spread)
        k_cols = _dot_tn(_rotary(part(ROW_RK, N_RET_HEADS), cos2, sin2) * RET_K_SCALE, spread)
        v = part(ROW_RV, N_RET_HEADS)
        gate = part(ROW_RG, N_RET_HEADS)
        heads = []
        for h in range(N_RET_HEADS):
            hs = slice(h * RET_DIM, (h + 1) * RET_DIM)
            s_next = s0_ref[b, h] * gamma_ref[h] + k_cols[:, hs] * v[h:h + 1, :]
            s_out_ref[b, h] = s_next
            o = jnp.sum(q_cols[:, hs] * s_next, axis=0, keepdims=True)
            heads.append(o * lax.rsqrt(jnp.mean(o * o, axis=-1, keepdims=True) + NORM_EPS) * rg_ref[:, hs])
        o_ret = jnp.concatenate(heads, axis=0) * (gate * jax.nn.sigmoid(gate))
        put_mix(b, o_attn, o_ret)


def _ffn_kernel(h_ref, hs_ref, g_ref, wg_ref, wu_ref, wd_ref, o_ref, os_ref, f_ref):
    i, j = pl.program_id(0), pl.program_id(1)
    tm = h_ref.shape[0]

    @pl.when(j == 0)
    def _():
        h = h_ref[...]
        f_ref[:tm] = _rms(h, g_ref[...]).astype(BF16)
        o_ref[...] = h

    @pl.when((i == 0) & (j == 0))
    def _():
        hs = hs_ref[...]
        f_ref[tm:] = _rms(hs, g_ref[...]).astype(BF16)
        os_ref[...] = hs

    def swiglu(f):
        gate = _dot(f, wg_ref[...])
        act = (gate * jax.nn.sigmoid(gate) * _dot(f, wu_ref[...])).astype(BF16)
        return _dot(act, wd_ref[...])

    @pl.when(i == 0)
    def _():
        y = swiglu(f_ref[...])
        o_ref[...] += y[:tm]
        os_ref[...] += y[tm:]

    @pl.when(i > 0)
    def _():
        o_ref[...] += swiglu(f_ref[:tm])


def _ffn(h, hs, g, wg, wu, wd, tm):
    m = h.shape[0]
    row = pl.BlockSpec((tm, D_MODEL), lambda i, j: (i, 0))
    return pl.pallas_call(
        _ffn_kernel,
        grid=(m // tm, D_FF // FF_TILE),
        in_specs=[
            row, _whole(hs), _whole(g),
            pl.BlockSpec((None, D_MODEL, FF_TILE), lambda i, j: (j, 0, 0)),
            pl.BlockSpec((None, D_MODEL, FF_TILE), lambda i, j: (j, 0, 0)),
            pl.BlockSpec((FF_TILE, D_MODEL), lambda i, j: (j, 0)),
        ],
        out_specs=[row, _whole(hs)],
        out_shape=[jax.ShapeDtypeStruct((m, D_MODEL), F32), jax.ShapeDtypeStruct(hs.shape, F32)],
        scratch_shapes=[pltpu.VMEM((tm + hs.shape[0], D_MODEL), BF16)],
        compiler_params=_params("arbitrary", "arbitrary"),
        name="ffn",
    )(h, hs, g, wg, wu, wd)


H_RING = 3


def _ple_kernel(h_hbm, p_ref, hs_ref, ps_ref, g_ref, wp_ref, wg_ref, o_ref, os_ref, ring, sems):
    i = pl.program_id(0)
    steps = pl.num_programs(0)
    tm = o_ref.shape[0]

    def h_copy(tile):
        slot = tile % H_RING
        return pltpu.make_async_copy(h_hbm.at[pl.ds(tile * tm, tm)], ring.at[slot], sems.at[slot])

    @pl.when(i == 0)
    def _():
        for tile in range(H_RING - 1):
            h_copy(tile).start()

    @pl.when(i + H_RING - 1 < steps)
    def _():
        h_copy(i + H_RING - 1).start()

    h_copy(i).wait()
    h_ref = ring.at[i % H_RING]

    def embed(h, p):
        gate = jax.nn.sigmoid(_dot(_rms(h, g_ref[...]).astype(BF16), wg_ref[...]))
        return h + _dot(p.astype(BF16), wp_ref[...]) * gate

    @pl.when(i == 0)
    def _():
        y = embed(jnp.concatenate([h_ref[...], hs_ref[...]], axis=0),
                  jnp.concatenate([p_ref[...], ps_ref[...]], axis=0))
        o_ref[...] = y[:tm]
        os_ref[...] = y[tm:]

    @pl.when(i > 0)
    def _():
        o_ref[...] = embed(h_ref[...], p_ref[...])


def _ple(h, p, hs, ps, g, w_ple, w_pg, tm):
    m = h.shape[0]
    assert m % tm == 0 and m // tm >= H_RING - 1
    row = pl.BlockSpec((tm, D_MODEL), lambda i: (i, 0))
    return pl.pallas_call(
        _ple_kernel,
        grid=(m // tm,),
        in_specs=[pl.BlockSpec(memory_space=pl.ANY), pl.BlockSpec((tm, PLE_DIM), lambda i: (i, 0)), _whole(hs),
                  _whole(ps), _whole(g), _whole(w_ple), _whole(w_pg)],
        out_specs=[row, _whole(hs)],
        out_shape=[jax.ShapeDtypeStruct((m, D_MODEL), F32), jax.ShapeDtypeStruct(hs.shape, F32)],
        scratch_shapes=[pltpu.VMEM((H_RING, tm, D_MODEL), F32), pltpu.SemaphoreType.DMA((H_RING,))],
        compiler_params=_params("arbitrary"),
        name="ple",
    )(h, p, hs, ps, g, w_ple, w_pg)


PLE_ROWS = 512
FFN_ROWS = 1024


def _dense_tail(h, p, hs, ps, w):
    h, hs = _ffn(h, hs, w["ffn_norm_g"], w["w_gate"], w["w_up"], w["w_down"], FFN_ROWS)
    return _ple(h, p, hs, ps, w["ple_norm_g"], w["w_ple"], w["w_ple_gate"], PLE_ROWS)


def kernel(x_prompt, x_sample, cache_k_win, cache_v_win, state_ret, p_prompt, p_sample,
           attn_norm_g, w_in, q_norm_g, k_norm_g, attn_sinks, ret_out_g, w_out,
           ffn_norm_g, w_gate, w_up, w_down, ple_norm_g, w_ple, w_ple_gate):
    assert w_in.shape[0] == 1 and x_sample.shape[1] == 1 and cache_k_win.shape[2] == WINDOW
    y_p, y_s = x_prompt, x_sample
    outs = [[] for _ in range(6)]
    for l in range(w_in.shape[0]):
        w = {
            "attn_norm_g": attn_norm_g[l].reshape(1, D_MODEL),
            "q_norm_g": q_norm_g[l].reshape(1, HEAD_DIM),
            "k_norm_g": k_norm_g[l].reshape(1, HEAD_DIM),
            "attn_sinks": attn_sinks[l],
            "ret_out_g": ret_out_g[l].reshape(1, RET_WIDTH),
            "ffn_norm_g": ffn_norm_g[l].reshape(1, D_MODEL),
            "ple_norm_g": ple_norm_g[l].reshape(1, D_MODEL),
        }
        batch, seq, _ = y_p.shape
        nb = y_s.shape[0]
        h_p = y_p.reshape(batch * seq, D_MODEL)
        h_s = y_s.reshape(nb, D_MODEL)
        z_p, z_s, *cast = _in_proj(h_p, h_s, w["attn_norm_g"], w_in[l], w_gate[l], w_up[l], w_down[l],
                                   w_out[l], w_ple_gate[l], w_ple[l])
        w.update(zip(("w_gate", "w_up", "w_down", "w_out", "w_ple_gate", "w_ple"), cast))
        h_p, h_s, kp, vp, sp, ks, vs, ss = _mixer_out(
            z_p, h_p, h_s, z_s, cache_k_win[l], cache_v_win[l], state_ret[l], w["w_out"], w["q_norm_g"],
            w["k_norm_g"], w["attn_sinks"], w["ret_out_g"], batch, seq)
        h_p, h_s = _dense_tail(h_p, p_prompt[l].reshape(batch * seq, PLE_DIM),
                               h_s, p_sample[l].reshape(nb, PLE_DIM), w)
        y_p, y_s = h_p.reshape(batch, seq, D_MODEL), h_s.reshape(nb, 1, D_MODEL)
        kp = kp.reshape(batch, BLK, N_KV_HEADS, HEAD_DIM)
        vp = vp.reshape(batch, BLK, N_KV_HEADS, HEAD_DIM)
        for acc, val in zip(outs, (kp, vp, sp, ks, vs, ss)):
            acc.append(val)
    kp, vp, sp, ks, vs, ss = (jnp.stack(o) for o in outs)
    return (y_p, y_s, kp, vp, sp, ks, vs, ss)
```

```python
import functools

import numpy as np
import jax
import jax.numpy as jnp
from jax import lax
from jax.experimental import pallas as pl
from jax.experimental.pallas import tpu as pltpu

D_MODEL = 2048
HEAD_DIM = 128
N_ATTN_HEADS = 8
N_KV_HEADS = 2
GQA_GROUP = N_ATTN_HEADS // N_KV_HEADS
ATTN_WIDTH = N_ATTN_HEADS * HEAD_DIM
KV_WIDTH = N_KV_HEADS * HEAD_DIM
WINDOW = 128
BLK = 128
N_RET_HEADS = 8
RET_DIM = 128
RET_WIDTH = N_RET_HEADS * RET_DIM
D_FF = 5632
IN_WIDTH = 5632
PLE_DIM = 256
PAST_LEN = 16384
ROPE_BASE = 10000.0
NORM_EPS = 1e-6
ATTN_SCALE = HEAD_DIM ** -0.5
RET_K_SCALE = RET_DIM ** -0.5

COL_RQ, COL_RK, COL_RV, COL_RG = 0, RET_WIDTH, 2 * RET_WIDTH, 3 * RET_WIDTH
COL_AQ = 4 * RET_WIDTH
COL_AK = COL_AQ + ATTN_WIDTH
COL_AV = COL_AK + KV_WIDTH
IN_TILE = 512
IN_TILES = IN_WIDTH // IN_TILE
IN_ROT = (ATTN_WIDTH + 2 * KV_WIDTH) // IN_TILE

VMEM_LIMIT = 56 * 1024 * 1024

F32 = jnp.float32
BF16 = jnp.bfloat16


def _params(*sem):
    return pltpu.CompilerParams(dimension_semantics=sem, vmem_limit_bytes=VMEM_LIMIT)


def _rms(x, g):
    return x * lax.rsqrt(jnp.mean(x * x, axis=-1, keepdims=True) + NORM_EPS) * g


def _dot(a, b):
    return jnp.dot(a, b, preferred_element_type=F32)


def _dot_nt(a, b):
    return lax.dot_general(a, b, (((1,), (1,)), ((), ())), preferred_element_type=F32)


def _dot_tn(a, b):
    return lax.dot_general(a, b, (((0,), (0,)), ((), ())), preferred_element_type=F32)


def _whole(a):
    return pl.BlockSpec(a.shape, lambda *_: (0,) * a.ndim)


def _in_proj_kernel(x_hbm, xs_ref, g_ref, w_ref, wg_ref, wu_ref, wd_ref, wo_ref, wpg_ref, wple_ref,
                    z_ref, zs_ref, wg_o, wu_o, wd_o, wo_o, wpg_o, wple_o,
                    xbuf, xsem, a_ref, wres):
    i, j = pl.program_id(0), pl.program_id(1)

    def x_copy(tile):
        return pltpu.make_async_copy(x_hbm.at[pl.ds(tile * IN_ROWS, IN_ROWS)], xbuf, xsem)

    def norm_tile():
        x_copy(i).wait()
        a = _rms(xbuf[...], g_ref[...]).astype(BF16)
        a_ref[:IN_ROWS] = a
        return a

    def fetch_next():
        @pl.when(i + 1 < pl.num_programs(0))
        def _():
            x_copy(i + 1).start()

    def cast_weights():
        for src, dst in ((wg_ref, wg_o), (wu_ref, wu_o), (wd_ref, wd_o), (wo_ref, wo_o), (wpg_ref, wpg_o),
                         (wple_ref, wple_o)):
            dst[...] = src[...].astype(BF16)

    def tile0(a):
        wres[j] = w_ref[...].astype(BF16)
        y = _dot(a, wres[j])
        z_ref[...] = y[:IN_ROWS].astype(z_ref.dtype)
        zs_ref[...] = y[IN_ROWS:]
        cast_weights()

    @pl.when((i == 0) & (j == 0))
    def _():
        x_copy(0).start()
        a_s = _rms(xs_ref[...], g_ref[...]).astype(BF16)
        a_ref[IN_ROWS:] = a_s
        tile0(jnp.concatenate([norm_tile(), a_s], axis=0))
        fetch_next()

    @pl.when((i == 0) & (j > 0))
    def _():
        tile0(a_ref[...])

    @pl.when((i > 0) & (j == 0))
    def _():
        z_ref[...] = _dot(norm_tile(), wres[j]).astype(z_ref.dtype)
        cast_weights()
        fetch_next()

    @pl.when((i > 0) & (j > 0))
    def _():
        z_ref[...] = _dot(a_ref[:IN_ROWS], wres[j]).astype(z_ref.dtype)
        cast_weights()


IN_ROWS = 1024
CAST_SPLIT = 8
FF_TILE = 512


def _in_proj(x, xs, g, w_in, w_gate, w_up, w_down, w_out, w_pg, w_ple):
    m = x.shape[0]
    ni = m // IN_ROWS
    steps = ni * IN_TILES
    assert m % IN_ROWS == 0 and FF_TILE == IN_TILE and D_FF == IN_WIDTH
    gu_rows, d_rows = D_MODEL // ni, D_FF // steps
    sq_rows, ple_rows = D_MODEL // (ni * CAST_SPLIT), PLE_DIM // ni
    assert gu_rows * ni == D_MODEL and d_rows * steps == D_FF and sq_rows % 16 == 0 and ple_rows % 16 == 0
    sq_idx = lambda i, j: (i * CAST_SPLIT + jnp.minimum(j, CAST_SPLIT - 1), 0)
    first = lambda i, j: jnp.where(i == 0, j, IN_TILES - 1)
    return pl.pallas_call(
        _in_proj_kernel,
        grid=(ni, IN_TILES),
        in_specs=[
            pl.BlockSpec(memory_space=pl.ANY),
            pl.BlockSpec(xs.shape, lambda i, j: (0, 0)),
            pl.BlockSpec((1, D_MODEL), lambda i, j: (0, 0)),
            pl.BlockSpec((D_MODEL, IN_TILE), lambda i, j: (0, (first(i, j) + IN_ROT) % IN_TILES)),
            pl.BlockSpec((gu_rows, FF_TILE), lambda i, j: (i, j)),
            pl.BlockSpec((gu_rows, FF_TILE), lambda i, j: (i, j)),
            pl.BlockSpec((d_rows, D_MODEL), lambda i, j: (i * IN_TILES + j, 0)),
            pl.BlockSpec((sq_rows, D_MODEL), sq_idx),
            pl.BlockSpec((sq_rows, D_MODEL), sq_idx),
            pl.BlockSpec((ple_rows, D_MODEL), lambda i, j: (i, 0)),
        ],
        out_specs=[
            pl.BlockSpec((IN_ROWS, IN_TILE), lambda i, j: (i, j)),
            pl.BlockSpec((xs.shape[0], IN_TILE), lambda i, j: (0, first(i, j))),
            pl.BlockSpec((None, gu_rows, FF_TILE), lambda i, j: (j, i, 0)),
            pl.BlockSpec((None, gu_rows, FF_TILE), lambda i, j: (j, i, 0)),
            pl.BlockSpec((d_rows, D_MODEL), lambda i, j: (i * IN_TILES + j, 0)),
            pl.BlockSpec((sq_rows, D_MODEL), sq_idx),
            pl.BlockSpec((sq_rows, D_MODEL), sq_idx),
            pl.BlockSpec((ple_rows, D_MODEL), lambda i, j: (i, 0)),
        ],
        out_shape=[
            jax.ShapeDtypeStruct((m, IN_WIDTH), BF16),
            jax.ShapeDtypeStruct((xs.shape[0], IN_WIDTH), F32),
            jax.ShapeDtypeStruct((D_FF // FF_TILE, D_MODEL, FF_TILE), BF16),
            jax.ShapeDtypeStruct((D_FF // FF_TILE, D_MODEL, FF_TILE), BF16),
            jax.ShapeDtypeStruct((D_FF, D_MODEL), BF16),
            jax.ShapeDtypeStruct((D_MODEL, D_MODEL), BF16),
            jax.ShapeDtypeStruct((D_MODEL, D_MODEL), BF16),
            jax.ShapeDtypeStruct((PLE_DIM, D_MODEL), BF16),
        ],
        scratch_shapes=[
            pltpu.VMEM((IN_ROWS, D_MODEL), F32),
            pltpu.SemaphoreType.DMA(()),
            pltpu.VMEM((IN_ROWS + xs.shape[0], D_MODEL), BF16),
            pltpu.VMEM((IN_TILES, D_MODEL, IN_TILE), BF16),
        ],
        compiler_params=_params("arbitrary", "arbitrary"),
        name="in_proj",
    )(x, xs, g, w_in, w_gate, w_up, w_down, w_out, w_pg, w_ple)


MIX_ROWS = 256
RET_CHUNK = 256
LOG2E = 1.4426950408889634


def _rotary(x, cos2, sin2):
    return x * cos2 + pltpu.roll(x, RET_DIM // 2, 1) * sin2


def _mix_tile(z_ref, kp_ref, vp_ref, qg_ref, kg_ref, sink_ref, cos_ref, sin_ref, qdec_ref, kdec_ref,
              cdec_ref, rg_ref, mix_ref, kwin_ref, vwin_ref, sout_ref, s_ref, first):
    qg = qg_ref[...] * (ATTN_SCALE * LOG2E)
    kg = kg_ref[...]
    row = lax.broadcasted_iota(jnp.int32, (BLK, 2 * BLK), 0)
    col = lax.broadcasted_iota(jnp.int32, (BLK, 2 * BLK), 1)
    cur_ok = (col >= BLK) & ((col - BLK) <= row)
    prev_ok = (col < BLK) & (col >= row)
    kp = kp_ref[...].astype(F32)
    kprev = jnp.concatenate([_rms(kp[:, h * HEAD_DIM:(h + 1) * HEAD_DIM], kg) for h in range(N_KV_HEADS)],
                            axis=1).astype(BF16)
    vprev = vp_ref[...]
    for blk in range(MIX_ROWS // BLK):
        rows = slice(blk * BLK, (blk + 1) * BLK)
        mask = cur_ok | (prev_ok & jnp.logical_not(first)) if blk == 0 else cur_ok | prev_ok
        k = z_ref[rows, COL_AK:COL_AK + KV_WIDTH].astype(F32)
        kn = [_rms(k[:, h * HEAD_DIM:(h + 1) * HEAD_DIM], kg) for h in range(N_KV_HEADS)]
        kcur = jnp.concatenate(kn, axis=1).astype(BF16)
        vcur = z_ref[rows, COL_AV:COL_AV + KV_WIDTH]
        for kh in range(N_KV_HEADS):
            sl = slice(kh * HEAD_DIM, (kh + 1) * HEAD_DIM)
            k_cat = jnp.concatenate([kprev[:, sl], kcur[:, sl]], axis=0)
            v_cat = jnp.concatenate([vprev[:, sl], vcur[:, sl]], axis=0)
            for g in range(GQA_GROUP):
                h = kh * GQA_GROUP + g
                q = z_ref[rows, COL_AQ + h * HEAD_DIM:COL_AQ + (h + 1) * HEAD_DIM].astype(F32)
                s = jnp.where(mask, _dot_nt(_rms(q, qg).astype(BF16), k_cat), -jnp.inf)
                sink = sink_ref[h] * LOG2E
                m = jnp.maximum(jnp.max(s, axis=-1, keepdims=True), sink)
                e = jnp.exp2(s - m)
                denom = jnp.sum(e, axis=-1, keepdims=True) + jnp.exp2(sink - m)
                mix_ref[rows, h * HEAD_DIM:(h + 1) * HEAD_DIM] = (
                    _dot(e.astype(BF16), v_cat) / denom).astype(mix_ref.dtype)
        kprev, vprev = kcur, vcur
    kwin_ref[0] = jnp.concatenate(kn, axis=1)
    vwin_ref[0] = vcur.astype(F32)

    tril = (lax.broadcasted_iota(jnp.int32, (RET_CHUNK, RET_CHUNK), 0)
            >= lax.broadcasted_iota(jnp.int32, (RET_CHUNK, RET_CHUNK), 1))
    for h in range(N_RET_HEADS):
        hs = slice(h * RET_DIM, (h + 1) * RET_DIM)
        col_of = lambda c: slice(c + h * RET_DIM, c + (h + 1) * RET_DIM)
        state = s_ref[h]
        for c in range(MIX_ROWS // RET_CHUNK):
            rows = slice(c * RET_CHUNK, (c + 1) * RET_CHUNK)
            cos2, sin2 = cos_ref[rows, :], sin_ref[rows, :]
            q = (_rotary(z_ref[rows, col_of(COL_RQ)].astype(F32), cos2, sin2) * qdec_ref[:, hs]).astype(BF16)
            k = (_rotary(z_ref[rows, col_of(COL_RK)].astype(F32), cos2, sin2) * kdec_ref[:, hs]).astype(BF16)
            v = z_ref[rows, col_of(COL_RV)]
            sc = jnp.where(tril, _dot_nt(q, k), 0.0)
            o = _dot(sc.astype(BF16), v) + _dot(q, state.astype(BF16))
            state = (state + _dot_tn(k, v)) * cdec_ref[h]
            o = o * lax.rsqrt(jnp.mean(o * o, axis=-1, keepdims=True) + NORM_EPS)
            gate = z_ref[rows, col_of(COL_RG)].astype(F32)
            mix_ref[rows, ATTN_WIDTH + h * RET_DIM:ATTN_WIDTH + (h + 1) * RET_DIM] = (
                o * rg_ref[:, hs] * (gate * jax.nn.sigmoid(gate))).astype(mix_ref.dtype)
        s_ref[h] = state
        sout_ref[0, h] = state


def _mixer_out_kernel(nsteps, tiles, groups, z_ref, kp_ref, vp_ref, qg_ref, kg_ref, sink_ref, cos_ref, sin_ref,
                      qdec_ref, kdec_ref, cdec_ref, rg_ref, h_ref, hs_ref, w_ref,
                      zs_ref, kbuf_ref, vbuf_ref, s0_ref, sinkc_ref, coss_ref, sins_ref, spread_ref, gamma_ref,
                      o_ref, os_ref, kwin_ref, vwin_ref, sout_ref, kout_ref, vout_ref, ssout_ref,
                      mix_a, mix_b, s_ref, mixs_scr):
    t = pl.program_id(0)
    per_group = tiles // groups

    @pl.when(t % nsteps == 0)
    def _():
        s_ref[...] = jnp.zeros_like(s_ref)

    def mix(dst):
        _mix_tile(z_ref, kp_ref, vp_ref, qg_ref, kg_ref, sink_ref, cos_ref, sin_ref, qdec_ref, kdec_ref,
                  cdec_ref, rg_ref, dst, kwin_ref, vwin_ref, sout_ref, s_ref, t % nsteps == 0)

    def project(src):
        o_ref[...] = h_ref[...] + _dot(src[...], w_ref[...])

    @pl.when(t == 0)
    def _():
        mix(mix_a)

    @pl.when((t > 0) & (t < tiles) & (t % 2 == 0))
    def _():
        project(mix_b)
        mix(mix_a)

    @pl.when((t < tiles) & (t % 2 == 1))
    def _():
        project(mix_a)
        mix(mix_b)

    @pl.when((t % per_group == per_group - 1) & (t < tiles))
    def _():
        g = t // per_group

        def put_mix(b, attn, ret):
            heads = [x[r:r + 1] for x in (attn, ret) for r in range(x.shape[0])]
            mixs_scr[g, b:b + 1, :] = jnp.concatenate(heads, axis=1)

        _sample_group(zs_ref, kbuf_ref, vbuf_ref, s0_ref, qg_ref, kg_ref, sinkc_ref, coss_ref, sins_ref,
                      spread_ref, gamma_ref, rg_ref, put_mix, kout_ref, vout_ref, ssout_ref)

    @pl.when(t == tiles)
    def _():
        project(mix_b if tiles % 2 == 0 else mix_a)
        mixs = jnp.concatenate([mixs_scr[g, :SAMPLE_GROUP, :] for g in range(groups)], axis=0)
        os_ref[...] = hs_ref[...] + _dot(mixs.astype(BF16), w_ref[...])


def _rope_tables(pos):
    half = RET_DIM // 2
    inv = ROPE_BASE ** (-np.arange(half, dtype=np.float64) / half)
    ang = np.asarray(pos, np.float64)[:, None] * inv[None, :]
    cos, sin = np.cos(ang), np.sin(ang)
    return (np.concatenate([cos, cos], axis=-1).astype(np.float32),
            np.concatenate([-sin, sin], axis=-1).astype(np.float32))


def _ret_log_decay():
    return np.log1p(-np.exp2(-5.0 - np.arange(N_RET_HEADS, dtype=np.float64)))


def _mixer_out(z, h, hs, zs, k_buf, v_buf, s0, w_out, qg, kg, sinks, rg, batch, seq):
    nsteps = seq // MIX_ROWS
    tiles = batch * nsteps
    nb = hs.shape[0]
    groups = nb // SAMPLE_GROUP
    assert nb % SAMPLE_GROUP == 0 and tiles % groups == 0 and SAMPLE_GROUP <= SUBLANES
    cache_rows = WINDOW * N_KV_HEADS
    cos2, sin2 = _rope_tables(np.arange(seq))
    cos_s, sin_s = _rope_tables(PAST_LEN + np.arange(1))
    lg = _ret_log_decay()
    steps = np.arange(RET_CHUNK, dtype=np.float64)[:, None, None] + 1.0
    lanes = lambda t: np.broadcast_to(t, (RET_CHUNK, N_RET_HEADS, RET_DIM)).reshape(RET_CHUNK, RET_WIDTH).astype(np.float32)
    q_dec = lanes(np.exp(lg[None, :, None] * steps))
    k_dec = lanes(np.exp(-lg[None, :, None] * steps) * RET_K_SCALE)
    chunk_dec = np.exp(lg * RET_CHUNK).astype(np.float32)
    gamma = np.exp(lg).astype(np.float32)
    spread = np.kron(np.eye(N_RET_HEADS), np.ones((1, RET_DIM))).astype(np.float32)
    sink_col = sinks.reshape(N_ATTN_HEADS, 1)
    smem = pl.BlockSpec(memory_space=pltpu.SMEM)
    win = jax.ShapeDtypeStruct((batch, BLK, KV_WIDTH), F32)
    blks = MIX_ROWS // BLK
    tile = lambda t: jnp.minimum(t, tiles - 1)
    prev_blk = lambda t: tile(t) * blks - jnp.minimum(tile(t) % nsteps, 1)
    behind = lambda t: (jnp.maximum(t - 1, 0), 0)
    seq_of = lambda t: tile(t) // nsteps
    group_of = lambda t: tile(t) // (tiles // groups)
    sample = lambda *shape: pl.BlockSpec((SAMPLE_GROUP,) + shape, lambda t: (group_of(t),) + (0,) * len(shape))
    cache, state = sample(cache_rows, HEAD_DIM), sample(N_RET_HEADS, RET_DIM, RET_DIM)
    cache_shape = jax.ShapeDtypeStruct((nb, cache_rows, HEAD_DIM), F32)
    y, ys, k_win, v_win, s_new, ks, vs, ss = pl.pallas_call(
        functools.partial(_mixer_out_kernel, nsteps, tiles, groups),
        grid=(tiles + 1,),
        in_specs=[
            pl.BlockSpec((MIX_ROWS, IN_WIDTH), lambda t: (tile(t), 0)),
            pl.BlockSpec((BLK, KV_WIDTH), lambda t: (prev_blk(t), COL_AK // KV_WIDTH)),
            pl.BlockSpec((BLK, KV_WIDTH), lambda t: (prev_blk(t), COL_AV // KV_WIDTH)),
            _whole(qg), _whole(kg), smem,
            pl.BlockSpec((MIX_ROWS, RET_DIM), lambda t: (tile(t) % nsteps, 0)),
            pl.BlockSpec((MIX_ROWS, RET_DIM), lambda t: (tile(t) % nsteps, 0)),
            _whole(q_dec), _whole(k_dec), smem, _whole(rg),
            pl.BlockSpec((MIX_ROWS, D_MODEL), behind), _whole(hs), _whole(w_out),
            sample(Z_HEADS, HEAD_DIM), cache, cache, state,
            _whole(sink_col), _whole(cos_s), _whole(sin_s), _whole(spread), smem,
        ],
        out_specs=[
            pl.BlockSpec((MIX_ROWS, D_MODEL), behind), _whole(hs),
            pl.BlockSpec((1, BLK, KV_WIDTH), lambda t: (seq_of(t), 0, 0)),
            pl.BlockSpec((1, BLK, KV_WIDTH), lambda t: (seq_of(t), 0, 0)),
            pl.BlockSpec((1, N_RET_HEADS, RET_DIM, RET_DIM), lambda t: (seq_of(t), 0, 0, 0)),
            cache, cache, state,
        ],
        out_shape=[jax.ShapeDtypeStruct(h.shape, F32), jax.ShapeDtypeStruct(hs.shape, F32), win, win,
                   jax.ShapeDtypeStruct((batch, N_RET_HEADS, RET_DIM, RET_DIM), F32),
                   cache_shape, cache_shape, jax.ShapeDtypeStruct(s0.shape, F32)],
        scratch_shapes=[pltpu.VMEM((MIX_ROWS, D_MODEL), BF16), pltpu.VMEM((MIX_ROWS, D_MODEL), BF16),
                        pltpu.VMEM((N_RET_HEADS, RET_DIM, RET_DIM), F32),
                        pltpu.VMEM((groups, SUBLANES, D_MODEL), F32)],
        compiler_params=_params("arbitrary"),
        name="mixer_out",
    )(z, z, z, qg, kg, sinks, cos2, sin2, q_dec, k_dec, chunk_dec, rg, h, hs, w_out,
      zs.reshape(nb, Z_HEADS, HEAD_DIM), k_buf.reshape(nb, cache_rows, HEAD_DIM),
      v_buf.reshape(nb, cache_rows, HEAD_DIM), s0, sink_col, cos_s, sin_s, spread, gamma)
    window = (nb, WINDOW, N_KV_HEADS, HEAD_DIM)
    return y, ys, k_win, v_win, s_new, ks.reshape(window), vs.reshape(window), ss


SAMPLE_GROUP = 4
Z_HEADS = IN_WIDTH // HEAD_DIM
ROW_RQ, ROW_RK, ROW_RV, ROW_RG, ROW_AQ, ROW_AK, ROW_AV = (
    c // HEAD_DIM for c in (COL_RQ, COL_RK, COL_RV, COL_RG, COL_AQ, COL_AK, COL_AV))
SUBLANES = 8


def _sample_group(z_ref, kbuf_ref, vbuf_ref, s0_ref, qg_ref, kg_ref, sink_ref, cos_ref, sin_ref,
                  spread_ref, gamma_ref, rg_ref, put_mix, kout_ref, vout_ref, s_out_ref):
    rows = WINDOW * N_KV_HEADS
    qg = qg_ref[...]
    kg = kg_ref[...]
    sink = sink_ref[...]
    head = lax.broadcasted_iota(jnp.int32, (N_ATTN_HEADS, HEAD_DIM), 0)
    first = head < GQA_GROUP
    own = ((lax.broadcasted_iota(jnp.int32, (N_ATTN_HEADS, rows), 1) % N_KV_HEADS)
           == (lax.broadcasted_iota(jnp.int32, (N_ATTN_HEADS, rows), 0) // GQA_GROUP))
    row = lax.broadcasted_iota(jnp.int32, (rows, HEAD_DIM), 0)
    cos2 = cos_ref[...]
    sin2 = sin_ref[...]
    spread = spread_ref[...]
    for b in range(SAMPLE_GROUP):
        part = lambda r, n: z_ref[b, r:r + n, :]
        qn = _rms(part(ROW_AQ, N_ATTN_HEADS), qg) * ATTN_SCALE
        kn = _rms(part(ROW_AK, N_KV_HEADS), kg)
        vn = part(ROW_AV, N_KV_HEADS)
        kbuf = kbuf_ref[b]
        vbuf = vbuf_ref[b]
        s = jnp.where(own, _dot_nt(qn.astype(BF16), kbuf.astype(BF16)), -jnp.inf)
        k_sel = jnp.where(first, kn[0:1], kn[1:2])
        v_sel = jnp.where(first, vn[0:1], vn[1:2])
        s_new = jnp.sum(qn * k_sel, axis=-1, keepdims=True)
        m = jnp.maximum(jnp.maximum(jnp.max(s, axis=-1, keepdims=True), s_new), sink)
        e = jnp.exp(s - m)
        e_new = jnp.exp(s_new - m)
        denom = jnp.sum(e, axis=-1, keepdims=True) + e_new + jnp.exp(sink - m)
        o_attn = (_dot(e.astype(BF16), vbuf.astype(BF16)) + e_new * v_sel) / denom
        for buf, new, out in ((kbuf, kn, kout_ref), (vbuf, vn, vout_ref)):
            shifted = pltpu.roll(buf, rows - N_KV_HEADS, 0)
            out[b] = jnp.where(row == rows - 2, new[0:1], jnp.where(row == rows - 1, new[1:2], shifted))
        q_cols = _dot_tn(_rotary(part(ROW_RQ, N_RET_HEADS), cos2, sin2), spread)
        k_cols = _dot_tn(_rotary(part(ROW_RK, N_RET_HEADS), cos2, sin2) * RET_K_SCALE, spread)
        v = part(ROW_RV, N_RET_HEADS)
        gate = part(ROW_RG, N_RET_HEADS)
        heads = []
        for h in range(N_RET_HEADS):
            hs = slice(h * RET_DIM, (h + 1) * RET_DIM)
            s_next = s0_ref[b, h] * gamma_ref[h] + k_cols[:, hs] * v[h:h + 1, :]
            s_out_ref[b, h] = s_next
            o = jnp.sum(q_cols[:, hs] * s_next, axis=0, keepdims=True)
            heads.append(o * lax.rsqrt(jnp.mean(o * o, axis=-1, keepdims=True) + NORM_EPS) * rg_ref[:, hs])
        o_ret = jnp.concatenate(heads, axis=0) * (gate * jax.nn.sigmoid(gate))
        put_mix(b, o_attn, o_ret)


def _ffn_kernel(h_ref, hs_ref, g_ref, wg_ref, wu_ref, wd_ref, o_ref, os_ref, f_ref):
    i, j = pl.program_id(0), pl.program_id(1)
    tm = h_ref.shape[0]

    def swiglu(f):
        gate = _dot(f, wg_ref[...])
        act = (gate * jax.nn.sigmoid(gate) * _dot(f, wu_ref[...])).astype(BF16)
        return _dot(act, wd_ref[...])

    @pl.when((i == 0) & (j == 0))
    def _():
        h, hs = h_ref[...], hs_ref[...]
        f = _rms(jnp.concatenate([h, hs], axis=0), g_ref[...]).astype(BF16)
        f_ref[...] = f
        y = swiglu(f)
        o_ref[...] = h + y[:tm]
        os_ref[...] = hs + y[tm:]

    @pl.when((i == 0) & (j > 0))
    def _():
        y = swiglu(f_ref[...])
        o_ref[...] += y[:tm]
        os_ref[...] += y[tm:]

    @pl.when((i > 0) & (j == 0))
    def _():
        h = h_ref[...]
        f = _rms(h, g_ref[...]).astype(BF16)
        f_ref[:tm] = f
        o_ref[...] = h + swiglu(f)

    @pl.when((i > 0) & (j > 0))
    def _():
        o_ref[...] += swiglu(f_ref[:tm])


def _ffn(h, hs, g, wg, wu, wd, tm):
    m = h.shape[0]
    row = pl.BlockSpec((tm, D_MODEL), lambda i, j: (i, 0))
    return pl.pallas_call(
        _ffn_kernel,
        grid=(m // tm, D_FF // FF_TILE),
        in_specs=[
            row, _whole(hs), _whole(g),
            pl.BlockSpec((None, D_MODEL, FF_TILE), lambda i, j: (j, 0, 0)),
            pl.BlockSpec((None, D_MODEL, FF_TILE), lambda i, j: (j, 0, 0)),
            pl.BlockSpec((FF_TILE, D_MODEL), lambda i, j: (j, 0)),
        ],
        out_specs=[row, _whole(hs)],
        out_shape=[jax.ShapeDtypeStruct((m, D_MODEL), F32), jax.ShapeDtypeStruct(hs.shape, F32)],
        scratch_shapes=[pltpu.VMEM((tm + hs.shape[0], D_MODEL), BF16)],
        compiler_params=_params("arbitrary", "arbitrary"),
        name="ffn",
    )(h, hs, g, wg, wu, wd)


def _ple_kernel(h_ref, p_ref, hs_ref, ps_ref, g_ref, wp_ref, wg_ref, o_ref, os_ref):
    tm = h_ref.shape[0]

    def embed(h, p):
        gate = jax.nn.sigmoid(_dot(_rms(h, g_ref[...]).astype(BF16), wg_ref[...]))
        return h + _dot(p.astype(BF16), wp_ref[...]) * gate

    @pl.when(pl.program_id(0) == 0)
    def _():
        y = embed(jnp.concatenate([h_ref[...], hs_ref[...]], axis=0),
                  jnp.concatenate([p_ref[...], ps_ref[...]], axis=0))
        o_ref[...] = y[:tm]
        os_ref[...] = y[tm:]

    @pl.when(pl.program_id(0) > 0)
    def _():
        o_ref[...] = embed(h_ref[...], p_ref[...])


def _ple(h, p, hs, ps, g, w_ple, w_pg, tm):
    m = h.shape[0]
    row = pl.BlockSpec((tm, D_MODEL), lambda i: (i, 0))
    return pl.pallas_call(
        _ple_kernel,
        grid=(m // tm,),
        in_specs=[row, pl.BlockSpec((tm, PLE_DIM), lambda i: (i, 0)), _whole(hs), _whole(ps), _whole(g),
                  _whole(w_ple), _whole(w_pg)],
        out_specs=[row, _whole(hs)],
        out_shape=[jax.ShapeDtypeStruct((m, D_MODEL), F32), jax.ShapeDtypeStruct(hs.shape, F32)],
        compiler_params=_params("arbitrary"),
        name="ple",
    )(h, p, hs, ps, g, w_ple, w_pg)


PLE_ROWS = 512
FFN_ROWS = 1024


def _dense_tail(h, p, hs, ps, w):
    h, hs = _ffn(h, hs, w["ffn_norm_g"], w["w_gate"], w["w_up"], w["w_down"], FFN_ROWS)
    return _ple(h, p, hs, ps, w["ple_norm_g"], w["w_ple"], w["w_ple_gate"], PLE_ROWS)


def kernel(x_prompt, x_sample, cache_k_win, cache_v_win, state_ret, p_prompt, p_sample,
           attn_norm_g, w_in, q_norm_g, k_norm_g, attn_sinks, ret_out_g, w_out,
           ffn_norm_g, w_gate, w_up, w_down, ple_norm_g, w_ple, w_ple_gate):
    assert w_in.shape[0] == 1 and x_sample.shape[1] == 1 and cache_k_win.shape[2] == WINDOW
    y_p, y_s = x_prompt, x_sample
    outs = [[] for _ in range(6)]
    for l in range(w_in.shape[0]):
        w = {
            "attn_norm_g": attn_norm_g[l].reshape(1, D_MODEL),
            "q_norm_g": q_norm_g[l].reshape(1, HEAD_DIM),
            "k_norm_g": k_norm_g[l].reshape(1, HEAD_DIM),
            "attn_sinks": attn_sinks[l],
            "ret_out_g": ret_out_g[l].reshape(1, RET_WIDTH),
            "ffn_norm_g": ffn_norm_g[l].reshape(1, D_MODEL),
            "ple_norm_g": ple_norm_g[l].reshape(1, D_MODEL),
        }
        batch, seq, _ = y_p.shape
        nb = y_s.shape[0]
        h_p = y_p.reshape(batch * seq, D_MODEL)
        h_s = y_s.reshape(nb, D_MODEL)
        z_p, z_s, *cast = _in_proj(h_p, h_s, w["attn_norm_g"], w_in[l], w_gate[l], w_up[l], w_down[l],
                                   w_out[l], w_ple_gate[l], w_ple[l])
        w.update(zip(("w_gate", "w_up", "w_down", "w_out", "w_ple_gate", "w_ple"), cast))
        h_p, h_s, kp, vp, sp, ks, vs, ss = _mixer_out(
            z_p, h_p, h_s, z_s, cache_k_win[l], cache_v_win[l], state_ret[l], w["w_out"], w["q_norm_g"],
            w["k_norm_g"], w["attn_sinks"], w["ret_out_g"], batch, seq)
        h_p, h_s = _dense_tail(h_p, p_prompt[l].reshape(batch * seq, PLE_DIM),
                               h_s, p_sample[l].reshape(nb, PLE_DIM), w)
        y_p, y_s = h_p.reshape(batch, seq, D_MODEL), h_s.reshape(nb, 1, D_MODEL)
        kp = kp.reshape(batch, BLK, N_KV_HEADS, HEAD_DIM)
        vp = vp.reshape(batch, BLK, N_KV_HEADS, HEAD_DIM)
        for acc, val in zip(outs, (kp, vp, sp, ks, vs, ss)):
            acc.append(val)
    kp, vp, sp, ks, vs, ss = (jnp.stack(o) for o in outs)
    return (y_p, y_s, kp, vp, sp, ks, vs, ss)
```

```python
import functools

import numpy as np
import jax
import jax.numpy as jnp
from jax import lax
from jax.experimental import pallas as pl
from jax.experimental.pallas import tpu as pltpu

D_MODEL = 2048
HEAD_DIM = 128
N_ATTN_HEADS = 8
N_KV_HEADS = 2
GQA_GROUP = N_ATTN_HEADS // N_KV_HEADS
ATTN_WIDTH = N_ATTN_HEADS * HEAD_DIM
KV_WIDTH = N_KV_HEADS * HEAD_DIM
WINDOW = 128
BLK = 128
N_RET_HEADS = 8
RET_DIM = 128
RET_WIDTH = N_RET_HEADS * RET_DIM
D_FF = 5632
IN_WIDTH = 5632
PLE_DIM = 256
PAST_LEN = 16384
ROPE_BASE = 10000.0
NORM_EPS = 1e-6
ATTN_SCALE = HEAD_DIM ** -0.5
RET_K_SCALE = RET_DIM ** -0.5

COL_RQ, COL_RK, COL_RV, COL_RG = 0, RET_WIDTH, 2 * RET_WIDTH, 3 * RET_WIDTH
COL_AQ = 4 * RET_WIDTH
COL_AK = COL_AQ + ATTN_WIDTH
COL_AV = COL_AK + KV_WIDTH
IN_TILE = 512
IN_TILES = IN_WIDTH // IN_TILE
IN_ROT = (ATTN_WIDTH + 2 * KV_WIDTH) // IN_TILE

VMEM_LIMIT = 56 * 1024 * 1024

F32 = jnp.float32
BF16 = jnp.bfloat16


def _params(*sem):
    return pltpu.CompilerParams(dimension_semantics=sem, vmem_limit_bytes=VMEM_LIMIT)


def _rms(x, g):
    return x * lax.rsqrt(jnp.mean(x * x, axis=-1, keepdims=True) + NORM_EPS) * g


def _dot(a, b):
    return jnp.dot(a, b, preferred_element_type=F32)


def _dot_nt(a, b):
    return lax.dot_general(a, b, (((1,), (1,)), ((), ())), preferred_element_type=F32)


def _dot_tn(a, b):
    return lax.dot_general(a, b, (((0,), (0,)), ((), ())), preferred_element_type=F32)


def _whole(a):
    return pl.BlockSpec(a.shape, lambda *_: (0,) * a.ndim)


def _in_proj_kernel(x_hbm, xs_ref, g_ref, w_ref, wg_ref, wu_ref, wd_ref, wo_ref, wpg_ref, wple_ref,
                    z_ref, zs_ref, wg_o, wu_o, wd_o, wo_o, wpg_o, wple_o,
                    xbuf, xsem, a_ref, wres):
    i, j = pl.program_id(0), pl.program_id(1)

    def x_copy(tile):
        return pltpu.make_async_copy(x_hbm.at[pl.ds(tile * IN_ROWS, IN_ROWS)], xbuf, xsem)

    @pl.when((i == 0) & (j == 0))
    def _():
        x_copy(0).start()
        a_ref[IN_ROWS:] = _rms(xs_ref[...], g_ref[...]).astype(BF16)

    @pl.when(j == 0)
    def _():
        x_copy(i).wait()
        a_ref[:IN_ROWS] = _rms(xbuf[...], g_ref[...]).astype(BF16)

        @pl.when(i + 1 < pl.num_programs(0))
        def _():
            x_copy(i + 1).start()

    def cast_weights():
        for src, dst in ((wg_ref, wg_o), (wu_ref, wu_o), (wd_ref, wd_o), (wo_ref, wo_o), (wpg_ref, wpg_o),
                         (wple_ref, wple_o)):
            dst[...] = src[...].astype(BF16)

    @pl.when(i == 0)
    def _():
        wres[j] = w_ref[...].astype(BF16)
        y = _dot(a_ref[...], wres[j])
        z_ref[...] = y[:IN_ROWS].astype(z_ref.dtype)
        zs_ref[...] = y[IN_ROWS:]
        cast_weights()

    @pl.when(i > 0)
    def _():
        z_ref[...] = _dot(a_ref[:IN_ROWS], wres[j]).astype(z_ref.dtype)
        cast_weights()


IN_ROWS = 1024
CAST_SPLIT = 8
FF_TILE = 512


def _in_proj(x, xs, g, w_in, w_gate, w_up, w_down, w_out, w_pg, w_ple):
    m = x.shape[0]
    ni = m // IN_ROWS
    steps = ni * IN_TILES
    assert m % IN_ROWS == 0 and FF_TILE == IN_TILE and D_FF == IN_WIDTH
    gu_rows, d_rows = D_MODEL // ni, D_FF // steps
    sq_rows, ple_rows = D_MODEL // (ni * CAST_SPLIT), PLE_DIM // ni
    assert gu_rows * ni == D_MODEL and d_rows * steps == D_FF and sq_rows % 16 == 0 and ple_rows % 16 == 0
    sq_idx = lambda i, j: (i * CAST_SPLIT + jnp.minimum(j, CAST_SPLIT - 1), 0)
    first = lambda i, j: jnp.where(i == 0, j, IN_TILES - 1)
    return pl.pallas_call(
        _in_proj_kernel,
        grid=(ni, IN_TILES),
        in_specs=[
            pl.BlockSpec(memory_space=pl.ANY),
            pl.BlockSpec(xs.shape, lambda i, j: (0, 0)),
            pl.BlockSpec((1, D_MODEL), lambda i, j: (0, 0)),
            pl.BlockSpec((D_MODEL, IN_TILE), lambda i, j: (0, (first(i, j) + IN_ROT) % IN_TILES)),
            pl.BlockSpec((gu_rows, FF_TILE), lambda i, j: (i, j)),
            pl.BlockSpec((gu_rows, FF_TILE), lambda i, j: (i, j)),
            pl.BlockSpec((d_rows, D_MODEL), lambda i, j: (i * IN_TILES + j, 0)),
            pl.BlockSpec((sq_rows, D_MODEL), sq_idx),
            pl.BlockSpec((sq_rows, D_MODEL), sq_idx),
            pl.BlockSpec((ple_rows, D_MODEL), lambda i, j: (i, 0)),
        ],
        out_specs=[
            pl.BlockSpec((IN_ROWS, IN_TILE), lambda i, j: (i, j)),
            pl.BlockSpec((xs.shape[0], IN_TILE), lambda i, j: (0, first(i, j))),
            pl.BlockSpec((None, gu_rows, FF_TILE), lambda i, j: (j, i, 0)),
            pl.BlockSpec((None, gu_rows, FF_TILE), lambda i, j: (j, i, 0)),
            pl.BlockSpec((d_rows, D_MODEL), lambda i, j: (i * IN_TILES + j, 0)),
            pl.BlockSpec((sq_rows, D_MODEL), sq_idx),
            pl.BlockSpec((sq_rows, D_MODEL), sq_idx),
            pl.BlockSpec((ple_rows, D_MODEL), lambda i, j: (i, 0)),
        ],
        out_shape=[
            jax.ShapeDtypeStruct((m, IN_WIDTH), BF16),
            jax.ShapeDtypeStruct((xs.shape[0], IN_WIDTH), F32),
            jax.ShapeDtypeStruct((D_FF // FF_TILE, D_MODEL, FF_TILE), BF16),
            jax.ShapeDtypeStruct((D_FF // FF_TILE, D_MODEL, FF_TILE), BF16),
            jax.ShapeDtypeStruct((D_FF, D_MODEL), BF16),
            jax.ShapeDtypeStruct((D_MODEL, D_MODEL), BF16),
            jax.ShapeDtypeStruct((D_MODEL, D_MODEL), BF16),
            jax.ShapeDtypeStruct((PLE_DIM, D_MODEL), BF16),
        ],
        scratch_shapes=[
            pltpu.VMEM((IN_ROWS, D_MODEL), F32),
            pltpu.SemaphoreType.DMA(()),
            pltpu.VMEM((IN_ROWS + xs.shape[0], D_MODEL), BF16),
            pltpu.VMEM((IN_TILES, D_MODEL, IN_TILE), BF16),
        ],
        compiler_params=_params("arbitrary", "arbitrary"),
        name="in_proj",
    )(x, xs, g, w_in, w_gate, w_up, w_down, w_out, w_pg, w_ple)


MIX_ROWS = 256
RET_CHUNK = 256
LOG2E = 1.4426950408889634


def _rotary(x, cos2, sin2):
    return x * cos2 + pltpu.roll(x, RET_DIM // 2, 1) * sin2


def _mix_tile(z_ref, kp_ref, vp_ref, qg_ref, kg_ref, sink_ref, cos_ref, sin_ref, qdec_ref, kdec_ref,
              cdec_ref, rg_ref, mix_ref, kwin_ref, vwin_ref, sout_ref, s_ref, first, row0):
    qg = qg_ref[...] * (ATTN_SCALE * LOG2E)
    kg = kg_ref[...]
    row = lax.broadcasted_iota(jnp.int32, (BLK, 2 * BLK), 0)
    col = lax.broadcasted_iota(jnp.int32, (BLK, 2 * BLK), 1)
    cur_ok = (col >= BLK) & ((col - BLK) <= row)
    prev_ok = (col < BLK) & (col >= row)
    kp = kp_ref[...].astype(F32)
    kprev = jnp.concatenate([_rms(kp[:, h * HEAD_DIM:(h + 1) * HEAD_DIM], kg) for h in range(N_KV_HEADS)],
                            axis=1).astype(BF16)
    vprev = vp_ref[...]
    for blk in range(MIX_ROWS // BLK):
        rows = slice(blk * BLK, (blk + 1) * BLK)
        mask = cur_ok | (prev_ok & jnp.logical_not(first)) if blk == 0 else cur_ok | prev_ok
        k = z_ref[rows, COL_AK:COL_AK + KV_WIDTH].astype(F32)
        kn = [_rms(k[:, h * HEAD_DIM:(h + 1) * HEAD_DIM], kg) for h in range(N_KV_HEADS)]
        kcur = jnp.concatenate(kn, axis=1).astype(BF16)
        vcur = z_ref[rows, COL_AV:COL_AV + KV_WIDTH]
        for kh in range(N_KV_HEADS):
            sl = slice(kh * HEAD_DIM, (kh + 1) * HEAD_DIM)
            k_cat = jnp.concatenate([kprev[:, sl], kcur[:, sl]], axis=0)
            v_cat = jnp.concatenate([vprev[:, sl], vcur[:, sl]], axis=0)
            for g in range(GQA_GROUP):
                h = kh * GQA_GROUP + g
                q = z_ref[rows, COL_AQ + h * HEAD_DIM:COL_AQ + (h + 1) * HEAD_DIM].astype(F32)
                s = jnp.where(mask, _dot_nt(_rms(q, qg).astype(BF16), k_cat), -jnp.inf)
                sink = sink_ref[h] * LOG2E
                m = jnp.maximum(jnp.max(s, axis=-1, keepdims=True), sink)
                e = jnp.exp2(s - m)
                denom = jnp.sum(e, axis=-1, keepdims=True) + jnp.exp2(sink - m)
                mix_ref[row0 + blk * BLK:row0 + (blk + 1) * BLK, h * HEAD_DIM:(h + 1) * HEAD_DIM] = (
                    _dot(e.astype(BF16), v_cat) / denom).astype(mix_ref.dtype)
        kprev, vprev = kcur, vcur
    kwin_ref[0] = jnp.concatenate(kn, axis=1)
    vwin_ref[0] = vcur.astype(F32)

    tril = (lax.broadcasted_iota(jnp.int32, (RET_CHUNK, RET_CHUNK), 0)
            >= lax.broadcasted_iota(jnp.int32, (RET_CHUNK, RET_CHUNK), 1))
    for h in range(N_RET_HEADS):
        hs = slice(h * RET_DIM, (h + 1) * RET_DIM)
        col_of = lambda c: slice(c + h * RET_DIM, c + (h + 1) * RET_DIM)
        state = s_ref[h]
        for c in range(MIX_ROWS // RET_CHUNK):
            rows = slice(c * RET_CHUNK, (c + 1) * RET_CHUNK)
            cos2, sin2 = cos_ref[rows, :], sin_ref[rows, :]
            q = (_rotary(z_ref[rows, col_of(COL_RQ)].astype(F32), cos2, sin2) * qdec_ref[:, hs]).astype(BF16)
            k = (_rotary(z_ref[rows, col_of(COL_RK)].astype(F32), cos2, sin2) * kdec_ref[:, hs]).astype(BF16)
            v = z_ref[rows, col_of(COL_RV)]
            sc = jnp.where(tril, _dot_nt(q, k), 0.0)
            o = _dot(sc.astype(BF16), v) + _dot(q, state.astype(BF16))
            state = (state + _dot_tn(k, v)) * cdec_ref[h]
            o = o * lax.rsqrt(jnp.mean(o * o, axis=-1, keepdims=True) + NORM_EPS)
            gate = z_ref[rows, col_of(COL_RG)].astype(F32)
            mix_ref[row0 + c * RET_CHUNK:row0 + (c + 1) * RET_CHUNK,
                    ATTN_WIDTH + h * RET_DIM:ATTN_WIDTH + (h + 1) * RET_DIM] = (
                o * rg_ref[:, hs] * (gate * jax.nn.sigmoid(gate))).astype(mix_ref.dtype)
        s_ref[h] = state
        sout_ref[0, h] = state


def _mixer_out_kernel(nsteps, tiles, groups, z_ref, kp_ref, vp_ref, qg_ref, kg_ref, sink_ref, cos_ref, sin_ref,
                      qdec_ref, kdec_ref, cdec_ref, rg_ref, h_ref, hs_ref, w_ref,
                      zs_ref, kbuf_ref, vbuf_ref, s0_ref, sinkc_ref, coss_ref, sins_ref, spread_ref, gamma_ref,
                      o_ref, os_ref, kwin_ref, vwin_ref, sout_ref, kout_ref, vout_ref, ssout_ref,
                      mix_scr, s_ref, mixs_scr):
    t = pl.program_id(0)
    per_group = tiles // groups

    @pl.when(t % nsteps == 0)
    def _():
        s_ref[...] = jnp.zeros_like(s_ref)

    def mix(quarter):
        _mix_tile(z_ref, kp_ref, vp_ref, qg_ref, kg_ref, sink_ref, cos_ref, sin_ref, qdec_ref, kdec_ref,
                  cdec_ref, rg_ref, mix_scr, kwin_ref, vwin_ref, sout_ref, s_ref, t % nsteps == 0,
                  quarter * MIX_ROWS)

    def project(pair):
        o_ref[...] = h_ref[...] + _dot(mix_scr[pair * 2 * MIX_ROWS:(pair + 1) * 2 * MIX_ROWS, :], w_ref[...])

    @pl.when(t == 0)
    def _():
        mix(0)

    for quarter in range(4):
        @pl.when((t > 0) & (t < tiles) & (t % 4 == quarter))
        def _(quarter=quarter):
            if quarter % 2 == 0:
                project(1 - quarter // 2)
            mix(quarter)

    @pl.when((t % per_group == per_group - 1) & (t < tiles))
    def _():
        g = t // per_group

        def put_mix(b, attn, ret):
            heads = [x[r:r + 1] for x in (attn, ret) for r in range(x.shape[0])]
            mixs_scr[g, b:b + 1, :] = jnp.concatenate(heads, axis=1)

        _sample_group(zs_ref, kbuf_ref, vbuf_ref, s0_ref, qg_ref, kg_ref, sinkc_ref, coss_ref, sins_ref,
                      spread_ref, gamma_ref, rg_ref, put_mix, kout_ref, vout_ref, ssout_ref)

    @pl.when(t == tiles)
    def _():
        project(1)
        mixs = jnp.concatenate([mixs_scr[g, :SAMPLE_GROUP, :] for g in range(groups)], axis=0)
        os_ref[...] = hs_ref[...] + _dot(mixs.astype(BF16), w_ref[...])


def _rope_tables(pos):
    half = RET_DIM // 2
    inv = ROPE_BASE ** (-np.arange(half, dtype=np.float64) / half)
    ang = np.asarray(pos, np.float64)[:, None] * inv[None, :]
    cos, sin = np.cos(ang), np.sin(ang)
    return (np.concatenate([cos, cos], axis=-1).astype(np.float32),
            np.concatenate([-sin, sin], axis=-1).astype(np.float32))


def _ret_log_decay():
    return np.log1p(-np.exp2(-5.0 - np.arange(N_RET_HEADS, dtype=np.float64)))


def _mixer_out(z, h, hs, zs, k_buf, v_buf, s0, w_out, qg, kg, sinks, rg, batch, seq):
    nsteps = seq // MIX_ROWS
    tiles = batch * nsteps
    nb = hs.shape[0]
    groups = nb // SAMPLE_GROUP
    assert nb % SAMPLE_GROUP == 0 and tiles % groups == 0 and SAMPLE_GROUP <= SUBLANES and tiles % 4 == 0
    cache_rows = WINDOW * N_KV_HEADS
    cos2, sin2 = _rope_tables(np.arange(seq))
    cos_s, sin_s = _rope_tables(PAST_LEN + np.arange(1))
    lg = _ret_log_decay()
    steps = np.arange(RET_CHUNK, dtype=np.float64)[:, None, None] + 1.0
    lanes = lambda t: np.broadcast_to(t, (RET_CHUNK, N_RET_HEADS, RET_DIM)).reshape(RET_CHUNK, RET_WIDTH).astype(np.float32)
    q_dec = lanes(np.exp(lg[None, :, None] * steps))
    k_dec = lanes(np.exp(-lg[None, :, None] * steps) * RET_K_SCALE)
    chunk_dec = np.exp(lg * RET_CHUNK).astype(np.float32)
    gamma = np.exp(lg).astype(np.float32)
    spread = np.kron(np.eye(N_RET_HEADS), np.ones((1, RET_DIM))).astype(np.float32)
    sink_col = sinks.reshape(N_ATTN_HEADS, 1)
    smem = pl.BlockSpec(memory_space=pltpu.SMEM)
    win = jax.ShapeDtypeStruct((batch, BLK, KV_WIDTH), F32)
    blks = MIX_ROWS // BLK
    tile = lambda t: jnp.minimum(t, tiles - 1)
    prev_blk = lambda t: tile(t) * blks - jnp.minimum(tile(t) % nsteps, 1)
    behind = lambda t: (jnp.maximum(t - 2, 0) // 2, 0)
    seq_of = lambda t: tile(t) // nsteps
    group_of = lambda t: tile(t) // (tiles // groups)
    sample = lambda *shape: pl.BlockSpec((SAMPLE_GROUP,) + shape, lambda t: (group_of(t),) + (0,) * len(shape))
    cache, state = sample(cache_rows, HEAD_DIM), sample(N_RET_HEADS, RET_DIM, RET_DIM)
    cache_shape = jax.ShapeDtypeStruct((nb, cache_rows, HEAD_DIM), F32)
    y, ys, k_win, v_win, s_new, ks, vs, ss = pl.pallas_call(
        functools.partial(_mixer_out_kernel, nsteps, tiles, groups),
        grid=(tiles + 1,),
        in_specs=[
            pl.BlockSpec((MIX_ROWS, IN_WIDTH), lambda t: (tile(t), 0)),
            pl.BlockSpec((BLK, KV_WIDTH), lambda t: (prev_blk(t), COL_AK // KV_WIDTH)),
            pl.BlockSpec((BLK, KV_WIDTH), lambda t: (prev_blk(t), COL_AV // KV_WIDTH)),
            _whole(qg), _whole(kg), smem,
            pl.BlockSpec((MIX_ROWS, RET_DIM), lambda t: (tile(t) % nsteps, 0)),
            pl.BlockSpec((MIX_ROWS, RET_DIM), lambda t: (tile(t) % nsteps, 0)),
            _whole(q_dec), _whole(k_dec), smem, _whole(rg),
            pl.BlockSpec((2 * MIX_ROWS, D_MODEL), behind), _whole(hs), _whole(w_out),
            sample(Z_HEADS, HEAD_DIM), cache, cache, state,
            _whole(sink_col), _whole(cos_s), _whole(sin_s), _whole(spread), smem,
        ],
        out_specs=[
            pl.BlockSpec((2 * MIX_ROWS, D_MODEL), behind), _whole(hs),
            pl.BlockSpec((1, BLK, KV_WIDTH), lambda t: (seq_of(t), 0, 0)),
            pl.BlockSpec((1, BLK, KV_WIDTH), lambda t: (seq_of(t), 0, 0)),
            pl.BlockSpec((1, N_RET_HEADS, RET_DIM, RET_DIM), lambda t: (seq_of(t), 0, 0, 0)),
            cache, cache, state,
        ],
        out_shape=[jax.ShapeDtypeStruct(h.shape, F32), jax.ShapeDtypeStruct(hs.shape, F32), win, win,
                   jax.ShapeDtypeStruct((batch, N_RET_HEADS, RET_DIM, RET_DIM), F32),
                   cache_shape, cache_shape, jax.ShapeDtypeStruct(s0.shape, F32)],
        scratch_shapes=[pltpu.VMEM((4 * MIX_ROWS, D_MODEL), BF16),
                        pltpu.VMEM((N_RET_HEADS, RET_DIM, RET_DIM), F32),
                        pltpu.VMEM((groups, SUBLANES, D_MODEL), F32)],
        compiler_params=_params("arbitrary"),
        name="mixer_out",
    )(z, z, z, qg, kg, sinks, cos2, sin2, q_dec, k_dec, chunk_dec, rg, h, hs, w_out,
      zs.reshape(nb, Z_HEADS, HEAD_DIM), k_buf.reshape(nb, cache_rows, HEAD_DIM),
      v_buf.reshape(nb, cache_rows, HEAD_DIM), s0, sink_col, cos_s, sin_s, spread, gamma)
    window = (nb, WINDOW, N_KV_HEADS, HEAD_DIM)
    return y, ys, k_win, v_win, s_new, ks.reshape(window), vs.reshape(window), ss


SAMPLE_GROUP = 4
Z_HEADS = IN_WIDTH // HEAD_DIM
ROW_RQ, ROW_RK, ROW_RV, ROW_RG, ROW_AQ, ROW_AK, ROW_AV = (
    c // HEAD_DIM for c in (COL_RQ, COL_RK, COL_RV, COL_RG, COL_AQ, COL_AK, COL_AV))
SUBLANES = 8


def _sample_group(z_ref, kbuf_ref, vbuf_ref, s0_ref, qg_ref, kg_ref, sink_ref, cos_ref, sin_ref,
                  spread_ref, gamma_ref, rg_ref, put_mix, kout_ref, vout_ref, s_out_ref):
    rows = WINDOW * N_KV_HEADS
    qg = qg_ref[...]
    kg = kg_ref[...]
    sink = sink_ref[...]
    head = lax.broadcasted_iota(jnp.int32, (N_ATTN_HEADS, HEAD_DIM), 0)
    first = head < GQA_GROUP
    own = ((lax.broadcasted_iota(jnp.int32, (N_ATTN_HEADS, rows), 1) % N_KV_HEADS)
           == (lax.broadcasted_iota(jnp.int32, (N_ATTN_HEADS, rows), 0) // GQA_GROUP))
    row = lax.broadcasted_iota(jnp.int32, (rows, HEAD_DIM), 0)
    cos2 = cos_ref[...]
    sin2 = sin_ref[...]
    spread = spread_ref[...]
    for b in range(SAMPLE_GROUP):
        part = lambda r, n: z_ref[b, r:r + n, :]
        qn = _rms(part(ROW_AQ, N_ATTN_HEADS), qg) * ATTN_SCALE
        kn = _rms(part(ROW_AK, N_KV_HEADS), kg)
        vn = part(ROW_AV, N_KV_HEADS)
        kbuf = kbuf_ref[b]
        vbuf = vbuf_ref[b]
        s = jnp.where(own, _dot_nt(qn.astype(BF16), kbuf.astype(BF16)), -jnp.inf)
        k_sel = jnp.where(first, kn[0:1], kn[1:2])
        v_sel = jnp.where(first, vn[0:1], vn[1:2])
        s_new = jnp.sum(qn * k_sel, axis=-1, keepdims=True)
        m = jnp.maximum(jnp.maximum(jnp.max(s, axis=-1, keepdims=True), s_new), sink)
        e = jnp.exp(s - m)
        e_new = jnp.exp(s_new - m)
        denom = jnp.sum(e, axis=-1, keepdims=True) + e_new + jnp.exp(sink - m)
        o_attn = (_dot(e.astype(BF16), vbuf.astype(BF16)) + e_new * v_sel) / denom
        for buf, new, out in ((kbuf, kn, kout_ref), (vbuf, vn, vout_ref)):
            shifted = pltpu.roll(buf, rows - N_KV_HEADS, 0)
            out[b] = jnp.where(row == rows - 2, new[0:1], jnp.where(row == rows - 1, new[1:2], shifted))
        q_cols = _dot_tn(_rotary(part(ROW_RQ, N_RET_HEADS), cos2, sin2), spread)
        k_cols = _dot_tn(_rotary(part(ROW_RK, N_RET_HEADS), cos2, sin2) * RET_K_SCALE, spread)
        v = part(ROW_RV, N_RET_HEADS)
        gate = part(ROW_RG, N_RET_HEADS)
        heads = []
        for h in range(N_RET_HEADS):
            hs = slice(h * RET_DIM, (h + 1) * RET_DIM)
            s_next = s0_ref[b, h] * gamma_ref[h] + k_cols[:, hs] * v[h:h + 1, :]
            s_out_ref[b, h] = s_next
            o = jnp.sum(q_cols[:, hs] * s_next, axis=0, keepdims=True)
            heads.append(o * lax.rsqrt(jnp.mean(o * o, axis=-1, keepdims=True) + NORM_EPS) * rg_ref[:, hs])
        o_ret = jnp.concatenate(heads, axis=0) * (gate * jax.nn.sigmoid(gate))
        put_mix(b, o_attn, o_ret)


def _ffn_kernel(h_ref, hs_ref, g_ref, wg_ref, wu_ref, wd_ref, o_ref, os_ref, f_ref):
    i, j = pl.program_id(0), pl.program_id(1)
    tm = h_ref.shape[0]

    @pl.when(j == 0)
    def _():
        h = h_ref[...]
        f_ref[:tm] = _rms(h, g_ref[...]).astype(BF16)
        o_ref[...] = h

    @pl.when((i == 0) & (j == 0))
    def _():
        hs = hs_ref[...]
        f_ref[tm:] = _rms(hs, g_ref[...]).astype(BF16)
        os_ref[...] = hs

    def swiglu(f):
        gate = _dot(f, wg_ref[...])
        act = (gate * jax.nn.sigmoid(gate) * _dot(f, wu_ref[...])).astype(BF16)
        return _dot(act, wd_ref[...])

    @pl.when(i == 0)
    def _():
        y = swiglu(f_ref[...])
        o_ref[...] += y[:tm]
        os_ref[...] += y[tm:]

    @pl.when(i > 0)
    def _():
        o_ref[...] += swiglu(f_ref[:tm])


def _ffn(h, hs, g, wg, wu, wd, tm):
    m = h.shape[0]
    row = pl.BlockSpec((tm, D_MODEL), lambda i, j: (i, 0))
    return pl.pallas_call(
        _ffn_kernel,
        grid=(m // tm, D_FF // FF_TILE),
        in_specs=[
            row, _whole(hs), _whole(g),
            pl.BlockSpec((None, D_MODEL, FF_TILE), lambda i, j: (j, 0, 0)),
            pl.BlockSpec((None, D_MODEL, FF_TILE), lambda i, j: (j, 0, 0)),
            pl.BlockSpec((FF_TILE, D_MODEL), lambda i, j: (j, 0)),
        ],
        out_specs=[row, _whole(hs)],
        out_shape=[jax.ShapeDtypeStruct((m, D_MODEL), F32), jax.ShapeDtypeStruct(hs.shape, F32)],
        scratch_shapes=[pltpu.VMEM((tm + hs.shape[0], D_MODEL), BF16)],
        compiler_params=_params("arbitrary", "arbitrary"),
        name="ffn",
    )(h, hs, g, wg, wu, wd)


def _ple_kernel(h_ref, p_ref, hs_ref, ps_ref, g_ref, wp_ref, wg_ref, o_ref, os_ref):
    tm = h_ref.shape[0]

    def embed(h, p):
        gate = jax.nn.sigmoid(_dot(_rms(h, g_ref[...]).astype(BF16), wg_ref[...]))
        return h + _dot(p.astype(BF16), wp_ref[...]) * gate

    @pl.when(pl.program_id(0) == 0)
    def _():
        y = embed(jnp.concatenate([h_ref[...], hs_ref[...]], axis=0),
                  jnp.concatenate([p_ref[...], ps_ref[...]], axis=0))
        o_ref[...] = y[:tm]
        os_ref[...] = y[tm:]

    @pl.when(pl.program_id(0) > 0)
    def _():
        o_ref[...] = embed(h_ref[...], p_ref[...])


def _ple(h, p, hs, ps, g, w_ple, w_pg, tm):
    m = h.shape[0]
    row = pl.BlockSpec((tm, D_MODEL), lambda i: (i, 0))
    return pl.pallas_call(
        _ple_kernel,
        grid=(m // tm,),
        in_specs=[row, pl.BlockSpec((tm, PLE_DIM), lambda i: (i, 0)), _whole(hs), _whole(ps), _whole(g),
                  _whole(w_ple), _whole(w_pg)],
        out_specs=[row, _whole(hs)],
        out_shape=[jax.ShapeDtypeStruct((m, D_MODEL), F32), jax.ShapeDtypeStruct(hs.shape, F32)],
        compiler_params=_params("arbitrary"),
        name="ple",
    )(h, p, hs, ps, g, w_ple, w_pg)


PLE_ROWS = 512
FFN_ROWS = 1024


def _dense_tail(h, p, hs, ps, w):
    h, hs = _ffn(h, hs, w["ffn_norm_g"], w["w_gate"], w["w_up"], w["w_down"], FFN_ROWS)
    return _ple(h, p, hs, ps, w["ple_norm_g"], w["w_ple"], w["w_ple_gate"], PLE_ROWS)


def kernel(x_prompt, x_sample, cache_k_win, cache_v_win, state_ret, p_prompt, p_sample,
           attn_norm_g, w_in, q_norm_g, k_norm_g, attn_sinks, ret_out_g, w_out,
           ffn_norm_g, w_gate, w_up, w_down, ple_norm_g, w_ple, w_ple_gate):
    assert w_in.shape[0] == 1 and x_sample.shape[1] == 1 and cache_k_win.shape[2] == WINDOW
    y_p, y_s = x_prompt, x_sample
    outs = [[] for _ in range(6)]
    for l in range(w_in.shape[0]):
        w = {
            "attn_norm_g": attn_norm_g[l].reshape(1, D_MODEL),
            "q_norm_g": q_norm_g[l].reshape(1, HEAD_DIM),
            "k_norm_g": k_norm_g[l].reshape(1, HEAD_DIM),
            "attn_sinks": attn_sinks[l],
            "ret_out_g": ret_out_g[l].reshape(1, RET_WIDTH),
            "ffn_norm_g": ffn_norm_g[l].reshape(1, D_MODEL),
            "ple_norm_g": ple_norm_g[l].reshape(1, D_MODEL),
        }
        batch, seq, _ = y_p.shape
        nb = y_s.shape[0]
        h_p = y_p.reshape(batch * seq, D_MODEL)
        h_s = y_s.reshape(nb, D_MODEL)
        z_p, z_s, *cast = _in_proj(h_p, h_s, w["attn_norm_g"], w_in[l], w_gate[l], w_up[l], w_down[l],
                                   w_out[l], w_ple_gate[l], w_ple[l])
        w.update(zip(("w_gate", "w_up", "w_down", "w_out", "w_ple_gate", "w_ple"), cast))
        h_p, h_s, kp, vp, sp, ks, vs, ss = _mixer_out(
            z_p, h_p, h_s, z_s, cache_k_win[l], cache_v_win[l], state_ret[l], w["w_out"], w["q_norm_g"],
            w["k_norm_g"], w["attn_sinks"], w["ret_out_g"], batch, seq)
        h_p, h_s = _dense_tail(h_p, p_prompt[l].reshape(batch * seq, PLE_DIM),
                               h_s, p_sample[l].reshape(nb, PLE_DIM), w)
        y_p, y_s = h_p.reshape(batch, seq, D_MODEL), h_s.reshape(nb, 1, D_MODEL)
        kp = kp.reshape(batch, BLK, N_KV_HEADS, HEAD_DIM)
        vp = vp.reshape(batch, BLK, N_KV_HEADS, HEAD_DIM)
        for acc, val in zip(outs, (kp, vp, sp, ks, vs, ss)):
            acc.append(val)
    kp, vp, sp, ks, vs, ss = (jnp.stack(o) for o in outs)
    return (y_p, y_s, kp, vp, sp, ks, vs, ss)
```

```python
import functools

import numpy as np
import jax
import jax.numpy as jnp
from jax import lax
from jax.experimental import pallas as pl
from jax.experimental.pallas import tpu as pltpu

D_MODEL = 2048
HEAD_DIM = 128
N_ATTN_HEADS = 8
N_KV_HEADS = 2
GQA_GROUP = N_ATTN_HEADS // N_KV_HEADS
ATTN_WIDTH = N_ATTN_HEADS * HEAD_DIM
KV_WIDTH = N_KV_HEADS * HEAD_DIM
WINDOW = 128
BLK = 128
N_RET_HEADS = 8
RET_DIM = 128
RET_WIDTH = N_RET_HEADS * RET_DIM
D_FF = 5632
IN_WIDTH = 5632
PLE_DIM = 256
PAST_LEN = 16384
ROPE_BASE = 10000.0
NORM_EPS = 1e-6
ATTN_SCALE = HEAD_DIM ** -0.5
RET_K_SCALE = RET_DIM ** -0.5

COL_RQ, COL_RK, COL_RV, COL_RG = 0, RET_WIDTH, 2 * RET_WIDTH, 3 * RET_WIDTH
COL_AQ = 4 * RET_WIDTH
COL_AK = COL_AQ + ATTN_WIDTH
COL_AV = COL_AK + KV_WIDTH
IN_TILE = 512
IN_TILES = IN_WIDTH // IN_TILE
IN_ROT = (ATTN_WIDTH + 2 * KV_WIDTH) // IN_TILE

VMEM_LIMIT = 56 * 1024 * 1024

F32 = jnp.float32
BF16 = jnp.bfloat16


def _params(*sem):
    return pltpu.CompilerParams(dimension_semantics=sem, vmem_limit_bytes=VMEM_LIMIT)


def _rms(x, g):
    return x * lax.rsqrt(jnp.mean(x * x, axis=-1, keepdims=True) + NORM_EPS) * g


def _dot(a, b):
    return jnp.dot(a, b, preferred_element_type=F32)


def _dot_nt(a, b):
    return lax.dot_general(a, b, (((1,), (1,)), ((), ())), preferred_element_type=F32)


def _dot_tn(a, b):
    return lax.dot_general(a, b, (((0,), (0,)), ((), ())), preferred_element_type=F32)


def _whole(a):
    return pl.BlockSpec(a.shape, lambda *_: (0,) * a.ndim)


def _in_proj_kernel(x_hbm, xs_ref, g_ref, w_ref, wg_ref, wu_ref, wd_ref, wo_ref, wpg_ref, wple_ref,
                    z_ref, zs_ref, wg_o, wu_o, wd_o, wo_o, wpg_o, wple_o,
                    xbuf, xsem, a_ref, wres):
    i, j = pl.program_id(0), pl.program_id(1)

    def x_copy(tile):
        return pltpu.make_async_copy(x_hbm.at[pl.ds(tile * IN_ROWS, IN_ROWS)], xbuf, xsem)

    @pl.when((i == 0) & (j == 0))
    def _():
        x_copy(0).start()
        a_ref[IN_ROWS:] = _rms(xs_ref[...], g_ref[...]).astype(BF16)

    @pl.when(j == 0)
    def _():
        x_copy(i).wait()
        a_ref[:IN_ROWS] = _rms(xbuf[...], g_ref[...]).astype(BF16)

        @pl.when(i + 1 < pl.num_programs(0))
        def _():
            x_copy(i + 1).start()

    def cast_weights():
        for src, dst in ((wg_ref, wg_o), (wu_ref, wu_o), (wd_ref, wd_o), (wo_ref, wo_o), (wpg_ref, wpg_o),
                         (wple_ref, wple_o)):
            dst[...] = src[...].astype(BF16)

    @pl.when(i == 0)
    def _():
        wres[j] = w_ref[...].astype(BF16)
        y = _dot(a_ref[...], wres[j])
        z_ref[...] = y[:IN_ROWS].astype(z_ref.dtype)
        zs_ref[...] = y[IN_ROWS:]
        cast_weights()

    @pl.when(i > 0)
    def _():
        z_ref[...] = _dot(a_ref[:IN_ROWS], wres[j]).astype(z_ref.dtype)
        cast_weights()


IN_ROWS = 1024
CAST_SPLIT = 8
FF_TILE = 512


def _in_proj(x, xs, g, w_in, w_gate, w_up, w_down, w_out, w_pg, w_ple):
    m = x.shape[0]
    ni = m // IN_ROWS
    steps = ni * IN_TILES
    assert m % IN_ROWS == 0 and FF_TILE == IN_TILE and D_FF == IN_WIDTH
    gu_rows, d_rows = D_MODEL // ni, D_FF // steps
    sq_rows, ple_rows = D_MODEL // (ni * CAST_SPLIT), PLE_DIM // ni
    assert gu_rows * ni == D_MODEL and d_rows * steps == D_FF and sq_rows % 16 == 0 and ple_rows % 16 == 0
    sq_idx = lambda i, j: (i * CAST_SPLIT + jnp.minimum(j, CAST_SPLIT - 1), 0)
    first = lambda i, j: jnp.where(i == 0, j, IN_TILES - 1)
    return pl.pallas_call(
        _in_proj_kernel,
        grid=(ni, IN_TILES),
        in_specs=[
            pl.BlockSpec(memory_space=pl.ANY),
            pl.BlockSpec(xs.shape, lambda i, j: (0, 0)),
            pl.BlockSpec((1, D_MODEL), lambda i, j: (0, 0)),
            pl.BlockSpec((D_MODEL, IN_TILE), lambda i, j: (0, (first(i, j) + IN_ROT) % IN_TILES)),
            pl.BlockSpec((gu_rows, FF_TILE), lambda i, j: (i, j)),
            pl.BlockSpec((gu_rows, FF_TILE), lambda i, j: (i, j)),
            pl.BlockSpec((d_rows, D_MODEL), lambda i, j: (i * IN_TILES + j, 0)),
            pl.BlockSpec((sq_rows, D_MODEL), sq_idx),
            pl.BlockSpec((sq_rows, D_MODEL), sq_idx),
            pl.BlockSpec((ple_rows, D_MODEL), lambda i, j: (i, 0)),
        ],
        out_specs=[
            pl.BlockSpec((IN_ROWS, IN_TILE), lambda i, j: (i, j)),
            pl.BlockSpec((xs.shape[0], IN_TILE), lambda i, j: (0, first(i, j))),
            pl.BlockSpec((None, gu_rows, FF_TILE), lambda i, j: (j, i, 0)),
            pl.BlockSpec((None, gu_rows, FF_TILE), lambda i, j: (j, i, 0)),
            pl.BlockSpec((d_rows, D_MODEL), lambda i, j: (i * IN_TILES + j, 0)),
            pl.BlockSpec((sq_rows, D_MODEL), sq_idx),
            pl.BlockSpec((sq_rows, D_MODEL), sq_idx),
            pl.BlockSpec((ple_rows, D_MODEL), lambda i, j: (i, 0)),
        ],
        out_shape=[
            jax.ShapeDtypeStruct((m, IN_WIDTH), BF16),
            jax.ShapeDtypeStruct((xs.shape[0], IN_WIDTH), F32),
            jax.ShapeDtypeStruct((D_FF // FF_TILE, D_MODEL, FF_TILE), BF16),
            jax.ShapeDtypeStruct((D_FF // FF_TILE, D_MODEL, FF_TILE), BF16),
            jax.ShapeDtypeStruct((D_FF, D_MODEL), BF16),
            jax.ShapeDtypeStruct((D_MODEL, D_MODEL), BF16),
            jax.ShapeDtypeStruct((D_MODEL, D_MODEL), BF16),
            jax.ShapeDtypeStruct((PLE_DIM, D_MODEL), BF16),
        ],
        scratch_shapes=[
            pltpu.VMEM((IN_ROWS, D_MODEL), F32),
            pltpu.SemaphoreType.DMA(()),
            pltpu.VMEM((IN_ROWS + xs.shape[0], D_MODEL), BF16),
            pltpu.VMEM((IN_TILES, D_MODEL, IN_TILE), BF16),
        ],
        compiler_params=_params("arbitrary", "arbitrary"),
        name="in_proj",
    )(x, xs, g, w_in, w_gate, w_up, w_down, w_out, w_pg, w_ple)


MIX_ROWS = 256
RET_CHUNK = 256
LOG2E = 1.4426950408889634


def _rotary(x, cos2, sin2):
    return x * cos2 + pltpu.roll(x, RET_DIM // 2, 1) * sin2


def _mix_tile(z_ref, kp_ref, vp_ref, qg_ref, kg_ref, sink_ref, cos_ref, sin_ref, qdec_ref, kdec_ref,
              cdec_ref, rg_ref, mix_ref, kwin_ref, vwin_ref, sout_ref, s_ref, first):
    qg = qg_ref[...] * (ATTN_SCALE * LOG2E)
    kg = kg_ref[...]
    row = lax.broadcasted_iota(jnp.int32, (BLK, 2 * BLK), 0)
    col = lax.broadcasted_iota(jnp.int32, (BLK, 2 * BLK), 1)
    cur_ok = (col >= BLK) & ((col - BLK) <= row)
    prev_ok = (col < BLK) & (col >= row)
    kp = kp_ref[...].astype(F32)
    kprev = jnp.concatenate([_rms(kp[:, h * HEAD_DIM:(h + 1) * HEAD_DIM], kg) for h in range(N_KV_HEADS)],
                            axis=1).astype(BF16)
    vprev = vp_ref[...]
    for blk in range(MIX_ROWS // BLK):
        rows = slice(blk * BLK, (blk + 1) * BLK)
        mask = cur_ok | (prev_ok & jnp.logical_not(first)) if blk == 0 else cur_ok | prev_ok
        k = z_ref[rows, COL_AK:COL_AK + KV_WIDTH].astype(F32)
        kn = [_rms(k[:, h * HEAD_DIM:(h + 1) * HEAD_DIM], kg) for h in range(N_KV_HEADS)]
        kcur = jnp.concatenate(kn, axis=1).astype(BF16)
        vcur = z_ref[rows, COL_AV:COL_AV + KV_WIDTH]
        for kh in range(N_KV_HEADS):
            sl = slice(kh * HEAD_DIM, (kh + 1) * HEAD_DIM)
            k_cat = jnp.concatenate([kprev[:, sl], kcur[:, sl]], axis=0)
            v_cat = jnp.concatenate([vprev[:, sl], vcur[:, sl]], axis=0)
            for g in range(GQA_GROUP):
                h = kh * GQA_GROUP + g
                q = z_ref[rows, COL_AQ + h * HEAD_DIM:COL_AQ + (h + 1) * HEAD_DIM].astype(F32)
                s = jnp.where(mask, _dot_nt(_rms(q, qg).astype(BF16), k_cat), -jnp.inf)
                sink = sink_ref[h] * LOG2E
                m = jnp.maximum(jnp.max(s, axis=-1, keepdims=True), sink)
                e = jnp.exp2(s - m)
                denom = jnp.sum(e, axis=-1, keepdims=True) + jnp.exp2(sink - m)
                mix_ref[rows, h * HEAD_DIM:(h + 1) * HEAD_DIM] = (
                    _dot(e.astype(BF16), v_cat) / denom).astype(mix_ref.dtype)
        kprev, vprev = kcur, vcur
    kwin_ref[0] = jnp.concatenate(kn, axis=1)
    vwin_ref[0] = vcur.astype(F32)

    tril = (lax.broadcasted_iota(jnp.int32, (RET_CHUNK, RET_CHUNK), 0)
            >= lax.broadcasted_iota(jnp.int32, (RET_CHUNK, RET_CHUNK), 1))
    for h in range(N_RET_HEADS):
        hs = slice(h * RET_DIM, (h + 1) * RET_DIM)
        col_of = lambda c: slice(c + h * RET_DIM, c + (h + 1) * RET_DIM)
        state = s_ref[h]
        for c in range(MIX_ROWS // RET_CHUNK):
            rows = slice(c * RET_CHUNK, (c + 1) * RET_CHUNK)
            cos2, sin2 = cos_ref[rows, :], sin_ref[rows, :]
            q = (_rotary(z_ref[rows, col_of(COL_RQ)].astype(F32), cos2, sin2) * qdec_ref[:, hs]).astype(BF16)
            k = (_rotary(z_ref[rows, col_of(COL_RK)].astype(F32), cos2, sin2) * kdec_ref[:, hs]).astype(BF16)
            v = z_ref[rows, col_of(COL_RV)]
            sc = jnp.where(tril, _dot_nt(q, k), 0.0)
            o = _dot(sc.astype(BF16), v) + _dot(q, state.astype(BF16))
            state = (state + _dot_tn(k, v)) * cdec_ref[h]
            o = o * lax.rsqrt(jnp.mean(o * o, axis=-1, keepdims=True) + NORM_EPS)
            gate = z_ref[rows, col_of(COL_RG)].astype(F32)
            mix_ref[rows, ATTN_WIDTH + h * RET_DIM:ATTN_WIDTH + (h + 1) * RET_DIM] = (
                o * rg_ref[:, hs] * (gate * jax.nn.sigmoid(gate))).astype(mix_ref.dtype)
        s_ref[h] = state
        sout_ref[0, h] = state


def _mixer_out_kernel(nsteps, tiles, groups, z_ref, kp_ref, vp_ref, qg_ref, kg_ref, sink_ref, cos_ref, sin_ref,
                      qdec_ref, kdec_ref, cdec_ref, rg_ref, h_ref, hs_ref, w_ref,
                      zs_ref, kbuf_ref, vbuf_ref, s0_ref, sinkc_ref, coss_ref, sins_ref, spread_ref, gamma_ref,
                      o_ref, os_ref, kwin_ref, vwin_ref, sout_ref, kout_ref, vout_ref, ssout_ref,
                      mix_a, mix_b, s_ref, mixs_scr):
    t = pl.program_id(0)
    per_group = tiles // groups

    @pl.when(t % nsteps == 0)
    def _():
        s_ref[...] = jnp.zeros_like(s_ref)

    def mix(dst):
        _mix_tile(z_ref, kp_ref, vp_ref, qg_ref, kg_ref, sink_ref, cos_ref, sin_ref, qdec_ref, kdec_ref,
                  cdec_ref, rg_ref, dst, kwin_ref, vwin_ref, sout_ref, s_ref, t % nsteps == 0)

    def project(src):
        o_ref[...] = h_ref[...] + _dot(src[...], w_ref[...])

    @pl.when(t == 0)
    def _():
        mix(mix_a)

    @pl.when((t > 0) & (t < tiles) & (t % 2 == 0))
    def _():
        project(mix_b)
        mix(mix_a)

    @pl.when((t < tiles) & (t % 2 == 1))
    def _():
        project(mix_a)
        mix(mix_b)

    @pl.when((t % per_group == per_group - 1) & (t < tiles))
    def _():
        g = t // per_group

        def put_mix(b, attn, ret):
            heads = [x[r:r + 1] for x in (attn, ret) for r in range(x.shape[0])]
            mixs_scr[g, b:b + 1, :] = jnp.concatenate(heads, axis=1)

        _sample_group(zs_ref, kbuf_ref, vbuf_ref, s0_ref, qg_ref, kg_ref, sinkc_ref, coss_ref, sins_ref,
                      spread_ref, gamma_ref, rg_ref, put_mix, kout_ref, vout_ref, ssout_ref)

    @pl.when(t == tiles)
    def _():
        project(mix_b if tiles % 2 == 0 else mix_a)
        mixs = jnp.concatenate([mixs_scr[g, :SAMPLE_GROUP, :] for g in range(groups)], axis=0)
        os_ref[...] = hs_ref[...] + _dot(mixs.astype(BF16), w_ref[...])


def _rope_tables(pos):
    half = RET_DIM // 2
    inv = ROPE_BASE ** (-np.arange(half, dtype=np.float64) / half)
    ang = np.asarray(pos, np.float64)[:, None] * inv[None, :]
    cos, sin = np.cos(ang), np.sin(ang)
    return (np.concatenate([cos, cos], axis=-1).astype(np.float32),
            np.concatenate([-sin, sin], axis=-1).astype(np.float32))


def _ret_log_decay():
    return np.log1p(-np.exp2(-5.0 - np.arange(N_RET_HEADS, dtype=np.float64)))


def _mixer_out(z, h, hs, zs, k_buf, v_buf, s0, w_out, qg, kg, sinks, rg, batch, seq):
    nsteps = seq // MIX_ROWS
    tiles = batch * nsteps
    nb = hs.shape[0]
    groups = nb // SAMPLE_GROUP
    assert nb % SAMPLE_GROUP == 0 and tiles % groups == 0 and SAMPLE_GROUP <= SUBLANES
    cache_rows = WINDOW * N_KV_HEADS
    cos2, sin2 = _rope_tables(np.arange(seq))
    cos_s, sin_s = _rope_tables(PAST_LEN + np.arange(1))
    lg = _ret_log_decay()
    steps = np.arange(RET_CHUNK, dtype=np.float64)[:, None, None] + 1.0
    lanes = lambda t: np.broadcast_to(t, (RET_CHUNK, N_RET_HEADS, RET_DIM)).reshape(RET_CHUNK, RET_WIDTH).astype(np.float32)
    q_dec = lanes(np.exp(lg[None, :, None] * steps))
    k_dec = lanes(np.exp(-lg[None, :, None] * steps) * RET_K_SCALE)
    chunk_dec = np.exp(lg * RET_CHUNK).astype(np.float32)
    gamma = np.exp(lg).astype(np.float32)
    spread = np.kron(np.eye(N_RET_HEADS), np.ones((1, RET_DIM))).astype(np.float32)
    sink_col = sinks.reshape(N_ATTN_HEADS, 1)
    smem = pl.BlockSpec(memory_space=pltpu.SMEM)
    win = jax.ShapeDtypeStruct((batch, BLK, KV_WIDTH), F32)
    blks = MIX_ROWS // BLK
    tile = lambda t: jnp.minimum(t, tiles - 1)
    prev_blk = lambda t: tile(t) * blks - jnp.minimum(tile(t) % nsteps, 1)
    behind = lambda t: (jnp.maximum(t - 1, 0), 0)
    seq_of = lambda t: tile(t) // nsteps
    group_of = lambda t: tile(t) // (tiles // groups)
    sample = lambda *shape: pl.BlockSpec((SAMPLE_GROUP,) + shape, lambda t: (group_of(t),) + (0,) * len(shape))
    cache, state = sample(cache_rows, HEAD_DIM), sample(N_RET_HEADS, RET_DIM, RET_DIM)
    cache_shape = jax.ShapeDtypeStruct((nb, cache_rows, HEAD_DIM), F32)
    y, ys, k_win, v_win, s_new, ks, vs, ss = pl.pallas_call(
        functools.partial(_mixer_out_kernel, nsteps, tiles, groups),
        grid=(tiles + 1,),
        in_specs=[
            pl.BlockSpec((MIX_ROWS, IN_WIDTH), lambda t: (tile(t), 0)),
            pl.BlockSpec((BLK, KV_WIDTH), lambda t: (prev_blk(t), COL_AK // KV_WIDTH)),
            pl.BlockSpec((BLK, KV_WIDTH), lambda t: (prev_blk(t), COL_AV // KV_WIDTH)),
            _whole(qg), _whole(kg), smem,
            pl.BlockSpec((MIX_ROWS, RET_DIM), lambda t: (tile(t) % nsteps, 0)),
            pl.BlockSpec((MIX_ROWS, RET_DIM), lambda t: (tile(t) % nsteps, 0)),
            _whole(q_dec), _whole(k_dec), smem, _whole(rg),
            pl.BlockSpec((MIX_ROWS, D_MODEL), behind), _whole(hs), _whole(w_out),
            sample(Z_HEADS, HEAD_DIM), cache, cache, state,
            _whole(sink_col), _whole(cos_s), _whole(sin_s), _whole(spread), smem,
        ],
        out_specs=[
            pl.BlockSpec((MIX_ROWS, D_MODEL), behind), _whole(hs),
            pl.BlockSpec((1, BLK, KV_WIDTH), lambda t: (seq_of(t), 0, 0)),
            pl.BlockSpec((1, BLK, KV_WIDTH), lambda t: (seq_of(t), 0, 0)),
            pl.BlockSpec((1, N_RET_HEADS, RET_DIM, RET_DIM), lambda t: (seq_of(t), 0, 0, 0)),
            cache, cache, state,
        ],
        out_shape=[jax.ShapeDtypeStruct(h.shape, F32), jax.ShapeDtypeStruct(hs.shape, F32), win, win,
                   jax.ShapeDtypeStruct((batch, N_RET_HEADS, RET_DIM, RET_DIM), F32),
                   cache_shape, cache_shape, jax.ShapeDtypeStruct(s0.shape, F32)],
        scratch_shapes=[pltpu.VMEM((MIX_ROWS, D_MODEL), BF16), pltpu.VMEM((MIX_ROWS, D_MODEL), BF16),
                        pltpu.VMEM((N_RET_HEADS, RET_DIM, RET_DIM), F32),
                        pltpu.VMEM((groups, SUBLANES, D_MODEL), F32)],
        compiler_params=_params("arbitrary"),
        name="mixer_out",
    )(z, z, z, qg, kg, sinks, cos2, sin2, q_dec, k_dec, chunk_dec, rg, h, hs, w_out,
      zs.reshape(nb, Z_HEADS, HEAD_DIM), k_buf.reshape(nb, cache_rows, HEAD_DIM),
      v_buf.reshape(nb, cache_rows, HEAD_DIM), s0, sink_col, cos_s, sin_s, spread, gamma)
    window = (nb, WINDOW, N_KV_HEADS, HEAD_DIM)
    return y, ys, k_win, v_win, s_new, ks.reshape(window), vs.reshape(window), ss


SAMPLE_GROUP = 4
Z_HEADS = IN_WIDTH // HEAD_DIM
ROW_RQ, ROW_RK, ROW_RV, ROW_RG, ROW_AQ, ROW_AK, ROW_AV = (
    c // HEAD_DIM for c in (COL_RQ, COL_RK, COL_RV, COL_RG, COL_AQ, COL_AK, COL_AV))
SUBLANES = 8


def _sample_group(z_ref, kbuf_ref, vbuf_ref, s0_ref, qg_ref, kg_ref, sink_ref, cos_ref, sin_ref,
                  spread_ref, gamma_ref, rg_ref, put_mix, kout_ref, vout_ref, s_out_ref):
    rows = WINDOW * N_KV_HEADS
    qg = qg_ref[...]
    kg = kg_ref[...]
    sink = sink_ref[...]
    head = lax.broadcasted_iota(jnp.int32, (N_ATTN_HEADS, HEAD_DIM), 0)
    first = head < GQA_GROUP
    own = ((lax.broadcasted_iota(jnp.int32, (N_ATTN_HEADS, rows), 1) % N_KV_HEADS)
           == (lax.broadcasted_iota(jnp.int32, (N_ATTN_HEADS, rows), 0) // GQA_GROUP))
    row = lax.broadcasted_iota(jnp.int32, (rows, HEAD_DIM), 0)
    cos2 = cos_ref[...]
    sin2 = sin_ref[...]
    spread = spread_ref[...]
    for b in range(SAMPLE_GROUP):
        part = lambda r, n: z_ref[b, r:r + n, :]
        qn = _rms(part(ROW_AQ, N_ATTN_HEADS), qg) * ATTN_SCALE
        kn = _rms(part(ROW_AK, N_KV_HEADS), kg)
        vn = part(ROW_AV, N_KV_HEADS)
        kbuf = kbuf_ref[b]
        vbuf = vbuf_ref[b]
        s = jnp.where(own, _dot_nt(qn.astype(BF16), kbuf.astype(BF16)), -jnp.inf)
        k_sel = jnp.where(first, kn[0:1], kn[1:2])
        v_sel = jnp.where(first, vn[0:1], vn[1:2])
        s_new = jnp.sum(qn * k_sel, axis=-1, keepdims=True)
        m = jnp.maximum(jnp.maximum(jnp.max(s, axis=-1, keepdims=True), s_new), sink)
        e = jnp.exp(s - m)
        e_new = jnp.exp(s_new - m)
        denom = jnp.sum(e, axis=-1, keepdims=True) + e_new + jnp.exp(sink - m)
        o_attn = (_dot(e.astype(BF16), vbuf.astype(BF16)) + e_new * v_sel) / denom
        for buf, new, out in ((kbuf, kn, kout_ref), (vbuf, vn, vout_ref)):
            shifted = pltpu.roll(buf, rows - N_KV_HEADS, 0)
            out[b] = jnp.where(row == rows - 2, new[0:1], jnp.where(row == rows - 1, new[1:2], shifted))
        q_cols = _dot_tn(_rotary(part(ROW_RQ, N_RET_HEADS), cos2, sin2), spread)
        k_cols = _dot_tn(_rotary(part(ROW_RK, N_RET_HEADS), cos2, sin2) * RET_K_SCALE, spread)
        v = part(ROW_RV, N_RET_HEADS)
        gate = part(ROW_RG, N_RET_HEADS)
        heads = []
        for h in range(N_RET_HEADS):
            hs = slice(h * RET_DIM, (h + 1) * RET_DIM)
            s_next = s0_ref[b, h] * gamma_ref[h] + k_cols[:, hs] * v[h:h + 1, :]
            s_out_ref[b, h] = s_next
            o = jnp.sum(q_cols[:, hs] * s_next, axis=0, keepdims=True)
            heads.append(o * lax.rsqrt(jnp.mean(o * o, axis=-1, keepdims=True) + NORM_EPS) * rg_ref[:, hs])
        o_ret = jnp.concatenate(heads, axis=0) * (gate * jax.nn.sigmoid(gate))
        put_mix(b, o_attn, o_ret)


def _ffn_kernel(h_ref, hs_ref, g_ref, wg_ref, wu_ref, wd_ref, o_ref, os_ref, f_ref):
    i, j = pl.program_id(0), pl.program_id(1)
    tm = h_ref.shape[0]

    @pl.when(j == 0)
    def _():
        h = h_ref[...]
        f_ref[:tm] = _rms(h, g_ref[...]).astype(BF16)
        o_ref[...] = h

    @pl.when((i == 0) & (j == 0))
    def _():
        hs = hs_ref[...]
        f_ref[tm:] = _rms(hs, g_ref[...]).astype(BF16)
        os_ref[...] = hs

    def swiglu(f):
        gate = _dot(f, wg_ref[...])
        act = (gate * jax.nn.sigmoid(gate) * _dot(f, wu_ref[...])).astype(BF16)
        return _dot(act, wd_ref[...])

    @pl.when(i == 0)
    def _():
        y = swiglu(f_ref[...])
        o_ref[...] += y[:tm]
        os_ref[...] += y[tm:]

    @pl.when(i > 0)
    def _():
        o_ref[...] += swiglu(f_ref[:tm])


def _ffn(h, hs, g, wg, wu, wd, tm):
    m = h.shape[0]
    row = pl.BlockSpec((tm, D_MODEL), lambda i, j: (i, 0))
    return pl.pallas_call(
        _ffn_kernel,
        grid=(m // tm, D_FF // FF_TILE),
        in_specs=[
            row, _whole(hs), _whole(g),
            pl.BlockSpec((None, D_MODEL, FF_TILE), lambda i, j: (j, 0, 0)),
            pl.BlockSpec((None, D_MODEL, FF_TILE), lambda i, j: (j, 0, 0)),
            pl.BlockSpec((FF_TILE, D_MODEL), lambda i, j: (j, 0)),
        ],
        out_specs=[row, _whole(hs)],
        out_shape=[jax.ShapeDtypeStruct((m, D_MODEL), F32), jax.ShapeDtypeStruct(hs.shape, F32)],
        scratch_shapes=[pltpu.VMEM((tm + hs.shape[0], D_MODEL), BF16)],
        compiler_params=_params("arbitrary", "arbitrary"),
        name="ffn",
    )(h, hs, g, wg, wu, wd)


def _ple_kernel(h_ref, p_ref, hs_ref, ps_ref, g_ref, wp_ref, wg_ref, o_ref, os_ref):
    tm = h_ref.shape[0]

    def embed(h, p):
        gate = jax.nn.sigmoid(_dot(_rms(h, g_ref[...]).astype(BF16), wg_ref[...]))
        return h + _dot(p.astype(BF16), wp_ref[...]) * gate

    @pl.when(pl.program_id(0) == 0)
    def _():
        y = embed(jnp.concatenate([h_ref[...], hs_ref[...]], axis=0),
                  jnp.concatenate([p_ref[...], ps_ref[...]], axis=0))
        o_ref[...] = y[:tm]
        os_ref[...] = y[tm:]

    @pl.when(pl.program_id(0) > 0)
    def _():
        o_ref[...] = embed(h_ref[...], p_ref[...])


def _ple(h, p, hs, ps, g, w_ple, w_pg, tm):
    m = h.shape[0]
    row = pl.BlockSpec((tm, D_MODEL), lambda i: (i, 0))
    return pl.pallas_call(
        _ple_kernel,
        grid=(m // tm,),
        in_specs=[row, pl.BlockSpec((tm, PLE_DIM), lambda i: (i, 0)), _whole(hs), _whole(ps), _whole(g),
                  _whole(w_ple), _whole(w_pg)],
        out_specs=[row, _whole(hs)],
        out_shape=[jax.ShapeDtypeStruct((m, D_MODEL), F32), jax.ShapeDtypeStruct(hs.shape, F32)],
        compiler_params=_params("arbitrary"),
        name="ple",
    )(h, p, hs, ps, g, w_ple, w_pg)


PLE_ROWS = 1024
FFN_ROWS = 1024


def _dense_tail(h, p, hs, ps, w):
    h, hs = _ffn(h, hs, w["ffn_norm_g"], w["w_gate"], w["w_up"], w["w_down"], FFN_ROWS)
    return _ple(h, p, hs, ps, w["ple_norm_g"], w["w_ple"], w["w_ple_gate"], PLE_ROWS)


def kernel(x_prompt, x_sample, cache_k_win, cache_v_win, state_ret, p_prompt, p_sample,
           attn_norm_g, w_in, q_norm_g, k_norm_g, attn_sinks, ret_out_g, w_out,
           ffn_norm_g, w_gate, w_up, w_down, ple_norm_g, w_ple, w_ple_gate):
    assert w_in.shape[0] == 1 and x_sample.shape[1] == 1 and cache_k_win.shape[2] == WINDOW
    y_p, y_s = x_prompt, x_sample
    outs = [[] for _ in range(6)]
    for l in range(w_in.shape[0]):
        w = {
            "attn_norm_g": attn_norm_g[l].reshape(1, D_MODEL),
            "q_norm_g": q_norm_g[l].reshape(1, HEAD_DIM),
            "k_norm_g": k_norm_g[l].reshape(1, HEAD_DIM),
            "attn_sinks": attn_sinks[l],
            "ret_out_g": ret_out_g[l].reshape(1, RET_WIDTH),
            "ffn_norm_g": ffn_norm_g[l].reshape(1, D_MODEL),
            "ple_norm_g": ple_norm_g[l].reshape(1, D_MODEL),
        }
        batch, seq, _ = y_p.shape
        nb = y_s.shape[0]
        h_p = y_p.reshape(batch * seq, D_MODEL)
        h_s = y_s.reshape(nb, D_MODEL)
        z_p, z_s, *cast = _in_proj(h_p, h_s, w["attn_norm_g"], w_in[l], w_gate[l], w_up[l], w_down[l],
                                   w_out[l], w_ple_gate[l], w_ple[l])
        w.update(zip(("w_gate", "w_up", "w_down", "w_out", "w_ple_gate", "w_ple"), cast))
        h_p, h_s, kp, vp, sp, ks, vs, ss = _mixer_out(
            z_p, h_p, h_s, z_s, cache_k_win[l], cache_v_win[l], state_ret[l], w["w_out"], w["q_norm_g"],
            w["k_norm_g"], w["attn_sinks"], w["ret_out_g"], batch, seq)
        h_p, h_s = _dense_tail(h_p, p_prompt[l].reshape(batch * seq, PLE_DIM),
                               h_s, p_sample[l].reshape(nb, PLE_DIM), w)
        y_p, y_s = h_p.reshape(batch, seq, D_MODEL), h_s.reshape(nb, 1, D_MODEL)
        kp = kp.reshape(batch, BLK, N_KV_HEADS, HEAD_DIM)
        vp = vp.reshape(batch, BLK, N_KV_HEADS, HEAD_DIM)
        for acc, val in zip(outs, (kp, vp, sp, ks, vs, ss)):
            acc.append(val)
    kp, vp, sp, ks, vs, ss = (jnp.stack(o) for o in outs)
    return (y_p, y_s, kp, vp, sp, ks, vs, ss)
```

```python
import functools

import numpy as np
import jax
import jax.numpy as jnp
from jax import lax
from jax.experimental import pallas as pl
from jax.experimental.pallas import tpu as pltpu

D_MODEL = 2048
HEAD_DIM = 128
N_ATTN_HEADS = 8
N_KV_HEADS = 2
GQA_GROUP = N_ATTN_HEADS // N_KV_HEADS
ATTN_WIDTH = N_ATTN_HEADS * HEAD_DIM
KV_WIDTH = N_KV_HEADS * HEAD_DIM
WINDOW = 128
BLK = 128
N_RET_HEADS = 8
RET_DIM = 128
RET_WIDTH = N_RET_HEADS * RET_DIM
D_FF = 5632
IN_WIDTH = 5632
PLE_DIM = 256
PAST_LEN = 16384
ROPE_BASE = 10000.0
NORM_EPS = 1e-6
ATTN_SCALE = HEAD_DIM ** -0.5
RET_K_SCALE = RET_DIM ** -0.5

COL_RQ, COL_RK, COL_RV, COL_RG = 0, RET_WIDTH, 2 * RET_WIDTH, 3 * RET_WIDTH
COL_AQ = 4 * RET_WIDTH
COL_AK = COL_AQ + ATTN_WIDTH
COL_AV = COL_AK + KV_WIDTH
IN_TILE = 512
IN_TILES = IN_WIDTH // IN_TILE
IN_ROT = (ATTN_WIDTH + 2 * KV_WIDTH) // IN_TILE

VMEM_LIMIT = 56 * 1024 * 1024

F32 = jnp.float32
BF16 = jnp.bfloat16


def _params(*sem):
    return pltpu.CompilerParams(dimension_semantics=sem, vmem_limit_bytes=VMEM_LIMIT)


def _rms(x, g):
    return x * lax.rsqrt(jnp.mean(x * x, axis=-1, keepdims=True) + NORM_EPS) * g


def _dot(a, b):
    return jnp.dot(a, b, preferred_element_type=F32)


def _dot_nt(a, b):
    return lax.dot_general(a, b, (((1,), (1,)), ((), ())), preferred_element_type=F32)


def _dot_tn(a, b):
    return lax.dot_general(a, b, (((0,), (0,)), ((), ())), preferred_element_type=F32)


def _whole(a):
    return pl.BlockSpec(a.shape, lambda *_: (0,) * a.ndim)


def _in_proj_kernel(x_hbm, xs_ref, g_ref, w_ref, wg_ref, wu_ref, wd_ref, wo_ref, wpg_ref, wple_ref,
                    z_ref, zs_ref, wg_o, wu_o, wd_o, wo_o, wpg_o, wple_o,
                    xbuf, xsem, a_ref, wres):
    i, j = pl.program_id(0), pl.program_id(1)

    def x_copy(tile):
        return pltpu.make_async_copy(x_hbm.at[pl.ds(tile * IN_ROWS, IN_ROWS)], xbuf, xsem)

    @pl.when((i == 0) & (j == 0))
    def _():
        x_copy(0).start()
        a_ref[IN_ROWS:] = _rms(xs_ref[...], g_ref[...]).astype(BF16)

    @pl.when(j == 0)
    def _():
        x_copy(i).wait()
        a_ref[:IN_ROWS] = _rms(xbuf[...], g_ref[...]).astype(BF16)

        @pl.when(i + 1 < pl.num_programs(0))
        def _():
            x_copy(i + 1).start()

    def cast_weights():
        for src, dst in ((wg_ref, wg_o), (wu_ref, wu_o), (wd_ref, wd_o), (wo_ref, wo_o), (wpg_ref, wpg_o),
                         (wple_ref, wple_o)):
            dst[...] = src[...].astype(BF16)

    @pl.when(i == 0)
    def _():
        wres[j] = w_ref[...].astype(BF16)
        y = _dot(a_ref[...], wres[j])
        z_ref[...] = y[:IN_ROWS].astype(z_ref.dtype)
        zs_ref[...] = y[IN_ROWS:]
        cast_weights()

    @pl.when(i > 0)
    def _():
        z_ref[...] = _dot(a_ref[:IN_ROWS], wres[j]).astype(z_ref.dtype)
        cast_weights()


IN_ROWS = 1024
CAST_SPLIT = 8
FF_TILE = 512


def _in_proj(x, xs, g, w_in, w_gate, w_up, w_down, w_out, w_pg, w_ple):
    m = x.shape[0]
    ni = m // IN_ROWS
    steps = ni * IN_TILES
    assert m % IN_ROWS == 0 and FF_TILE == IN_TILE and D_FF == IN_WIDTH
    gu_rows, d_rows = D_MODEL // ni, D_FF // steps
    sq_rows, ple_rows = D_MODEL // (ni * CAST_SPLIT), PLE_DIM // ni
    assert gu_rows * ni == D_MODEL and d_rows * steps == D_FF and sq_rows % 16 == 0 and ple_rows % 16 == 0
    sq_idx = lambda i, j: (i * CAST_SPLIT + jnp.minimum(j, CAST_SPLIT - 1), 0)
    first = lambda i, j: jnp.where(i == 0, j, IN_TILES - 1)
    return pl.pallas_call(
        _in_proj_kernel,
        grid=(ni, IN_TILES),
        in_specs=[
            pl.BlockSpec(memory_space=pl.ANY),
            pl.BlockSpec(xs.shape, lambda i, j: (0, 0)),
            pl.BlockSpec((1, D_MODEL), lambda i, j: (0, 0)),
            pl.BlockSpec((D_MODEL, IN_TILE), lambda i, j: (0, (first(i, j) + IN_ROT) % IN_TILES)),
            pl.BlockSpec((gu_rows, FF_TILE), lambda i, j: (i, j)),
            pl.BlockSpec((gu_rows, FF_TILE), lambda i, j: (i, j)),
            pl.BlockSpec((d_rows, D_MODEL), lambda i, j: (i * IN_TILES + j, 0)),
            pl.BlockSpec((sq_rows, D_MODEL), sq_idx),
            pl.BlockSpec((sq_rows, D_MODEL), sq_idx),
            pl.BlockSpec((ple_rows, D_MODEL), lambda i, j: (i, 0)),
        ],
        out_specs=[
            pl.BlockSpec((IN_ROWS, IN_TILE), lambda i, j: (i, j)),
            pl.BlockSpec((xs.shape[0], IN_TILE), lambda i, j: (0, first(i, j))),
            pl.BlockSpec((None, gu_rows, FF_TILE), lambda i, j: (j, i, 0)),
            pl.BlockSpec((None, gu_rows, FF_TILE), lambda i, j: (j, i, 0)),
            pl.BlockSpec((d_rows, D_MODEL), lambda i, j: (i * IN_TILES + j, 0)),
            pl.BlockSpec((sq_rows, D_MODEL), sq_idx),
            pl.BlockSpec((sq_rows, D_MODEL), sq_idx),
            pl.BlockSpec((ple_rows, D_MODEL), lambda i, j: (i, 0)),
        ],
        out_shape=[
            jax.ShapeDtypeStruct((m, IN_WIDTH), BF16),
            jax.ShapeDtypeStruct((xs.shape[0], IN_WIDTH), F32),
            jax.ShapeDtypeStruct((D_FF // FF_TILE, D_MODEL, FF_TILE), BF16),
            jax.ShapeDtypeStruct((D_FF // FF_TILE, D_MODEL, FF_TILE), BF16),
            jax.ShapeDtypeStruct((D_FF, D_MODEL), BF16),
            jax.ShapeDtypeStruct((D_MODEL, D_MODEL), BF16),
            jax.ShapeDtypeStruct((D_MODEL, D_MODEL), BF16),
            jax.ShapeDtypeStruct((PLE_DIM, D_MODEL), BF16),
        ],
        scratch_shapes=[
            pltpu.VMEM((IN_ROWS, D_MODEL), F32),
            pltpu.SemaphoreType.DMA(()),
            pltpu.VMEM((IN_ROWS + xs.shape[0], D_MODEL), BF16),
            pltpu.VMEM((IN_TILES, D_MODEL, IN_TILE), BF16),
        ],
        compiler_params=_params("arbitrary", "arbitrary"),
        name="in_proj",
    )(x, xs, g, w_in, w_gate, w_up, w_down, w_out, w_pg, w_ple)


MIX_ROWS = 256
RET_CHUNK = 256
LOG2E = 1.4426950408889634


def _rotary(x, cos2, sin2):
    return x * cos2 + pltpu.roll(x, RET_DIM // 2, 1) * sin2


def _mix_tile(z_ref, kp_ref, vp_ref, qg_ref, kg_ref, sink_ref, cos_ref, sin_ref, qdec_ref, kdec_ref,
              cdec_ref, rg_ref, mix_ref, kwin_ref, vwin_ref, sout_ref, s_ref, first):
    qg = qg_ref[...] * (ATTN_SCALE * LOG2E)
    kg = kg_ref[...]
    row = lax.broadcasted_iota(jnp.int32, (BLK, 2 * BLK), 0)
    col = lax.broadcasted_iota(jnp.int32, (BLK, 2 * BLK), 1)
    cur_ok = (col >= BLK) & ((col - BLK) <= row)
    prev_ok = (col < BLK) & (col >= row)
    kp = kp_ref[...].astype(F32)
    kprev = jnp.concatenate([_rms(kp[:, h * HEAD_DIM:(h + 1) * HEAD_DIM], kg) for h in range(N_KV_HEADS)],
                            axis=1).astype(BF16)
    vprev = vp_ref[...]
    for blk in range(MIX_ROWS // BLK):
        rows = slice(blk * BLK, (blk + 1) * BLK)
        mask = cur_ok | (prev_ok & jnp.logical_not(first)) if blk == 0 else cur_ok | prev_ok
        k = z_ref[rows, COL_AK:COL_AK + KV_WIDTH].astype(F32)
        kn = [_rms(k[:, h * HEAD_DIM:(h + 1) * HEAD_DIM], kg) for h in range(N_KV_HEADS)]
        kcur = jnp.concatenate(kn, axis=1).astype(BF16)
        vcur = z_ref[rows, COL_AV:COL_AV + KV_WIDTH]
        for kh in range(N_KV_HEADS):
            sl = slice(kh * HEAD_DIM, (kh + 1) * HEAD_DIM)
            k_cat = jnp.concatenate([kprev[:, sl], kcur[:, sl]], axis=0)
            v_cat = jnp.concatenate([vprev[:, sl], vcur[:, sl]], axis=0)
            for g in range(GQA_GROUP):
                h = kh * GQA_GROUP + g
                q = z_ref[rows, COL_AQ + h * HEAD_DIM:COL_AQ + (h + 1) * HEAD_DIM].astype(F32)
                s = jnp.where(mask, _dot_nt(_rms(q, qg).astype(BF16), k_cat), -jnp.inf)
                sink = sink_ref[h] * LOG2E
                m = jnp.maximum(jnp.max(s, axis=-1, keepdims=True), sink)
                e = jnp.exp2(s - m)
                denom = jnp.sum(e, axis=-1, keepdims=True) + jnp.exp2(sink - m)
                mix_ref[rows, h * HEAD_DIM:(h + 1) * HEAD_DIM] = (
                    _dot(e.astype(BF16), v_cat) / denom).astype(mix_ref.dtype)
        kprev, vprev = kcur, vcur
    kwin_ref[0] = jnp.concatenate(kn, axis=1)
    vwin_ref[0] = vcur.astype(F32)

    tril = (lax.broadcasted_iota(jnp.int32, (RET_CHUNK, RET_CHUNK), 0)
            >= lax.broadcasted_iota(jnp.int32, (RET_CHUNK, RET_CHUNK), 1))
    for h in range(N_RET_HEADS):
        hs = slice(h * RET_DIM, (h + 1) * RET_DIM)
        col_of = lambda c: slice(c + h * RET_DIM, c + (h + 1) * RET_DIM)
        state = s_ref[h]
        for c in range(MIX_ROWS // RET_CHUNK):
            rows = slice(c * RET_CHUNK, (c + 1) * RET_CHUNK)
            cos2, sin2 = cos_ref[rows, :], sin_ref[rows, :]
            q = (_rotary(z_ref[rows, col_of(COL_RQ)].astype(F32), cos2, sin2) * qdec_ref[:, hs]).astype(BF16)
            k = (_rotary(z_ref[rows, col_of(COL_RK)].astype(F32), cos2, sin2) * kdec_ref[:, hs]).astype(BF16)
            v = z_ref[rows, col_of(COL_RV)]
            sc = jnp.where(tril, _dot_nt(q, k), 0.0)
            o = _dot(jnp.concatenate([sc.astype(BF16), q], axis=1),
                     jnp.concatenate([v, state.astype(BF16)], axis=0))
            state = (state + _dot_tn(k, v)) * cdec_ref[h]
            o = o * lax.rsqrt(jnp.mean(o * o, axis=-1, keepdims=True) + NORM_EPS)
            gate = z_ref[rows, col_of(COL_RG)].astype(F32)
            mix_ref[rows, ATTN_WIDTH + h * RET_DIM:ATTN_WIDTH + (h + 1) * RET_DIM] = (
                o * rg_ref[:, hs] * (gate * jax.nn.sigmoid(gate))).astype(mix_ref.dtype)
        s_ref[h] = state
        sout_ref[0, h] = state


def _mixer_out_kernel(nsteps, tiles, groups, z_ref, kp_ref, vp_ref, qg_ref, kg_ref, sink_ref, cos_ref, sin_ref,
                      qdec_ref, kdec_ref, cdec_ref, rg_ref, h_ref, hs_ref, w_ref,
                      zs_ref, kbuf_ref, vbuf_ref, s0_ref, sinkc_ref, coss_ref, sins_ref, spread_ref, gamma_ref,
                      o_ref, os_ref, kwin_ref, vwin_ref, sout_ref, kout_ref, vout_ref, ssout_ref,
                      mix_a, mix_b, s_ref, mixs_scr):
    t = pl.program_id(0)
    per_group = tiles // groups

    @pl.when(t % nsteps == 0)
    def _():
        s_ref[...] = jnp.zeros_like(s_ref)

    def mix(dst):
        _mix_tile(z_ref, kp_ref, vp_ref, qg_ref, kg_ref, sink_ref, cos_ref, sin_ref, qdec_ref, kdec_ref,
                  cdec_ref, rg_ref, dst, kwin_ref, vwin_ref, sout_ref, s_ref, t % nsteps == 0)

    def project(src):
        o_ref[...] = h_ref[...] + _dot(src[...], w_ref[...])

    @pl.when(t == 0)
    def _():
        mix(mix_a)

    @pl.when((t > 0) & (t < tiles) & (t % 2 == 0))
    def _():
        project(mix_b)
        mix(mix_a)

    @pl.when((t < tiles) & (t % 2 == 1))
    def _():
        project(mix_a)
        mix(mix_b)

    @pl.when((t % per_group == per_group - 1) & (t < tiles))
    def _():
        g = t // per_group

        def put_mix(b, attn, ret):
            heads = [x[r:r + 1] for x in (attn, ret) for r in range(x.shape[0])]
            mixs_scr[g, b:b + 1, :] = jnp.concatenate(heads, axis=1)

        _sample_group(zs_ref, kbuf_ref, vbuf_ref, s0_ref, qg_ref, kg_ref, sinkc_ref, coss_ref, sins_ref,
                      spread_ref, gamma_ref, rg_ref, put_mix, kout_ref, vout_ref, ssout_ref)

    @pl.when(t == tiles)
    def _():
        project(mix_b if tiles % 2 == 0 else mix_a)
        mixs = jnp.concatenate([mixs_scr[g, :SAMPLE_GROUP, :] for g in range(groups)], axis=0)
        os_ref[...] = hs_ref[...] + _dot(mixs.astype(BF16), w_ref[...])


def _rope_tables(pos):
    half = RET_DIM // 2
    inv = ROPE_BASE ** (-np.arange(half, dtype=np.float64) / half)
    ang = np.asarray(pos, np.float64)[:, None] * inv[None, :]
    cos, sin = np.cos(ang), np.sin(ang)
    return (np.concatenate([cos, cos], axis=-1).astype(np.float32),
            np.concatenate([-sin, sin], axis=-1).astype(np.float32))


def _ret_log_decay():
    return np.log1p(-np.exp2(-5.0 - np.arange(N_RET_HEADS, dtype=np.float64)))


def _mixer_out(z, h, hs, zs, k_buf, v_buf, s0, w_out, qg, kg, sinks, rg, batch, seq):
    nsteps = seq // MIX_ROWS
    tiles = batch * nsteps
    nb = hs.shape[0]
    groups = nb // SAMPLE_GROUP
    assert nb % SAMPLE_GROUP == 0 and tiles % groups == 0 and SAMPLE_GROUP <= SUBLANES
    cache_rows = WINDOW * N_KV_HEADS
    cos2, sin2 = _rope_tables(np.arange(seq))
    cos_s, sin_s = _rope_tables(PAST_LEN + np.arange(1))
    lg = _ret_log_decay()
    steps = np.arange(RET_CHUNK, dtype=np.float64)[:, None, None] + 1.0
    lanes = lambda t: np.broadcast_to(t, (RET_CHUNK, N_RET_HEADS, RET_DIM)).reshape(RET_CHUNK, RET_WIDTH).astype(np.float32)
    q_dec = lanes(np.exp(lg[None, :, None] * steps))
    k_dec = lanes(np.exp(-lg[None, :, None] * steps) * RET_K_SCALE)
    chunk_dec = np.exp(lg * RET_CHUNK).astype(np.float32)
    gamma = np.exp(lg).astype(np.float32)
    spread = np.kron(np.eye(N_RET_HEADS), np.ones((1, RET_DIM))).astype(np.float32)
    sink_col = sinks.reshape(N_ATTN_HEADS, 1)
    smem = pl.BlockSpec(memory_space=pltpu.SMEM)
    win = jax.ShapeDtypeStruct((batch, BLK, KV_WIDTH), F32)
    blks = MIX_ROWS // BLK
    tile = lambda t: jnp.minimum(t, tiles - 1)
    prev_blk = lambda t: tile(t) * blks - jnp.minimum(tile(t) % nsteps, 1)
    behind = lambda t: (jnp.maximum(t - 1, 0), 0)
    seq_of = lambda t: tile(t) // nsteps
    group_of = lambda t: tile(t) // (tiles // groups)
    sample = lambda *shape: pl.BlockSpec((SAMPLE_GROUP,) + shape, lambda t: (group_of(t),) + (0,) * len(shape))
    cache, state = sample(cache_rows, HEAD_DIM), sample(N_RET_HEADS, RET_DIM, RET_DIM)
    cache_shape = jax.ShapeDtypeStruct((nb, cache_rows, HEAD_DIM), F32)
    y, ys, k_win, v_win, s_new, ks, vs, ss = pl.pallas_call(
        functools.partial(_mixer_out_kernel, nsteps, tiles, groups),
        grid=(tiles + 1,),
        in_specs=[
            pl.BlockSpec((MIX_ROWS, IN_WIDTH), lambda t: (tile(t), 0)),
            pl.BlockSpec((BLK, KV_WIDTH), lambda t: (prev_blk(t), COL_AK // KV_WIDTH)),
            pl.BlockSpec((BLK, KV_WIDTH), lambda t: (prev_blk(t), COL_AV // KV_WIDTH)),
            _whole(qg), _whole(kg), smem,
            pl.BlockSpec((MIX_ROWS, RET_DIM), lambda t: (tile(t) % nsteps, 0)),
            pl.BlockSpec((MIX_ROWS, RET_DIM), lambda t: (tile(t) % nsteps, 0)),
            _whole(q_dec), _whole(k_dec), smem, _whole(rg),
            pl.BlockSpec((MIX_ROWS, D_MODEL), behind), _whole(hs), _whole(w_out),
            sample(Z_HEADS, HEAD_DIM), cache, cache, state,
            _whole(sink_col), _whole(cos_s), _whole(sin_s), _whole(spread), smem,
        ],
        out_specs=[
            pl.BlockSpec((MIX_ROWS, D_MODEL), behind), _whole(hs),
            pl.BlockSpec((1, BLK, KV_WIDTH), lambda t: (seq_of(t), 0, 0)),
            pl.BlockSpec((1, BLK, KV_WIDTH), lambda t: (seq_of(t), 0, 0)),
            pl.BlockSpec((1, N_RET_HEADS, RET_DIM, RET_DIM), lambda t: (seq_of(t), 0, 0, 0)),
            cache, cache, state,
        ],
        out_shape=[jax.ShapeDtypeStruct(h.shape, F32), jax.ShapeDtypeStruct(hs.shape, F32), win, win,
                   jax.ShapeDtypeStruct((batch, N_RET_HEADS, RET_DIM, RET_DIM), F32),
                   cache_shape, cache_shape, jax.ShapeDtypeStruct(s0.shape, F32)],
        scratch_shapes=[pltpu.VMEM((MIX_ROWS, D_MODEL), BF16), pltpu.VMEM((MIX_ROWS, D_MODEL), BF16),
                        pltpu.VMEM((N_RET_HEADS, RET_DIM, RET_DIM), F32),
                        pltpu.VMEM((groups, SUBLANES, D_MODEL), F32)],
        compiler_params=_params("arbitrary"),
        name="mixer_out",
    )(z, z, z, qg, kg, sinks, cos2, sin2, q_dec, k_dec, chunk_dec, rg, h, hs, w_out,
      zs.reshape(nb, Z_HEADS, HEAD_DIM), k_buf.reshape(nb, cache_rows, HEAD_DIM),
      v_buf.reshape(nb, cache_rows, HEAD_DIM), s0, sink_col, cos_s, sin_s, spread, gamma)
    window = (nb, WINDOW, N_KV_HEADS, HEAD_DIM)
    return y, ys, k_win, v_win, s_new, ks.reshape(window), vs.reshape(window), ss


SAMPLE_GROUP = 4
Z_HEADS = IN_WIDTH // HEAD_DIM
ROW_RQ, ROW_RK, ROW_RV, ROW_RG, ROW_AQ, ROW_AK, ROW_AV = (
    c // HEAD_DIM for c in (COL_RQ, COL_RK, COL_RV, COL_RG, COL_AQ, COL_AK, COL_AV))
SUBLANES = 8


def _sample_group(z_ref, kbuf_ref, vbuf_ref, s0_ref, qg_ref, kg_ref, sink_ref, cos_ref, sin_ref,
                  spread_ref, gamma_ref, rg_ref, put_mix, kout_ref, vout_ref, s_out_ref):
    rows = WINDOW * N_KV_HEADS
    qg = qg_ref[...]
    kg = kg_ref[...]
    sink = sink_ref[...]
    head = lax.broadcasted_iota(jnp.int32, (N_ATTN_HEADS, HEAD_DIM), 0)
    first = head < GQA_GROUP
    own = ((lax.broadcasted_iota(jnp.int32, (N_ATTN_HEADS, rows), 1) % N_KV_HEADS)
           == (lax.broadcasted_iota(jnp.int32, (N_ATTN_HEADS, rows), 0) // GQA_GROUP))
    row = lax.broadcasted_iota(jnp.int32, (rows, HEAD_DIM), 0)
    cos2 = cos_ref[...]
    sin2 = sin_ref[...]
    spread = spread_ref[...]
    for b in range(SAMPLE_GROUP):
        part = lambda r, n: z_ref[b, r:r + n, :]
        qn = _rms(part(ROW_AQ, N_ATTN_HEADS), qg) * ATTN_SCALE
        kn = _rms(part(ROW_AK, N_KV_HEADS), kg)
        vn = part(ROW_AV, N_KV_HEADS)
        kbuf = kbuf_ref[b]
        vbuf = vbuf_ref[b]
        s = jnp.where(own, _dot_nt(qn.astype(BF16), kbuf.astype(BF16)), -jnp.inf)
        k_sel = jnp.where(first, kn[0:1], kn[1:2])
        v_sel = jnp.where(first, vn[0:1], vn[1:2])
        s_new = jnp.sum(qn * k_sel, axis=-1, keepdims=True)
        m = jnp.maximum(jnp.maximum(jnp.max(s, axis=-1, keepdims=True), s_new), sink)
        e = jnp.exp(s - m)
        e_new = jnp.exp(s_new - m)
        denom = jnp.sum(e, axis=-1, keepdims=True) + e_new + jnp.exp(sink - m)
        o_attn = (_dot(e.astype(BF16), vbuf.astype(BF16)) + e_new * v_sel) / denom
        for buf, new, out in ((kbuf, kn, kout_ref), (vbuf, vn, vout_ref)):
            shifted = pltpu.roll(buf, rows - N_KV_HEADS, 0)
            out[b] = jnp.where(row == rows - 2, new[0:1], jnp.where(row == rows - 1, new[1:2], shifted))
        q_cols = _dot_tn(_rotary(part(ROW_RQ, N_RET_HEADS), cos2, sin2), spread)
        k_cols = _dot_tn(_rotary(part(ROW_RK, N_RET_HEADS), cos2, sin2) * RET_K_SCALE, spread)
        v = part(ROW_RV, N_RET_HEADS)
        gate = part(ROW_RG, N_RET_HEADS)
        heads = []
        for h in range(N_RET_HEADS):
            hs = slice(h * RET_DIM, (h + 1) * RET_DIM)
            s_next = s0_ref[b, h] * gamma_ref[h] + k_cols[:, hs] * v[h:h + 1, :]
            s_out_ref[b, h] = s_next
            o = jnp.sum(q_cols[:, hs] * s_next, axis=0, keepdims=True)
            heads.append(o * lax.rsqrt(jnp.mean(o * o, axis=-1, keepdims=True) + NORM_EPS) * rg_ref[:, hs])
        o_ret = jnp.concatenate(heads, axis=0) * (gate * jax.nn.sigmoid(gate))
        put_mix(b, o_attn, o_ret)


def _ffn_kernel(h_ref, hs_ref, g_ref, wg_ref, wu_ref, wd_ref, o_ref, os_ref, f_ref):
    i, j = pl.program_id(0), pl.program_id(1)
    tm = h_ref.shape[0]

    @pl.when(j == 0)
    def _():
        h = h_ref[...]
        f_ref[:tm] = _rms(h, g_ref[...]).astype(BF16)
        o_ref[...] = h

    @pl.when((i == 0) & (j == 0))
    def _():
        hs = hs_ref[...]
        f_ref[tm:] = _rms(hs, g_ref[...]).astype(BF16)
        os_ref[...] = hs

    def swiglu(f):
        gate = _dot(f, wg_ref[...])
        act = (gate * jax.nn.sigmoid(gate) * _dot(f, wu_ref[...])).astype(BF16)
        return _dot(act, wd_ref[...])

    @pl.when(i == 0)
    def _():
        y = swiglu(f_ref[...])
        o_ref[...] += y[:tm]
        os_ref[...] += y[tm:]

    @pl.when(i > 0)
    def _():
        o_ref[...] += swiglu(f_ref[:tm])


def _ffn(h, hs, g, wg, wu, wd, tm):
    m = h.shape[0]
    row = pl.BlockSpec((tm, D_MODEL), lambda i, j: (i, 0))
    return pl.pallas_call(
        _ffn_kernel,
        grid=(m // tm, D_FF // FF_TILE),
        in_specs=[
            row, _whole(hs), _whole(g),
            pl.BlockSpec((None, D_MODEL, FF_TILE), lambda i, j: (j, 0, 0)),
            pl.BlockSpec((None, D_MODEL, FF_TILE), lambda i, j: (j, 0, 0)),
            pl.BlockSpec((FF_TILE, D_MODEL), lambda i, j: (j, 0)),
        ],
        out_specs=[row, _whole(hs)],
        out_shape=[jax.ShapeDtypeStruct((m, D_MODEL), F32), jax.ShapeDtypeStruct(hs.shape, F32)],
        scratch_shapes=[pltpu.VMEM((tm + hs.shape[0], D_MODEL), BF16)],
        compiler_params=_params("arbitrary", "arbitrary"),
        name="ffn",
    )(h, hs, g, wg, wu, wd)


def _ple_kernel(h_ref, p_ref, hs_ref, ps_ref, g_ref, wp_ref, wg_ref, o_ref, os_ref):
    tm = h_ref.shape[0]

    def embed(h, p):
        gate = jax.nn.sigmoid(_dot(_rms(h, g_ref[...]).astype(BF16), wg_ref[...]))
        return h + _dot(p.astype(BF16), wp_ref[...]) * gate

    @pl.when(pl.program_id(0) == 0)
    def _():
        y = embed(jnp.concatenate([h_ref[...], hs_ref[...]], axis=0),
                  jnp.concatenate([p_ref[...], ps_ref[...]], axis=0))
        o_ref[...] = y[:tm]
        os_ref[...] = y[tm:]

    @pl.when(pl.program_id(0) > 0)
    def _():
        o_ref[...] = embed(h_ref[...], p_ref[...])


def _ple(h, p, hs, ps, g, w_ple, w_pg, tm):
    m = h.shape[0]
    row = pl.BlockSpec((tm, D_MODEL), lambda i: (i, 0))
    return pl.pallas_call(
        _ple_kernel,
        grid=(m // tm,),
        in_specs=[row, pl.BlockSpec((tm, PLE_DIM), lambda i: (i, 0)), _whole(hs), _whole(ps), _whole(g),
                  _whole(w_ple), _whole(w_pg)],
        out_specs=[row, _whole(hs)],
        out_shape=[jax.ShapeDtypeStruct((m, D_MODEL), F32), jax.ShapeDtypeStruct(hs.shape, F32)],
        compiler_params=_params("arbitrary"),
        name="ple",
    )(h, p, hs, ps, g, w_ple, w_pg)


PLE_ROWS = 512
FFN_ROWS = 1024


def _dense_tail(h, p, hs, ps, w):
    h, hs = _ffn(h, hs, w["ffn_norm_g"], w["w_gate"], w["w_up"], w["w_down"], FFN_ROWS)
    return _ple(h, p, hs, ps, w["ple_norm_g"], w["w_ple"], w["w_ple_gate"], PLE_ROWS)


def kernel(x_prompt, x_sample, cache_k_win, cache_v_win, state_ret, p_prompt, p_sample,
           attn_norm_g, w_in, q_norm_g, k_norm_g, attn_sinks, ret_out_g, w_out,
           ffn_norm_g, w_gate, w_up, w_down, ple_norm_g, w_ple, w_ple_gate):
    assert w_in.shape[0] == 1 and x_sample.shape[1] == 1 and cache_k_win.shape[2] == WINDOW
    y_p, y_s = x_prompt, x_sample
    outs = [[] for _ in range(6)]
    for l in range(w_in.shape[0]):
        w = {
            "attn_norm_g": attn_norm_g[l].reshape(1, D_MODEL),
            "q_norm_g": q_norm_g[l].reshape(1, HEAD_DIM),
            "k_norm_g": k_norm_g[l].reshape(1, HEAD_DIM),
            "attn_sinks": attn_sinks[l],
            "ret_out_g": ret_out_g[l].reshape(1, RET_WIDTH),
            "ffn_norm_g": ffn_norm_g[l].reshape(1, D_MODEL),
            "ple_norm_g": ple_norm_g[l].reshape(1, D_MODEL),
        }
        batch, seq, _ = y_p.shape
        nb = y_s.shape[0]
        h_p = y_p.reshape(batch * seq, D_MODEL)
        h_s = y_s.reshape(nb, D_MODEL)
        z_p, z_s, *cast = _in_proj(h_p, h_s, w["attn_norm_g"], w_in[l], w_gate[l], w_up[l], w_down[l],
                                   w_out[l], w_ple_gate[l], w_ple[l])
        w.update(zip(("w_gate", "w_up", "w_down", "w_out", "w_ple_gate", "w_ple"), cast))
        h_p, h_s, kp, vp, sp, ks, vs, ss = _mixer_out(
            z_p, h_p, h_s, z_s, cache_k_win[l], cache_v_win[l], state_ret[l], w["w_out"], w["q_norm_g"],
            w["k_norm_g"], w["attn_sinks"], w["ret_out_g"], batch, seq)
        h_p, h_s = _dense_tail(h_p, p_prompt[l].reshape(batch * seq, PLE_DIM),
                               h_s, p_sample[l].reshape(nb, PLE_DIM), w)
        y_p, y_s = h_p.reshape(batch, seq, D_MODEL), h_s.reshape(nb, 1, D_MODEL)
        kp = kp.reshape(batch, BLK, N_KV_HEADS, HEAD_DIM)
        vp = vp.reshape(batch, BLK, N_KV_HEADS, HEAD_DIM)
        for acc, val in zip(outs, (kp, vp, sp, ks, vs, ss)):
            acc.append(val)
    kp, vp, sp, ks, vs, ss = (jnp.stack(o) for o in outs)
    return (y_p, y_s, kp, vp, sp, ks, vs, ss)
```

```python
import functools

import numpy as np
import jax
import jax.numpy as jnp
from jax import lax
from jax.experimental import pallas as pl
from jax.experimental.pallas import tpu as pltpu

D_MODEL = 2048
HEAD_DIM = 128
N_ATTN_HEADS = 8
N_KV_HEADS = 2
GQA_GROUP = N_ATTN_HEADS // N_KV_HEADS
ATTN_WIDTH = N_ATTN_HEADS * HEAD_DIM
KV_WIDTH = N_KV_HEADS * HEAD_DIM
WINDOW = 128
BLK = 128
N_RET_HEADS = 8
RET_DIM = 128
RET_WIDTH = N_RET_HEADS * RET_DIM
D_FF = 5632
IN_WIDTH = 5632
PLE_DIM = 256
PAST_LEN = 16384
ROPE_BASE = 10000.0
NORM_EPS = 1e-6
ATTN_SCALE = HEAD_DIM ** -0.5
RET_K_SCALE = RET_DIM ** -0.5

COL_RQ, COL_RK, COL_RV, COL_RG = 0, RET_WIDTH, 2 * RET_WIDTH, 3 * RET_WIDTH
COL_AQ = 4 * RET_WIDTH
COL_AK = COL_AQ + ATTN_WIDTH
COL_AV = COL_AK + KV_WIDTH
IN_TILE = 512
IN_TILES = IN_WIDTH // IN_TILE
IN_ROT = (ATTN_WIDTH + 2 * KV_WIDTH) // IN_TILE

VMEM_LIMIT = 56 * 1024 * 1024

F32 = jnp.float32
BF16 = jnp.bfloat16


def _params(*sem):
    return pltpu.CompilerParams(dimension_semantics=sem, vmem_limit_bytes=VMEM_LIMIT)


def _rms(x, g):
    return x * lax.rsqrt(jnp.mean(x * x, axis=-1, keepdims=True) + NORM_EPS) * g


def _dot(a, b):
    return jnp.dot(a, b, preferred_element_type=F32)


def _dot_nt(a, b):
    return lax.dot_general(a, b, (((1,), (1,)), ((), ())), preferred_element_type=F32)


def _dot_tn(a, b):
    return lax.dot_general(a, b, (((0,), (0,)), ((), ())), preferred_element_type=F32)


def _whole(a):
    return pl.BlockSpec(a.shape, lambda *_: (0,) * a.ndim)


def _in_proj_kernel(x_hbm, xs_ref, g_ref, w_ref, wg_ref, wu_ref, wd_ref, wo_ref, wpg_ref, wple_ref,
                    z_ref, zs_ref, wg_o, wu_o, wd_o, wo_o, wpg_o, wple_o,
                    xbuf, xsem, a_ref, wres):
    i, j = pl.program_id(0), pl.program_id(1)

    def x_copy(tile):
        return pltpu.make_async_copy(x_hbm.at[pl.ds(tile * IN_ROWS, IN_ROWS)], xbuf, xsem)

    @pl.when((i == 0) & (j == 0))
    def _():
        x_copy(0).start()
        a_ref[IN_ROWS:] = _rms(xs_ref[...], g_ref[...]).astype(BF16)

    @pl.when(j == 0)
    def _():
        x_copy(i).wait()
        a_ref[:IN_ROWS] = _rms(xbuf[...], g_ref[...]).astype(BF16)

        @pl.when(i + 1 < pl.num_programs(0))
        def _():
            x_copy(i + 1).start()

    def cast_weights():
        for src, dst in ((wg_ref, wg_o), (wu_ref, wu_o), (wd_ref, wd_o), (wo_ref, wo_o), (wpg_ref, wpg_o),
                         (wple_ref, wple_o)):
            dst[...] = src[...].astype(BF16)

    @pl.when(i == 0)
    def _():
        wres[j] = w_ref[...].astype(BF16)
        y = _dot(a_ref[...], wres[j])
        z_ref[...] = y[:IN_ROWS].astype(z_ref.dtype)
        zs_ref[...] = y[IN_ROWS:]
        cast_weights()

    @pl.when(i > 0)
    def _():
        z_ref[...] = _dot(a_ref[:IN_ROWS], wres[j]).astype(z_ref.dtype)
        cast_weights()


IN_ROWS = 1024
CAST_SPLIT = 8
FF_TILE = 512


def _in_proj(x, xs, g, w_in, w_gate, w_up, w_down, w_out, w_pg, w_ple):
    m = x.shape[0]
    ni = m // IN_ROWS
    steps = ni * IN_TILES
    assert m % IN_ROWS == 0 and FF_TILE == IN_TILE and D_FF == IN_WIDTH
    gu_rows, d_rows = D_MODEL // ni, D_FF // steps
    sq_rows, ple_rows = D_MODEL // (ni * CAST_SPLIT), PLE_DIM // ni
    assert gu_rows * ni == D_MODEL and d_rows * steps == D_FF and sq_rows % 16 == 0 and ple_rows % 16 == 0
    sq_idx = lambda i, j: (i * CAST_SPLIT + jnp.minimum(j, CAST_SPLIT - 1), 0)
    first = lambda i, j: jnp.where(i == 0, j, IN_TILES - 1)
    return pl.pallas_call(
        _in_proj_kernel,
        grid=(ni, IN_TILES),
        in_specs=[
            pl.BlockSpec(memory_space=pl.ANY),
            pl.BlockSpec(xs.shape, lambda i, j: (0, 0)),
            pl.BlockSpec((1, D_MODEL), lambda i, j: (0, 0)),
            pl.BlockSpec((D_MODEL, IN_TILE), lambda i, j: (0, (first(i, j) + IN_ROT) % IN_TILES)),
            pl.BlockSpec((gu_rows, FF_TILE), lambda i, j: (i, j)),
            pl.BlockSpec((gu_rows, FF_TILE), lambda i, j: (i, j)),
            pl.BlockSpec((d_rows, D_MODEL), lambda i, j: (i * IN_TILES + j, 0)),
            pl.BlockSpec((sq_rows, D_MODEL), sq_idx),
            pl.BlockSpec((sq_rows, D_MODEL), sq_idx),
            pl.BlockSpec((ple_rows, D_MODEL), lambda i, j: (i, 0)),
        ],
        out_specs=[
            pl.BlockSpec((IN_ROWS, IN_TILE), lambda i, j: (i, j)),
            pl.BlockSpec((xs.shape[0], IN_TILE), lambda i, j: (0, first(i, j))),
            pl.BlockSpec((None, gu_rows, FF_TILE), lambda i, j: (j, i, 0)),
            pl.BlockSpec((None, gu_rows, FF_TILE), lambda i, j: (j, i, 0)),
            pl.BlockSpec((d_rows, D_MODEL), lambda i, j: (i * IN_TILES + j, 0)),
            pl.BlockSpec((sq_rows, D_MODEL), sq_idx),
            pl.BlockSpec((sq_rows, D_MODEL), sq_idx),
            pl.BlockSpec((ple_rows, D_MODEL), lambda i, j: (i, 0)),
        ],
        out_shape=[
            jax.ShapeDtypeStruct((m, IN_WIDTH), BF16),
            jax.ShapeDtypeStruct((xs.shape[0], IN_WIDTH), F32),
            jax.ShapeDtypeStruct((D_FF // FF_TILE, D_MODEL, FF_TILE), BF16),
            jax.ShapeDtypeStruct((D_FF // FF_TILE, D_MODEL, FF_TILE), BF16),
            jax.ShapeDtypeStruct((D_FF, D_MODEL), BF16),
            jax.ShapeDtypeStruct((D_MODEL, D_MODEL), BF16),
            jax.ShapeDtypeStruct((D_MODEL, D_MODEL), BF16),
            jax.ShapeDtypeStruct((PLE_DIM, D_MODEL), BF16),
        ],
        scratch_shapes=[
            pltpu.VMEM((IN_ROWS, D_MODEL), F32),
            pltpu.SemaphoreType.DMA(()),
            pltpu.VMEM((IN_ROWS + xs.shape[0], D_MODEL), BF16),
            pltpu.VMEM((IN_TILES, D_MODEL, IN_TILE), BF16),
        ],
        compiler_params=_params("arbitrary", "arbitrary"),
        name="in_proj",
    )(x, xs, g, w_in, w_gate, w_up, w_down, w_out, w_pg, w_ple)


MIX_ROWS = 256
RET_CHUNK = 256
LOG2E = 1.4426950408889634


def _rotary(x, cos2, sin2):
    return x * cos2 + pltpu.roll(x, RET_DIM // 2, 1) * sin2


def _mix_tile(z_ref, kp_ref, vp_ref, qg_ref, kg_ref, sink_ref, cos_ref, sin_ref, qdec_ref, kdec_ref,
              cdec_ref, rg_ref, mix_ref, kwin_ref, vwin_ref, sout_ref, s_ref, first):
    qg = qg_ref[...] * (ATTN_SCALE * LOG2E)
    kg = kg_ref[...]
    row = lax.broadcasted_iota(jnp.int32, (BLK, 2 * BLK), 0)
    col = lax.broadcasted_iota(jnp.int32, (BLK, 2 * BLK), 1)
    cur_ok = (col >= BLK) & ((col - BLK) <= row)
    prev_ok = (col < BLK) & (col >= row)
    kp = kp_ref[...].astype(F32)
    kprev = jnp.concatenate([_rms(kp[:, h * HEAD_DIM:(h + 1) * HEAD_DIM], kg) for h in range(N_KV_HEADS)],
                            axis=1).astype(BF16)
    vprev = vp_ref[...]
    for blk in range(MIX_ROWS // BLK):
        rows = slice(blk * BLK, (blk + 1) * BLK)
        mask = cur_ok | (prev_ok & jnp.logical_not(first)) if blk == 0 else cur_ok | prev_ok
        k = z_ref[rows, COL_AK:COL_AK + KV_WIDTH].astype(F32)
        kn = [_rms(k[:, h * HEAD_DIM:(h + 1) * HEAD_DIM], kg) for h in range(N_KV_HEADS)]
        kcur = jnp.concatenate(kn, axis=1).astype(BF16)
        vcur = z_ref[rows, COL_AV:COL_AV + KV_WIDTH]
        for kh in range(N_KV_HEADS):
            sl = slice(kh * HEAD_DIM, (kh + 1) * HEAD_DIM)
            k_cat = jnp.concatenate([kprev[:, sl], kcur[:, sl]], axis=0)
            v_cat = jnp.concatenate([vprev[:, sl], vcur[:, sl]], axis=0)
            for g in range(GQA_GROUP):
                h = kh * GQA_GROUP + g
                q = z_ref[rows, COL_AQ + h * HEAD_DIM:COL_AQ + (h + 1) * HEAD_DIM].astype(F32)
                s = jnp.where(mask, _dot_nt(_rms(q, qg).astype(BF16), k_cat), -jnp.inf)
                sink = sink_ref[h] * LOG2E
                m = jnp.maximum(jnp.max(s, axis=-1, keepdims=True), sink)
                e = jnp.exp2(s - m)
                denom = jnp.sum(e, axis=-1, keepdims=True) + jnp.exp2(sink - m)
                mix_ref[rows, h * HEAD_DIM:(h + 1) * HEAD_DIM] = (
                    _dot(e.astype(BF16), v_cat) / denom).astype(mix_ref.dtype)
        kprev, vprev = kcur, vcur
    kwin_ref[0] = jnp.concatenate(kn, axis=1)
    vwin_ref[0] = vcur.astype(F32)

    tril = (lax.broadcasted_iota(jnp.int32, (RET_CHUNK, RET_CHUNK), 0)
            >= lax.broadcasted_iota(jnp.int32, (RET_CHUNK, RET_CHUNK), 1))
    for h in range(N_RET_HEADS):
        hs = slice(h * RET_DIM, (h + 1) * RET_DIM)
        col_of = lambda c: slice(c + h * RET_DIM, c + (h + 1) * RET_DIM)
        state = s_ref[h]
        for c in range(MIX_ROWS // RET_CHUNK):
            rows = slice(c * RET_CHUNK, (c + 1) * RET_CHUNK)
            cos2, sin2 = cos_ref[rows, :], sin_ref[rows, :]
            q = (_rotary(z_ref[rows, col_of(COL_RQ)].astype(F32), cos2, sin2) * qdec_ref[:, hs]).astype(BF16)
            k = (_rotary(z_ref[rows, col_of(COL_RK)].astype(F32), cos2, sin2) * kdec_ref[:, hs]).astype(BF16)
            v = z_ref[rows, col_of(COL_RV)]
            sc = jnp.where(tril, _dot_nt(q, k), 0.0)
            o = _dot(jnp.concatenate([sc.astype(BF16), q], axis=1),
                     jnp.concatenate([v, state.astype(BF16)], axis=0))
            state = (state + _dot_tn(k, v)) * cdec_ref[h]
            o = o * lax.rsqrt(jnp.mean(o * o, axis=-1, keepdims=True) + NORM_EPS)
            gate = z_ref[rows, col_of(COL_RG)].astype(F32)
            mix_ref[rows, ATTN_WIDTH + h * RET_DIM:ATTN_WIDTH + (h + 1) * RET_DIM] = (
                o * rg_ref[:, hs] * (gate * jax.nn.sigmoid(gate))).astype(mix_ref.dtype)
        s_ref[h] = state
        sout_ref[0, h] = state


def _mixer_out_kernel(nsteps, tiles, groups, z_ref, kp_ref, vp_ref, qg_ref, kg_ref, sink_ref, cos_ref, sin_ref,
                      qdec_ref, kdec_ref, cdec_ref, rg_ref, h_ref, hs_ref, w_ref,
                      zs_ref, kbuf_ref, vbuf_ref, s0_ref, sinkc_ref, coss_ref, sins_ref, spread_ref, gamma_ref,
                      o_ref, os_ref, kwin_ref, vwin_ref, sout_ref, kout_ref, vout_ref, ssout_ref,
                      mix_a, mix_b, s_ref, mixs_scr):
    t = pl.program_id(0)
    per_group = tiles // groups

    @pl.when(t % nsteps == 0)
    def _():
        s_ref[...] = jnp.zeros_like(s_ref)

    def mix(dst):
        _mix_tile(z_ref, kp_ref, vp_ref, qg_ref, kg_ref, sink_ref, cos_ref, sin_ref, qdec_ref, kdec_ref,
                  cdec_ref, rg_ref, dst, kwin_ref, vwin_ref, sout_ref, s_ref, t % nsteps == 0)

    def project(src):
        o_ref[...] = h_ref[...] + _dot(src[...], w_ref[...])

    @pl.when(t == 0)
    def _():
        mix(mix_a)

    @pl.when((t > 0) & (t < tiles) & (t % 2 == 0))
    def _():
        project(mix_b)
        mix(mix_a)

    @pl.when((t < tiles) & (t % 2 == 1))
    def _():
        project(mix_a)
        mix(mix_b)

    @pl.when((t % per_group == per_group - 1) & (t < tiles))
    def _():
        g = t // per_group

        def put_mix(b, attn, ret):
            heads = [x[r:r + 1] for x in (attn, ret) for r in range(x.shape[0])]
            mixs_scr[g, b:b + 1, :] = jnp.concatenate(heads, axis=1)

        _sample_group(zs_ref, kbuf_ref, vbuf_ref, s0_ref, qg_ref, kg_ref, sinkc_ref, coss_ref, sins_ref,
                      spread_ref, gamma_ref, rg_ref, put_mix, kout_ref, vout_ref, ssout_ref)

    @pl.when(t == tiles)
    def _():
        src = mix_b if tiles % 2 == 0 else mix_a
        mixs = jnp.concatenate([mixs_scr[g, :SAMPLE_GROUP, :] for g in range(groups)], axis=0)
        y = _dot(jnp.concatenate([src[...], mixs.astype(BF16)], axis=0), w_ref[...])
        o_ref[...] = h_ref[...] + y[:MIX_ROWS]
        os_ref[...] = hs_ref[...] + y[MIX_ROWS:]


def _rope_tables(pos):
    half = RET_DIM // 2
    inv = ROPE_BASE ** (-np.arange(half, dtype=np.float64) / half)
    ang = np.asarray(pos, np.float64)[:, None] * inv[None, :]
    cos, sin = np.cos(ang), np.sin(ang)
    return (np.concatenate([cos, cos], axis=-1).astype(np.float32),
            np.concatenate([-sin, sin], axis=-1).astype(np.float32))


def _ret_log_decay():
    return np.log1p(-np.exp2(-5.0 - np.arange(N_RET_HEADS, dtype=np.float64)))


def _mixer_out(z, h, hs, zs, k_buf, v_buf, s0, w_out, qg, kg, sinks, rg, batch, seq):
    nsteps = seq // MIX_ROWS
    tiles = batch * nsteps
    nb = hs.shape[0]
    groups = nb // SAMPLE_GROUP
    assert nb % SAMPLE_GROUP == 0 and tiles % groups == 0 and SAMPLE_GROUP <= SUBLANES
    cache_rows = WINDOW * N_KV_HEADS
    cos2, sin2 = _rope_tables(np.arange(seq))
    cos_s, sin_s = _rope_tables(PAST_LEN + np.arange(1))
    lg = _ret_log_decay()
    steps = np.arange(RET_CHUNK, dtype=np.float64)[:, None, None] + 1.0
    lanes = lambda t: np.broadcast_to(t, (RET_CHUNK, N_RET_HEADS, RET_DIM)).reshape(RET_CHUNK, RET_WIDTH).astype(np.float32)
    q_dec = lanes(np.exp(lg[None, :, None] * steps))
    k_dec = lanes(np.exp(-lg[None, :, None] * steps) * RET_K_SCALE)
    chunk_dec = np.exp(lg * RET_CHUNK).astype(np.float32)
    gamma = np.exp(lg).astype(np.float32)
    spread = np.kron(np.eye(N_RET_HEADS), np.ones((1, RET_DIM))).astype(np.float32)
    sink_col = sinks.reshape(N_ATTN_HEADS, 1)
    smem = pl.BlockSpec(memory_space=pltpu.SMEM)
    win = jax.ShapeDtypeStruct((batch, BLK, KV_WIDTH), F32)
    blks = MIX_ROWS // BLK
    tile = lambda t: jnp.minimum(t, tiles - 1)
    prev_blk = lambda t: tile(t) * blks - jnp.minimum(tile(t) % nsteps, 1)
    behind = lambda t: (jnp.maximum(t - 1, 0), 0)
    seq_of = lambda t: tile(t) // nsteps
    group_of = lambda t: tile(t) // (tiles // groups)
    sample = lambda *shape: pl.BlockSpec((SAMPLE_GROUP,) + shape, lambda t: (group_of(t),) + (0,) * len(shape))
    cache, state = sample(cache_rows, HEAD_DIM), sample(N_RET_HEADS, RET_DIM, RET_DIM)
    cache_shape = jax.ShapeDtypeStruct((nb, cache_rows, HEAD_DIM), F32)
    y, ys, k_win, v_win, s_new, ks, vs, ss = pl.pallas_call(
        functools.partial(_mixer_out_kernel, nsteps, tiles, groups),
        grid=(tiles + 1,),
        in_specs=[
            pl.BlockSpec((MIX_ROWS, IN_WIDTH), lambda t: (tile(t), 0)),
            pl.BlockSpec((BLK, KV_WIDTH), lambda t: (prev_blk(t), COL_AK // KV_WIDTH)),
            pl.BlockSpec((BLK, KV_WIDTH), lambda t: (prev_blk(t), COL_AV // KV_WIDTH)),
            _whole(qg), _whole(kg), smem,
            pl.BlockSpec((MIX_ROWS, RET_DIM), lambda t: (tile(t) % nsteps, 0)),
            pl.BlockSpec((MIX_ROWS, RET_DIM), lambda t: (tile(t) % nsteps, 0)),
            _whole(q_dec), _whole(k_dec), smem, _whole(rg),
            pl.BlockSpec((MIX_ROWS, D_MODEL), behind), _whole(hs), _whole(w_out),
            sample(Z_HEADS, HEAD_DIM), cache, cache, state,
            _whole(sink_col), _whole(cos_s), _whole(sin_s), _whole(spread), smem,
        ],
        out_specs=[
            pl.BlockSpec((MIX_ROWS, D_MODEL), behind), _whole(hs),
            pl.BlockSpec((1, BLK, KV_WIDTH), lambda t: (seq_of(t), 0, 0)),
            pl.BlockSpec((1, BLK, KV_WIDTH), lambda t: (seq_of(t), 0, 0)),
            pl.BlockSpec((1, N_RET_HEADS, RET_DIM, RET_DIM), lambda t: (seq_of(t), 0, 0, 0)),
            cache, cache, state,
        ],
        out_shape=[jax.ShapeDtypeStruct(h.shape, F32), jax.ShapeDtypeStruct(hs.shape, F32), win, win,
                   jax.ShapeDtypeStruct((batch, N_RET_HEADS, RET_DIM, RET_DIM), F32),
                   cache_shape, cache_shape, jax.ShapeDtypeStruct(s0.shape, F32)],
        scratch_shapes=[pltpu.VMEM((MIX_ROWS, D_MODEL), BF16), pltpu.VMEM((MIX_ROWS, D_MODEL), BF16),
                        pltpu.VMEM((N_RET_HEADS, RET_DIM, RET_DIM), F32),
                        pltpu.VMEM((groups, SUBLANES, D_MODEL), F32)],
        compiler_params=_params("arbitrary"),
        name="mixer_out",
    )(z, z, z, qg, kg, sinks, cos2, sin2, q_dec, k_dec, chunk_dec, rg, h, hs, w_out,
      zs.reshape(nb, Z_HEADS, HEAD_DIM), k_buf.reshape(nb, cache_rows, HEAD_DIM),
      v_buf.reshape(nb, cache_rows, HEAD_DIM), s0, sink_col, cos_s, sin_s, spread, gamma)
    window = (nb, WINDOW, N_KV_HEADS, HEAD_DIM)
    return y, ys, k_win, v_win, s_new, ks.reshape(window), vs.reshape(window), ss


SAMPLE_GROUP = 4
Z_HEADS = IN_WIDTH // HEAD_DIM
ROW_RQ, ROW_RK, ROW_RV, ROW_RG, ROW_AQ, ROW_AK, ROW_AV = (
    c // HEAD_DIM for c in (COL_RQ, COL_RK, COL_RV, COL_RG, COL_AQ, COL_AK, COL_AV))
SUBLANES = 8


def _sample_group(z_ref, kbuf_ref, vbuf_ref, s0_ref, qg_ref, kg_ref, sink_ref, cos_ref, sin_ref,
                  spread_ref, gamma_ref, rg_ref, put_mix, kout_ref, vout_ref, s_out_ref):
    rows = WINDOW * N_KV_HEADS
    qg = qg_ref[...]
    kg = kg_ref[...]
    sink = sink_ref[...]
    head = lax.broadcasted_iota(jnp.int32, (N_ATTN_HEADS, HEAD_DIM), 0)
    first = head < GQA_GROUP
    own = ((lax.broadcasted_iota(jnp.int32, (N_ATTN_HEADS, rows), 1) % N_KV_HEADS)
           == (lax.broadcasted_iota(jnp.int32, (N_ATTN_HEADS, rows), 0) // GQA_GROUP))
    row = lax.broadcasted_iota(jnp.int32, (rows, HEAD_DIM), 0)
    cos2 = cos_ref[...]
    sin2 = sin_ref[...]
    spread = spread_ref[...]
    for b in range(SAMPLE_GROUP):
        part = lambda r, n: z_ref[b, r:r + n, :]
        qn = _rms(part(ROW_AQ, N_ATTN_HEADS), qg) * ATTN_SCALE
        kn = _rms(part(ROW_AK, N_KV_HEADS), kg)
        vn = part(ROW_AV, N_KV_HEADS)
        kbuf = kbuf_ref[b]
        vbuf = vbuf_ref[b]
        s = jnp.where(own, _dot_nt(qn.astype(BF16), kbuf.astype(BF16)), -jnp.inf)
        k_sel = jnp.where(first, kn[0:1], kn[1:2])
        v_sel = jnp.where(first, vn[0:1], vn[1:2])
        s_new = jnp.sum(qn * k_sel, axis=-1, keepdims=True)
        m = jnp.maximum(jnp.maximum(jnp.max(s, axis=-1, keepdims=True), s_new), sink)
        e = jnp.exp(s - m)
        e_new = jnp.exp(s_new - m)
        denom = jnp.sum(e, axis=-1, keepdims=True) + e_new + jnp.exp(sink - m)
        o_attn = (_dot(e.astype(BF16), vbuf.astype(BF16)) + e_new * v_sel) / denom
        for buf, new, out in ((kbuf, kn, kout_ref), (vbuf, vn, vout_ref)):
            shifted = pltpu.roll(buf, rows - N_KV_HEADS, 0)
            out[b] = jnp.where(row == rows - 2, new[0:1], jnp.where(row == rows - 1, new[1:2], shifted))
        q_cols = _dot_tn(_rotary(part(ROW_RQ, N_RET_HEADS), cos2, sin2), spread)
        k_cols = _dot_tn(_rotary(part(ROW_RK, N_RET_HEADS), cos2, sin2) * RET_K_SCALE, spread)
        v = part(ROW_RV, N_RET_HEADS)
        gate = part(ROW_RG, N_RET_HEADS)
        heads = []
        for h in range(N_RET_HEADS):
            hs = slice(h * RET_DIM, (h + 1) * RET_DIM)
            s_next = s0_ref[b, h] * gamma_ref[h] + k_cols[:, hs] * v[h:h + 1, :]
            s_out_ref[b, h] = s_next
            o = jnp.sum(q_cols[:, hs] * s_next, axis=0, keepdims=True)
            heads.append(o * lax.rsqrt(jnp.mean(o * o, axis=-1, keepdims=True) + NORM_EPS) * rg_ref[:, hs])
        o_ret = jnp.concatenate(heads, axis=0) * (gate * jax.nn.sigmoid(gate))
        put_mix(b, o_attn, o_ret)


def _ffn_kernel(h_ref, hs_ref, g_ref, wg_ref, wu_ref, wd_ref, o_ref, os_ref, f_ref):
    i, j = pl.program_id(0), pl.program_id(1)
    tm = h_ref.shape[0]

    @pl.when(j == 0)
    def _():
        f_ref[:tm] = _rms(h_ref[...], g_ref[...]).astype(BF16)

    @pl.when((i == 0) & (j == 0))
    def _():
        f_ref[tm:] = _rms(hs_ref[...], g_ref[...]).astype(BF16)

    def swiglu(f):
        gate = _dot(f, wg_ref[...])
        act = (gate * jax.nn.sigmoid(gate) * _dot(f, wu_ref[...])).astype(BF16)
        return _dot(act, wd_ref[...])

    @pl.when((i == 0) & (j == 0))
    def _():
        y = swiglu(f_ref[...])
        o_ref[...] = h_ref[...] + y[:tm]
        os_ref[...] = hs_ref[...] + y[tm:]

    @pl.when((i == 0) & (j > 0))
    def _():
        y = swiglu(f_ref[...])
        o_ref[...] += y[:tm]
        os_ref[...] += y[tm:]

    @pl.when((i > 0) & (j == 0))
    def _():
        o_ref[...] = h_ref[...] + swiglu(f_ref[:tm])

    @pl.when((i > 0) & (j > 0))
    def _():
        o_ref[...] += swiglu(f_ref[:tm])


def _ffn(h, hs, g, wg, wu, wd, tm):
    m = h.shape[0]
    row = pl.BlockSpec((tm, D_MODEL), lambda i, j: (i, 0))
    return pl.pallas_call(
        _ffn_kernel,
        grid=(m // tm, D_FF // FF_TILE),
        in_specs=[
            row, _whole(hs), _whole(g),
            pl.BlockSpec((None, D_MODEL, FF_TILE), lambda i, j: (j, 0, 0)),
            pl.BlockSpec((None, D_MODEL, FF_TILE), lambda i, j: (j, 0, 0)),
            pl.BlockSpec((FF_TILE, D_MODEL), lambda i, j: (j, 0)),
        ],
        out_specs=[row, _whole(hs)],
        out_shape=[jax.ShapeDtypeStruct((m, D_MODEL), F32), jax.ShapeDtypeStruct(hs.shape, F32)],
        scratch_shapes=[pltpu.VMEM((tm + hs.shape[0], D_MODEL), BF16)],
        compiler_params=_params("arbitrary", "arbitrary"),
        name="ffn",
    )(h, hs, g, wg, wu, wd)


def _ple_kernel(h_ref, p_ref, hs_ref, ps_ref, g_ref, wp_ref, wg_ref, o_ref, os_ref):
    tm = h_ref.shape[0]

    def embed(h, p):
        gate = jax.nn.sigmoid(_dot(_rms(h, g_ref[...]).astype(BF16), wg_ref[...]))
        return h + _dot(p.astype(BF16), wp_ref[...]) * gate

    @pl.when(pl.program_id(0) == 0)
    def _():
        y = embed(jnp.concatenate([h_ref[...], hs_ref[...]], axis=0),
                  jnp.concatenate([p_ref[...], ps_ref[...]], axis=0))
        o_ref[...] = y[:tm]
        os_ref[...] = y[tm:]

    @pl.when(pl.program_id(0) > 0)
    def _():
        o_ref[...] = embed(h_ref[...], p_ref[...])


def _ple(h, p, hs, ps, g, w_ple, w_pg, tm):
    m = h.shape[0]
    row = pl.BlockSpec((tm, D_MODEL), lambda i: (i, 0))
    return pl.pallas_call(
        _ple_kernel,
        grid=(m // tm,),
        in_specs=[row, pl.BlockSpec((tm, PLE_DIM), lambda i: (i, 0)), _whole(hs), _whole(ps), _whole(g),
                  _whole(w_ple), _whole(w_pg)],
        out_specs=[row, _whole(hs)],
        out_shape=[jax.ShapeDtypeStruct((m, D_MODEL), F32), jax.ShapeDtypeStruct(hs.shape, F32)],
        compiler_params=_params("arbitrary"),
        name="ple",
    )(h, p, hs, ps, g, w_ple, w_pg)


PLE_ROWS = 512
FFN_ROWS = 1024


def _dense_tail(h, p, hs, ps, w):
    h, hs = _ffn(h, hs, w["ffn_norm_g"], w["w_gate"], w["w_up"], w["w_down"], FFN_ROWS)
    return _ple(h, p, hs, ps, w["ple_norm_g"], w["w_ple"], w["w_ple_gate"], PLE_ROWS)


def kernel(x_prompt, x_sample, cache_k_win, cache_v_win, state_ret, p_prompt, p_sample,
           attn_norm_g, w_in, q_norm_g, k_norm_g, attn_sinks, ret_out_g, w_out,
           ffn_norm_g, w_gate, w_up, w_down, ple_norm_g, w_ple, w_ple_gate):
    assert w_in.shape[0] == 1 and x_sample.shape[1] == 1 and cache_k_win.shape[2] == WINDOW
    y_p, y_s = x_prompt, x_sample
    outs = [[] for _ in range(6)]
    for l in range(w_in.shape[0]):
        w = {
            "attn_norm_g": attn_norm_g[l].reshape(1, D_MODEL),
            "q_norm_g": q_norm_g[l].reshape(1, HEAD_DIM),
            "k_norm_g": k_norm_g[l].reshape(1, HEAD_DIM),
            "attn_sinks": attn_sinks[l],
            "ret_out_g": ret_out_g[l].reshape(1, RET_WIDTH),
            "ffn_norm_g": ffn_norm_g[l].reshape(1, D_MODEL),
            "ple_norm_g": ple_norm_g[l].reshape(1, D_MODEL),
        }
        batch, seq, _ = y_p.shape
        nb = y_s.shape[0]
        h_p = y_p.reshape(batch * seq, D_MODEL)
        h_s = y_s.reshape(nb, D_MODEL)
        z_p, z_s, *cast = _in_proj(h_p, h_s, w["attn_norm_g"], w_in[l], w_gate[l], w_up[l], w_down[l],
                                   w_out[l], w_ple_gate[l], w_ple[l])
        w.update(zip(("w_gate", "w_up", "w_down", "w_out", "w_ple_gate", "w_ple"), cast))
        h_p, h_s, kp, vp, sp, ks, vs, ss = _mixer_out(
            z_p, h_p, h_s, z_s, cache_k_win[l], cache_v_win[l], state_ret[l], w["w_out"], w["q_norm_g"],
            w["k_norm_g"], w["attn_sinks"], w["ret_out_g"], batch, seq)
        h_p, h_s = _dense_tail(h_p, p_prompt[l].reshape(batch * seq, PLE_DIM),
                               h_s, p_sample[l].reshape(nb, PLE_DIM), w)
        y_p, y_s = h_p.reshape(batch, seq, D_MODEL), h_s.reshape(nb, 1, D_MODEL)
        kp = kp.reshape(batch, BLK, N_KV_HEADS, HEAD_DIM)
        vp = vp.reshape(batch, BLK, N_KV_HEADS, HEAD_DIM)
        for acc, val in zip(outs, (kp, vp, sp, ks, vs, ss)):
            acc.append(val)
    kp, vp, sp, ks, vs, ss = (jnp.stack(o) for o in outs)
    return (y_p, y_s, kp, vp, sp, ks, vs, ss)
```

```python
import functools

import numpy as np
import jax
import jax.numpy as jnp
from jax import lax
from jax.experimental import pallas as pl
from jax.experimental.pallas import tpu as pltpu

D_MODEL = 2048
HEAD_DIM = 128
N_ATTN_HEADS = 8
N_KV_HEADS = 2
GQA_GROUP = N_ATTN_HEADS // N_KV_HEADS
ATTN_WIDTH = N_ATTN_HEADS * HEAD_DIM
KV_WIDTH = N_KV_HEADS * HEAD_DIM
WINDOW = 128
BLK = 128
N_RET_HEADS = 8
RET_DIM = 128
RET_WIDTH = N_RET_HEADS * RET_DIM
D_FF = 5632
IN_WIDTH = 5632
PLE_DIM = 256
PAST_LEN = 16384
ROPE_BASE = 10000.0
NORM_EPS = 1e-6
ATTN_SCALE = HEAD_DIM ** -0.5
RET_K_SCALE = RET_DIM ** -0.5

COL_RQ, COL_RK, COL_RV, COL_RG = 0, RET_WIDTH, 2 * RET_WIDTH, 3 * RET_WIDTH
COL_AQ = 4 * RET_WIDTH
COL_AK = COL_AQ + ATTN_WIDTH
COL_AV = COL_AK + KV_WIDTH
IN_TILE = 512
IN_TILES = IN_WIDTH // IN_TILE
IN_ROT = (ATTN_WIDTH + 2 * KV_WIDTH) // IN_TILE

VMEM_LIMIT = 56 * 1024 * 1024

F32 = jnp.float32
BF16 = jnp.bfloat16


def _params(*sem):
    return pltpu.CompilerParams(dimension_semantics=sem, vmem_limit_bytes=VMEM_LIMIT)


def _rms(x, g):
    return x * lax.rsqrt(jnp.mean(x * x, axis=-1, keepdims=True) + NORM_EPS) * g


def _dot(a, b):
    return jnp.dot(a, b, preferred_element_type=F32)


def _dot_nt(a, b):
    return lax.dot_general(a, b, (((1,), (1,)), ((), ())), preferred_element_type=F32)


def _dot_tn(a, b):
    return lax.dot_general(a, b, (((0,), (0,)), ((), ())), preferred_element_type=F32)


def _whole(a):
    return pl.BlockSpec(a.shape, lambda *_: (0,) * a.ndim)


def _in_proj_kernel(x_hbm, xs_ref, g_ref, w_ref, wg_ref, wu_ref, wd_ref, wo_ref, wpg_ref, wple_ref,
                    z_ref, zs_ref, wg_o, wu_o, wd_o, wo_o, wpg_o, wple_o,
                    xbuf, xsem, a_ref, wres):
    i, j = pl.program_id(0), pl.program_id(1)

    def x_copy(tile):
        return pltpu.make_async_copy(x_hbm.at[pl.ds(tile * IN_ROWS, IN_ROWS)], xbuf, xsem)

    @pl.when((i == 0) & (j == 0))
    def _():
        x_copy(0).start()
        a_ref[IN_ROWS:] = _rms(xs_ref[...], g_ref[...]).astype(BF16)

    @pl.when(j == 0)
    def _():
        x_copy(i).wait()
        a_ref[:IN_ROWS] = _rms(xbuf[...], g_ref[...]).astype(BF16)

        @pl.when(i + 1 < pl.num_programs(0))
        def _():
            x_copy(i + 1).start()

    def cast_weights():
        for src, dst in ((wg_ref, wg_o), (wu_ref, wu_o), (wd_ref, wd_o), (wo_ref, wo_o), (wpg_ref, wpg_o),
                         (wple_ref, wple_o)):
            dst[...] = src[...].astype(BF16)

    @pl.when(i == 0)
    def _():
        wres[j] = w_ref[...].astype(BF16)
        y = _dot(a_ref[...], wres[j])
        z_ref[...] = y[:IN_ROWS].astype(z_ref.dtype)
        zs_ref[...] = y[IN_ROWS:]
        cast_weights()

    @pl.when(i > 0)
    def _():
        z_ref[...] = _dot(a_ref[:IN_ROWS], wres[j]).astype(z_ref.dtype)
        cast_weights()


IN_ROWS = 1024
CAST_SPLIT = 8
FF_TILE = 512


def _in_proj(x, xs, g, w_in, w_gate, w_up, w_down, w_out, w_pg, w_ple):
    m = x.shape[0]
    ni = m // IN_ROWS
    steps = ni * IN_TILES
    assert m % IN_ROWS == 0 and FF_TILE == IN_TILE and D_FF == IN_WIDTH
    gu_rows, d_rows = D_MODEL // ni, D_FF // steps
    sq_rows, ple_rows = D_MODEL // (ni * CAST_SPLIT), PLE_DIM // ni
    assert gu_rows * ni == D_MODEL and d_rows * steps == D_FF and sq_rows % 16 == 0 and ple_rows % 16 == 0
    sq_idx = lambda i, j: (i * CAST_SPLIT + jnp.minimum(j, CAST_SPLIT - 1), 0)
    first = lambda i, j: jnp.where(i == 0, j, IN_TILES - 1)
    return pl.pallas_call(
        _in_proj_kernel,
        grid=(ni, IN_TILES),
        in_specs=[
            pl.BlockSpec(memory_space=pl.ANY),
            pl.BlockSpec(xs.shape, lambda i, j: (0, 0)),
            pl.BlockSpec((1, D_MODEL), lambda i, j: (0, 0)),
            pl.BlockSpec((D_MODEL, IN_TILE), lambda i, j: (0, (first(i, j) + IN_ROT) % IN_TILES)),
            pl.BlockSpec((gu_rows, FF_TILE), lambda i, j: (i, j)),
            pl.BlockSpec((gu_rows, FF_TILE), lambda i, j: (i, j)),
            pl.BlockSpec((d_rows, D_MODEL), lambda i, j: (i * IN_TILES + j, 0)),
            pl.BlockSpec((sq_rows, D_MODEL), sq_idx),
            pl.BlockSpec((sq_rows, D_MODEL), sq_idx),
            pl.BlockSpec((ple_rows, D_MODEL), lambda i, j: (i, 0)),
        ],
        out_specs=[
            pl.BlockSpec((IN_ROWS, IN_TILE), lambda i, j: (i, j)),
            pl.BlockSpec((xs.shape[0], IN_TILE), lambda i, j: (0, first(i, j))),
            pl.BlockSpec((None, gu_rows, FF_TILE), lambda i, j: (j, i, 0)),
            pl.BlockSpec((None, gu_rows, FF_TILE), lambda i, j: (j, i, 0)),
            pl.BlockSpec((d_rows, D_MODEL), lambda i, j: (i * IN_TILES + j, 0)),
            pl.BlockSpec((sq_rows, D_MODEL), sq_idx),
            pl.BlockSpec((sq_rows, D_MODEL), sq_idx),
            pl.BlockSpec((ple_rows, D_MODEL), lambda i, j: (i, 0)),
        ],
        out_shape=[
            jax.ShapeDtypeStruct((m, IN_WIDTH), BF16),
            jax.ShapeDtypeStruct((xs.shape[0], IN_WIDTH), F32),
            jax.ShapeDtypeStruct((D_FF // FF_TILE, D_MODEL, FF_TILE), BF16),
            jax.ShapeDtypeStruct((D_FF // FF_TILE, D_MODEL, FF_TILE), BF16),
            jax.ShapeDtypeStruct((D_FF, D_MODEL), BF16),
            jax.ShapeDtypeStruct((D_MODEL, D_MODEL), BF16),
            jax.ShapeDtypeStruct((D_MODEL, D_MODEL), BF16),
            jax.ShapeDtypeStruct((PLE_DIM, D_MODEL), BF16),
        ],
        scratch_shapes=[
            pltpu.VMEM((IN_ROWS, D_MODEL), F32),
            pltpu.SemaphoreType.DMA(()),
            pltpu.VMEM((IN_ROWS + xs.shape[0], D_MODEL), BF16),
            pltpu.VMEM((IN_TILES, D_MODEL, IN_TILE), BF16),
        ],
        compiler_params=_params("arbitrary", "arbitrary"),
        name="in_proj",
    )(x, xs, g, w_in, w_gate, w_up, w_down, w_out, w_pg, w_ple)


MIX_ROWS = 256
RET_CHUNK = 256
LOG2E = 1.4426950408889634


def _rotary(x, cos2, sin2):
    return x * cos2 + pltpu.roll(x, RET_DIM // 2, 1) * sin2


def _mix_tile(z_ref, kp_ref, vp_ref, kc_ref, vc_ref, qg_ref, kg_ref, sink_ref, cos_ref, sin_ref, qdec_ref, kdec_ref,
              cdec_ref, rg_ref, mix_ref, kwin_ref, vwin_ref, sout_ref, s_ref, first):
    qg = qg_ref[...] * (ATTN_SCALE * LOG2E)
    kg = kg_ref[...]
    row = lax.broadcasted_iota(jnp.int32, (BLK, 2 * BLK), 0)
    col = lax.broadcasted_iota(jnp.int32, (BLK, 2 * BLK), 1)
    cur_ok = (col >= BLK) & ((col - BLK) <= row)
    prev_ok = (col < BLK) & (col >= row)
    kprev = kp_ref[...]
    vprev = vp_ref[...]
    for blk in range(MIX_ROWS // BLK):
        rows = slice(blk * BLK, (blk + 1) * BLK)
        mask = cur_ok | (prev_ok & jnp.logical_not(first)) if blk == 0 else cur_ok | prev_ok
        k = z_ref[rows, COL_AK:COL_AK + KV_WIDTH].astype(F32)
        kn = [_rms(k[:, h * HEAD_DIM:(h + 1) * HEAD_DIM], kg) for h in range(N_KV_HEADS)]
        kcur = jnp.concatenate(kn, axis=1).astype(BF16)
        vcur = z_ref[rows, COL_AV:COL_AV + KV_WIDTH]
        for kh in range(N_KV_HEADS):
            sl = slice(kh * HEAD_DIM, (kh + 1) * HEAD_DIM)
            k_cat = jnp.concatenate([kprev[:, sl], kcur[:, sl]], axis=0)
            v_cat = jnp.concatenate([vprev[:, sl], vcur[:, sl]], axis=0)
            for g in range(GQA_GROUP):
                h = kh * GQA_GROUP + g
                q = z_ref[rows, COL_AQ + h * HEAD_DIM:COL_AQ + (h + 1) * HEAD_DIM].astype(F32)
                s = jnp.where(mask, _dot_nt(_rms(q, qg).astype(BF16), k_cat), -jnp.inf)
                sink = sink_ref[h] * LOG2E
                m = jnp.maximum(jnp.max(s, axis=-1, keepdims=True), sink)
                e = jnp.exp2(s - m)
                denom = jnp.sum(e, axis=-1, keepdims=True) + jnp.exp2(sink - m)
                mix_ref[rows, h * HEAD_DIM:(h + 1) * HEAD_DIM] = (
                    _dot(e.astype(BF16), v_cat) / denom).astype(mix_ref.dtype)
        kprev, vprev = kcur, vcur
    kwin_ref[0] = jnp.concatenate(kn, axis=1)
    kc_ref[...] = kcur
    vc_ref[...] = vcur
    vwin_ref[0] = vcur.astype(F32)

    tril = (lax.broadcasted_iota(jnp.int32, (RET_CHUNK, RET_CHUNK), 0)
            >= lax.broadcasted_iota(jnp.int32, (RET_CHUNK, RET_CHUNK), 1))
    for h in range(N_RET_HEADS):
        hs = slice(h * RET_DIM, (h + 1) * RET_DIM)
        col_of = lambda c: slice(c + h * RET_DIM, c + (h + 1) * RET_DIM)
        state = s_ref[h]
        for c in range(MIX_ROWS // RET_CHUNK):
            rows = slice(c * RET_CHUNK, (c + 1) * RET_CHUNK)
            cos2, sin2 = cos_ref[rows, :], sin_ref[rows, :]
            q = (_rotary(z_ref[rows, col_of(COL_RQ)].astype(F32), cos2, sin2) * qdec_ref[:, hs]).astype(BF16)
            k = (_rotary(z_ref[rows, col_of(COL_RK)].astype(F32), cos2, sin2) * kdec_ref[:, hs]).astype(BF16)
            v = z_ref[rows, col_of(COL_RV)]
            sc = jnp.where(tril, _dot_nt(q, k), 0.0)
            o = _dot(jnp.concatenate([sc.astype(BF16), q], axis=1),
                     jnp.concatenate([v, state.astype(BF16)], axis=0))
            state = (state + _dot_tn(k, v)) * cdec_ref[h]
            o = o * lax.rsqrt(jnp.mean(o * o, axis=-1, keepdims=True) + NORM_EPS)
            gate = z_ref[rows, col_of(COL_RG)].astype(F32)
            mix_ref[rows, ATTN_WIDTH + h * RET_DIM:ATTN_WIDTH + (h + 1) * RET_DIM] = (
                o * rg_ref[:, hs] * (gate * jax.nn.sigmoid(gate))).astype(mix_ref.dtype)
        s_ref[h] = state
        sout_ref[0, h] = state


def _mixer_out_kernel(nsteps, tiles, groups, z_ref, qg_ref, kg_ref, sink_ref, cos_ref, sin_ref,
                      qdec_ref, kdec_ref, cdec_ref, rg_ref, h_ref, hs_ref, w_ref,
                      zs_ref, kbuf_ref, vbuf_ref, s0_ref, sinkc_ref, coss_ref, sins_ref, spread_ref, gamma_ref,
                      o_ref, os_ref, kwin_ref, vwin_ref, sout_ref, kout_ref, vout_ref, ssout_ref,
                      mix_a, mix_b, s_ref, mixs_scr, kc_a, vc_a, kc_b, vc_b):
    t = pl.program_id(0)
    per_group = tiles // groups

    @pl.when(t % nsteps == 0)
    def _():
        s_ref[...] = jnp.zeros_like(s_ref)

    @pl.when(t == 0)
    def _():
        kc_b[...] = jnp.zeros_like(kc_b)
        vc_b[...] = jnp.zeros_like(vc_b)

    def mix(dst):
        prev, cur = ((kc_b, vc_b), (kc_a, vc_a)) if dst is mix_a else ((kc_a, vc_a), (kc_b, vc_b))
        _mix_tile(z_ref, prev[0], prev[1], cur[0], cur[1], qg_ref, kg_ref, sink_ref, cos_ref, sin_ref, qdec_ref, kdec_ref,
                  cdec_ref, rg_ref, dst, kwin_ref, vwin_ref, sout_ref, s_ref, t % nsteps == 0)

    def project(src):
        o_ref[...] = h_ref[...] + _dot(src[...], w_ref[...])

    @pl.when(t == 0)
    def _():
        mix(mix_a)

    @pl.when((t > 0) & (t < tiles) & (t % 2 == 0))
    def _():
        project(mix_b)
        mix(mix_a)

    @pl.when((t < tiles) & (t % 2 == 1))
    def _():
        project(mix_a)
        mix(mix_b)

    @pl.when((t % per_group == per_group - 1) & (t < tiles))
    def _():
        g = t // per_group

        def put_mix(b, attn, ret):
            heads = [x[r:r + 1] for x in (attn, ret) for r in range(x.shape[0])]
            mixs_scr[g, b:b + 1, :] = jnp.concatenate(heads, axis=1)

        _sample_group(zs_ref, kbuf_ref, vbuf_ref, s0_ref, qg_ref, kg_ref, sinkc_ref, coss_ref, sins_ref,
                      spread_ref, gamma_ref, rg_ref, put_mix, kout_ref, vout_ref, ssout_ref)

    @pl.when(t == tiles)
    def _():
        project(mix_b if tiles % 2 == 0 else mix_a)
        mixs = jnp.concatenate([mixs_scr[g, :SAMPLE_GROUP, :] for g in range(groups)], axis=0)
        os_ref[...] = hs_ref[...] + _dot(mixs.astype(BF16), w_ref[...])


def _rope_tables(pos):
    half = RET_DIM // 2
    inv = ROPE_BASE ** (-np.arange(half, dtype=np.float64) / half)
    ang = np.asarray(pos, np.float64)[:, None] * inv[None, :]
    cos, sin = np.cos(ang), np.sin(ang)
    return (np.concatenate([cos, cos], axis=-1).astype(np.float32),
            np.concatenate([-sin, sin], axis=-1).astype(np.float32))


def _ret_log_decay():
    return np.log1p(-np.exp2(-5.0 - np.arange(N_RET_HEADS, dtype=np.float64)))


def _mixer_out(z, h, hs, zs, k_buf, v_buf, s0, w_out, qg, kg, sinks, rg, batch, seq):
    nsteps = seq // MIX_ROWS
    tiles = batch * nsteps
    nb = hs.shape[0]
    groups = nb // SAMPLE_GROUP
    assert nb % SAMPLE_GROUP == 0 and tiles % groups == 0 and SAMPLE_GROUP <= SUBLANES
    cache_rows = WINDOW * N_KV_HEADS
    cos2, sin2 = _rope_tables(np.arange(seq))
    cos_s, sin_s = _rope_tables(PAST_LEN + np.arange(1))
    lg = _ret_log_decay()
    steps = np.arange(RET_CHUNK, dtype=np.float64)[:, None, None] + 1.0
    lanes = lambda t: np.broadcast_to(t, (RET_CHUNK, N_RET_HEADS, RET_DIM)).reshape(RET_CHUNK, RET_WIDTH).astype(np.float32)
    q_dec = lanes(np.exp(lg[None, :, None] * steps))
    k_dec = lanes(np.exp(-lg[None, :, None] * steps) * RET_K_SCALE)
    chunk_dec = np.exp(lg * RET_CHUNK).astype(np.float32)
    gamma = np.exp(lg).astype(np.float32)
    spread = np.kron(np.eye(N_RET_HEADS), np.ones((1, RET_DIM))).astype(np.float32)
    sink_col = sinks.reshape(N_ATTN_HEADS, 1)
    smem = pl.BlockSpec(memory_space=pltpu.SMEM)
    win = jax.ShapeDtypeStruct((batch, BLK, KV_WIDTH), F32)
    blks = MIX_ROWS // BLK
    tile = lambda t: jnp.minimum(t, tiles - 1)
    prev_blk = lambda t: tile(t) * blks - jnp.minimum(tile(t) % nsteps, 1)
    behind = lambda t: (jnp.maximum(t - 1, 0), 0)
    seq_of = lambda t: tile(t) // nsteps
    group_of = lambda t: tile(t) // (tiles // groups)
    sample = lambda *shape: pl.BlockSpec((SAMPLE_GROUP,) + shape, lambda t: (group_of(t),) + (0,) * len(shape))
    cache, state = sample(cache_rows, HEAD_DIM), sample(N_RET_HEADS, RET_DIM, RET_DIM)
    cache_shape = jax.ShapeDtypeStruct((nb, cache_rows, HEAD_DIM), F32)
    y, ys, k_win, v_win, s_new, ks, vs, ss = pl.pallas_call(
        functools.partial(_mixer_out_kernel, nsteps, tiles, groups),
        grid=(tiles + 1,),
        in_specs=[
            pl.BlockSpec((MIX_ROWS, IN_WIDTH), lambda t: (tile(t), 0)),
            _whole(qg), _whole(kg), smem,
            pl.BlockSpec((MIX_ROWS, RET_DIM), lambda t: (tile(t) % nsteps, 0)),
            pl.BlockSpec((MIX_ROWS, RET_DIM), lambda t: (tile(t) % nsteps, 0)),
            _whole(q_dec), _whole(k_dec), smem, _whole(rg),
            pl.BlockSpec((MIX_ROWS, D_MODEL), behind), _whole(hs), _whole(w_out),
            sample(Z_HEADS, HEAD_DIM), cache, cache, state,
            _whole(sink_col), _whole(cos_s), _whole(sin_s), _whole(spread), smem,
        ],
        out_specs=[
            pl.BlockSpec((MIX_ROWS, D_MODEL), behind), _whole(hs),
            pl.BlockSpec((1, BLK, KV_WIDTH), lambda t: (seq_of(t), 0, 0)),
            pl.BlockSpec((1, BLK, KV_WIDTH), lambda t: (seq_of(t), 0, 0)),
            pl.BlockSpec((1, N_RET_HEADS, RET_DIM, RET_DIM), lambda t: (seq_of(t), 0, 0, 0)),
            cache, cache, state,
        ],
        out_shape=[jax.ShapeDtypeStruct(h.shape, F32), jax.ShapeDtypeStruct(hs.shape, F32), win, win,
                   jax.ShapeDtypeStruct((batch, N_RET_HEADS, RET_DIM, RET_DIM), F32),
                   cache_shape, cache_shape, jax.ShapeDtypeStruct(s0.shape, F32)],
        scratch_shapes=[pltpu.VMEM((MIX_ROWS, D_MODEL), BF16), pltpu.VMEM((MIX_ROWS, D_MODEL), BF16),
                        pltpu.VMEM((N_RET_HEADS, RET_DIM, RET_DIM), F32),
                        pltpu.VMEM((groups, SUBLANES, D_MODEL), F32)]
                       + [pltpu.VMEM((BLK, KV_WIDTH), BF16)] * 4,
        compiler_params=_params("arbitrary"),
        name="mixer_out",
    )(z, qg, kg, sinks, cos2, sin2, q_dec, k_dec, chunk_dec, rg, h, hs, w_out,
      zs.reshape(nb, Z_HEADS, HEAD_DIM), k_buf.reshape(nb, cache_rows, HEAD_DIM),
      v_buf.reshape(nb, cache_rows, HEAD_DIM), s0, sink_col, cos_s, sin_s, spread, gamma)
    window = (nb, WINDOW, N_KV_HEADS, HEAD_DIM)
    return y, ys, k_win, v_win, s_new, ks.reshape(window), vs.reshape(window), ss


SAMPLE_GROUP = 4
Z_HEADS = IN_WIDTH // HEAD_DIM
ROW_RQ, ROW_RK, ROW_RV, ROW_RG, ROW_AQ, ROW_AK, ROW_AV = (
    c // HEAD_DIM for c in (COL_RQ, COL_RK, COL_RV, COL_RG, COL_AQ, COL_AK, COL_AV))
SUBLANES = 8


def _sample_group(z_ref, kbuf_ref, vbuf_ref, s0_ref, qg_ref, kg_ref, sink_ref, cos_ref, sin_ref,
                  spread_ref, gamma_ref, rg_ref, put_mix, kout_ref, vout_ref, s_out_ref):
    rows = WINDOW * N_KV_HEADS
    qg = qg_ref[...]
    kg = kg_ref[...]
    sink = sink_ref[...]
    head = lax.broadcasted_iota(jnp.int32, (N_ATTN_HEADS, HEAD_DIM), 0)
    first = head < GQA_GROUP
    own = ((lax.broadcasted_iota(jnp.int32, (N_ATTN_HEADS, rows), 1) % N_KV_HEADS)
           == (lax.broadcasted_iota(jnp.int32, (N_ATTN_HEADS, rows), 0) // GQA_GROUP))
    row = lax.broadcasted_iota(jnp.int32, (rows, HEAD_DIM), 0)
    cos2 = cos_ref[...]
    sin2 = sin_ref[...]
    spread = spread_ref[...]
    for b in range(SAMPLE_GROUP):
        part = lambda r, n: z_ref[b, r:r + n, :]
        qn = _rms(part(ROW_AQ, N_ATTN_HEADS), qg) * ATTN_SCALE
        kn = _rms(part(ROW_AK, N_KV_HEADS), kg)
        vn = part(ROW_AV, N_KV_HEADS)
        kbuf = kbuf_ref[b]
        vbuf = vbuf_ref[b]
        s = jnp.where(own, _dot_nt(qn.astype(BF16), kbuf.astype(BF16)), -jnp.inf)
        k_sel = jnp.where(first, kn[0:1], kn[1:2])
        v_sel = jnp.where(first, vn[0:1], vn[1:2])
        s_new = jnp.sum(qn * k_sel, axis=-1, keepdims=True)
        m = jnp.maximum(jnp.maximum(jnp.max(s, axis=-1, keepdims=True), s_new), sink)
        e = jnp.exp(s - m)
        e_new = jnp.exp(s_new - m)
        denom = jnp.sum(e, axis=-1, keepdims=True) + e_new + jnp.exp(sink - m)
        o_attn = (_dot(e.astype(BF16), vbuf.astype(BF16)) + e_new * v_sel) / denom
        for buf, new, out in ((kbuf, kn, kout_ref), (vbuf, vn, vout_ref)):
            shifted = pltpu.roll(buf, rows - N_KV_HEADS, 0)
            out[b] = jnp.where(row == rows - 2, new[0:1], jnp.where(row == rows - 1, new[1:2], shifted))
        q_cols = _dot_tn(_rotary(part(ROW_RQ, N_RET_HEADS), cos2, sin2), spread)
        k_cols = _dot_tn(_rotary(part(ROW_RK, N_RET_HEADS), cos2, sin2) * RET_K_SCALE, spread)
        v = part(ROW_RV, N_RET_HEADS)
        gate = part(ROW_RG, N_RET_HEADS)
        heads = []
        for h in range(N_RET_HEADS):
            hs = slice(h * RET_DIM, (h + 1) * RET_DIM)
            s_next = s0_ref[b, h] * gamma_ref[h] + k_cols[:, hs] * v[h:h + 1, :]
            s_out_ref[b, h] = s_next
            o = jnp.sum(q_cols[:, hs] * s_next, axis=0, keepdims=True)
            heads.append(o * lax.rsqrt(jnp.mean(o * o, axis=-1, keepdims=True) + NORM_EPS) * rg_ref[:, hs])
        o_ret = jnp.concatenate(heads, axis=0) * (gate * jax.nn.sigmoid(gate))
        put_mix(b, o_attn, o_ret)


def _ffn_kernel(h_ref, hs_ref, g_ref, wg_ref, wu_ref, wd_ref, o_ref, os_ref, f_ref):
    i, j = pl.program_id(0), pl.program_id(1)
    tm = h_ref.shape[0]

    @pl.when(j == 0)
    def _():
        h = h_ref[...]
        f_ref[:tm] = _rms(h, g_ref[...]).astype(BF16)
        o_ref[...] = h

    @pl.when((i == 0) & (j == 0))
    def _():
        hs = hs_ref[...]
        f_ref[tm:] = _rms(hs, g_ref[...]).astype(BF16)
        os_ref[...] = hs

    def swiglu(f):
        gate = _dot(f, wg_ref[...])
        act = (gate * jax.nn.sigmoid(gate) * _dot(f, wu_ref[...])).astype(BF16)
        return _dot(act, wd_ref[...])

    @pl.when(i == 0)
    def _():
        y = swiglu(f_ref[...])
        o_ref[...] += y[:tm]
        os_ref[...] += y[tm:]

    @pl.when(i > 0)
    def _():
        o_ref[...] += swiglu(f_ref[:tm])


def _ffn(h, hs, g, wg, wu, wd, tm):
    m = h.shape[0]
    row = pl.BlockSpec((tm, D_MODEL), lambda i, j: (i, 0))
    return pl.pallas_call(
        _ffn_kernel,
        grid=(m // tm, D_FF // FF_TILE),
        in_specs=[
            row, _whole(hs), _whole(g),
            pl.BlockSpec((None, D_MODEL, FF_TILE), lambda i, j: (j, 0, 0)),
            pl.BlockSpec((None, D_MODEL, FF_TILE), lambda i, j: (j, 0, 0)),
            pl.BlockSpec((FF_TILE, D_MODEL), lambda i, j: (j, 0)),
        ],
        out_specs=[row, _whole(hs)],
        out_shape=[jax.ShapeDtypeStruct((m, D_MODEL), F32), jax.ShapeDtypeStruct(hs.shape, F32)],
        scratch_shapes=[pltpu.VMEM((tm + hs.shape[0], D_MODEL), BF16)],
        compiler_params=_params("arbitrary", "arbitrary"),
        name="ffn",
    )(h, hs, g, wg, wu, wd)


def _ple_kernel(h_ref, p_ref, hs_ref, ps_ref, g_ref, wp_ref, wg_ref, o_ref, os_ref):
    tm = h_ref.shape[0]

    def embed(h, p):
        gate = jax.nn.sigmoid(_dot(_rms(h, g_ref[...]).astype(BF16), wg_ref[...]))
        return h + _dot(p.astype(BF16), wp_ref[...]) * gate

    @pl.when(pl.program_id(0) == 0)
    def _():
        y = embed(jnp.concatenate([h_ref[...], hs_ref[...]], axis=0),
                  jnp.concatenate([p_ref[...], ps_ref[...]], axis=0))
        o_ref[...] = y[:tm]
        os_ref[...] = y[tm:]

    @pl.when(pl.program_id(0) > 0)
    def _():
        o_ref[...] = embed(h_ref[...], p_ref[...])


def _ple(h, p, hs, ps, g, w_ple, w_pg, tm):
    m = h.shape[0]
    row = pl.BlockSpec((tm, D_MODEL), lambda i: (i, 0))
    return pl.pallas_call(
        _ple_kernel,
        grid=(m // tm,),
        in_specs=[row, pl.BlockSpec((tm, PLE_DIM), lambda i: (i, 0)), _whole(hs), _whole(ps), _whole(g),
                  _whole(w_ple), _whole(w_pg)],
        out_specs=[row, _whole(hs)],
        out_shape=[jax.ShapeDtypeStruct((m, D_MODEL), F32), jax.ShapeDtypeStruct(hs.shape, F32)],
        compiler_params=_params("arbitrary"),
        name="ple",
    )(h, p, hs, ps, g, w_ple, w_pg)


PLE_ROWS = 512
FFN_ROWS = 1024


def _dense_tail(h, p, hs, ps, w):
    h, hs = _ffn(h, hs, w["ffn_norm_g"], w["w_gate"], w["w_up"], w["w_down"], FFN_ROWS)
    return _ple(h, p, hs, ps, w["ple_norm_g"], w["w_ple"], w["w_ple_gate"], PLE_ROWS)


def kernel(x_prompt, x_sample, cache_k_win, cache_v_win, state_ret, p_prompt, p_sample,
           attn_norm_g, w_in, q_norm_g, k_norm_g, attn_sinks, ret_out_g, w_out,
           ffn_norm_g, w_gate, w_up, w_down, ple_norm_g, w_ple, w_ple_gate):
    assert w_in.shape[0] == 1 and x_sample.shape[1] == 1 and cache_k_win.shape[2] == WINDOW
    y_p, y_s = x_prompt, x_sample
    outs = [[] for _ in range(6)]
    for l in range(w_in.shape[0]):
        w = {
            "attn_norm_g": attn_norm_g[l].reshape(1, D_MODEL),
            "q_norm_g": q_norm_g[l].reshape(1, HEAD_DIM),
            "k_norm_g": k_norm_g[l].reshape(1, HEAD_DIM),
            "attn_sinks": attn_sinks[l],
            "ret_out_g": ret_out_g[l].reshape(1, RET_WIDTH),
            "ffn_norm_g": ffn_norm_g[l].reshape(1, D_MODEL),
            "ple_norm_g": ple_norm_g[l].reshape(1, D_MODEL),
        }
        batch, seq, _ = y_p.shape
        nb = y_s.shape[0]
        h_p = y_p.reshape(batch * seq, D_MODEL)
        h_s = y_s.reshape(nb, D_MODEL)
        z_p, z_s, *cast = _in_proj(h_p, h_s, w["attn_norm_g"], w_in[l], w_gate[l], w_up[l], w_down[l],
                                   w_out[l], w_ple_gate[l], w_ple[l])
        w.update(zip(("w_gate", "w_up", "w_down", "w_out", "w_ple_gate", "w_ple"), cast))
        h_p, h_s, kp, vp, sp, ks, vs, ss = _mixer_out(
            z_p, h_p, h_s, z_s, cache_k_win[l], cache_v_win[l], state_ret[l], w["w_out"], w["q_norm_g"],
            w["k_norm_g"], w["attn_sinks"], w["ret_out_g"], batch, seq)
        h_p, h_s = _dense_tail(h_p, p_prompt[l].reshape(batch * seq, PLE_DIM),
                               h_s, p_sample[l].reshape(nb, PLE_DIM), w)
        y_p, y_s = h_p.reshape(batch, seq, D_MODEL), h_s.reshape(nb, 1, D_MODEL)
        kp = kp.reshape(batch, BLK, N_KV_HEADS, HEAD_DIM)
        vp = vp.reshape(batch, BLK, N_KV_HEADS, HEAD_DIM)
        for acc, val in zip(outs, (kp, vp, sp, ks, vs, ss)):
            acc.append(val)
    kp, vp, sp, ks, vs, ss = (jnp.stack(o) for o in outs)
    return (y_p, y_s, kp, vp, sp, ks, vs, ss)
```

```python
import functools

import numpy as np
import jax
import jax.numpy as jnp
from jax import lax
from jax.experimental import pallas as pl
from jax.experimental.pallas import tpu as pltpu

D_MODEL = 2048
HEAD_DIM = 128
N_ATTN_HEADS = 8
N_KV_HEADS = 2
GQA_GROUP = N_ATTN_HEADS // N_KV_HEADS
ATTN_WIDTH = N_ATTN_HEADS * HEAD_DIM
KV_WIDTH = N_KV_HEADS * HEAD_DIM
WINDOW = 128
BLK = 128
N_RET_HEADS = 8
RET_DIM = 128
RET_WIDTH = N_RET_HEADS * RET_DIM
D_FF = 5632
IN_WIDTH = 5632
PLE_DIM = 256
PAST_LEN = 16384
ROPE_BASE = 10000.0
NORM_EPS = 1e-6
ATTN_SCALE = HEAD_DIM ** -0.5
RET_K_SCALE = RET_DIM ** -0.5

COL_RQ, COL_RK, COL_RV, COL_RG = 0, RET_WIDTH, 2 * RET_WIDTH, 3 * RET_WIDTH
COL_AQ = 4 * RET_WIDTH
COL_AK = COL_AQ + ATTN_WIDTH
COL_AV = COL_AK + KV_WIDTH
IN_TILE = 512
IN_TILES = IN_WIDTH // IN_TILE
IN_ROT = (ATTN_WIDTH + 2 * KV_WIDTH) // IN_TILE

VMEM_LIMIT = 56 * 1024 * 1024

F32 = jnp.float32
BF16 = jnp.bfloat16


def _params(*sem):
    return pltpu.CompilerParams(dimension_semantics=sem, vmem_limit_bytes=VMEM_LIMIT)


def _rms(x, g):
    return x * lax.rsqrt(jnp.mean(x * x, axis=-1, keepdims=True) + NORM_EPS) * g


def _dot(a, b):
    return jnp.dot(a, b, preferred_element_type=F32)


def _dot_nt(a, b):
    return lax.dot_general(a, b, (((1,), (1,)), ((), ())), preferred_element_type=F32)


def _dot_tn(a, b):
    return lax.dot_general(a, b, (((0,), (0,)), ((), ())), preferred_element_type=F32)


def _whole(a):
    return pl.BlockSpec(a.shape, lambda *_: (0,) * a.ndim)


def _in_proj_kernel(x_hbm, xs_ref, g_ref, w_ref, wg_ref, wu_ref, wd_ref, wo_ref, wpg_ref, wple_ref,
                    z_ref, zs_ref, wg_o, wu_o, wd_o, wo_o, wpg_o, wple_o,
                    xbuf, xsem, a_ref, wres):
    i, j = pl.program_id(0), pl.program_id(1)

    def x_copy(tile):
        return pltpu.make_async_copy(x_hbm.at[pl.ds(tile * IN_ROWS, IN_ROWS)], xbuf, xsem)

    @pl.when((i == 0) & (j == 0))
    def _():
        x_copy(0).start()
        a_ref[IN_ROWS:] = _rms(xs_ref[...], g_ref[...]).astype(BF16)

    @pl.when(j == 0)
    def _():
        x_copy(i).wait()
        a_ref[:IN_ROWS] = _rms(xbuf[...], g_ref[...]).astype(BF16)

        @pl.when(i + 1 < pl.num_programs(0))
        def _():
            x_copy(i + 1).start(priority=1)

    def cast_weights():
        for src, dst in ((wg_ref, wg_o), (wu_ref, wu_o), (wd_ref, wd_o), (wo_ref, wo_o), (wpg_ref, wpg_o),
                         (wple_ref, wple_o)):
            dst[...] = src[...].astype(BF16)

    @pl.when(i == 0)
    def _():
        wres[j] = w_ref[...].astype(BF16)
        y = _dot(a_ref[...], wres[j])
        z_ref[...] = y[:IN_ROWS].astype(z_ref.dtype)
        zs_ref[...] = y[IN_ROWS:]
        cast_weights()

    @pl.when(i > 0)
    def _():
        z_ref[...] = _dot(a_ref[:IN_ROWS], wres[j]).astype(z_ref.dtype)
        cast_weights()


IN_ROWS = 1024
CAST_SPLIT = 8
FF_TILE = 512


def _in_proj(x, xs, g, w_in, w_gate, w_up, w_down, w_out, w_pg, w_ple):
    m = x.shape[0]
    ni = m // IN_ROWS
    steps = ni * IN_TILES
    assert m % IN_ROWS == 0 and FF_TILE == IN_TILE and D_FF == IN_WIDTH
    gu_rows, d_rows = D_MODEL // ni, D_FF // steps
    sq_rows, ple_rows = D_MODEL // (ni * CAST_SPLIT), PLE_DIM // ni
    assert gu_rows * ni == D_MODEL and d_rows * steps == D_FF and sq_rows % 16 == 0 and ple_rows % 16 == 0
    sq_idx = lambda i, j: (i * CAST_SPLIT + jnp.minimum(j, CAST_SPLIT - 1), 0)
    first = lambda i, j: jnp.where(i == 0, j, IN_TILES - 1)
    return pl.pallas_call(
        _in_proj_kernel,
        grid=(ni, IN_TILES),
        in_specs=[
            pl.BlockSpec(memory_space=pl.ANY),
            pl.BlockSpec(xs.shape, lambda i, j: (0, 0)),
            pl.BlockSpec((1, D_MODEL), lambda i, j: (0, 0)),
            pl.BlockSpec((D_MODEL, IN_TILE), lambda i, j: (0, (first(i, j) + IN_ROT) % IN_TILES)),
            pl.BlockSpec((gu_rows, FF_TILE), lambda i, j: (i, j)),
            pl.BlockSpec((gu_rows, FF_TILE), lambda i, j: (i, j)),
            pl.BlockSpec((d_rows, D_MODEL), lambda i, j: (i * IN_TILES + j, 0)),
            pl.BlockSpec((sq_rows, D_MODEL), sq_idx),
            pl.BlockSpec((sq_rows, D_MODEL), sq_idx),
            pl.BlockSpec((ple_rows, D_MODEL), lambda i, j: (i, 0)),
        ],
        out_specs=[
            pl.BlockSpec((IN_ROWS, IN_TILE), lambda i, j: (i, j)),
            pl.BlockSpec((xs.shape[0], IN_TILE), lambda i, j: (0, first(i, j))),
            pl.BlockSpec((None, gu_rows, FF_TILE), lambda i, j: (j, i, 0)),
            pl.BlockSpec((None, gu_rows, FF_TILE), lambda i, j: (j, i, 0)),
            pl.BlockSpec((d_rows, D_MODEL), lambda i, j: (i * IN_TILES + j, 0)),
            pl.BlockSpec((sq_rows, D_MODEL), sq_idx),
            pl.BlockSpec((sq_rows, D_MODEL), sq_idx),
            pl.BlockSpec((ple_rows, D_MODEL), lambda i, j: (i, 0)),
        ],
        out_shape=[
            jax.ShapeDtypeStruct((m, IN_WIDTH), BF16),
            jax.ShapeDtypeStruct((xs.shape[0], IN_WIDTH), F32),
            jax.ShapeDtypeStruct((D_FF // FF_TILE, D_MODEL, FF_TILE), BF16),
            jax.ShapeDtypeStruct((D_FF // FF_TILE, D_MODEL, FF_TILE), BF16),
            jax.ShapeDtypeStruct((D_FF, D_MODEL), BF16),
            jax.ShapeDtypeStruct((D_MODEL, D_MODEL), BF16),
            jax.ShapeDtypeStruct((D_MODEL, D_MODEL), BF16),
            jax.ShapeDtypeStruct((PLE_DIM, D_MODEL), BF16),
        ],
        scratch_shapes=[
            pltpu.VMEM((IN_ROWS, D_MODEL), F32),
            pltpu.SemaphoreType.DMA(()),
            pltpu.VMEM((IN_ROWS + xs.shape[0], D_MODEL), BF16),
            pltpu.VMEM((IN_TILES, D_MODEL, IN_TILE), BF16),
        ],
        compiler_params=_params("arbitrary", "arbitrary"),
        name="in_proj",
    )(x, xs, g, w_in, w_gate, w_up, w_down, w_out, w_pg, w_ple)


MIX_ROWS = 256
RET_CHUNK = 256
LOG2E = 1.4426950408889634


def _rotary(x, cos2, sin2):
    return x * cos2 + pltpu.roll(x, RET_DIM // 2, 1) * sin2


def _mix_tile(z_ref, kp_ref, vp_ref, qg_ref, kg_ref, sink_ref, cos_ref, sin_ref, qdec_ref, kdec_ref,
              cdec_ref, rg_ref, mix_ref, kwin_ref, vwin_ref, sout_ref, s_ref, first):
    qg = qg_ref[...] * (ATTN_SCALE * LOG2E)
    kg = kg_ref[...]
    row = lax.broadcasted_iota(jnp.int32, (BLK, 2 * BLK), 0)
    col = lax.broadcasted_iota(jnp.int32, (BLK, 2 * BLK), 1)
    cur_ok = (col >= BLK) & ((col - BLK) <= row)
    prev_ok = (col < BLK) & (col >= row)
    kp = kp_ref[...].astype(F32)
    kprev = jnp.concatenate([_rms(kp[:, h * HEAD_DIM:(h + 1) * HEAD_DIM], kg) for h in range(N_KV_HEADS)],
                            axis=1).astype(BF16)
    vprev = vp_ref[...]
    for blk in range(MIX_ROWS // BLK):
        rows = slice(blk * BLK, (blk + 1) * BLK)
        mask = cur_ok | (prev_ok & jnp.logical_not(first)) if blk == 0 else cur_ok | prev_ok
        k = z_ref[rows, COL_AK:COL_AK + KV_WIDTH].astype(F32)
        kn = [_rms(k[:, h * HEAD_DIM:(h + 1) * HEAD_DIM], kg) for h in range(N_KV_HEADS)]
        kcur = jnp.concatenate(kn, axis=1).astype(BF16)
        vcur = z_ref[rows, COL_AV:COL_AV + KV_WIDTH]
        for kh in range(N_KV_HEADS):
            sl = slice(kh * HEAD_DIM, (kh + 1) * HEAD_DIM)
            k_cat = jnp.concatenate([kprev[:, sl], kcur[:, sl]], axis=0)
            v_cat = jnp.concatenate([vprev[:, sl], vcur[:, sl]], axis=0)
            for g in range(GQA_GROUP):
                h = kh * GQA_GROUP + g
                q = z_ref[rows, COL_AQ + h * HEAD_DIM:COL_AQ + (h + 1) * HEAD_DIM].astype(F32)
                s = jnp.where(mask, _dot_nt(_rms(q, qg).astype(BF16), k_cat), -jnp.inf)
                sink = sink_ref[h] * LOG2E
                m = jnp.maximum(jnp.max(s, axis=-1, keepdims=True), sink)
                e = jnp.exp2(s - m)
                denom = jnp.sum(e, axis=-1, keepdims=True) + jnp.exp2(sink - m)
                mix_ref[rows, h * HEAD_DIM:(h + 1) * HEAD_DIM] = (
                    _dot(e.astype(BF16), v_cat) / denom).astype(mix_ref.dtype)
        kprev, vprev = kcur, vcur
    kwin_ref[0] = jnp.concatenate(kn, axis=1)
    vwin_ref[0] = vcur.astype(F32)

    tril = (lax.broadcasted_iota(jnp.int32, (RET_CHUNK, RET_CHUNK), 0)
            >= lax.broadcasted_iota(jnp.int32, (RET_CHUNK, RET_CHUNK), 1))
    for h in range(N_RET_HEADS):
        hs = slice(h * RET_DIM, (h + 1) * RET_DIM)
        col_of = lambda c: slice(c + h * RET_DIM, c + (h + 1) * RET_DIM)
        state = s_ref[h]
        for c in range(MIX_ROWS // RET_CHUNK):
            rows = slice(c * RET_CHUNK, (c + 1) * RET_CHUNK)
            cos2, sin2 = cos_ref[rows, :], sin_ref[rows, :]
            q = (_rotary(z_ref[rows, col_of(COL_RQ)].astype(F32), cos2, sin2) * qdec_ref[:, hs]).astype(BF16)
            k = (_rotary(z_ref[rows, col_of(COL_RK)].astype(F32), cos2, sin2) * kdec_ref[:, hs]).astype(BF16)
            v = z_ref[rows, col_of(COL_RV)]
            sc = jnp.where(tril, _dot_nt(q, k), 0.0)
            o = _dot(jnp.concatenate([sc.astype(BF16), q], axis=1),
                     jnp.concatenate([v, state.astype(BF16)], axis=0))
            state = (state + _dot_tn(k, v)) * cdec_ref[h]
            o = o * lax.rsqrt(jnp.mean(o * o, axis=-1, keepdims=True) + NORM_EPS)
            gate = z_ref[rows, col_of(COL_RG)].astype(F32)
            mix_ref[rows, ATTN_WIDTH + h * RET_DIM:ATTN_WIDTH + (h + 1) * RET_DIM] = (
                o * rg_ref[:, hs] * (gate * jax.nn.sigmoid(gate))).astype(mix_ref.dtype)
        s_ref[h] = state
        sout_ref[0, h] = state


def _mixer_out_kernel(nsteps, tiles, groups, z_ref, kp_ref, vp_ref, qg_ref, kg_ref, sink_ref, cos_ref, sin_ref,
                      qdec_ref, kdec_ref, cdec_ref, rg_ref, h_ref, hs_ref, w_ref,
                      zs_ref, kbuf_ref, vbuf_ref, s0_ref, sinkc_ref, coss_ref, sins_ref, spread_ref, gamma_ref,
                      o_ref, os_ref, kwin_ref, vwin_ref, sout_ref, kout_ref, vout_ref, ssout_ref,
                      mix_a, mix_b, s_ref, mixs_scr):
    t = pl.program_id(0)
    per_group = tiles // groups

    @pl.when(t % nsteps == 0)
    def _():
        s_ref[...] = jnp.zeros_like(s_ref)

    def mix(dst):
        _mix_tile(z_ref, kp_ref, vp_ref, qg_ref, kg_ref, sink_ref, cos_ref, sin_ref, qdec_ref, kdec_ref,
                  cdec_ref, rg_ref, dst, kwin_ref, vwin_ref, sout_ref, s_ref, t % nsteps == 0)

    def project(src):
        o_ref[...] = h_ref[...] + _dot(src[...], w_ref[...])

    @pl.when(t == 0)
    def _():
        mix(mix_a)

    @pl.when((t > 0) & (t < tiles) & (t % 2 == 0))
    def _():
        project(mix_b)
        mix(mix_a)

    @pl.when((t < tiles) & (t % 2 == 1))
    def _():
        project(mix_a)
        mix(mix_b)

    @pl.when((t % per_group == per_group - 1) & (t < tiles))
    def _():
        g = t // per_group

        def put_mix(b, attn, ret):
            heads = [x[r:r + 1] for x in (attn, ret) for r in range(x.shape[0])]
            mixs_scr[g, b:b + 1, :] = jnp.concatenate(heads, axis=1)

        _sample_group(zs_ref, kbuf_ref, vbuf_ref, s0_ref, qg_ref, kg_ref, sinkc_ref, coss_ref, sins_ref,
                      spread_ref, gamma_ref, rg_ref, put_mix, kout_ref, vout_ref, ssout_ref)

    @pl.when(t == tiles)
    def _():
        project(mix_b if tiles % 2 == 0 else mix_a)
        mixs = jnp.concatenate([mixs_scr[g, :SAMPLE_GROUP, :] for g in range(groups)], axis=0)
        os_ref[...] = hs_ref[...] + _dot(mixs.astype(BF16), w_ref[...])


def _rope_tables(pos):
    half = RET_DIM // 2
    inv = ROPE_BASE ** (-np.arange(half, dtype=np.float64) / half)
    ang = np.asarray(pos, np.float64)[:, None] * inv[None, :]
    cos, sin = np.cos(ang), np.sin(ang)
    return (np.concatenate([cos, cos], axis=-1).astype(np.float32),
            np.concatenate([-sin, sin], axis=-1).astype(np.float32))


def _ret_log_decay():
    return np.log1p(-np.exp2(-5.0 - np.arange(N_RET_HEADS, dtype=np.float64)))


def _mixer_out(z, h, hs, zs, k_buf, v_buf, s0, w_out, qg, kg, sinks, rg, batch, seq):
    nsteps = seq // MIX_ROWS
    tiles = batch * nsteps
    nb = hs.shape[0]
    groups = nb // SAMPLE_GROUP
    assert nb % SAMPLE_GROUP == 0 and tiles % groups == 0 and SAMPLE_GROUP <= SUBLANES
    cache_rows = WINDOW * N_KV_HEADS
    cos2, sin2 = _rope_tables(np.arange(seq))
    cos_s, sin_s = _rope_tables(PAST_LEN + np.arange(1))
    lg = _ret_log_decay()
    steps = np.arange(RET_CHUNK, dtype=np.float64)[:, None, None] + 1.0
    lanes = lambda t: np.broadcast_to(t, (RET_CHUNK, N_RET_HEADS, RET_DIM)).reshape(RET_CHUNK, RET_WIDTH).astype(np.float32)
    q_dec = lanes(np.exp(lg[None, :, None] * steps))
    k_dec = lanes(np.exp(-lg[None, :, None] * steps) * RET_K_SCALE)
    chunk_dec = np.exp(lg * RET_CHUNK).astype(np.float32)
    gamma = np.exp(lg).astype(np.float32)
    spread = np.kron(np.eye(N_RET_HEADS), np.ones((1, RET_DIM))).astype(np.float32)
    sink_col = sinks.reshape(N_ATTN_HEADS, 1)
    smem = pl.BlockSpec(memory_space=pltpu.SMEM)
    win = jax.ShapeDtypeStruct((batch, BLK, KV_WIDTH), F32)
    blks = MIX_ROWS // BLK
    tile = lambda t: jnp.minimum(t, tiles - 1)
    prev_blk = lambda t: tile(t) * blks - jnp.minimum(tile(t) % nsteps, 1)
    behind = lambda t: (jnp.maximum(t - 1, 0), 0)
    seq_of = lambda t: tile(t) // nsteps
    group_of = lambda t: tile(t) // (tiles // groups)
    sample = lambda *shape: pl.BlockSpec((SAMPLE_GROUP,) + shape, lambda t: (group_of(t),) + (0,) * len(shape))
    cache, state = sample(cache_rows, HEAD_DIM), sample(N_RET_HEADS, RET_DIM, RET_DIM)
    cache_shape = jax.ShapeDtypeStruct((nb, cache_rows, HEAD_DIM), F32)
    y, ys, k_win, v_win, s_new, ks, vs, ss = pl.pallas_call(
        functools.partial(_mixer_out_kernel, nsteps, tiles, groups),
        grid=(tiles + 1,),
        in_specs=[
            pl.BlockSpec((MIX_ROWS, IN_WIDTH), lambda t: (tile(t), 0)),
            pl.BlockSpec((BLK, KV_WIDTH), lambda t: (prev_blk(t), COL_AK // KV_WIDTH)),
            pl.BlockSpec((BLK, KV_WIDTH), lambda t: (prev_blk(t), COL_AV // KV_WIDTH)),
            _whole(qg), _whole(kg), smem,
            pl.BlockSpec((MIX_ROWS, RET_DIM), lambda t: (tile(t) % nsteps, 0)),
            pl.BlockSpec((MIX_ROWS, RET_DIM), lambda t: (tile(t) % nsteps, 0)),
            _whole(q_dec), _whole(k_dec), smem, _whole(rg),
            pl.BlockSpec((MIX_ROWS, D_MODEL), behind), _whole(hs), _whole(w_out),
            sample(Z_HEADS, HEAD_DIM), cache, cache, state,
            _whole(sink_col), _whole(cos_s), _whole(sin_s), _whole(spread), smem,
        ],
        out_specs=[
            pl.BlockSpec((MIX_ROWS, D_MODEL), behind), _whole(hs),
            pl.BlockSpec((1, BLK, KV_WIDTH), lambda t: (seq_of(t), 0, 0)),
            pl.BlockSpec((1, BLK, KV_WIDTH), lambda t: (seq_of(t), 0, 0)),
            pl.BlockSpec((1, N_RET_HEADS, RET_DIM, RET_DIM), lambda t: (seq_of(t), 0, 0, 0)),
            cache, cache, state,
        ],
        out_shape=[jax.ShapeDtypeStruct(h.shape, F32), jax.ShapeDtypeStruct(hs.shape, F32), win, win,
                   jax.ShapeDtypeStruct((batch, N_RET_HEADS, RET_DIM, RET_DIM), F32),
                   cache_shape, cache_shape, jax.ShapeDtypeStruct(s0.shape, F32)],
        scratch_shapes=[pltpu.VMEM((MIX_ROWS, D_MODEL), BF16), pltpu.VMEM((MIX_ROWS, D_MODEL), BF16),
                        pltpu.VMEM((N_RET_HEADS, RET_DIM, RET_DIM), F32),
                        pltpu.VMEM((groups, SUBLANES, D_MODEL), F32)],
        compiler_params=_params("arbitrary"),
        name="mixer_out",
    )(z, z, z, qg, kg, sinks, cos2, sin2, q_dec, k_dec, chunk_dec, rg, h, hs, w_out,
      zs.reshape(nb, Z_HEADS, HEAD_DIM), k_buf.reshape(nb, cache_rows, HEAD_DIM),
      v_buf.reshape(nb, cache_rows, HEAD_DIM), s0, sink_col, cos_s, sin_s, spread, gamma)
    window = (nb, WINDOW, N_KV_HEADS, HEAD_DIM)
    return y, ys, k_win, v_win, s_new, ks.reshape(window), vs.reshape(window), ss


SAMPLE_GROUP = 4
Z_HEADS = IN_WIDTH // HEAD_DIM
ROW_RQ, ROW_RK, ROW_RV, ROW_RG, ROW_AQ, ROW_AK, ROW_AV = (
    c // HEAD_DIM for c in (COL_RQ, COL_RK, COL_RV, COL_RG, COL_AQ, COL_AK, COL_AV))
SUBLANES = 8


def _sample_group(z_ref, kbuf_ref, vbuf_ref, s0_ref, qg_ref, kg_ref, sink_ref, cos_ref, sin_ref,
                  spread_ref, gamma_ref, rg_ref, put_mix, kout_ref, vout_ref, s_out_ref):
    rows = WINDOW * N_KV_HEADS
    qg = qg_ref[...]
    kg = kg_ref[...]
    sink = sink_ref[...]
    head = lax.broadcasted_iota(jnp.int32, (N_ATTN_HEADS, HEAD_DIM), 0)
    first = head < GQA_GROUP
    own = ((lax.broadcasted_iota(jnp.int32, (N_ATTN_HEADS, rows), 1) % N_KV_HEADS)
           == (lax.broadcasted_iota(jnp.int32, (N_ATTN_HEADS, rows), 0) // GQA_GROUP))
    row = lax.broadcasted_iota(jnp.int32, (rows, HEAD_DIM), 0)
    cos2 = cos_ref[...]
    sin2 = sin_ref[...]
    spread = spread_ref[...]
    for b in range(SAMPLE_GROUP):
        part = lambda r, n: z_ref[b, r:r + n, :]
        qn = _rms(part(ROW_AQ, N_ATTN_HEADS), qg) * ATTN_SCALE
        kn = _rms(part(ROW_AK, N_KV_HEADS), kg)
        vn = part(ROW_AV, N_KV_HEADS)
        kbuf = kbuf_ref[b]
        vbuf = vbuf_ref[b]
        s = jnp.where(own, _dot_nt(qn.astype(BF16), kbuf.astype(BF16)), -jnp.inf)
        k_sel = jnp.where(first, kn[0:1], kn[1:2])
        v_sel = jnp.where(first, vn[0:1], vn[1:2])
        s_new = jnp.sum(qn * k_sel, axis=-1, keepdims=True)
        m = jnp.maximum(jnp.maximum(jnp.max(s, axis=-1, keepdims=True), s_new), sink)
        e = jnp.exp(s - m)
        e_new = jnp.exp(s_new - m)
        denom = jnp.sum(e, axis=-1, keepdims=True) + e_new + jnp.exp(sink - m)
        o_attn = (_dot(e.astype(BF16), vbuf.astype(BF16)) + e_new * v_sel) / denom
        for buf, new, out in ((kbuf, kn, kout_ref), (vbuf, vn, vout_ref)):
            shifted = pltpu.roll(buf, rows - N_KV_HEADS, 0)
            out[b] = jnp.where(row == rows - 2, new[0:1], jnp.where(row == rows - 1, new[1:2], shifted))
        q_cols = _dot_tn(_rotary(part(ROW_RQ, N_RET_HEADS), cos2, sin2), spread)
        k_cols = _dot_tn(_rotary(part(ROW_RK, N_RET_HEADS), cos2, sin2) * RET_K_SCALE, spread)
        v = part(ROW_RV, N_RET_HEADS)
        gate = part(ROW_RG, N_RET_HEADS)
        heads = []
        for h in range(N_RET_HEADS):
            hs = slice(h * RET_DIM, (h + 1) * RET_DIM)
            s_next = s0_ref[b, h] * gamma_ref[h] + k_cols[:, hs] * v[h:h + 1, :]
            s_out_ref[b, h] = s_next
            o = jnp.sum(q_cols[:, hs] * s_next, axis=0, keepdims=True)
            heads.append(o * lax.rsqrt(jnp.mean(o * o, axis=-1, keepdims=True) + NORM_EPS) * rg_ref[:, hs])
        o_ret = jnp.concatenate(heads, axis=0) * (gate * jax.nn.sigmoid(gate))
        put_mix(b, o_attn, o_ret)


def _ffn_kernel(h_ref, hs_ref, g_ref, wg_ref, wu_ref, wd_ref, o_ref, os_ref, f_ref):
    i, j = pl.program_id(0), pl.program_id(1)
    tm = h_ref.shape[0]

    @pl.when(j == 0)
    def _():
        h = h_ref[...]
        f_ref[:tm] = _rms(h, g_ref[...]).astype(BF16)
        o_ref[...] = h

    @pl.when((i == 0) & (j == 0))
    def _():
        hs = hs_ref[...]
        f_ref[tm:] = _rms(hs, g_ref[...]).astype(BF16)
        os_ref[...] = hs

    def swiglu(f):
        gate = _dot(f, wg_ref[...])
        act = (gate * jax.nn.sigmoid(gate) * _dot(f, wu_ref[...])).astype(BF16)
        return _dot(act, wd_ref[...])

    @pl.when(i == 0)
    def _():
        y = swiglu(f_ref[...])
        o_ref[...] += y[:tm]
        os_ref[...] += y[tm:]

    @pl.when(i > 0)
    def _():
        o_ref[...] += swiglu(f_ref[:tm])


def _ffn(h, hs, g, wg, wu, wd, tm):
    m = h.shape[0]
    row = pl.BlockSpec((tm, D_MODEL), lambda i, j: (i, 0))
    return pl.pallas_call(
        _ffn_kernel,
        grid=(m // tm, D_FF // FF_TILE),
        in_specs=[
            row, _whole(hs), _whole(g),
            pl.BlockSpec((None, D_MODEL, FF_TILE), lambda i, j: (j, 0, 0)),
            pl.BlockSpec((None, D_MODEL, FF_TILE), lambda i, j: (j, 0, 0)),
            pl.BlockSpec((FF_TILE, D_MODEL), lambda i, j: (j, 0)),
        ],
        out_specs=[row, _whole(hs)],
        out_shape=[jax.ShapeDtypeStruct((m, D_MODEL), F32), jax.ShapeDtypeStruct(hs.shape, F32)],
        scratch_shapes=[pltpu.VMEM((tm + hs.shape[0], D_MODEL), BF16)],
        compiler_params=_params("arbitrary", "arbitrary"),
        name="ffn",
    )(h, hs, g, wg, wu, wd)


def _ple_kernel(h_ref, p_ref, hs_ref, ps_ref, g_ref, wp_ref, wg_ref, o_ref, os_ref):
    tm = h_ref.shape[0]

    def embed(h, p):
        gate = jax.nn.sigmoid(_dot(_rms(h, g_ref[...]).astype(BF16), wg_ref[...]))
        return h + _dot(p.astype(BF16), wp_ref[...]) * gate

    @pl.when(pl.program_id(0) == 0)
    def _():
        y = embed(jnp.concatenate([h_ref[...], hs_ref[...]], axis=0),
                  jnp.concatenate([p_ref[...], ps_ref[...]], axis=0))
        o_ref[...] = y[:tm]
        os_ref[...] = y[tm:]

    @pl.when(pl.program_id(0) > 0)
    def _():
        o_ref[...] = embed(h_ref[...], p_ref[...])


def _ple(h, p, hs, ps, g, w_ple, w_pg, tm):
    m = h.shape[0]
    row = pl.BlockSpec((tm, D_MODEL), lambda i: (i, 0))
    return pl.pallas_call(
        _ple_kernel,
        grid=(m // tm,),
        in_specs=[row, pl.BlockSpec((tm, PLE_DIM), lambda i: (i, 0)), _whole(hs), _whole(ps), _whole(g),
                  _whole(w_ple), _whole(w_pg)],
        out_specs=[row, _whole(hs)],
        out_shape=[jax.ShapeDtypeStruct((m, D_MODEL), F32), jax.ShapeDtypeStruct(hs.shape, F32)],
        compiler_params=_params("arbitrary"),
        name="ple",
    )(h, p, hs, ps, g, w_ple, w_pg)


PLE_ROWS = 512
FFN_ROWS = 1024


def _dense_tail(h, p, hs, ps, w):
    h, hs = _ffn(h, hs, w["ffn_norm_g"], w["w_gate"], w["w_up"], w["w_down"], FFN_ROWS)
    return _ple(h, p, hs, ps, w["ple_norm_g"], w["w_ple"], w["w_ple_gate"], PLE_ROWS)


def kernel(x_prompt, x_sample, cache_k_win, cache_v_win, state_ret, p_prompt, p_sample,
           attn_norm_g, w_in, q_norm_g, k_norm_g, attn_sinks, ret_out_g, w_out,
           ffn_norm_g, w_gate, w_up, w_down, ple_norm_g, w_ple, w_ple_gate):
    assert w_in.shape[0] == 1 and x_sample.shape[1] == 1 and cache_k_win.shape[2] == WINDOW
    y_p, y_s = x_prompt, x_sample
    outs = [[] for _ in range(6)]
    for l in range(w_in.shape[0]):
        w = {
            "attn_norm_g": attn_norm_g[l].reshape(1, D_MODEL),
            "q_norm_g": q_norm_g[l].reshape(1, HEAD_DIM),
            "k_norm_g": k_norm_g[l].reshape(1, HEAD_DIM),
            "attn_sinks": attn_sinks[l],
            "ret_out_g": ret_out_g[l].reshape(1, RET_WIDTH),
            "ffn_norm_g": ffn_norm_g[l].reshape(1, D_MODEL),
            "ple_norm_g": ple_norm_g[l].reshape(1, D_MODEL),
        }
        batch, seq, _ = y_p.shape
        nb = y_s.shape[0]
        h_p = y_p.reshape(batch * seq, D_MODEL)
        h_s = y_s.reshape(nb, D_MODEL)
        z_p, z_s, *cast = _in_proj(h_p, h_s, w["attn_norm_g"], w_in[l], w_gate[l], w_up[l], w_down[l],
                                   w_out[l], w_ple_gate[l], w_ple[l])
        w.update(zip(("w_gate", "w_up", "w_down", "w_out", "w_ple_gate", "w_ple"), cast))
        h_p, h_s, kp, vp, sp, ks, vs, ss = _mixer_out(
            z_p, h_p, h_s, z_s, cache_k_win[l], cache_v_win[l], state_ret[l], w["w_out"], w["q_norm_g"],
            w["k_norm_g"], w["attn_sinks"], w["ret_out_g"], batch, seq)
        h_p, h_s = _dense_tail(h_p, p_prompt[l].reshape(batch * seq, PLE_DIM),
                               h_s, p_sample[l].reshape(nb, PLE_DIM), w)
        y_p, y_s = h_p.reshape(batch, seq, D_MODEL), h_s.reshape(nb, 1, D_MODEL)
        kp = kp.reshape(batch, BLK, N_KV_HEADS, HEAD_DIM)
        vp = vp.reshape(batch, BLK, N_KV_HEADS, HEAD_DIM)
        for acc, val in zip(outs, (kp, vp, sp, ks, vs, ss)):
            acc.append(val)
    kp, vp, sp, ks, vs, ss = (jnp.stack(o) for o in outs)
    return (y_p, y_s, kp, vp, sp, ks, vs, ss)
```
